```python
import jax, jax.numpy as jnp
from jax import lax
import numpy as np

D_MODEL = 1024
BATCH = 8
SEQ = 2048
DEPTH = 2
DEC_BATCH = 32
DEC_SEQ = 16
PAST_LEN = 2048

CHUNK = 64
WINDOW = 128
HEAD_DIM = 64
N_HEADS_A = D_MODEL // (2 * HEAD_DIM)
N_KV_A = 2
GQA_GROUP = N_HEADS_A // N_KV_A
N_HEADS_B = D_MODEL // (2 * HEAD_DIM)
DK_B = 64
DV_B = 64
Q_A = N_HEADS_A * HEAD_DIM
KV_A = N_KV_A * HEAD_DIM
W_B = N_HEADS_B * DK_B
V_B = N_HEADS_B * DV_B
IN_WIDTH = Q_A + 2 * KV_A + 2 * W_B + 2 * V_B
SPLITS = (Q_A, Q_A + KV_A, Q_A + 2 * KV_A, Q_A + 2 * KV_A + W_B,
          Q_A + 2 * KV_A + 2 * W_B, Q_A + 2 * KV_A + 2 * W_B + V_B)
MIX_WIDTH = Q_A + V_B
N_EXPERTS = 16
N_GROUPS = 4
EXPERTS_PER_GROUP = N_EXPERTS // N_GROUPS
TOP_K = 2
D_EXPERT = D_MODEL // 2
RMS_EPS = 1e-6
NEG_INF = -1e30

kernel_name = 'hymba_swa_sink_hgrn2_grouped_moe_stream_step'


def rmsnorm(x, g):
    xf = x.astype(jnp.float32)
    y = xf * lax.rsqrt(jnp.mean(xf * xf, axis=-1, keepdims=True) + RMS_EPS)
    return (y * g.astype(jnp.float32)).astype(x.dtype)


def alibi_slopes():
    return jnp.exp2(-8.0 * jnp.arange(1, N_HEADS_A + 1, dtype=jnp.float32) / N_HEADS_A)


def lower_bounds(lb_param):
    cum = jnp.cumsum(jax.nn.softmax(lb_param.astype(jnp.float32), axis=0), axis=0)
    return cum - cum[0:1]


def sink_attention(q, k, v, dist, valid, sinks):
    s = jnp.einsum('bnqkgd,bnskd->bnkgqs', q, k).astype(jnp.float32) * (HEAD_DIM ** -0.5)
    slopes = alibi_slopes().reshape(N_KV_A, GQA_GROUP)
    s = s - slopes[None, None, :, :, None, None] * dist[None, :, None, None, :, :]
    s = jnp.where(valid[None, :, None, None, None, :], s, NEG_INF)
    sink = sinks.astype(jnp.float32).reshape(N_KV_A, GQA_GROUP)[None, None, :, :, None, None]
    sink = jnp.broadcast_to(sink, s.shape[:-1] + (1,))
    p = jax.nn.softmax(jnp.concatenate([s, sink], axis=-1), axis=-1)[..., :-1]
    return jnp.einsum('bnkgqs,bnskd->bnqkgd', p.astype(v.dtype), v)


def swa_prompt(q, k, v, sinks):
    B, T = q.shape[:2]
    nc, nb = T // CHUNK, WINDOW // CHUNK
    qc = q.reshape(B, nc, CHUNK, N_KV_A, GQA_GROUP, HEAD_DIM)

    def band(a):
        ap = jnp.pad(a, ((0, 0), (WINDOW, 0), (0, 0), (0, 0)))
        ap = ap.reshape(B, nc + nb, CHUNK, N_KV_A, HEAD_DIM)
        return jnp.concatenate([ap[:, j:j + nc] for j in range(nb + 1)], axis=2)

    q_pos = jnp.arange(T).reshape(nc, CHUNK)
    k_pos = jnp.arange(nc)[:, None] * CHUNK - WINDOW + jnp.arange(WINDOW + CHUNK)[None, :]
    dist = jnp.abs(q_pos[:, :, None] - k_pos[:, None, :]).astype(jnp.float32)
    o = sink_attention(qc, band(k), band(v), dist, k_pos >= 0, sinks)
    return o.reshape(B, T, Q_A)


def swa_sample(q, k, v, k_cache, v_cache, sinks):
    B, L = q.shape[:2]
    R = k_cache.shape[1]
    k_all = jnp.concatenate([k_cache.astype(k.dtype), k], axis=1)[:, None]
    v_all = jnp.concatenate([v_cache.astype(v.dtype), v], axis=1)[:, None]
    q_pos = R + jnp.arange(L)
    k_pos = jnp.arange(R + L)
    dist = jnp.abs(q_pos[:, None] - k_pos[None, :]).astype(jnp.float32)[None]
    valid = jnp.ones((1, R + L), dtype=bool)
    o = sink_attention(q.reshape(B, 1, L, N_KV_A, GQA_GROUP, HEAD_DIM), k_all, v_all, dist, valid, sinks)
    return o.reshape(B, L, Q_A)


def hgrn_block(S, q, k, i, logf):
    L = q.shape[1]
    b = jnp.cumsum(logf, axis=1)
    causal = jnp.tril(jnp.ones((L, L), dtype=bool))[None, :, :, None, None]
    decay = jnp.exp(jnp.where(causal, b[:, :, None] - b[:, None, :], -jnp.inf))
    a = jnp.sum(q[:, :, None] * k[:, None] * decay, axis=-1)
    o = jnp.einsum('btsh,bshv->bthv', a, i) + jnp.einsum('bthk,bhkv->bthv', q * jnp.exp(b), S)
    b_last = b[:, -1]
    kd = k * jnp.exp(b_last[:, None] - b)
    S_new = jnp.exp(b_last)[..., None] * S + jnp.einsum('bshk,bshv->bhkv', kd, i)
    return S_new, o


def hgrn_scan(S0, q, k, i, logf):
    B, T = q.shape[:2]
    n = T // CHUNK

    def to_blocks(a):
        return jnp.moveaxis(a.reshape((B, n, CHUNK) + a.shape[2:]), 1, 0)

    def step(S, blk):
        return hgrn_block(S, *blk)

    S, o = lax.scan(step, S0, (to_blocks(q), to_blocks(k), to_blocks(i), to_blocks(logf)))
    return S, jnp.moveaxis(o, 0, 1).reshape(B, T, N_HEADS_B, DV_B)


def moe(h, w_router, router_bias, w_e_gate, w_e_up, w_e_down):
    B, T, D = h.shape
    hf = h.reshape(B * T, D)
    aff = jax.nn.sigmoid(jnp.dot(hf, w_router).astype(jnp.float32))
    sel = aff + router_bias.astype(jnp.float32)
    grp = sel.reshape(-1, N_GROUPS, EXPERTS_PER_GROUP)
    g_score = jnp.sum(lax.top_k(grp, TOP_K)[0], axis=-1)
    g_sel = jnp.argmax(g_score, axis=-1)
    in_group = (jnp.arange(N_EXPERTS) // EXPERTS_PER_GROUP)[None, :] == g_sel[:, None]
    _, idx = lax.top_k(jnp.where(in_group, sel, -jnp.inf), TOP_K)
    w = jnp.take_along_axis(aff, idx, axis=-1)
    w = w / jnp.sum(w, axis=-1, keepdims=True)
    gates = jnp.einsum('nk,nke->ne', w, jax.nn.one_hot(idx, N_EXPERTS, dtype=jnp.float32)).astype(h.dtype)
    y = jnp.zeros_like(hf)
    for e in range(N_EXPERTS):
        u = jax.nn.silu(jnp.dot(hf, w_e_gate[e])) * jnp.dot(hf, w_e_up[e])
        y = y + gates[:, e:e + 1] * jnp.dot(u, w_e_down[e])
    return y.reshape(B, T, D)


def trunk_layer(x, c, S0, k_cache, v_cache, lb, w_ada, b_ada, ln1, ln2, w_in, q_norm, k_norm, sinks,
                o_norm, w_out, w_router, router_bias, w_e_gate, w_e_up, w_e_down):
    B, T, _ = x.shape
    mod = jnp.dot(jax.nn.silu(c), w_ada) + b_ada
    sh1, sc1, g1, sh2, sc2, g2 = [m[:, None, :] for m in jnp.split(mod, 6, axis=-1)]

    h = rmsnorm(x, ln1) * (1 + sc1) + sh1
    qa, ka, va, qb, fb, ib, gb = jnp.split(jnp.dot(h, w_in), SPLITS, axis=-1)
    qa = rmsnorm(qa.reshape(B, T, N_HEADS_A, HEAD_DIM), q_norm)
    ka = rmsnorm(ka.reshape(B, T, N_KV_A, HEAD_DIM), k_norm)
    va = va.reshape(B, T, N_KV_A, HEAD_DIM)
    if k_cache is None:
        oa = swa_prompt(qa, ka, va, sinks)
        k_state, v_state = ka[:, -WINDOW:], va[:, -WINDOW:]
    else:
        oa = swa_sample(qa, ka, va, k_cache, v_cache, sinks)
        k_state, v_state = ka, va

    z = fb.astype(jnp.float32).reshape(B, T, N_HEADS_B, DK_B)
    lbh = lb.reshape(N_HEADS_B, DK_B)
    logf = jnp.logaddexp(jnp.log(lbh), jnp.log1p(-lbh) + jax.nn.log_sigmoid(z))
    kb = (1 - lbh) * jax.nn.sigmoid(-z)
    qh = qb.astype(jnp.float32).reshape(B, T, N_HEADS_B, DK_B) * (DK_B ** -0.5)
    ih = ib.astype(jnp.float32).reshape(B, T, N_HEADS_B, DV_B)
    S0 = S0.astype(jnp.float32)
    if k_cache is None:
        S, ob = hgrn_scan(S0, qh, kb, ih, logf)
    else:
        S, ob = hgrn_block(S0, qh, kb, ih, logf)
    ob = rmsnorm(ob, o_norm) * jax.nn.silu(gb.astype(jnp.float32).reshape(B, T, N_HEADS_B, DV_B))

    mix = jnp.concatenate([oa, ob.reshape(B, T, V_B).astype(x.dtype)], axis=-1)
    x = x + g1 * jnp.dot(mix, w_out)

    h2 = rmsnorm(x, ln2) * (1 + sc2) + sh2
    x = x + g2 * moe(h2, w_router, router_bias, w_e_gate, w_e_up, w_e_down)
    return x, k_state, v_state, S


def setup_inputs(seed: int = 0) -> dict:
    key = jax.random.key(seed)
    ks = jax.random.split(key, 24)
    f32 = jnp.float32
    R = min(WINDOW, PAST_LEN)

    def nrm(k, shape, scale):
        return jax.random.normal(k, shape, f32) * scale

    return {
        'x_prompt': nrm(ks[0], (BATCH, SEQ, D_MODEL), 1.0),
        'x_sample': nrm(ks[1], (DEC_BATCH, DEC_SEQ, D_MODEL), 1.0),
        'cache_k': nrm(ks[2], (DEPTH, DEC_BATCH, R, N_KV_A, HEAD_DIM), 1.0),
        'cache_v': nrm(ks[3], (DEPTH, DEC_BATCH, R, N_KV_A, HEAD_DIM), 1.0),
        'state_hgrn': nrm(ks[4], (DEPTH, DEC_BATCH, N_HEADS_B, DK_B, DV_B), 0.5),
        'c_prompt': nrm(ks[5], (BATCH, D_MODEL), 1.0),
        'c_sample': nrm(ks[6], (DEC_BATCH, D_MODEL), 1.0),
        'w_ada': nrm(ks[7], (DEPTH, D_MODEL, 6 * D_MODEL), 0.5 * D_MODEL ** -0.5),
        'b_ada': nrm(ks[8], (DEPTH, 6 * D_MODEL), 0.02),
        'ln1': 1.0 + nrm(ks[9], (DEPTH, D_MODEL), 0.02),
        'ln2': 1.0 + nrm(ks[10], (DEPTH, D_MODEL), 0.02),
        'w_in': nrm(ks[11], (DEPTH, D_MODEL, IN_WIDTH), D_MODEL ** -0.5),
        'q_norm': 1.0 + nrm(ks[12], (DEPTH, HEAD_DIM), 0.02),
        'k_norm': 1.0 + nrm(ks[13], (DEPTH, HEAD_DIM), 0.02),
        'sinks': nrm(ks[14], (DEPTH, N_HEADS_A), 1.0),
        'lb_param': nrm(ks[15], (DEPTH, N_HEADS_B * DK_B), 0.5),
        'o_norm': 1.0 + nrm(ks[16], (DEPTH, DV_B), 0.02),
        'w_out': nrm(ks[17], (DEPTH, MIX_WIDTH, D_MODEL), MIX_WIDTH ** -0.5),
        'w_router': nrm(ks[18], (D_MODEL, N_EXPERTS), D_MODEL ** -0.5),
        'router_bias': nrm(ks[19], (N_EXPERTS,), 0.01),
        'w_e_gate': nrm(ks[20], (DEPTH, N_EXPERTS, D_MODEL, D_EXPERT), D_MODEL ** -0.5),
        'w_e_up': nrm(ks[21], (DEPTH, N_EXPERTS, D_MODEL, D_EXPERT), D_MODEL ** -0.5),
        'w_e_down': nrm(ks[22], (DEPTH, N_EXPERTS, D_EXPERT, D_MODEL), D_EXPERT ** -0.5),
    }


def reference(x_prompt, x_sample, cache_k, cache_v, state_hgrn, c_prompt, c_sample, w_ada, b_ada,
              ln1, ln2, w_in, q_norm, k_norm, sinks, lb_param, o_norm, w_out, w_router, router_bias,
              w_e_gate, w_e_up, w_e_down):
    lbs = lower_bounds(lb_param)
    yp, ys = x_prompt, x_sample
    kp, vp, sp, kss, vss, sss = [], [], [], [], [], []
    s0_prompt = jnp.zeros((x_prompt.shape[0], N_HEADS_B, DK_B, DV_B), jnp.float32)
    for l in range(DEPTH):
        params = (lbs[l], w_ada[l], b_ada[l], ln1[l], ln2[l], w_in[l], q_norm[l], k_norm[l], sinks[l],
                  o_norm[l], w_out[l], w_router, router_bias, w_e_gate[l], w_e_up[l], w_e_down[l])
        yp, k_new, v_new, s_new = trunk_layer(yp, c_prompt, s0_prompt, None, None, *params)
        kp.append(k_new); vp.append(v_new); sp.append(s_new)
        ys, k_new, v_new, s_new = trunk_layer(ys, c_sample, state_hgrn[l], cache_k[l], cache_v[l], *params)
        kss.append(k_new); vss.append(v_new); sss.append(s_new)
    return (yp, ys, jnp.stack(kp), jnp.stack(vp), jnp.stack(sp), jnp.stack(kss), jnp.stack(vss), jnp.stack(sss))
```

```python
import functools

import numpy as np
import jax
import jax.numpy as jnp
from jax import lax
from jax.experimental import pallas as pl
from jax.experimental.pallas import tpu as pltpu

F32 = jnp.float32
BF16 = jnp.bfloat16
I32 = jnp.int32

D_MODEL = 1024
HEAD_DIM = 64
N_HEADS_A = 8
N_KV_A = 2
GQA_GROUP = N_HEADS_A // N_KV_A
N_HEADS_B = 8
DK_B = 64
DV_B = 64
Q_A = N_HEADS_A * HEAD_DIM
KV_A = N_KV_A * HEAD_DIM
W_B = N_HEADS_B * DK_B
V_B = N_HEADS_B * DV_B
CHUNK = 64
WINDOW = 128
N_EXPERTS = 16
N_GROUPS = 4
EXPERTS_PER_GROUP = 4
D_EXPERT = D_MODEL // 2
RMS_EPS = 1e-6
NEG_INF = -1e30
N_MOD = 6

LANES = 128
ROWS = 512
GRP = 32
SUB = ROWS // GRP
ATT_ROWS = 256
N_PAIR = N_HEADS_B // 2
TM = 256
PAIRS_PER_GROUP = 6
N_CLASS = N_GROUPS * PAIRS_PER_GROUP
AUG = D_MODEL + LANES
VMEM_LIMIT = 56 * 1024 * 1024


def _cparams(sem):
    return pltpu.CompilerParams(dimension_semantics=sem, vmem_limit_bytes=VMEM_LIMIT)


def _dot(a, b):
    return jnp.dot(a, b, preferred_element_type=F32)


def _dot_nt(a, b):
    return lax.dot_general(a, b, (((1,), (1,)), ((), ())), preferred_element_type=F32)


def _dot_tn(a, b):
    return lax.dot_general(a, b, (((0,), (0,)), ((), ())), preferred_element_type=F32)


def _sigmoid(x):
    return 1.0 / (1.0 + jnp.exp(-x))


def _split2(x):
    hi = x.astype(BF16)
    lo = (x - hi.astype(F32)).astype(BF16)
    return hi, lo


def _split3(x):
    hi = x.astype(BF16)
    r = x - hi.astype(F32)
    mid = r.astype(BF16)
    lo = (r - mid.astype(F32)).astype(BF16)
    return hi, mid, lo


def _seg_mean(sq, bd):
    hi, lo = _split2(sq)
    return _dot(hi, bd) + _dot(lo, bd)


def _adaln_kernel(c_ref, w_ref, b_ref, o_ref):
    c = c_ref[...]
    a = c * _sigmoid(c)
    a_hi, a_lo = _split2(a)
    w_hi, w_lo = _split2(w_ref[...])
    acc = _dot(a_hi, w_hi) + _dot(a_lo, w_hi) + _dot(a_hi, w_lo)
    o_ref[...] = acc + b_ref[...]


def _adaln(c_exp, w_ada, b_ada):
    depth = w_ada.shape[0]
    rows = c_exp.shape[0]
    tn = D_MODEL
    return pl.pallas_call(
        _adaln_kernel,
        grid=(depth, N_MOD * D_MODEL // tn),
        in_specs=[
            pl.BlockSpec((rows, D_MODEL), lambda l, n: (0, 0)),
            pl.BlockSpec((None, D_MODEL, tn), lambda l, n: (l, 0, n)),
            pl.BlockSpec((None, 1, tn), lambda l, n: (l, 0, n)),
        ],
        out_specs=pl.BlockSpec((None, rows, tn), lambda l, n: (l, 0, n)),
        out_shape=jax.ShapeDtypeStruct((depth, rows, N_MOD * D_MODEL), F32),
        compiler_params=_cparams(("arbitrary", "arbitrary")),
        name="adaln",
    )(c_exp, w_ada, b_ada.reshape(depth, 1, N_MOD * D_MODEL))


def _inproj_kernel(layer, x_ref, sh_ref, sc_ref, ln_ref, w_ref, qn_ref, kn_ref, lbp_ref, bd_ref,
                   q_out, k_out, v_out, qb_out, logf_out, kb_out, ib_out, gb_out):
    x = x_ref[...]
    ms = jnp.mean(x * x, axis=-1, keepdims=True)
    h = x * lax.rsqrt(ms + RMS_EPS) * ln_ref[...]
    h = h * (1.0 + sc_ref[...]) + sh_ref[...]
    hb = h.reshape(ROWS, D_MODEL).astype(BF16)
    bd = bd_ref[...]

    qa = _dot(hb, w_ref[:, 0:Q_A])
    qa = qa * lax.rsqrt(_seg_mean(qa * qa, bd) + RMS_EPS) * qn_ref[...]
    qa = (qa * (HEAD_DIM ** -0.5)).astype(BF16)
    for hh in range(GQA_GROUP):
        q_out[hh] = qa[:, hh * LANES:(hh + 1) * LANES]

    ka = _dot(hb, w_ref[:, Q_A:Q_A + KV_A])
    ka = ka * lax.rsqrt(_seg_mean(ka * ka, bd[:KV_A, :KV_A]) + RMS_EPS) * kn_ref[...]
    k_out[...] = ka
    v_out[...] = _dot(hb, w_ref[:, Q_A + KV_A:Q_A + 2 * KV_A])

    c0 = Q_A + 2 * KV_A
    qb_out[...] = (_dot(hb, w_ref[:, c0:c0 + W_B]) * (DK_B ** -0.5)).astype(BF16)

    p = lbp_ref[...]
    e = jnp.exp(p - jnp.max(p, axis=0, keepdims=True))
    sm = e / jnp.sum(e, axis=0, keepdims=True)
    cum0 = sm[0:1]
    cum = cum0
    for r in range(1, layer + 1):
        cum = cum + sm[r:r + 1]
    lb = cum - cum0

    z = _dot(hb, w_ref[:, c0 + W_B:c0 + 2 * W_B])
    log_sig = jnp.minimum(z, 0.0) - jnp.log1p(jnp.exp(-jnp.abs(z)))
    a = jnp.log(lb)
    c = jnp.log1p(-lb) + log_sig
    logf_out[...] = jnp.maximum(a, c) + jnp.log1p(jnp.exp(-jnp.abs(a - c)))
    kb_out[...] = ((1.0 - lb) * _sigmoid(-z)).astype(BF16)

    c1 = c0 + 2 * W_B
    ib_out[...] = _dot(hb, w_ref[:, c1:c1 + V_B]).astype(BF16)
    gb_out[...] = _dot(hb, w_ref[:, c1 + V_B:c1 + 2 * V_B]).astype(BF16)


def _inproj(layer, x4, mod, ln1, w_in_b, q_norm_t, k_norm_t, lb_param, bd, tiles_per_batch):
    n_tiles = x4.shape[0]
    depth = lb_param.shape[0]
    n = n_tiles * ROWS
    in_width = w_in_b.shape[-1]

    def mod_spec(j):
        return pl.BlockSpec((None, None, GRP, 1, D_MODEL),
                            lambda i: (layer, i // tiles_per_batch, 0, 0, j))

    row_spec = lambda w: pl.BlockSpec((ROWS, w), lambda i: (i, 0))
    return pl.pallas_call(
        functools.partial(_inproj_kernel, layer),
        grid=(n_tiles,),
        in_specs=[
            pl.BlockSpec((None, GRP, SUB, D_MODEL), lambda i: (i, 0, 0, 0)),
            mod_spec(0), mod_spec(1),
            pl.BlockSpec((None, 1, D_MODEL), lambda i: (layer, 0, 0)),
            pl.BlockSpec((None, D_MODEL, in_width), lambda i: (layer, 0, 0)),
            pl.BlockSpec((None, 1, Q_A), lambda i: (layer, 0, 0)),
            pl.BlockSpec((None, 1, KV_A), lambda i: (layer, 0, 0)),
            pl.BlockSpec((depth, W_B), lambda i: (0, 0)),
            pl.BlockSpec((Q_A, Q_A), lambda i: (0, 0)),
        ],
        out_specs=[
            pl.BlockSpec((None, GQA_GROUP, ROWS, LANES), lambda i: (i, 0, 0, 0)),
            row_spec(KV_A), row_spec(KV_A),
            row_spec(W_B), row_spec(W_B), row_spec(W_B), row_spec(V_B), row_spec(V_B),
        ],
        out_shape=[
            jax.ShapeDtypeStruct((n_tiles, GQA_GROUP, ROWS, LANES), BF16),
            jax.ShapeDtypeStruct((n, KV_A), F32),
            jax.ShapeDtypeStruct((n, KV_A), F32),
            jax.ShapeDtypeStruct((n, W_B), BF16),
            jax.ShapeDtypeStruct((n, W_B), F32),
            jax.ShapeDtypeStruct((n, W_B), BF16),
            jax.ShapeDtypeStruct((n, V_B), BF16),
            jax.ShapeDtypeStruct((n, V_B), BF16),
        ],
        compiler_params=_cparams(("arbitrary",)),
        name=f"inproj{layer}",
    )(x4, mod, mod, ln1, w_in_b, q_norm_t, k_norm_t, lb_param, bd)


def _attend(q, k, v, sinks_ref, q_len, k_off, k_pos0=None):
    n_q = q.shape[0]
    n_k = k.shape[0]
    row = lax.broadcasted_iota(I32, (n_q, n_k), 0)
    col = lax.broadcasted_iota(I32, (n_q, n_k), 1)
    dist = jnp.abs((row % q_len) + k_off - col).astype(F32)
    lane = lax.broadcasted_iota(I32, (n_k, LANES), 1)
    hh_col = lax.broadcasted_iota(I32, (n_q, 1), 0) // q_len
    out = jnp.zeros((n_q, LANES), F32)
    for g in range(N_KV_A):
        in_g = (lane // HEAD_DIM) == g
        kg = jnp.where(in_g, k, 0.0).astype(BF16)
        vg = jnp.where(in_g, v, 0.0).astype(BF16)
        s = _dot_nt(q, kg)
        head = (g * GQA_GROUP + 1 + hh_col).astype(F32)
        s = s - jnp.exp2(-8.0 * head / N_HEADS_A) * dist
        if k_pos0 is not None:
            s = jnp.where(col + k_pos0 >= 0, s, NEG_INF)
        sink = jnp.zeros((n_q, 1), F32)
        for j in range(GQA_GROUP):
            sink = jnp.where(hh_col == j, sinks_ref[g * GQA_GROUP + j], sink)
        m = jnp.maximum(jnp.max(s, axis=-1, keepdims=True), sink)
        p = jnp.exp(s - m)
        denom = jnp.sum(p, axis=-1, keepdims=True) + jnp.exp(sink - m)
        out = out + _dot(p.astype(BF16), vg) / denom
    return out


def _attn_prompt_kernel(sinks_ref, q_ref, kp_ref, kc_ref, vp_ref, vc_ref, o_ref):
    j = pl.program_id(1)
    kcat = jnp.concatenate([kp_ref[...], kc_ref[...]], axis=0)
    vcat = jnp.concatenate([vp_ref[...], vc_ref[...]], axis=0)
    for c in range(ATT_ROWS // CHUNK):
        q = q_ref[:, c * CHUNK:(c + 1) * CHUNK, :].reshape(GQA_GROUP * CHUNK, LANES)
        k = kcat[c * CHUNK:c * CHUNK + WINDOW + CHUNK]
        v = vcat[c * CHUNK:c * CHUNK + WINDOW + CHUNK]
        k_pos0 = j * ATT_ROWS + c * CHUNK - WINDOW
        out = _attend(q, k, v, sinks_ref, CHUNK, WINDOW, k_pos0)
        o_ref[:, c * CHUNK:(c + 1) * CHUNK, :] = out.reshape(GQA_GROUP, CHUNK, LANES).astype(BF16)


def _attn_prompt(q4, k, v, sinks, batch, seq):
    steps = seq // ATT_ROWS
    per_tile = ROWS // ATT_ROWS
    win_blocks = ATT_ROWS // WINDOW
    n_tiles = batch * seq // ROWS
    cur = pl.BlockSpec((ATT_ROWS, KV_A), lambda b, j: (b * steps + j, 0))
    prev = pl.BlockSpec(
        (WINDOW, KV_A), lambda b, j: (b * steps * win_blocks + jnp.maximum(j * win_blocks - 1, 0), 0))
    qo = pl.BlockSpec((None, GQA_GROUP, ATT_ROWS, LANES),
                      lambda b, j: ((b * steps + j) // per_tile, 0, j % per_tile, 0))
    return pl.pallas_call(
        _attn_prompt_kernel,
        grid=(batch, steps),
        in_specs=[pl.BlockSpec(memory_space=pltpu.SMEM), qo, prev, cur, prev, cur],
        out_specs=qo,
        out_shape=jax.ShapeDtypeStruct((n_tiles, GQA_GROUP, ROWS, LANES), BF16),
        compiler_params=_cparams(("arbitrary", "arbitrary")),
        name="attn_prompt",
    )(sinks, q4, k, k, v, v)


def _attn_sample_kernel(sinks_ref, q_ref, kc_ref, kn_ref, vc_ref, vn_ref, o_ref):
    n_new = kn_ref.shape[0]
    n_cache = kc_ref.shape[0]
    q = q_ref[...].reshape(GQA_GROUP * n_new, LANES)
    k = jnp.concatenate([kc_ref[...], kn_ref[...]], axis=0)
    v = jnp.concatenate([vc_ref[...], vn_ref[...]], axis=0)
    out = _attend(q, k, v, sinks_ref, n_new, n_cache)
    o_ref[...] = out.reshape(GQA_GROUP, n_new, LANES).astype(BF16)


def _attn_sample(q4, k, v, cache_k, cache_v, sinks, tile, row0, dec_batch, dec_seq):
    n_cache = cache_k.shape[1]
    q_spec = pl.BlockSpec((None, GQA_GROUP, dec_seq, LANES), lambda b: (tile, 0, b, 0))
    new = pl.BlockSpec((dec_seq, KV_A), lambda b: (row0 // dec_seq + b, 0))
    cache = pl.BlockSpec((None, n_cache, KV_A), lambda b: (b, 0, 0))
    return pl.pallas_call(
        _attn_sample_kernel,
        grid=(dec_batch,),
        in_specs=[pl.BlockSpec(memory_space=pltpu.SMEM), q_spec, cache, new, cache, new],
        out_specs=pl.BlockSpec((None, GQA_GROUP, dec_seq, LANES), lambda b: (0, 0, b, 0)),
        out_shape=jax.ShapeDtypeStruct((1, GQA_GROUP, dec_batch * dec_seq, LANES), BF16),
        compiler_params=_cparams(("arbitrary",)),
        name="attn_sample",
    )(sinks, q4, cache_k, k, cache_v, v)


def _level_sizes(length):
    sizes = []
    m = length // 2
    while m >= 1:
        sizes.append(m)
        m //= 2
    return sizes


def _hgrn_consts(length):
    t = np.arange(length)
    tri = (t[None, :] <= t[:, None]).astype(np.float32)
    after = (t[None, :] > t[:, None]).astype(np.float32)
    mats = [tri, after]
    masks = []
    s = np.tile(t, 2)[None, :]
    tt = t[:, None]
    for m in _level_sizes(length):
        blk = t // m
        same = blk[None, :] == blk[:, None]
        q_rows = (blk % 2 == 1)[:, None]
        sel = np.where(q_rows, same & (t[None, :] <= t[:, None]), same & (t[None, :] > t[:, None]))
        mats.append(sel.astype(np.float32))
        masks.append(((tt // m) % 2 == 1) & ((s // m) == (tt // m) - 1))
    masks.append(s == tt)
    sel = np.concatenate(mats, axis=0)
    sel3 = np.concatenate([sel, sel, sel], axis=1)
    return sel3, np.stack(masks).astype(np.int32)


def _hgrn_chunk(q, k, i, g, logf, sel_ref, mask_ref, bd2, onorm, st_ref, arg_ref):
    length = q.shape[0]
    n_lev = len(_level_sizes(length))
    hi, mid, lo = _split3(logf)
    arg_ref[...] = _dot(sel_ref[...], jnp.concatenate([hi, mid, lo], axis=0))
    lane = lax.broadcasted_iota(I32, (length, LANES), 1)
    first = lane < DK_B
    rowp = lax.broadcasted_iota(I32, (LANES, LANES), 0) // DV_B
    colp = lax.broadcasted_iota(I32, (LANES, LANES), 1) // DK_B
    same_head = rowp == colp
    outs = []
    for p in range(N_PAIR):
        sl = slice(p * LANES, (p + 1) * LANES)
        qp = q[:, sl].astype(F32)
        kp = k[:, sl].astype(F32)
        ip = i[:, sl].astype(F32)

        def stack2(x):
            return jnp.concatenate([jnp.where(first, x, 0.0), jnp.where(first, 0.0, x)],
                                   axis=0).astype(BF16)

        a = _dot_nt(qp.astype(BF16), stack2(kp))
        a = jnp.where(mask_ref[n_lev] != 0, a, 0.0)
        for lev in range(n_lev):
            e = jnp.exp(arg_ref[(2 + lev) * length:(3 + lev) * length, sl])
            pr = _dot_nt((qp * e).astype(BF16), stack2(kp * e))
            a = jnp.where(mask_ref[lev] != 0, pr, a)
        e_b = jnp.exp(arg_ref[0:length, sl])
        e_k = jnp.exp(arg_ref[length:2 * length, sl])
        st = st_ref[p]
        o = _dot(a.astype(BF16), stack2(ip)) + _dot_nt((qp * e_b).astype(BF16), st.astype(BF16))
        upd = _dot_tn(ip.astype(BF16), (kp * e_k).astype(BF16))
        st_ref[p] = st * e_b[length - 1:length, :] + jnp.where(same_head, upd, 0.0)
        ms = _seg_mean(o * o, bd2)
        gp = g[:, sl].astype(F32)
        o = o * lax.rsqrt(ms + RMS_EPS) * onorm[:, sl] * (gp * _sigmoid(gp))
        outs.append(o.astype(BF16))
    return jnp.concatenate(outs, axis=1)


def _hgrn_prompt_kernel(q_ref, k_ref, i_ref, g_ref, f_ref, sel_ref, mask_ref, bd2_ref, on_ref,
                        o_ref, s_ref, st_ref, arg_ref):
    j = pl.program_id(1)

    @pl.when(j == 0)
    def _():
        st_ref[...] = jnp.zeros_like(st_ref)

    bd2 = bd2_ref[...]
    onorm = on_ref[...]

    def body(c, carry):
        r = pl.ds(pl.multiple_of(c * CHUNK, CHUNK), CHUNK)
        o_ref[r, :] = _hgrn_chunk(q_ref[r, :], k_ref[r, :], i_ref[r, :], g_ref[r, :], f_ref[r, :],
                                  sel_ref, mask_ref, bd2, onorm, st_ref, arg_ref)
        return carry

    lax.fori_loop(0, ATT_ROWS // CHUNK, body, 0)

    @pl.when(j == pl.num_programs(1) - 1)
    def _():
        s_ref[...] = st_ref[...]


def _hgrn_prompt(qb, kb, ib, gb, logf, sel, masks, bd2, onorm_t, batch, seq):
    steps = seq // ATT_ROWS
    n = batch * seq
    row = pl.BlockSpec((ATT_ROWS, W_B), lambda b, j: (b * steps + j, 0))
    whole = lambda a: pl.BlockSpec(a.shape, lambda b, j: (0,) * a.ndim)
    return pl.pallas_call(
        _hgrn_prompt_kernel,
        grid=(batch, steps),
        in_specs=[row, row, row, row, row, whole(sel), whole(masks), whole(bd2), whole(onorm_t)],
        out_specs=[row, pl.BlockSpec((None, N_PAIR, LANES, LANES), lambda b, j: (b, 0, 0, 0))],
        out_shape=[jax.ShapeDtypeStruct((n, V_B), BF16),
                   jax.ShapeDtypeStruct((batch, N_PAIR, LANES, LANES), F32)],
        scratch_shapes=[pltpu.VMEM((N_PAIR, LANES, LANES), F32),
                        pltpu.VMEM((sel.shape[0], W_B), F32)],
        compiler_params=_cparams(("arbitrary", "arbitrary")),
        name="hgrn_prompt",
    )(qb, kb, ib, gb, logf, sel, masks, bd2, onorm_t)


def _hgrn_sample_kernel(q_ref, k_ref, i_ref, g_ref, f_ref, s0_ref, sel_ref, mask_ref, bd2_ref,
                        on_ref, o_ref, s_ref, st_ref, arg_ref):
    st_ref[...] = s0_ref[...]
    o_ref[...] = _hgrn_chunk(q_ref[...], k_ref[...], i_ref[...], g_ref[...], f_ref[...],
                             sel_ref, mask_ref, bd2_ref[...], on_ref[...], st_ref, arg_ref)
    s_ref[...] = st_ref[...]


def _hgrn_sample(qb, kb, ib, gb, logf, s0, sel, masks, bd2, onorm_t, row0, dec_batch, dec_seq):
    row = pl.BlockSpec((dec_seq, W_B), lambda b: (row0 // dec_seq + b, 0))
    whole = lambda a: pl.BlockSpec(a.shape, lambda b: (0,) * a.ndim)
    state = pl.BlockSpec((None, N_PAIR, LANES, LANES), lambda b: (b, 0, 0, 0))
    return pl.pallas_call(
        _hgrn_sample_kernel,
        grid=(dec_batch,),
        in_specs=[row, row, row, row, row, state, whole(sel), whole(masks), whole(bd2),
                  whole(onorm_t)],
        out_specs=[pl.BlockSpec((dec_seq, V_B), lambda b: (b, 0)), state],
        out_shape=[jax.ShapeDtypeStruct((dec_batch * dec_seq, V_B), BF16),
                   jax.ShapeDtypeStruct((dec_batch, N_PAIR, LANES, LANES), F32)],
        scratch_shapes=[pltpu.VMEM((N_PAIR, LANES, LANES), F32),
                        pltpu.VMEM((sel.shape[0], W_B), F32)],
        compiler_params=_cparams(("arbitrary",)),
        name="hgrn_sample",
    )(qb, kb, ib, gb, logf, s0, sel, masks, bd2, onorm_t)


def _outproj_kernel(prompt_tiles, x_ref, oap_ref, obp_ref, oas_ref, obs_ref, g1_ref, sh_ref, sc_ref,
                    ln_ref, w_ref, wr_ref, rb_ref, x1_out, aug_out, cls_out):
    def mix_of(oa_ref, ob_ref):
        return jnp.concatenate([oa_ref[hh] for hh in range(GQA_GROUP)] + [ob_ref[...]], axis=1)

    is_prompt = pl.program_id(0) < prompt_tiles
    mix = jnp.where(is_prompt, mix_of(oap_ref, obp_ref), mix_of(oas_ref, obs_ref))
    att = _dot(mix, w_ref[...])
    x = x_ref[...]
    x1 = x + g1_ref[...] * att.reshape(GRP, SUB, D_MODEL)
    x1_out[...] = x1
    ms = jnp.mean(x1 * x1, axis=-1, keepdims=True)
    h = x1 * lax.rsqrt(ms + RMS_EPS) * ln_ref[...]
    h = (h * (1.0 + sc_ref[...]) + sh_ref[...]).reshape(ROWS, D_MODEL)
    aug_out[:, 0:D_MODEL] = h

    h_hi, h_lo = _split2(h)
    w_hi, w_lo = _split2(wr_ref[...])
    logits = (_dot(h_hi, w_hi) + _dot(h_lo, w_hi) + _dot(h_hi, w_lo)).T[0:N_EXPERTS]
    aff = _sigmoid(logits)
    sel = aff + rb_ref[...]
    e_idx = lax.broadcasted_iota(I32, (N_EXPERTS, ROWS), 0)
    l_idx = lax.broadcasted_iota(I32, (EXPERTS_PER_GROUP, ROWS), 0)

    def first_argmax(vals, idx, big):
        top = jnp.max(vals, axis=0, keepdims=True)
        return top, jnp.min(jnp.where(vals == top, idx, big), axis=0, keepdims=True)

    g_scores = []
    for g in range(N_GROUPS):
        sg = sel[g * EXPERTS_PER_GROUP:(g + 1) * EXPERTS_PER_GROUP]
        m1, i1 = first_argmax(sg, l_idx, EXPERTS_PER_GROUP)
        m2 = jnp.max(jnp.where(l_idx == i1, -jnp.inf, sg), axis=0, keepdims=True)
        g_scores.append(m1 + m2)
    g_score = jnp.concatenate(g_scores, axis=0)
    g_idx = lax.broadcasted_iota(I32, (N_GROUPS, ROWS), 0)
    _, g_sel = first_argmax(g_score, g_idx, N_GROUPS)
    masked = jnp.where((e_idx // EXPERTS_PER_GROUP) == g_sel, sel, -jnp.inf)
    _, e1 = first_argmax(masked, e_idx, N_EXPERTS)
    _, e2 = first_argmax(jnp.where(e_idx == e1, -jnp.inf, masked), e_idx, N_EXPERTS)
    w1 = jnp.sum(jnp.where(e_idx == e1, aff, 0.0), axis=0, keepdims=True)
    w2 = jnp.sum(jnp.where(e_idx == e2, aff, 0.0), axis=0, keepdims=True)
    den = w1 + w2
    w1 = w1 / den
    w2 = w2 / den
    lo_first = e1 < e2
    gate_a = jnp.where(lo_first, w1, w2)
    gate_b = jnp.where(lo_first, w2, w1)
    a = jnp.minimum(e1, e2) - g_sel * EXPERTS_PER_GROUP
    b = jnp.maximum(e1, e2) - g_sel * EXPERTS_PER_GROUP
    base = jnp.where(a == 0, 0, jnp.where(a == 1, 3, 5))
    cls_out[...] = g_sel * PAIRS_PER_GROUP + base + (b - a - 1)

    gates_t = jnp.concatenate([gate_a, gate_b, jnp.zeros((LANES - 2, ROWS), F32)], axis=0)
    aug_out[:, D_MODEL:AUG] = gates_t.T


def _outproj(layer, x4, oap4, obp, oas4, obs, mod, ln2, w_out_b, w_router_p, router_bias_c,
             tiles_per_batch):
    n_tiles = x4.shape[0]

    def mod_spec(j):
        return pl.BlockSpec((None, None, GRP, 1, D_MODEL),
                            lambda i: (layer, i // tiles_per_batch, 0, 0, j))

    x_spec = pl.BlockSpec((None, GRP, SUB, D_MODEL), lambda i: (i, 0, 0, 0))
    prompt_tiles = oap4.shape[0]
    last = prompt_tiles - 1
    return pl.pallas_call(
        functools.partial(_outproj_kernel, prompt_tiles),
        grid=(n_tiles,),
        in_specs=[
            x_spec,
            pl.BlockSpec((None, GQA_GROUP, ROWS, LANES), lambda i: (jnp.minimum(i, last), 0, 0, 0)),
            pl.BlockSpec((ROWS, V_B), lambda i: (jnp.minimum(i, last), 0)),
            pl.BlockSpec((None, GQA_GROUP, ROWS, LANES), lambda i: (0, 0, 0, 0)),
            pl.BlockSpec((ROWS, V_B), lambda i: (0, 0)),
            mod_spec(2), mod_spec(3), mod_spec(4),
            pl.BlockSpec((None, 1, D_MODEL), lambda i: (layer, 0, 0)),
            pl.BlockSpec((None, Q_A + V_B, D_MODEL), lambda i: (layer, 0, 0)),
            pl.BlockSpec((D_MODEL, LANES), lambda i: (0, 0)),
            pl.BlockSpec((N_EXPERTS, 1), lambda i: (0, 0)),
        ],
        out_specs=[
            x_spec,
            pl.BlockSpec((ROWS, AUG), lambda i: (i, 0)),
            pl.BlockSpec((None, 1, ROWS), lambda i: (i, 0, 0)),
        ],
        out_shape=[
            jax.ShapeDtypeStruct(x4.shape, F32),
            jax.ShapeDtypeStruct((n_tiles * ROWS, AUG), F32),
            jax.ShapeDtypeStruct((n_tiles, 1, ROWS), I32),
        ],
        compiler_params=_cparams(("arbitrary",)),
        name=f"outproj{layer}",
    )(x4, oap4, obp, oas4, obs, mod, mod, mod, ln2, w_out_b, w_router_p, router_bias_c)


def _rank_kernel(cls_ref, upper_ref, rank_ref, count_ref):
    n_tiles = cls_ref.shape[0]
    c_idx = lax.broadcasted_iota(I32, (GRP, ROWS), 0)
    upper = upper_ref[...]

    def body(i, carry):
        onehot = c_idx == cls_ref[i]
        within = _dot(onehot.astype(BF16), upper)
        rank = jnp.sum(jnp.where(onehot, within + carry, 0.0), axis=0, keepdims=True)
        rank_ref[i] = rank.astype(I32)
        return carry + jnp.sum(onehot.astype(F32), axis=1, keepdims=True)

    total = lax.fori_loop(0, n_tiles, body, jnp.zeros((GRP, 1), F32))
    count_ref[...] = jnp.broadcast_to(total, (GRP, LANES)).astype(I32)


def _plan_kernel(n_tok, p_max, cls_ref, rank_ref, count_ref, pos_ref, src_ref, ea_ref, eb_ref,
                 nt_ref, start_ref):
    t_max = p_max // TM

    def fill_src(p, c):
        src_ref[p] = 0
        return c

    lax.fori_loop(0, p_max, fill_src, 0)

    def fill_tiles(t, c):
        ea_ref[t] = 0
        eb_ref[t] = 0
        return c

    lax.fori_loop(0, t_max, fill_tiles, 0)

    tile = 0
    for c in range(N_CLASS):
        g, pi = divmod(c, PAIRS_PER_GROUP)
        a, b = [(0, 1), (0, 2), (0, 3), (1, 2), (1, 3), (2, 3)][pi]
        start_ref[c] = tile * TM
        n_c = (count_ref[c, 0] + (TM - 1)) // TM

        def mark(t, carry, g=g, a=a, b=b):
            ea_ref[t] = g * EXPERTS_PER_GROUP + a
            eb_ref[t] = g * EXPERTS_PER_GROUP + b
            return carry

        lax.fori_loop(tile, tile + n_c, mark, 0)
        tile = tile + n_c
    nt_ref[0] = tile

    def place(t, c):
        p = start_ref[cls_ref[t]] + rank_ref[t]
        pos_ref[t] = p
        src_ref[p] = t
        return c

    lax.fori_loop(0, n_tok, place, 0)


def _route_plan(cls3, upper):
    n_tiles = cls3.shape[0]
    n_tok = n_tiles * ROWS
    p_max = ((n_tok + N_CLASS * (TM - 1)) // TM + 1) * TM
    rank, count = pl.pallas_call(
        _rank_kernel,
        out_shape=[jax.ShapeDtypeStruct((n_tiles, 1, ROWS), I32),
                   jax.ShapeDtypeStruct((GRP, LANES), I32)],
        name="moe_rank",
    )(cls3, upper)
    smem = pl.BlockSpec(memory_space=pltpu.SMEM)
    pos, src, ea, eb, nt = pl.pallas_call(
        functools.partial(_plan_kernel, n_tok, p_max),
        in_specs=[smem, smem, smem],
        out_specs=[smem, smem, smem, smem, smem],
        out_shape=[jax.ShapeDtypeStruct((n_tok,), I32),
                   jax.ShapeDtypeStruct((p_max,), I32),
                   jax.ShapeDtypeStruct((p_max // TM,), I32),
                   jax.ShapeDtypeStruct((p_max // TM,), I32),
                   jax.ShapeDtypeStruct((1,), I32)],
        scratch_shapes=[pltpu.SMEM((N_CLASS,), I32)],
        name="moe_plan",
    )(cls3.reshape(n_tok), rank.reshape(n_tok), count)
    return pos, src, ea, eb, nt


def _moe_row_copy(aug_hbm, buf, sem, src_ref, tile, slot, r):
    tok = src_ref[tile * TM + r]
    return pltpu.make_async_copy(aug_hbm.at[pl.ds(tok, 1)], buf.at[slot, pl.ds(r, 1)], sem.at[slot])


def _moe_kernel(src_ref, ea_ref, eb_ref, nt_ref, aug_hbm, wga_ref, wua_ref, wda_ref, wgb_ref,
                wub_ref, wdb_ref, y_ref, buf, sem):
    i = pl.program_id(0)
    n_used = nt_ref[0]

    def start_gather(tile, slot):
        def body(r, c):
            _moe_row_copy(aug_hbm, buf, sem, src_ref, tile, slot, r).start()
            return c
        lax.fori_loop(0, TM, body, 0)

    def wait_gather(tile, slot):
        def body(r, c):
            _moe_row_copy(aug_hbm, buf, sem, src_ref, tile, slot, r).wait()
            return c
        lax.fori_loop(0, TM, body, 0)

    @pl.when(jnp.logical_and(i == 0, n_used > 0))
    def _():
        start_gather(0, 0)

    @pl.when(i + 1 < n_used)
    def _():
        start_gather(i + 1, (i + 1) % 2)

    @pl.when(i < n_used)
    def _():
        slot = i % 2
        wait_gather(i, slot)
        rows = buf[slot]
        xb = rows[:, 0:D_MODEL].astype(BF16)
        y = jnp.zeros((TM, D_MODEL), F32)
        for e, (wg, wu, wd) in enumerate(((wga_ref, wua_ref, wda_ref), (wgb_ref, wub_ref, wdb_ref))):
            gate = rows[:, D_MODEL + e:D_MODEL + e + 1]
            gt = _dot(xb, wg[...])
            up = _dot(xb, wu[...])
            act = gt * _sigmoid(gt) * up * gate
            y = y + _dot(act.astype(BF16), wd[...])
        y_ref[...] = y

    @pl.when(i >= n_used)
    def _():
        y_ref[...] = jnp.zeros_like(y_ref)


def _moe(layer, aug, src, ea, eb, nt, wg_b, wu_b, wd_b):
    n_t = ea.shape[0]

    def w_spec(shape, which):
        def index(i, src_ref, ea_ref, eb_ref, nt_ref):
            t = jnp.minimum(i, jnp.maximum(nt_ref[0] - 1, 0))
            e = (ea_ref, eb_ref)[which][t]
            return (layer, e, 0, 0)
        return pl.BlockSpec((None, None) + shape, index)

    gu = (D_MODEL, D_EXPERT)
    dn = (D_EXPERT, D_MODEL)
    grid_spec = pltpu.PrefetchScalarGridSpec(
        num_scalar_prefetch=4,
        grid=(n_t,),
        in_specs=[pl.BlockSpec(memory_space=pl.ANY),
                  w_spec(gu, 0), w_spec(gu, 0), w_spec(dn, 0),
                  w_spec(gu, 1), w_spec(gu, 1), w_spec(dn, 1)],
        out_specs=pl.BlockSpec((TM, D_MODEL), lambda i, *_: (i, 0)),
        scratch_shapes=[pltpu.VMEM((2, TM, AUG), F32), pltpu.SemaphoreType.DMA((2,))],
    )
    return pl.pallas_call(
        _moe_kernel,
        grid_spec=grid_spec,
        out_shape=jax.ShapeDtypeStruct((n_t * TM, D_MODEL), F32),
        compiler_params=_cparams(("arbitrary",)),
        name=f"moe{layer}",
    )(src, ea, eb, nt, aug, wg_b, wu_b, wd_b, wg_b, wu_b, wd_b)


def _combine_row_copy(y_hbm, buf, sem, pos_ref, tile, slot, r):
    p = pos_ref[tile * ROWS + r]
    return pltpu.make_async_copy(y_hbm.at[pl.ds(p, 1)], buf.at[slot, pl.ds(r, 1)], sem.at[slot])


def _combine_kernel(pos_ref, x1_ref, g2_ref, y_hbm, o_ref, buf, sem):
    i = pl.program_id(0)
    n = pl.num_programs(0)

    def start_gather(tile, slot):
        def body(r, c):
            _combine_row_copy(y_hbm, buf, sem, pos_ref, tile, slot, r).start()
            return c
        lax.fori_loop(0, ROWS, body, 0)

    def wait_gather(tile, slot):
        def body(r, c):
            _combine_row_copy(y_hbm, buf, sem, pos_ref, tile, slot, r).wait()
            return c
        lax.fori_loop(0, ROWS, body, 0)

    @pl.when(i == 0)
    def _():
        start_gather(0, 0)

    @pl.when(i + 1 < n)
    def _():
        start_gather(i + 1, (i + 1) % 2)

    slot = i % 2
    wait_gather(i, slot)
    y = buf[slot].reshape(GRP, SUB, D_MODEL)
    o_ref[...] = x1_ref[...] + g2_ref[...] * y


def _combine(layer, x1, mod, y_sorted, pos, tiles_per_batch):
    n_tiles = x1.shape[0]
    x_spec = pl.BlockSpec((None, GRP, SUB, D_MODEL), lambda i, pos_ref: (i, 0, 0, 0))
    grid_spec = pltpu.PrefetchScalarGridSpec(
        num_scalar_prefetch=1,
        grid=(n_tiles,),
        in_specs=[x_spec,
                  pl.BlockSpec((None, None, GRP, 1, D_MODEL),
                               lambda i, pos_ref: (layer, i // tiles_per_batch, 0, 0, 5)),
                  pl.BlockSpec(memory_space=pl.ANY)],
        out_specs=x_spec,
        scratch_shapes=[pltpu.VMEM((2, ROWS, D_MODEL), F32), pltpu.SemaphoreType.DMA((2,))],
    )
    return pl.pallas_call(
        _combine_kernel,
        grid_spec=grid_spec,
        out_shape=jax.ShapeDtypeStruct(x1.shape, F32),
        compiler_params=_cparams(("arbitrary",)),
        name=f"combine{layer}",
    )(pos, x1, mod, y_sorted)


def _pair_state_in(state):
    b = state.shape[0]
    st = jnp.swapaxes(state, -1, -2).reshape(b, N_PAIR, 2, DV_B, DK_B)
    z = jnp.zeros_like(st[:, :, 0])
    top = jnp.concatenate([st[:, :, 0], z], axis=-1)
    bot = jnp.concatenate([z, st[:, :, 1]], axis=-1)
    return jnp.concatenate([top, bot], axis=-2)


def _pair_state_out(st):
    b = st.shape[0]
    h0 = st[:, :, :DV_B, :DK_B]
    h1 = st[:, :, DV_B:, DK_B:]
    out = jnp.stack([h0, h1], axis=2).reshape(b, N_HEADS_B, DV_B, DK_B)
    return jnp.swapaxes(out, -1, -2)


def kernel(x_prompt, x_sample, cache_k, cache_v, state_hgrn, c_prompt, c_sample, w_ada, b_ada,
           ln1, ln2, w_in, q_norm, k_norm, sinks, lb_param, o_norm, w_out, w_router, router_bias,
           w_e_gate, w_e_up, w_e_down):
    batch, seq, _ = x_prompt.shape
    dec_batch, dec_seq, _ = x_sample.shape
    depth = w_ada.shape[0]
    assert dec_batch == GRP and dec_seq == SUB and seq % ROWS == 0
    tiles_per_batch = seq // ROWS
    n_prompt = batch * seq
    n_tok = n_prompt + dec_batch * dec_seq
    n_tiles = n_tok // ROWS
    prompt_tiles = n_prompt // ROWS
    n_cache = cache_k.shape[2]

    seg = np.arange(Q_A) // HEAD_DIM
    bd = jnp.asarray((seg[:, None] == seg[None, :]).astype(np.float32) / HEAD_DIM, BF16)
    bd2 = bd[:LANES, :LANES]
    sel_p, mask_p = _hgrn_consts(CHUNK)
    sel_s, mask_s = _hgrn_consts(dec_seq)
    sel_p, sel_s = jnp.asarray(sel_p, BF16), jnp.asarray(sel_s, BF16)
    mask_p, mask_s = jnp.asarray(mask_p), jnp.asarray(mask_s)
    upper = jnp.asarray(np.triu(np.ones((ROWS, ROWS), np.float32), 1), BF16)

    wq = w_in[:, :, :Q_A].reshape(depth, D_MODEL, N_KV_A, GQA_GROUP, HEAD_DIM)
    wq = jnp.swapaxes(wq, 2, 3).reshape(depth, D_MODEL, Q_A)
    w_in_b = jnp.concatenate([wq, w_in[:, :, Q_A:]], axis=-1).astype(BF16)
    wo = w_out[:, :Q_A].reshape(depth, N_KV_A, GQA_GROUP, HEAD_DIM, D_MODEL)
    wo = jnp.swapaxes(wo, 1, 2).reshape(depth, Q_A, D_MODEL)
    w_out_b = jnp.concatenate([wo, w_out[:, Q_A:]], axis=1).astype(BF16)
    wg_b, wu_b, wd_b = w_e_gate.astype(BF16), w_e_up.astype(BF16), w_e_down.astype(BF16)
    q_norm_t = jnp.tile(q_norm, (1, N_HEADS_A)).reshape(depth, 1, Q_A)
    k_norm_t = jnp.tile(k_norm, (1, N_KV_A)).reshape(depth, 1, KV_A)
    o_norm_t = jnp.tile(o_norm, (1, N_HEADS_B)).reshape(depth, 1, V_B)
    ln1_r = ln1.reshape(depth, 1, D_MODEL)
    ln2_r = ln2.reshape(depth, 1, D_MODEL)
    w_router_p = jnp.pad(w_router, ((0, 0), (0, LANES - N_EXPERTS)))
    router_bias_c = router_bias.reshape(N_EXPERTS, 1)

    c_exp = jnp.concatenate([jnp.repeat(c_prompt, GRP, axis=0), c_sample], axis=0)
    mod = _adaln(c_exp, w_ada, b_ada).reshape(depth, batch + 1, GRP, 1, N_MOD * D_MODEL)

    x = jnp.concatenate([x_prompt.reshape(n_prompt, D_MODEL), x_sample.reshape(-1, D_MODEL)], axis=0)
    x = x.reshape(n_tiles, GRP, SUB, D_MODEL)
    cache_k2 = cache_k.reshape(depth, dec_batch, n_cache, KV_A)
    cache_v2 = cache_v.reshape(depth, dec_batch, n_cache, KV_A)

    kp, vp, sp, kss, vss, sss = [], [], [], [], [], []
    for l in range(depth):
        q4, k, v, qb, logf, kb, ib, gb = _inproj(
            l, x, mod, ln1_r, w_in_b, q_norm_t, k_norm_t, lb_param, bd, tiles_per_batch)

        oap4 = _attn_prompt(q4, k, v, sinks[l], batch, seq)
        oas4 = _attn_sample(q4, k, v, cache_k2[l], cache_v2[l], sinks[l], prompt_tiles,
                            n_prompt, dec_batch, dec_seq)
        obp, s_p = _hgrn_prompt(qb, kb, ib, gb, logf, sel_p, mask_p, bd2, o_norm_t[l], batch, seq)
        obs, s_s = _hgrn_sample(qb, kb, ib, gb, logf, _pair_state_in(state_hgrn[l]), sel_s,
                                mask_s, bd2, o_norm_t[l], n_prompt, dec_batch, dec_seq)

        x1, aug, cls3 = _outproj(l, x, oap4, obp, oas4, obs, mod, ln2_r, w_out_b, w_router_p,
                                 router_bias_c, tiles_per_batch)
        pos, src, ea, eb, nt = _route_plan(cls3, upper)
        y_sorted = _moe(l, aug, src, ea, eb, nt, wg_b, wu_b, wd_b)
        x = _combine(l, x1, mod, y_sorted, pos, tiles_per_batch)

        kp.append(k[:n_prompt].reshape(batch, seq, N_KV_A, HEAD_DIM)[:, -WINDOW:])
        vp.append(v[:n_prompt].reshape(batch, seq, N_KV_A, HEAD_DIM)[:, -WINDOW:])
        sp.append(_pair_state_out(s_p))
        kss.append(k[n_prompt:].reshape(dec_batch, dec_seq, N_KV_A, HEAD_DIM))
        vss.append(v[n_prompt:].reshape(dec_batch, dec_seq, N_KV_A, HEAD_DIM))
        sss.append(_pair_state_out(s_s))

    xf = x.reshape(n_tok, D_MODEL)
    return (xf[:n_prompt].reshape(batch, seq, D_MODEL),
            xf[n_prompt:].reshape(dec_batch, dec_seq, D_MODEL),
            jnp.stack(kp), jnp.stack(vp), jnp.stack(sp),
            jnp.stack(kss), jnp.stack(vss), jnp.stack(sss))
```

```python
import functools

import numpy as np
import jax
import jax.numpy as jnp
from jax import lax
from jax.experimental import pallas as pl
from jax.experimental.pallas import tpu as pltpu

F32 = jnp.float32
BF16 = jnp.bfloat16
I32 = jnp.int32
U32 = jnp.uint32

D_MODEL = 1024
HEAD_DIM = 64
N_HEADS_A = 8
N_KV_A = 2
GQA_GROUP = N_HEADS_A // N_KV_A
N_HEADS_B = 8
DK_B = 64
DV_B = 64
Q_A = N_HEADS_A * HEAD_DIM
KV_A = N_KV_A * HEAD_DIM
W_B = N_HEADS_B * DK_B
V_B = N_HEADS_B * DV_B
CHUNK = 64
WINDOW = 128
N_EXPERTS = 16
N_GROUPS = 4
EXPERTS_PER_GROUP = 4
D_EXPERT = D_MODEL // 2
RMS_EPS = 1e-6
NEG_INF = -1e30
N_MOD = 6

LANES = 128
SUBLANES = 8
ROWS = 512
GRP = 32
SUB = ROWS // GRP
ATT_ROWS = 256
N_PAIR = N_HEADS_B // 2
SAMPLE_SEQS = 4
HGRN_SPAN_LIMIT = 80.0
TM = 256
PAIRS_PER_GROUP = 6
PAIR_TABLE = ((0, 1), (0, 2), (0, 3), (1, 2), (1, 3), (2, 3))
N_CLASS = N_GROUPS * PAIRS_PER_GROUP
HALF_WORDS = D_MODEL // 2 // LANES
GATE_ROW = HALF_WORDS
VMEM_LIMIT = 56 * 1024 * 1024


def _cparams(sem):
    return pltpu.CompilerParams(dimension_semantics=sem, vmem_limit_bytes=VMEM_LIMIT)


def _dot(a, b):
    return jnp.dot(a, b, preferred_element_type=F32)


def _dot_nt(a, b):
    return lax.dot_general(a, b, (((1,), (1,)), ((), ())), preferred_element_type=F32)


def _dot_tn(a, b):
    return lax.dot_general(a, b, (((0,), (0,)), ((), ())), preferred_element_type=F32)


def _sigmoid(x):
    return 1.0 / (1.0 + jnp.exp(-x))


def _split2(x):
    hi = x.astype(BF16)
    lo = (x - hi.astype(F32)).astype(BF16)
    return hi, lo


def _split3(x):
    hi = x.astype(BF16)
    r = x - hi.astype(F32)
    mid = r.astype(BF16)
    lo = (r - mid.astype(F32)).astype(BF16)
    return hi, mid, lo


def _seg_mean(sq, bd):
    hi, lo = _split2(sq)
    return _dot(hi, bd) + _dot(lo, bd)


def _token_rows(ref, s, n, base=0):
    return ref[pl.ds(base + s, n, stride=SUBLANES), :]


def _adaln_kernel(c_ref, w_ref, b_ref, o_ref):
    c = c_ref[...]
    a = c * _sigmoid(c)
    a_hi, a_lo = _split2(a)
    w_hi, w_lo = _split2(w_ref[...])
    acc = _dot(a_hi, w_hi) + _dot(a_lo, w_hi) + _dot(a_hi, w_lo)
    o_ref[...] = acc + b_ref[...]


def _adaln(c_exp, w_ada, b_ada):
    depth = w_ada.shape[0]
    rows = c_exp.shape[0]
    tn = D_MODEL
    return pl.pallas_call(
        _adaln_kernel,
        grid=(depth, N_MOD * D_MODEL // tn),
        in_specs=[
            pl.BlockSpec((rows, D_MODEL), lambda l, n: (0, 0)),
            pl.BlockSpec((None, D_MODEL, tn), lambda l, n: (l, 0, n)),
            pl.BlockSpec((None, 1, tn), lambda l, n: (l, 0, n)),
        ],
        out_specs=pl.BlockSpec((None, rows, tn), lambda l, n: (l, 0, n)),
        out_shape=jax.ShapeDtypeStruct((depth, rows, N_MOD * D_MODEL), F32),
        compiler_params=_cparams(("arbitrary", "arbitrary")),
        name="adaln",
    )(c_exp, w_ada, b_ada.reshape(depth, 1, N_MOD * D_MODEL))


def _mod_spec(layer, j, tiles_per_batch):
    return pl.BlockSpec((None, None, GRP, 1, D_MODEL),
                        lambda i, *_: (layer, i // tiles_per_batch, 0, 0, j))


def _moe_output_tile(y_ref):
    y = jnp.concatenate([_token_rows(y_ref, s, ROWS) for s in range(SUBLANES)], axis=1)
    return y.reshape(GRP, SUB, D_MODEL)


def _inproj_kernel(layer, prompt_tiles, n_src, *refs):
    src = refs[:n_src]
    sh_ref, sc_ref, ln_ref, w_ref, qn_ref, kn_ref, lbp_ref, bd_ref = refs[n_src:n_src + 8]
    x_out, q_out, k_out, v_out, qb_out, logf_out, kb_out, ib_out, gb_out = refs[n_src + 8:]
    if layer == 0:
        xp_ref, xs_ref = src
        x = jnp.where(pl.program_id(0) < prompt_tiles, xp_ref[...], xs_ref[...])
    else:
        x1_ref, y_ref, g2_ref = src
        x = x1_ref[...] + g2_ref[...] * _moe_output_tile(y_ref)
    x_out[...] = x
    ms = jnp.mean(x * x, axis=-1, keepdims=True)
    h = x * lax.rsqrt(ms + RMS_EPS) * ln_ref[...]
    h = h * (1.0 + sc_ref[...]) + sh_ref[...]
    hb = h.reshape(ROWS, D_MODEL).astype(BF16)
    bd = bd_ref[...]

    qa = _dot(hb, w_ref[:, 0:Q_A])
    qa = qa * lax.rsqrt(_seg_mean(qa * qa, bd) + RMS_EPS) * qn_ref[...]
    qa = (qa * (HEAD_DIM ** -0.5)).astype(BF16)
    for hh in range(GQA_GROUP):
        q_out[hh] = qa[:, hh * LANES:(hh + 1) * LANES]

    ka = _dot(hb, w_ref[:, Q_A:Q_A + KV_A])
    ka = ka * lax.rsqrt(_seg_mean(ka * ka, bd[:KV_A, :KV_A]) + RMS_EPS) * kn_ref[...]
    k_out[...] = ka
    v_out[...] = _dot(hb, w_ref[:, Q_A + KV_A:Q_A + 2 * KV_A])

    c0 = Q_A + 2 * KV_A
    qb_out[...] = (_dot(hb, w_ref[:, c0:c0 + W_B]) * (DK_B ** -0.5)).astype(BF16)

    p = lbp_ref[...]
    e = jnp.exp(p - jnp.max(p, axis=0, keepdims=True))
    sm = e / jnp.sum(e, axis=0, keepdims=True)
    cum0 = sm[0:1]
    cum = cum0
    for r in range(1, layer + 1):
        cum = cum + sm[r:r + 1]
    lb = cum - cum0

    z = _dot(hb, w_ref[:, c0 + W_B:c0 + 2 * W_B])
    log_sig = jnp.minimum(z, 0.0) - jnp.log1p(jnp.exp(-jnp.abs(z)))
    a = jnp.log(lb)
    c = jnp.log1p(-lb) + log_sig
    logf_out[...] = jnp.maximum(a, c) + jnp.log1p(jnp.exp(-jnp.abs(a - c)))
    kb_out[...] = ((1.0 - lb) * _sigmoid(-z)).astype(BF16)

    c1 = c0 + 2 * W_B
    ib_out[...] = _dot(hb, w_ref[:, c1:c1 + V_B]).astype(BF16)
    gb_out[...] = _dot(hb, w_ref[:, c1 + V_B:c1 + 2 * V_B]).astype(BF16)


def _inproj(layer, src, n_tiles, prompt_tiles, mod, ln1, w_in_b, q_norm_t, k_norm_t, lb_param, bd,
            tiles_per_batch):
    depth = lb_param.shape[0]
    n = n_tiles * ROWS
    in_width = w_in_b.shape[-1]
    x_spec = pl.BlockSpec((None, GRP, SUB, D_MODEL), lambda i: (i, 0, 0, 0))
    if layer == 0:
        last = prompt_tiles - 1
        src_specs = [
            pl.BlockSpec((None, GRP, SUB, D_MODEL), lambda i: (jnp.minimum(i, last), 0, 0, 0)),
            pl.BlockSpec((None, GRP, SUB, D_MODEL), lambda i: (0, 0, 0, 0)),
        ]
        src_args = list(src)
    else:
        src_specs = [x_spec, pl.BlockSpec((ROWS * SUBLANES, LANES), lambda i: (i, 0)),
                     _mod_spec(layer - 1, 5, tiles_per_batch)]
        src_args = list(src) + [mod]
    row_spec = lambda w: pl.BlockSpec((ROWS, w), lambda i: (i, 0))
    return pl.pallas_call(
        functools.partial(_inproj_kernel, layer, prompt_tiles, len(src_specs)),
        grid=(n_tiles,),
        in_specs=src_specs + [
            _mod_spec(layer, 0, tiles_per_batch), _mod_spec(layer, 1, tiles_per_batch),
            pl.BlockSpec((None, 1, D_MODEL), lambda i: (layer, 0, 0)),
            pl.BlockSpec((None, D_MODEL, in_width), lambda i: (layer, 0, 0)),
            pl.BlockSpec((None, 1, Q_A), lambda i: (layer, 0, 0)),
            pl.BlockSpec((None, 1, KV_A), lambda i: (layer, 0, 0)),
            pl.BlockSpec((depth, W_B), lambda i: (0, 0)),
            pl.BlockSpec((Q_A, Q_A), lambda i: (0, 0)),
        ],
        out_specs=[
            x_spec,
            pl.BlockSpec((None, GQA_GROUP, ROWS, LANES), lambda i: (i, 0, 0, 0)),
            row_spec(KV_A), row_spec(KV_A),
            row_spec(W_B), row_spec(W_B), row_spec(W_B), row_spec(V_B), row_spec(V_B),
        ],
        out_shape=[
            jax.ShapeDtypeStruct((n_tiles, GRP, SUB, D_MODEL), F32),
            jax.ShapeDtypeStruct((n_tiles, GQA_GROUP, ROWS, LANES), BF16),
            jax.ShapeDtypeStruct((n, KV_A), F32),
            jax.ShapeDtypeStruct((n, KV_A), F32),
            jax.ShapeDtypeStruct((n, W_B), BF16),
            jax.ShapeDtypeStruct((n, W_B), F32),
            jax.ShapeDtypeStruct((n, W_B), BF16),
            jax.ShapeDtypeStruct((n, V_B), BF16),
            jax.ShapeDtypeStruct((n, V_B), BF16),
        ],
        compiler_params=_cparams(("arbitrary",)),
        name=f"inproj{layer}",
    )(*src_args, mod, mod, ln1, w_in_b, q_norm_t, k_norm_t, lb_param, bd)


def _final_kernel(prompt_tiles, x1_ref, y_ref, g2_ref, yp_ref, ys_ref):
    i = pl.program_id(0)
    x = x1_ref[...] + g2_ref[...] * _moe_output_tile(y_ref)

    @pl.when(i < prompt_tiles)
    def _():
        yp_ref[...] = x

    @pl.when(i >= prompt_tiles)
    def _():
        ys_ref[...] = x


def _final(layer, x1, y, mod, prompt_tiles, tiles_per_batch):
    n_tiles = x1.shape[0]
    last = prompt_tiles - 1
    tile = (None, GRP, SUB, D_MODEL)
    return pl.pallas_call(
        functools.partial(_final_kernel, prompt_tiles),
        grid=(n_tiles,),
        in_specs=[pl.BlockSpec(tile, lambda i: (i, 0, 0, 0)),
                  pl.BlockSpec((ROWS * SUBLANES, LANES), lambda i: (i, 0)),
                  _mod_spec(layer, 5, tiles_per_batch)],
        out_specs=[pl.BlockSpec(tile, lambda i: (jnp.minimum(i, last), 0, 0, 0)),
                   pl.BlockSpec(tile, lambda i: (0, 0, 0, 0))],
        out_shape=[jax.ShapeDtypeStruct((prompt_tiles, GRP, SUB, D_MODEL), F32),
                   jax.ShapeDtypeStruct((n_tiles - prompt_tiles, GRP, SUB, D_MODEL), F32)],
        compiler_params=_cparams(("arbitrary",)),
        name="final",
    )(x1, y, mod)


def _attend(q, k, v, sinks_ref, q_len, k_off, k_pos0=None):
    n_q = q.shape[0]
    n_k = k.shape[0]
    row = lax.broadcasted_iota(I32, (n_q, n_k), 0)
    col = lax.broadcasted_iota(I32, (n_q, n_k), 1)
    dist = jnp.abs((row % q_len) + k_off - col).astype(F32)
    lane = lax.broadcasted_iota(I32, (n_k, LANES), 1)
    hh_col = lax.broadcasted_iota(I32, (n_q, 1), 0) // q_len
    out = jnp.zeros((n_q, LANES), F32)
    for g in range(N_KV_A):
        in_g = (lane // HEAD_DIM) == g
        kg = jnp.where(in_g, k, 0.0).astype(BF16)
        vg = jnp.where(in_g, v, 0.0).astype(BF16)
        s = _dot_nt(q, kg)
        head = (g * GQA_GROUP + 1 + hh_col).astype(F32)
        s = s - jnp.exp2(-8.0 * head / N_HEADS_A) * dist
        if k_pos0 is not None:
            s = jnp.where(col + k_pos0 >= 0, s, NEG_INF)
        sink = jnp.zeros((n_q, 1), F32)
        for j in range(GQA_GROUP):
            sink = jnp.where(hh_col == j, sinks_ref[g * GQA_GROUP + j], sink)
        m = jnp.maximum(jnp.max(s, axis=-1, keepdims=True), sink)
        p = jnp.exp(s - m)
        denom = jnp.sum(p, axis=-1, keepdims=True) + jnp.exp(sink - m)
        out = out + _dot(p.astype(BF16), vg) / denom
    return out


def _attn_prompt_kernel(sinks_ref, q_ref, kp_ref, kc_ref, vp_ref, vc_ref, o_ref):
    j = pl.program_id(1)
    kcat = jnp.concatenate([kp_ref[...], kc_ref[...]], axis=0)
    vcat = jnp.concatenate([vp_ref[...], vc_ref[...]], axis=0)
    for c in range(ATT_ROWS // CHUNK):
        q = q_ref[:, c * CHUNK:(c + 1) * CHUNK, :].reshape(GQA_GROUP * CHUNK, LANES)
        k = kcat[c * CHUNK:c * CHUNK + WINDOW + CHUNK]
        v = vcat[c * CHUNK:c * CHUNK + WINDOW + CHUNK]
        k_pos0 = j * ATT_ROWS + c * CHUNK - WINDOW
        out = _attend(q, k, v, sinks_ref, CHUNK, WINDOW, k_pos0)
        o_ref[:, c * CHUNK:(c + 1) * CHUNK, :] = out.reshape(GQA_GROUP, CHUNK, LANES).astype(BF16)


def _attn_prompt(q4, k, v, sinks, batch, seq):
    steps = seq // ATT_ROWS
    per_tile = ROWS // ATT_ROWS
    win_blocks = ATT_ROWS // WINDOW
    n_tiles = batch * seq // ROWS
    cur = pl.BlockSpec((ATT_ROWS, KV_A), lambda b, j: (b * steps + j, 0))
    prev = pl.BlockSpec(
        (WINDOW, KV_A), lambda b, j: (b * steps * win_blocks + jnp.maximum(j * win_blocks - 1, 0), 0))
    qo = pl.BlockSpec((None, GQA_GROUP, ATT_ROWS, LANES),
                      lambda b, j: ((b * steps + j) // per_tile, 0, j % per_tile, 0))
    return pl.pallas_call(
        _attn_prompt_kernel,
        grid=(batch, steps),
        in_specs=[pl.BlockSpec(memory_space=pltpu.SMEM), qo, prev, cur, prev, cur],
        out_specs=qo,
        out_shape=jax.ShapeDtypeStruct((n_tiles, GQA_GROUP, ROWS, LANES), BF16),
        compiler_params=_cparams(("arbitrary", "arbitrary")),
        name="attn_prompt",
    )(sinks, q4, k, k, v, v)


def _attn_sample_kernel(sinks_ref, q_ref, kc_ref, kn_ref, vc_ref, vn_ref, o_ref):
    n_new = kn_ref.shape[0]
    n_cache = kc_ref.shape[0]
    q = q_ref[...].reshape(GQA_GROUP * n_new, LANES)
    k = jnp.concatenate([kc_ref[...], kn_ref[...]], axis=0)
    v = jnp.concatenate([vc_ref[...], vn_ref[...]], axis=0)
    out = _attend(q, k, v, sinks_ref, n_new, n_cache)
    o_ref[...] = out.reshape(GQA_GROUP, n_new, LANES).astype(BF16)


def _attn_sample(q4, k, v, cache_k, cache_v, sinks, tile, row0, dec_batch, dec_seq):
    n_cache = cache_k.shape[1]
    q_spec = pl.BlockSpec((None, GQA_GROUP, dec_seq, LANES), lambda b: (tile, 0, b, 0))
    new = pl.BlockSpec((dec_seq, KV_A), lambda b: (row0 // dec_seq + b, 0))
    cache = pl.BlockSpec((None, n_cache, KV_A), lambda b: (b, 0, 0))
    return pl.pallas_call(
        _attn_sample_kernel,
        grid=(dec_batch,),
        in_specs=[pl.BlockSpec(memory_space=pltpu.SMEM), q_spec, cache, new, cache, new],
        out_specs=pl.BlockSpec((None, GQA_GROUP, dec_seq, LANES), lambda b: (0, 0, b, 0)),
        out_shape=jax.ShapeDtypeStruct((1, GQA_GROUP, dec_batch * dec_seq, LANES), BF16),
        compiler_params=_cparams(("arbitrary",)),
        name="attn_sample",
    )(sinks, q4, cache_k, k, cache_v, v)


def _level_sizes(length):
    sizes = []
    m = length // 2
    while m >= 1:
        sizes.append(m)
        m //= 2
    return sizes


def _hgrn_consts(length):
    t = np.arange(length)
    tri = (t[None, :] <= t[:, None]).astype(np.float32)
    after = (t[None, :] > t[:, None]).astype(np.float32)
    levels = []
    masks = []
    s = np.tile(t, 2)[None, :]
    tt = t[:, None]
    for m in _level_sizes(length):
        blk = t // m
        same = blk[None, :] == blk[:, None]
        q_rows = (blk % 2 == 1)[:, None]
        sel = np.where(q_rows, same & (t[None, :] <= t[:, None]), same & (t[None, :] > t[:, None]))
        levels.append(sel.astype(np.float32))
        masks.append(((tt // m) % 2 == 1) & ((s // m) == (tt // m) - 1))
    masks.append(s == tt)
    masks.append(s <= tt)
    base = np.concatenate([tri, after], axis=0)
    lev = np.concatenate(levels, axis=0)
    return (np.concatenate([base] * 3, axis=1), np.concatenate([lev] * 3, axis=1),
            np.stack(masks).astype(np.int32))


def _stack2(x, first):
    return jnp.concatenate([jnp.where(first, x, 0.0), jnp.where(first, 0.0, x)], axis=0).astype(BF16)


def _hgrn_chunk(q, k, i, g, base, seq, mask_ref, bd2, onorm, st_ref, arg_ref):
    length = q.shape[0]
    n_lev = len(_level_sizes(length))
    lane = lax.broadcasted_iota(I32, (length, LANES), 1)
    first = lane < DK_B
    rowp = lax.broadcasted_iota(I32, (LANES, LANES), 0) // DV_B
    colp = lax.broadcasted_iota(I32, (LANES, LANES), 1) // DK_B
    same_head = rowp == colp
    mid = length // 2 - 1
    outs = []
    for p in range(N_PAIR):
        sl = slice(p * LANES, (p + 1) * LANES)
        qp = q[:, sl].astype(F32)
        kp = k[:, sl].astype(F32)
        ip = i[:, sl].astype(F32)
        b = base[0:length, sl]
        if arg_ref is None:
            rel = b - b[mid:mid + 1]
            a = _dot_nt((qp * jnp.exp(rel)).astype(BF16), _stack2(kp * jnp.exp(-rel), first))
            a = jnp.where(mask_ref[n_lev + 1] != 0, a, 0.0)
        else:
            a = _dot_nt(qp.astype(BF16), _stack2(kp, first))
            a = jnp.where(mask_ref[n_lev] != 0, a, 0.0)
            for lev in range(n_lev):
                e = jnp.exp(arg_ref[lev * length:(lev + 1) * length, sl])
                pr = _dot_nt((qp * e).astype(BF16), _stack2(kp * e, first))
                a = jnp.where(mask_ref[lev] != 0, pr, a)
        e_b = jnp.exp(b)
        e_k = jnp.exp(base[length:2 * length, sl])
        st = st_ref[seq, p]
        o = (_dot(a.astype(BF16), _stack2(ip, first))
             + _dot_nt((qp * e_b).astype(BF16), st.astype(BF16)))
        upd = _dot_tn(ip.astype(BF16), (kp * e_k).astype(BF16))
        st_ref[seq, p] = st * e_b[length - 1:length, :] + jnp.where(same_head, upd, 0.0)
        ms = _seg_mean(o * o, bd2)
        gp = g[:, sl].astype(F32)
        o = o * lax.rsqrt(ms + RMS_EPS) * onorm[:, sl] * (gp * _sigmoid(gp))
        outs.append(o.astype(BF16))
    return jnp.concatenate(outs, axis=1)


def _hgrn_kernel(n_seq, n_chunk, length, carry_state, *refs):
    if carry_state:
        (q_ref, k_ref, i_ref, g_ref, f_ref, selb_ref, sell_ref, mask_ref, bd2_ref, on_ref,
         o_ref, s_ref, st_ref, base_ref, arg_ref) = refs

        @pl.when(pl.program_id(1) == 0)
        def _():
            st_ref[...] = jnp.zeros_like(st_ref)
    else:
        (q_ref, k_ref, i_ref, g_ref, f_ref, s0_ref, selb_ref, sell_ref, mask_ref, bd2_ref, on_ref,
         o_ref, s_ref, st_ref, base_ref, arg_ref) = refs
        st_ref[...] = s0_ref[...]

    bd2 = bd2_ref[...]
    onorm = on_ref[...]
    n_items = n_seq * n_chunk
    mid = length // 2 - 1

    def pieces(rows):
        return jnp.concatenate(_split3(f_ref[rows, :]), axis=0)

    span = jnp.float32(0.0)
    for it in range(n_items):
        base = _dot(selb_ref[...], pieces(slice(it * length, (it + 1) * length)))
        base_ref[it * 2 * length:(it + 1) * 2 * length, :] = base
        b_mid = base[mid:mid + 1]
        span = jnp.maximum(span, jnp.max(base[0:1] - b_mid))
        span = jnp.maximum(span, jnp.max(b_mid - base[length - 1:length]))
    safe = span <= HGRN_SPAN_LIMIT

    @pl.when(safe)
    def _():
        for it in range(n_items):
            r = slice(it * length, (it + 1) * length)
            o_ref[r, :] = _hgrn_chunk(q_ref[r, :], k_ref[r, :], i_ref[r, :], g_ref[r, :],
                                      base_ref[it * 2 * length:(it + 1) * 2 * length, :],
                                      it // n_chunk, mask_ref, bd2, onorm, st_ref, None)

    @pl.when(jnp.logical_not(safe))
    def _():
        def body(it, carry):
            r = pl.ds(pl.multiple_of(it * length, length), length)
            arg_ref[...] = _dot(sell_ref[...], pieces(r))
            base = base_ref[pl.ds(pl.multiple_of(it * 2 * length, 2 * length), 2 * length), :]
            seq = it // n_chunk if n_chunk > 1 and n_seq > 1 else (it if n_chunk == 1 else 0)
            o_ref[r, :] = _hgrn_chunk(q_ref[r, :], k_ref[r, :], i_ref[r, :], g_ref[r, :], base,
                                      seq, mask_ref, bd2, onorm, st_ref, arg_ref)
            return carry

        lax.fori_loop(0, n_items, body, 0)

    if carry_state:
        @pl.when(pl.program_id(1) == pl.num_programs(1) - 1)
        def _():
            s_ref[...] = st_ref[...]
    else:
        s_ref[...] = st_ref[...]


def _hgrn_scratch(n_seq, n_items, length, sell):
    return [pltpu.VMEM((n_seq, N_PAIR, LANES, LANES), F32),
            pltpu.VMEM((n_items * 2 * length, W_B), F32),
            pltpu.VMEM((sell.shape[0], W_B), F32)]


def _hgrn_prompt(qb, kb, ib, gb, logf, consts, bd2, onorm_t, batch, seq):
    selb, sell, masks = consts
    steps = seq // ATT_ROWS
    n_chunk = ATT_ROWS // CHUNK
    row = pl.BlockSpec((ATT_ROWS, W_B), lambda b, j: (b * steps + j, 0))
    whole = lambda a: pl.BlockSpec(a.shape, lambda b, j: (0,) * a.ndim)
    return pl.pallas_call(
        functools.partial(_hgrn_kernel, 1, n_chunk, CHUNK, True),
        grid=(batch, steps),
        in_specs=[row, row, row, row, row, whole(selb), whole(sell), whole(masks), whole(bd2),
                  whole(onorm_t)],
        out_specs=[row, pl.BlockSpec((None, 1, N_PAIR, LANES, LANES), lambda b, j: (b, 0, 0, 0, 0))],
        out_shape=[jax.ShapeDtypeStruct((batch * seq, V_B), BF16),
                   jax.ShapeDtypeStruct((batch, 1, N_PAIR, LANES, LANES), F32)],
        scratch_shapes=_hgrn_scratch(1, n_chunk, CHUNK, sell),
        compiler_params=_cparams(("arbitrary", "arbitrary")),
        name="hgrn_prompt",
    )(qb, kb, ib, gb, logf, selb, sell, masks, bd2, onorm_t)


def _hgrn_sample(qb, kb, ib, gb, logf, s0, consts, bd2, onorm_t, row0, dec_batch, dec_seq):
    selb, sell, masks = consts
    rows = SAMPLE_SEQS * dec_seq
    row = pl.BlockSpec((rows, W_B), lambda b: (row0 // rows + b, 0))
    whole = lambda a: pl.BlockSpec(a.shape, lambda b: (0,) * a.ndim)
    state = pl.BlockSpec((SAMPLE_SEQS, N_PAIR, LANES, LANES), lambda b: (b, 0, 0, 0))
    return pl.pallas_call(
        functools.partial(_hgrn_kernel, SAMPLE_SEQS, 1, dec_seq, False),
        grid=(dec_batch // SAMPLE_SEQS,),
        in_specs=[row, row, row, row, row, state, whole(selb), whole(sell), whole(masks),
                  whole(bd2), whole(onorm_t)],
        out_specs=[pl.BlockSpec((rows, V_B), lambda b: (b, 0)), state],
        out_shape=[jax.ShapeDtypeStruct((dec_batch * dec_seq, V_B), BF16),
                   jax.ShapeDtypeStruct((dec_batch, N_PAIR, LANES, LANES), F32)],
        scratch_shapes=_hgrn_scratch(SAMPLE_SEQS, SAMPLE_SEQS, dec_seq, sell),
        compiler_params=_cparams(("arbitrary",)),
        name="hgrn_sample",
    )(qb, kb, ib, gb, logf, s0, selb, sell, masks, bd2, onorm_t)


def _outproj_kernel(prompt_tiles, x_ref, oap_ref, obp_ref, oas_ref, obs_ref, g1_ref, sh_ref, sc_ref,
                    ln_ref, w_ref, wr_ref, rb_ref, x1_out, aug_out, cls_out):
    def mix_of(oa_ref, ob_ref):
        return jnp.concatenate([oa_ref[hh] for hh in range(GQA_GROUP)] + [ob_ref[...]], axis=1)

    is_prompt = pl.program_id(0) < prompt_tiles
    mix = jnp.where(is_prompt, mix_of(oap_ref, obp_ref), mix_of(oas_ref, obs_ref))
    att = _dot(mix, w_ref[...])
    x = x_ref[...]
    x1 = x + g1_ref[...] * att.reshape(GRP, SUB, D_MODEL)
    x1_out[...] = x1
    ms = jnp.mean(x1 * x1, axis=-1, keepdims=True)
    h = x1 * lax.rsqrt(ms + RMS_EPS) * ln_ref[...]
    h = (h * (1.0 + sc_ref[...]) + sh_ref[...]).reshape(ROWS, D_MODEL)

    h_hi, h_lo = _split2(h)
    bits = lax.bitcast_convert_type(h_hi.astype(F32), U32)
    words = bits[:, :D_MODEL // 2] | (bits[:, D_MODEL // 2:] >> 16)
    for s in range(HALF_WORDS):
        aug_out[pl.ds(s, ROWS, stride=SUBLANES), :] = words[:, s * LANES:(s + 1) * LANES]

    w_hi, w_lo = _split2(wr_ref[...])
    logits = (_dot(h_hi, w_hi) + _dot(h_lo, w_hi) + _dot(h_hi, w_lo)).T[0:N_EXPERTS]
    aff = _sigmoid(logits)
    sel = aff + rb_ref[...]
    e_idx = lax.broadcasted_iota(I32, (N_EXPERTS, ROWS), 0)
    l_idx = lax.broadcasted_iota(I32, (EXPERTS_PER_GROUP, ROWS), 0)

    def first_argmax(vals, idx, big):
        top = jnp.max(vals, axis=0, keepdims=True)
        return top, jnp.min(jnp.where(vals == top, idx, big), axis=0, keepdims=True)

    g_scores = []
    for g in range(N_GROUPS):
        sg = sel[g * EXPERTS_PER_GROUP:(g + 1) * EXPERTS_PER_GROUP]
        m1, i1 = first_argmax(sg, l_idx, EXPERTS_PER_GROUP)
        m2 = jnp.max(jnp.where(l_idx == i1, -jnp.inf, sg), axis=0, keepdims=True)
        g_scores.append(m1 + m2)
    g_score = jnp.concatenate(g_scores, axis=0)
    g_idx = lax.broadcasted_iota(I32, (N_GROUPS, ROWS), 0)
    _, g_sel = first_argmax(g_score, g_idx, N_GROUPS)
    masked = jnp.where((e_idx // EXPERTS_PER_GROUP) == g_sel, sel, -jnp.inf)
    _, e1 = first_argmax(masked, e_idx, N_EXPERTS)
    _, e2 = first_argmax(jnp.where(e_idx == e1, -jnp.inf, masked), e_idx, N_EXPERTS)
    w1 = jnp.sum(jnp.where(e_idx == e1, aff, 0.0), axis=0, keepdims=True)
    w2 = jnp.sum(jnp.where(e_idx == e2, aff, 0.0), axis=0, keepdims=True)
    den = w1 + w2
    w1 = w1 / den
    w2 = w2 / den
    lo_first = e1 < e2
    gate_a = jnp.where(lo_first, w1, w2)
    gate_b = jnp.where(lo_first, w2, w1)
    a = jnp.minimum(e1, e2) - g_sel * EXPERTS_PER_GROUP
    b = jnp.maximum(e1, e2) - g_sel * EXPERTS_PER_GROUP
    base = jnp.where(a == 0, 0, jnp.where(a == 1, 3, 5))
    cls_out[...] = g_sel * PAIRS_PER_GROUP + base + (b - a - 1)

    gates_t = jnp.concatenate([gate_a, gate_b, jnp.zeros((LANES - 2, ROWS), F32)], axis=0)
    aug_out[pl.ds(GATE_ROW, ROWS, stride=SUBLANES), :] = lax.bitcast_convert_type(gates_t.T, U32)
    for s in range(GATE_ROW + 1, SUBLANES):
        aug_out[pl.ds(s, ROWS, stride=SUBLANES), :] = jnp.zeros((ROWS, LANES), U32)


def _outproj(layer, x4, oap4, obp, oas4, obs, mod, ln2, w_out_b, w_router_p, router_bias_c,
             tiles_per_batch):
    n_tiles = x4.shape[0]
    x_spec = pl.BlockSpec((None, GRP, SUB, D_MODEL), lambda i: (i, 0, 0, 0))
    prompt_tiles = oap4.shape[0]
    last = prompt_tiles - 1
    return pl.pallas_call(
        functools.partial(_outproj_kernel, prompt_tiles),
        grid=(n_tiles,),
        in_specs=[
            x_spec,
            pl.BlockSpec((None, GQA_GROUP, ROWS, LANES), lambda i: (jnp.minimum(i, last), 0, 0, 0)),
            pl.BlockSpec((ROWS, V_B), lambda i: (jnp.minimum(i, last), 0)),
            pl.BlockSpec((None, GQA_GROUP, ROWS, LANES), lambda i: (0, 0, 0, 0)),
            pl.BlockSpec((ROWS, V_B), lambda i: (0, 0)),
            _mod_spec(layer, 2, tiles_per_batch), _mod_spec(layer, 3, tiles_per_batch),
            _mod_spec(layer, 4, tiles_per_batch),
            pl.BlockSpec((None, 1, D_MODEL), lambda i: (layer, 0, 0)),
            pl.BlockSpec((None, Q_A + V_B, D_MODEL), lambda i: (layer, 0, 0)),
            pl.BlockSpec((D_MODEL, LANES), lambda i: (0, 0)),
            pl.BlockSpec((N_EXPERTS, 1), lambda i: (0, 0)),
        ],
        out_specs=[
            x_spec,
            pl.BlockSpec((ROWS * SUBLANES, LANES), lambda i: (i, 0)),
            pl.BlockSpec((None, 1, ROWS), lambda i: (i, 0, 0)),
        ],
        out_shape=[
            jax.ShapeDtypeStruct(x4.shape, F32),
            jax.ShapeDtypeStruct((n_tiles * ROWS * SUBLANES, LANES), U32),
            jax.ShapeDtypeStruct((n_tiles, 1, ROWS), I32),
        ],
        compiler_params=_cparams(("arbitrary",)),
        name=f"outproj{layer}",
    )(x4, oap4, obp, oas4, obs, mod, mod, mod, ln2, w_out_b, w_router_p, router_bias_c)


def _rank_kernel(cls_ref, upper_ref, lower_ref, dest_ref, count_ref, rank_ref):
    n_tiles = cls_ref.shape[0]
    c_idx = lax.broadcasted_iota(I32, (GRP, ROWS), 0)
    upper = upper_ref[...]

    def rank_body(i, carry):
        onehot = c_idx == cls_ref[i]
        within = _dot(onehot.astype(BF16), upper)
        rank_ref[i] = jnp.sum(jnp.where(onehot, within + carry, 0.0), axis=0, keepdims=True)
        return carry + jnp.sum(onehot.astype(F32), axis=1, keepdims=True)

    count = lax.fori_loop(0, n_tiles, rank_body, jnp.zeros((GRP, 1), F32))
    count_ref[...] = jnp.broadcast_to(count, (GRP, LANES)).astype(I32)
    tiles = jnp.floor((count + (TM - 1)) * (1.0 / TM))
    first_tile = _dot(lower_ref[...], jnp.broadcast_to(tiles, (GRP, LANES)).astype(BF16))
    start = first_tile[:, 0:1] * TM

    def dest_body(i, carry):
        onehot = c_idx == cls_ref[i]
        dest = rank_ref[i] + jnp.sum(jnp.where(onehot, start, 0.0), axis=0, keepdims=True)
        dest_ref[i] = dest.astype(I32)
        return carry

    lax.fori_loop(0, n_tiles, dest_body, 0)


def _plan_kernel(n_tok, n_t, dest_ref, count_ref, src_ref, ea_ref, eb_ref, nv_ref, nt_ref):
    tile = 0
    for c in range(N_CLASS):
        g, pi = divmod(c, PAIRS_PER_GROUP)
        a, b = PAIR_TABLE[pi]
        cnt = count_ref[c, 0]
        n_c = lax.shift_right_logical(cnt + (TM - 1), TM.bit_length() - 1)
        first = tile

        def mark(t, carry, g=g, a=a, b=b, cnt=cnt, first=first):
            ea_ref[t] = g * EXPERTS_PER_GROUP + a
            eb_ref[t] = g * EXPERTS_PER_GROUP + b
            nv_ref[t] = jnp.minimum(cnt - (t - first) * TM, TM)
            return carry

        lax.fori_loop(tile, tile + n_c, mark, 0)

        def pad(p, carry):
            src_ref[p] = 0
            return carry

        lax.fori_loop(first * TM + cnt, (first + n_c) * TM, pad, 0)
        tile = tile + n_c
    nt_ref[0] = tile

    def unused(t, carry):
        ea_ref[t] = 0
        eb_ref[t] = 0
        nv_ref[t] = 0
        return carry

    lax.fori_loop(tile, n_t, unused, 0)

    def unused_rows(p, carry):
        src_ref[p] = 0
        return carry

    lax.fori_loop(tile * TM, n_t * TM, unused_rows, 0)

    def place(t, carry):
        src_ref[dest_ref[t]] = t
        return carry

    lax.fori_loop(0, n_tok, place, 0, unroll=8)


def _route_plan(cls3, upper, lower):
    n_tiles = cls3.shape[0]
    n_tok = n_tiles * ROWS
    n_t = (n_tok + N_CLASS * (TM - 1)) // TM + 2
    dest, count = pl.pallas_call(
        _rank_kernel,
        out_shape=[jax.ShapeDtypeStruct((n_tiles, 1, ROWS), I32),
                   jax.ShapeDtypeStruct((GRP, LANES), I32)],
        scratch_shapes=[pltpu.VMEM((n_tiles, 1, ROWS), F32)],
        name="moe_rank",
    )(cls3, upper, lower)
    smem = pl.BlockSpec(memory_space=pltpu.SMEM)
    return pl.pallas_call(
        functools.partial(_plan_kernel, n_tok, n_t),
        in_specs=[smem, smem],
        out_specs=[smem, smem, smem, smem, smem],
        out_shape=[jax.ShapeDtypeStruct((n_t * TM,), I32),
                   jax.ShapeDtypeStruct((n_t,), I32),
                   jax.ShapeDtypeStruct((n_t,), I32),
                   jax.ShapeDtypeStruct((n_t,), I32),
                   jax.ShapeDtypeStruct((1,), I32)],
        name="moe_plan",
    )(dest.reshape(n_tok), count)


def _tile_rows(ref, slot, r):
    return ref.at[pl.ds(pl.multiple_of((slot * TM + r) * SUBLANES, SUBLANES), SUBLANES)]


def _moe_kernel(n_tok, src_ref, ea_ref, eb_ref, nv_ref, nt_ref, aug_hbm, wga_ref, wua_ref, wda_ref,
                wgb_ref, wub_ref, wdb_ref, y_hbm, xbuf, ybuf, gsem, ssem):
    i = pl.program_id(0)
    n_used = nt_ref[0]

    def token_tile(hbm, tok):
        return hbm.at[pl.ds(pl.multiple_of(tok * SUBLANES, SUBLANES), SUBLANES)]

    def start_gather(tile, slot):
        def body(r, c):
            tok = src_ref[tile * TM + r]
            pltpu.make_async_copy(token_tile(aug_hbm, tok), _tile_rows(xbuf, slot, r),
                                  gsem.at[slot]).start()
            return c
        lax.fori_loop(0, TM, body, 0, unroll=8)

    def wait_gather(slot):
        pltpu.make_async_copy(aug_hbm.at[pl.ds(0, TM * SUBLANES)],
                              xbuf.at[pl.ds(slot * TM * SUBLANES, TM * SUBLANES)],
                              gsem.at[slot]).wait()

    def start_scatter(tile, slot):
        n_valid = nv_ref[tile]

        def body(r, c):
            tok = jnp.where(r < n_valid, src_ref[tile * TM + r], n_tok + slot * TM + r)
            pltpu.make_async_copy(_tile_rows(ybuf, slot, r), token_tile(y_hbm, tok),
                                  ssem.at[slot]).start()
            return c
        lax.fori_loop(0, TM, body, 0, unroll=8)

    def wait_scatter(slot):
        pltpu.make_async_copy(ybuf.at[pl.ds(slot * TM * SUBLANES, TM * SUBLANES)],
                              y_hbm.at[pl.ds(0, TM * SUBLANES)], ssem.at[slot]).wait()

    @pl.when(i == 0)
    def _():
        ybuf[...] = jnp.zeros_like(ybuf)
        spare = pltpu.make_async_copy(
            ybuf, y_hbm.at[pl.ds(n_tok * SUBLANES, 2 * TM * SUBLANES)], ssem.at[0])
        spare.start()
        spare.wait()

    @pl.when(jnp.logical_and(i == 0, n_used > 0))
    def _():
        start_gather(0, 0)

    @pl.when(i + 1 < n_used)
    def _():
        start_gather(i + 1, (i + 1) % 2)

    @pl.when(jnp.logical_and(i >= 2, i - 2 < n_used))
    def _():
        wait_scatter(i % 2)

    @pl.when(i < n_used)
    def _():
        slot = i % 2
        wait_gather(slot)
        base = slot * TM * SUBLANES
        his, los = [], []
        for s in range(HALF_WORDS):
            w = _token_rows(xbuf, s, TM, base)
            his.append(lax.bitcast_convert_type(w & jnp.uint32(0xFFFF0000), F32).astype(BF16))
            los.append(lax.bitcast_convert_type(w << 16, F32).astype(BF16))
        xb = jnp.concatenate(his + los, axis=1)
        gates = lax.bitcast_convert_type(_token_rows(xbuf, GATE_ROW, TM, base), F32)
        y = jnp.zeros((TM, D_MODEL), F32)
        for e, (wg, wu, wd) in enumerate(((wga_ref, wua_ref, wda_ref), (wgb_ref, wub_ref, wdb_ref))):
            gt = _dot(xb, wg[...])
            up = _dot(xb, wu[...])
            act = gt * _sigmoid(gt) * up * gates[:, e:e + 1]
            y = y + _dot(act.astype(BF16), wd[...])
        for s in range(SUBLANES):
            ybuf[pl.ds(base + s, TM, stride=SUBLANES), :] = y[:, s * LANES:(s + 1) * LANES]
        start_scatter(i, slot)


def _moe(layer, aug, plan, wg_b, wu_b, wd_b):
    src, ea, eb, nv, nt = plan
    n_t = ea.shape[0]
    n_tok = aug.shape[0] // SUBLANES

    def w_spec(shape, which):
        def index(i, src_ref, ea_ref, eb_ref, nv_ref, nt_ref):
            t = jnp.minimum(i, jnp.maximum(nt_ref[0] - 1, 0))
            e = (ea_ref, eb_ref)[which][t]
            return (layer, e, 0, 0)
        return pl.BlockSpec((None, None) + shape, index)

    gu = (D_MODEL, D_EXPERT)
    dn = (D_EXPERT, D_MODEL)
    grid_spec = pltpu.PrefetchScalarGridSpec(
        num_scalar_prefetch=5,
        grid=(n_t,),
        in_specs=[pl.BlockSpec(memory_space=pl.ANY),
                  w_spec(gu, 0), w_spec(gu, 0), w_spec(dn, 0),
                  w_spec(gu, 1), w_spec(gu, 1), w_spec(dn, 1)],
        out_specs=pl.BlockSpec(memory_space=pl.ANY),
        scratch_shapes=[pltpu.VMEM((2 * TM * SUBLANES, LANES), U32),
                        pltpu.VMEM((2 * TM * SUBLANES, LANES), F32),
                        pltpu.SemaphoreType.DMA((2,)), pltpu.SemaphoreType.DMA((2,))],
    )
    return pl.pallas_call(
        functools.partial(_moe_kernel, n_tok),
        grid_spec=grid_spec,
        out_shape=jax.ShapeDtypeStruct(((n_tok + 2 * TM) * SUBLANES, LANES), F32),
        compiler_params=_cparams(("arbitrary",)),
        name=f"moe{layer}",
    )(src, ea, eb, nv, nt, aug, wg_b, wu_b, wd_b, wg_b, wu_b, wd_b)


def _pair_state_in(state):
    b = state.shape[0]
    st = jnp.swapaxes(state, -1, -2).reshape(b, N_PAIR, 2, DV_B, DK_B)
    z = jnp.zeros_like(st[:, :, 0])
    top = jnp.concatenate([st[:, :, 0], z], axis=-1)
    bot = jnp.concatenate([z, st[:, :, 1]], axis=-1)
    return jnp.concatenate([top, bot], axis=-2)


def _pair_state_out(st):
    b = st.shape[0]
    h0 = st[:, :, :DV_B, :DK_B]
    h1 = st[:, :, DV_B:, DK_B:]
    out = jnp.stack([h0, h1], axis=2).reshape(b, N_HEADS_B, DV_B, DK_B)
    return jnp.swapaxes(out, -1, -2)


def kernel(x_prompt, x_sample, cache_k, cache_v, state_hgrn, c_prompt, c_sample, w_ada, b_ada,
           ln1, ln2, w_in, q_norm, k_norm, sinks, lb_param, o_norm, w_out, w_router, router_bias,
           w_e_gate, w_e_up, w_e_down):
    batch, seq, _ = x_prompt.shape
    dec_batch, dec_seq, _ = x_sample.shape
    depth = w_ada.shape[0]
    assert dec_batch == GRP and dec_seq == SUB and seq % ROWS == 0
    tiles_per_batch = seq // ROWS
    n_prompt = batch * seq
    n_tok = n_prompt + dec_batch * dec_seq
    n_tiles = n_tok // ROWS
    prompt_tiles = n_prompt // ROWS
    n_cache = cache_k.shape[2]

    seg = np.arange(Q_A) // HEAD_DIM
    bd = jnp.asarray((seg[:, None] == seg[None, :]).astype(np.float32) / HEAD_DIM, BF16)
    bd2 = bd[:LANES, :LANES]

    def hgrn_consts(length):
        selb, sell, masks = _hgrn_consts(length)
        return jnp.asarray(selb, BF16), jnp.asarray(sell, BF16), jnp.asarray(masks)

    consts_p = hgrn_consts(CHUNK)
    consts_s = hgrn_consts(dec_seq)
    upper = jnp.asarray(np.triu(np.ones((ROWS, ROWS), np.float32), 1), BF16)
    lower = jnp.asarray(np.tril(np.ones((GRP, GRP), np.float32), -1), BF16)

    wq = w_in[:, :, :Q_A].reshape(depth, D_MODEL, N_KV_A, GQA_GROUP, HEAD_DIM)
    wq = jnp.swapaxes(wq, 2, 3).reshape(depth, D_MODEL, Q_A)
    w_in_b = jnp.concatenate([wq, w_in[:, :, Q_A:]], axis=-1).astype(BF16)
    wo = w_out[:, :Q_A].reshape(depth, N_KV_A, GQA_GROUP, HEAD_DIM, D_MODEL)
    wo = jnp.swapaxes(wo, 1, 2).reshape(depth, Q_A, D_MODEL)
    w_out_b = jnp.concatenate([wo, w_out[:, Q_A:]], axis=1).astype(BF16)
    wg_b, wu_b, wd_b = w_e_gate.astype(BF16), w_e_up.astype(BF16), w_e_down.astype(BF16)
    q_norm_t = jnp.tile(q_norm, (1, N_HEADS_A)).reshape(depth, 1, Q_A)
    k_norm_t = jnp.tile(k_norm, (1, N_KV_A)).reshape(depth, 1, KV_A)
    o_norm_t = jnp.tile(o_norm, (1, N_HEADS_B)).reshape(depth, 1, V_B)
    ln1_r = ln1.reshape(depth, 1, D_MODEL)
    ln2_r = ln2.reshape(depth, 1, D_MODEL)
    w_router_p = jnp.pad(w_router, ((0, 0), (0, LANES - N_EXPERTS)))
    router_bias_c = router_bias.reshape(N_EXPERTS, 1)

    c_exp = jnp.concatenate([jnp.repeat(c_prompt, GRP, axis=0), c_sample], axis=0)
    mod = _adaln(c_exp, w_ada, b_ada).reshape(depth, batch + 1, GRP, 1, N_MOD * D_MODEL)

    cache_k2 = cache_k.reshape(depth, dec_batch, n_cache, KV_A)
    cache_v2 = cache_v.reshape(depth, dec_batch, n_cache, KV_A)

    src = (x_prompt.reshape(prompt_tiles, GRP, SUB, D_MODEL),
           x_sample.reshape(n_tiles - prompt_tiles, GRP, SUB, D_MODEL))
    kp, vp, sp, kss, vss, sss = [], [], [], [], [], []
    for l in range(depth):
        x, q4, k, v, qb, logf, kb, ib, gb = _inproj(
            l, src, n_tiles, prompt_tiles, mod, ln1_r, w_in_b, q_norm_t, k_norm_t, lb_param, bd,
            tiles_per_batch)

        oap4 = _attn_prompt(q4, k, v, sinks[l], batch, seq)
        oas4 = _attn_sample(q4, k, v, cache_k2[l], cache_v2[l], sinks[l], prompt_tiles,
                            n_prompt, dec_batch, dec_seq)
        obp, s_p = _hgrn_prompt(qb, kb, ib, gb, logf, consts_p, bd2, o_norm_t[l], batch, seq)
        obs, s_s = _hgrn_sample(qb, kb, ib, gb, logf, _pair_state_in(state_hgrn[l]), consts_s,
                                bd2, o_norm_t[l], n_prompt, dec_batch, dec_seq)

        x1, aug, cls3 = _outproj(l, x, oap4, obp, oas4, obs, mod, ln2_r, w_out_b, w_router_p,
                                 router_bias_c, tiles_per_batch)
        y = _moe(l, aug, _route_plan(cls3, upper, lower), wg_b, wu_b, wd_b)
        src = (x1, y)

        kp.append(k[:n_prompt].reshape(batch, seq, N_KV_A, HEAD_DIM)[:, -WINDOW:])
        vp.append(v[:n_prompt].reshape(batch, seq, N_KV_A, HEAD_DIM)[:, -WINDOW:])
        sp.append(_pair_state_out(s_p[:, 0]))
        kss.append(k[n_prompt:].reshape(dec_batch, dec_seq, N_KV_A, HEAD_DIM))
        vss.append(v[n_prompt:].reshape(dec_batch, dec_seq, N_KV_A, HEAD_DIM))
        sss.append(_pair_state_out(s_s))

    yp, ys = _final(depth - 1, src[0], src[1], mod, prompt_tiles, tiles_per_batch)
    return (yp.reshape(batch, seq, D_MODEL), ys.reshape(dec_batch, dec_seq, D_MODEL),
            jnp.stack(kp), jnp.stack(vp), jnp.stack(sp),
            jnp.stack(kss), jnp.stack(vss), jnp.stack(sss))
```

```python
import functools

import numpy as np
import jax
import jax.numpy as jnp
from jax import lax
from jax.experimental import pallas as pl
from jax.experimental.pallas import tpu as pltpu

F32 = jnp.float32
BF16 = jnp.bfloat16
I32 = jnp.int32

D_MODEL = 1024
HEAD_DIM = 64
N_HEADS_A = 8
N_KV_A = 2
GQA_GROUP = N_HEADS_A // N_KV_A
N_HEADS_B = 8
DK_B = 64
DV_B = 64
Q_A = N_HEADS_A * HEAD_DIM
KV_A = N_KV_A * HEAD_DIM
W_B = N_HEADS_B * DK_B
V_B = N_HEADS_B * DV_B
CHUNK = 64
WINDOW = 128
N_EXPERTS = 16
N_GROUPS = 4
EXPERTS_PER_GROUP = 4
D_EXPERT = D_MODEL // 2
RMS_EPS = 1e-6
NEG_INF = -1e30
N_MOD = 6

LANES = 128
SUBLANES = 8
ROWS = 512
GRP = 32
SUB = ROWS // GRP
ATT_ROWS = 256
N_PAIR = N_HEADS_B // 2
SAMPLE_SEQS = 4
HGRN_SPAN_LIMIT = 80.0
TM = 256
PAIRS_PER_GROUP = 6
PAIR_TABLE = ((0, 1), (0, 2), (0, 3), (1, 2), (1, 3), (2, 3))
N_CLASS = N_GROUPS * PAIRS_PER_GROUP
DMA_UNROLL = 8
VMEM_LIMIT = 56 * 1024 * 1024


def _cparams(sem):
    return pltpu.CompilerParams(dimension_semantics=sem, vmem_limit_bytes=VMEM_LIMIT)


def _dot(a, b):
    return jnp.dot(a, b, preferred_element_type=F32)


def _dot_nt(a, b):
    return lax.dot_general(a, b, (((1,), (1,)), ((), ())), preferred_element_type=F32)


def _dot_tn(a, b):
    return lax.dot_general(a, b, (((0,), (0,)), ((), ())), preferred_element_type=F32)


def _sigmoid(x):
    return 1.0 / (1.0 + jnp.exp(-x))


def _split2(x):
    hi = x.astype(BF16)
    lo = (x - hi.astype(F32)).astype(BF16)
    return hi, lo


def _split3(x):
    hi = x.astype(BF16)
    r = x - hi.astype(F32)
    mid = r.astype(BF16)
    lo = (r - mid.astype(F32)).astype(BF16)
    return hi, mid, lo


def _seg_mean(sq, bd):
    hi, lo = _split2(sq)
    return _dot(hi, bd) + _dot(lo, bd)


def _token_rows(ref, s, n, base=0):
    return ref[pl.ds(base + s, n, stride=SUBLANES), :]


def _adaln_kernel(c_ref, w_ref, b_ref, o_ref):
    c = c_ref[...]
    a = c * _sigmoid(c)
    a_hi, a_lo = _split2(a)
    w_hi, w_lo = _split2(w_ref[...])
    acc = _dot(a_hi, w_hi) + _dot(a_lo, w_hi) + _dot(a_hi, w_lo)
    o_ref[...] = acc + b_ref[...]


def _adaln(c_exp, w_ada, b_ada):
    depth = w_ada.shape[0]
    rows = c_exp.shape[0]
    tn = D_MODEL
    return pl.pallas_call(
        _adaln_kernel,
        grid=(depth, N_MOD * D_MODEL // tn),
        in_specs=[
            pl.BlockSpec((rows, D_MODEL), lambda l, n: (0, 0)),
            pl.BlockSpec((None, D_MODEL, tn), lambda l, n: (l, 0, n)),
            pl.BlockSpec((None, 1, tn), lambda l, n: (l, 0, n)),
        ],
        out_specs=pl.BlockSpec((None, rows, tn), lambda l, n: (l, 0, n)),
        out_shape=jax.ShapeDtypeStruct((depth, rows, N_MOD * D_MODEL), F32),
        compiler_params=_cparams(("arbitrary", "arbitrary")),
        name="adaln",
    )(c_exp, w_ada, b_ada.reshape(depth, 1, N_MOD * D_MODEL))


def _mod_spec(layer, j, tiles_per_batch):
    return pl.BlockSpec((None, None, GRP, 1, D_MODEL),
                        lambda i, *_: (layer, i // tiles_per_batch, 0, 0, j))


def _moe_output_tile(y_ref):
    y = jnp.concatenate([_token_rows(y_ref, s, ROWS) for s in range(SUBLANES)], axis=1)
    return y.reshape(GRP, SUB, D_MODEL)


def _inproj_kernel(layer, prompt_tiles, n_src, *refs):
    src = refs[:n_src]
    sh_ref, sc_ref, ln_ref, w_ref, qn_ref, kn_ref, lbp_ref, bd_ref = refs[n_src:n_src + 8]
    x_out, q_out, k_out, v_out, qb_out, logf_out, kb_out, ib_out, gb_out = refs[n_src + 8:]
    if layer == 0:
        xp_ref, xs_ref = src
        x = jnp.where(pl.program_id(0) < prompt_tiles, xp_ref[...], xs_ref[...])
    else:
        x1_ref, y_ref, g2_ref = src
        x = x1_ref[...] + g2_ref[...] * _moe_output_tile(y_ref)
    x_out[...] = x
    ms = jnp.mean(x * x, axis=-1, keepdims=True)
    h = x * lax.rsqrt(ms + RMS_EPS) * ln_ref[...]
    h = h * (1.0 + sc_ref[...]) + sh_ref[...]
    hb = h.reshape(ROWS, D_MODEL).astype(BF16)
    bd = bd_ref[...]

    qa = _dot(hb, w_ref[:, 0:Q_A])
    qa = qa * lax.rsqrt(_seg_mean(qa * qa, bd) + RMS_EPS) * qn_ref[...]
    qa = (qa * (HEAD_DIM ** -0.5)).astype(BF16)
    for hh in range(GQA_GROUP):
        q_out[hh] = qa[:, hh * LANES:(hh + 1) * LANES]

    ka = _dot(hb, w_ref[:, Q_A:Q_A + KV_A])
    ka = ka * lax.rsqrt(_seg_mean(ka * ka, bd[:KV_A, :KV_A]) + RMS_EPS) * kn_ref[...]
    k_out[...] = ka
    v_out[...] = _dot(hb, w_ref[:, Q_A + KV_A:Q_A + 2 * KV_A])

    c0 = Q_A + 2 * KV_A
    qb_out[...] = (_dot(hb, w_ref[:, c0:c0 + W_B]) * (DK_B ** -0.5)).astype(BF16)

    p = lbp_ref[...]
    e = jnp.exp(p - jnp.max(p, axis=0, keepdims=True))
    sm = e / jnp.sum(e, axis=0, keepdims=True)
    cum0 = sm[0:1]
    cum = cum0
    for r in range(1, layer + 1):
        cum = cum + sm[r:r + 1]
    lb = cum - cum0

    z = _dot(hb, w_ref[:, c0 + W_B:c0 + 2 * W_B])
    log_sig = jnp.minimum(z, 0.0) - jnp.log1p(jnp.exp(-jnp.abs(z)))
    a = jnp.log(lb)
    c = jnp.log1p(-lb) + log_sig
    logf_out[...] = jnp.maximum(a, c) + jnp.log1p(jnp.exp(-jnp.abs(a - c)))
    kb_out[...] = ((1.0 - lb) * _sigmoid(-z)).astype(BF16)

    c1 = c0 + 2 * W_B
    ib_out[...] = _dot(hb, w_ref[:, c1:c1 + V_B]).astype(BF16)
    gb_out[...] = _dot(hb, w_ref[:, c1 + V_B:c1 + 2 * V_B]).astype(BF16)


def _inproj(layer, src, n_tiles, prompt_tiles, mod, ln1, w_in_b, q_norm_t, k_norm_t, lb_param, bd,
            tiles_per_batch):
    depth = lb_param.shape[0]
    n = n_tiles * ROWS
    in_width = w_in_b.shape[-1]
    x_spec = pl.BlockSpec((None, GRP, SUB, D_MODEL), lambda i: (i, 0, 0, 0))
    if layer == 0:
        last = prompt_tiles - 1
        src_specs = [
            pl.BlockSpec((None, GRP, SUB, D_MODEL), lambda i: (jnp.minimum(i, last), 0, 0, 0)),
            pl.BlockSpec((None, GRP, SUB, D_MODEL), lambda i: (0, 0, 0, 0)),
        ]
        src_args = list(src)
    else:
        src_specs = [x_spec, pl.BlockSpec((ROWS * SUBLANES, LANES), lambda i: (i, 0)),
                     _mod_spec(layer - 1, 5, tiles_per_batch)]
        src_args = list(src) + [mod]
    row_spec = lambda w: pl.BlockSpec((ROWS, w), lambda i: (i, 0))
    return pl.pallas_call(
        functools.partial(_inproj_kernel, layer, prompt_tiles, len(src_specs)),
        grid=(n_tiles,),
        in_specs=src_specs + [
            _mod_spec(layer, 0, tiles_per_batch), _mod_spec(layer, 1, tiles_per_batch),
            pl.BlockSpec((None, 1, D_MODEL), lambda i: (layer, 0, 0)),
            pl.BlockSpec((None, D_MODEL, in_width), lambda i: (layer, 0, 0)),
            pl.BlockSpec((None, 1, Q_A), lambda i: (layer, 0, 0)),
            pl.BlockSpec((None, 1, KV_A), lambda i: (layer, 0, 0)),
            pl.BlockSpec((depth, W_B), lambda i: (0, 0)),
            pl.BlockSpec((Q_A, Q_A), lambda i: (0, 0)),
        ],
        out_specs=[
            x_spec,
            pl.BlockSpec((None, GQA_GROUP, ROWS, LANES), lambda i: (i, 0, 0, 0)),
            row_spec(KV_A), row_spec(KV_A),
            row_spec(W_B), row_spec(W_B), row_spec(W_B), row_spec(V_B), row_spec(V_B),
        ],
        out_shape=[
            jax.ShapeDtypeStruct((n_tiles, GRP, SUB, D_MODEL), F32),
            jax.ShapeDtypeStruct((n_tiles, GQA_GROUP, ROWS, LANES), BF16),
            jax.ShapeDtypeStruct((n, KV_A), F32),
            jax.ShapeDtypeStruct((n, KV_A), F32),
            jax.ShapeDtypeStruct((n, W_B), BF16),
            jax.ShapeDtypeStruct((n, W_B), F32),
            jax.ShapeDtypeStruct((n, W_B), BF16),
            jax.ShapeDtypeStruct((n, V_B), BF16),
            jax.ShapeDtypeStruct((n, V_B), BF16),
        ],
        compiler_params=_cparams(("arbitrary",)),
        name=f"inproj{layer}",
    )(*src_args, mod, mod, ln1, w_in_b, q_norm_t, k_norm_t, lb_param, bd)


def _final_kernel(prompt_tiles, x1_ref, y_ref, g2_ref, yp_ref, ys_ref):
    i = pl.program_id(0)
    x = x1_ref[...] + g2_ref[...] * _moe_output_tile(y_ref)

    @pl.when(i < prompt_tiles)
    def _():
        yp_ref[...] = x

    @pl.when(i >= prompt_tiles)
    def _():
        ys_ref[...] = x


def _final(layer, x1, y, mod, prompt_tiles, tiles_per_batch):
    n_tiles = x1.shape[0]
    last = prompt_tiles - 1
    tile = (None, GRP, SUB, D_MODEL)
    return pl.pallas_call(
        functools.partial(_final_kernel, prompt_tiles),
        grid=(n_tiles,),
        in_specs=[pl.BlockSpec(tile, lambda i: (i, 0, 0, 0)),
                  pl.BlockSpec((ROWS * SUBLANES, LANES), lambda i: (i, 0)),
                  _mod_spec(layer, 5, tiles_per_batch)],
        out_specs=[pl.BlockSpec(tile, lambda i: (jnp.minimum(i, last), 0, 0, 0)),
                   pl.BlockSpec(tile, lambda i: (0, 0, 0, 0))],
        out_shape=[jax.ShapeDtypeStruct((prompt_tiles, GRP, SUB, D_MODEL), F32),
                   jax.ShapeDtypeStruct((n_tiles - prompt_tiles, GRP, SUB, D_MODEL), F32)],
        compiler_params=_cparams(("arbitrary",)),
        name="final",
    )(x1, y, mod)


def _attend(q, k, v, sinks_ref, q_len, k_off, k_pos0=None):
    n_q = q.shape[0]
    n_k = k.shape[0]
    row = lax.broadcasted_iota(I32, (n_q, n_k), 0)
    col = lax.broadcasted_iota(I32, (n_q, n_k), 1)
    dist = jnp.abs((row % q_len) + k_off - col).astype(F32)
    lane = lax.broadcasted_iota(I32, (n_k, LANES), 1)
    hh_col = lax.broadcasted_iota(I32, (n_q, 1), 0) // q_len
    out = jnp.zeros((n_q, LANES), F32)
    for g in range(N_KV_A):
        in_g = (lane // HEAD_DIM) == g
        kg = jnp.where(in_g, k, 0.0).astype(BF16)
        vg = jnp.where(in_g, v, 0.0).astype(BF16)
        s = _dot_nt(q, kg)
        head = (g * GQA_GROUP + 1 + hh_col).astype(F32)
        s = s - jnp.exp2(-8.0 * head / N_HEADS_A) * dist
        if k_pos0 is not None:
            s = jnp.where(col + k_pos0 >= 0, s, NEG_INF)
        sink = jnp.zeros((n_q, 1), F32)
        for j in range(GQA_GROUP):
            sink = jnp.where(hh_col == j, sinks_ref[g * GQA_GROUP + j], sink)
        m = jnp.maximum(jnp.max(s, axis=-1, keepdims=True), sink)
        p = jnp.exp(s - m)
        denom = jnp.sum(p, axis=-1, keepdims=True) + jnp.exp(sink - m)
        out = out + _dot(p.astype(BF16), vg) / denom
    return out


def _attn_prompt_kernel(sinks_ref, q_ref, kp_ref, kc_ref, vp_ref, vc_ref, o_ref):
    j = pl.program_id(1)
    kcat = jnp.concatenate([kp_ref[...], kc_ref[...]], axis=0)
    vcat = jnp.concatenate([vp_ref[...], vc_ref[...]], axis=0)
    for c in range(ATT_ROWS // CHUNK):
        q = q_ref[:, c * CHUNK:(c + 1) * CHUNK, :].reshape(GQA_GROUP * CHUNK, LANES)
        k = kcat[c * CHUNK:c * CHUNK + WINDOW + CHUNK]
        v = vcat[c * CHUNK:c * CHUNK + WINDOW + CHUNK]
        k_pos0 = j * ATT_ROWS + c * CHUNK - WINDOW
        out = _attend(q, k, v, sinks_ref, CHUNK, WINDOW, k_pos0)
        o_ref[:, c * CHUNK:(c + 1) * CHUNK, :] = out.reshape(GQA_GROUP, CHUNK, LANES).astype(BF16)


def _attn_prompt(q4, k, v, sinks, batch, seq):
    steps = seq // ATT_ROWS
    per_tile = ROWS // ATT_ROWS
    win_blocks = ATT_ROWS // WINDOW
    n_tiles = batch * seq // ROWS
    cur = pl.BlockSpec((ATT_ROWS, KV_A), lambda b, j: (b * steps + j, 0))
    prev = pl.BlockSpec(
        (WINDOW, KV_A), lambda b, j: (b * steps * win_blocks + jnp.maximum(j * win_blocks - 1, 0), 0))
    qo = pl.BlockSpec((None, GQA_GROUP, ATT_ROWS, LANES),
                      lambda b, j: ((b * steps + j) // per_tile, 0, j % per_tile, 0))
    return pl.pallas_call(
        _attn_prompt_kernel,
        grid=(batch, steps),
        in_specs=[pl.BlockSpec(memory_space=pltpu.SMEM), qo, prev, cur, prev, cur],
        out_specs=qo,
        out_shape=jax.ShapeDtypeStruct((n_tiles, GQA_GROUP, ROWS, LANES), BF16),
        compiler_params=_cparams(("arbitrary", "arbitrary")),
        name="attn_prompt",
    )(sinks, q4, k, k, v, v)


def _attn_sample_kernel(sinks_ref, q_ref, kc_ref, kn_ref, vc_ref, vn_ref, o_ref):
    n_new = kn_ref.shape[0]
    n_cache = kc_ref.shape[0]
    q = q_ref[...].reshape(GQA_GROUP * n_new, LANES)
    k = jnp.concatenate([kc_ref[...], kn_ref[...]], axis=0)
    v = jnp.concatenate([vc_ref[...], vn_ref[...]], axis=0)
    out = _attend(q, k, v, sinks_ref, n_new, n_cache)
    o_ref[...] = out.reshape(GQA_GROUP, n_new, LANES).astype(BF16)


def _attn_sample(q4, k, v, cache_k, cache_v, sinks, tile, row0, dec_batch, dec_seq):
    n_cache = cache_k.shape[1]
    q_spec = pl.BlockSpec((None, GQA_GROUP, dec_seq, LANES), lambda b: (tile, 0, b, 0))
    new = pl.BlockSpec((dec_seq, KV_A), lambda b: (row0 // dec_seq + b, 0))
    cache = pl.BlockSpec((None, n_cache, KV_A), lambda b: (b, 0, 0))
    return pl.pallas_call(
        _attn_sample_kernel,
        grid=(dec_batch,),
        in_specs=[pl.BlockSpec(memory_space=pltpu.SMEM), q_spec, cache, new, cache, new],
        out_specs=pl.BlockSpec((None, GQA_GROUP, dec_seq, LANES), lambda b: (0, 0, b, 0)),
        out_shape=jax.ShapeDtypeStruct((1, GQA_GROUP, dec_batch * dec_seq, LANES), BF16),
        compiler_params=_cparams(("arbitrary",)),
        name="attn_sample",
    )(sinks, q4, cache_k, k, cache_v, v)


def _level_sizes(length):
    sizes = []
    m = length // 2
    while m >= 1:
        sizes.append(m)
        m //= 2
    return sizes


def _hgrn_consts(length):
    t = np.arange(length)
    tri = (t[None, :] <= t[:, None]).astype(np.float32)
    after = (t[None, :] > t[:, None]).astype(np.float32)
    levels = []
    masks = []
    s = np.tile(t, 2)[None, :]
    tt = t[:, None]
    for m in _level_sizes(length):
        blk = t // m
        same = blk[None, :] == blk[:, None]
        q_rows = (blk % 2 == 1)[:, None]
        sel = np.where(q_rows, same & (t[None, :] <= t[:, None]), same & (t[None, :] > t[:, None]))
        levels.append(sel.astype(np.float32))
        masks.append(((tt // m) % 2 == 1) & ((s // m) == (tt // m) - 1))
    masks.append(s == tt)
    masks.append(s <= tt)
    base = np.concatenate([tri, after], axis=0)
    lev = np.concatenate(levels, axis=0)
    return (np.concatenate([base] * 3, axis=1), np.concatenate([lev] * 3, axis=1),
            np.stack(masks).astype(np.int32))


def _stack2(x, first):
    return jnp.concatenate([jnp.where(first, x, 0.0), jnp.where(first, 0.0, x)], axis=0).astype(BF16)


def _hgrn_chunk(q, k, i, g, base, seq, mask_ref, bd2, onorm, st_ref, arg_ref):
    length = q.shape[0]
    n_lev = len(_level_sizes(length))
    lane = lax.broadcasted_iota(I32, (length, LANES), 1)
    first = lane < DK_B
    rowp = lax.broadcasted_iota(I32, (LANES, LANES), 0) // DV_B
    colp = lax.broadcasted_iota(I32, (LANES, LANES), 1) // DK_B
    same_head = rowp == colp
    mid = length // 2 - 1
    outs = []
    for p in range(N_PAIR):
        sl = slice(p * LANES, (p + 1) * LANES)
        qp = q[:, sl].astype(F32)
        kp = k[:, sl].astype(F32)
        ip = i[:, sl].astype(F32)
        b = base[0:length, sl]
        if arg_ref is None:
            rel = b - b[mid:mid + 1]
            a = _dot_nt((qp * jnp.exp(rel)).astype(BF16), _stack2(kp * jnp.exp(-rel), first))
            a = jnp.where(mask_ref[n_lev + 1] != 0, a, 0.0)
        else:
            a = _dot_nt(qp.astype(BF16), _stack2(kp, first))
            a = jnp.where(mask_ref[n_lev] != 0, a, 0.0)
            for lev in range(n_lev):
                e = jnp.exp(arg_ref[lev * length:(lev + 1) * length, sl])
                pr = _dot_nt((qp * e).astype(BF16), _stack2(kp * e, first))
                a = jnp.where(mask_ref[lev] != 0, pr, a)
        e_b = jnp.exp(b)
        e_k = jnp.exp(base[length:2 * length, sl])
        st = st_ref[seq, p]
        o = (_dot(a.astype(BF16), _stack2(ip, first))
             + _dot_nt((qp * e_b).astype(BF16), st.astype(BF16)))
        upd = _dot_tn(ip.astype(BF16), (kp * e_k).astype(BF16))
        st_ref[seq, p] = st * e_b[length - 1:length, :] + jnp.where(same_head, upd, 0.0)
        ms = _seg_mean(o * o, bd2)
        gp = g[:, sl].astype(F32)
        o = o * lax.rsqrt(ms + RMS_EPS) * onorm[:, sl] * (gp * _sigmoid(gp))
        outs.append(o.astype(BF16))
    return jnp.concatenate(outs, axis=1)


def _hgrn_kernel(n_seq, n_chunk, length, carry_state, *refs):
    if carry_state:
        (q_ref, k_ref, i_ref, g_ref, f_ref, selb_ref, sell_ref, mask_ref, bd2_ref, on_ref,
         o_ref, s_ref, st_ref, base_ref, arg_ref) = refs

        @pl.when(pl.program_id(1) == 0)
        def _():
            st_ref[...] = jnp.zeros_like(st_ref)
    else:
        (q_ref, k_ref, i_ref, g_ref, f_ref, s0_ref, selb_ref, sell_ref, mask_ref, bd2_ref, on_ref,
         o_ref, s_ref, st_ref, base_ref, arg_ref) = refs
        st_ref[...] = s0_ref[...]

    bd2 = bd2_ref[...]
    onorm = on_ref[...]
    n_items = n_seq * n_chunk
    mid = length // 2 - 1

    def pieces(rows):
        return jnp.concatenate(_split3(f_ref[rows, :]), axis=0)

    span = jnp.float32(0.0)
    for it in range(n_items):
        base = _dot(selb_ref[...], pieces(slice(it * length, (it + 1) * length)))
        base_ref[it * 2 * length:(it + 1) * 2 * length, :] = base
        b_mid = base[mid:mid + 1]
        span = jnp.maximum(span, jnp.max(base[0:1] - b_mid))
        span = jnp.maximum(span, jnp.max(b_mid - base[length - 1:length]))
    safe = span <= HGRN_SPAN_LIMIT

    @pl.when(safe)
    def _():
        for it in range(n_items):
            r = slice(it * length, (it + 1) * length)
            o_ref[r, :] = _hgrn_chunk(q_ref[r, :], k_ref[r, :], i_ref[r, :], g_ref[r, :],
                                      base_ref[it * 2 * length:(it + 1) * 2 * length, :],
                                      it // n_chunk, mask_ref, bd2, onorm, st_ref, None)

    @pl.when(jnp.logical_not(safe))
    def _():
        def body(it, carry):
            r = pl.ds(pl.multiple_of(it * length, length), length)
            arg_ref[...] = _dot(sell_ref[...], pieces(r))
            base = base_ref[pl.ds(pl.multiple_of(it * 2 * length, 2 * length), 2 * length), :]
            seq = it // n_chunk if n_chunk > 1 and n_seq > 1 else (it if n_chunk == 1 else 0)
            o_ref[r, :] = _hgrn_chunk(q_ref[r, :], k_ref[r, :], i_ref[r, :], g_ref[r, :], base,
                                      seq, mask_ref, bd2, onorm, st_ref, arg_ref)
            return carry

        lax.fori_loop(0, n_items, body, 0)

    if carry_state:
        @pl.when(pl.program_id(1) == pl.num_programs(1) - 1)
        def _():
            s_ref[...] = st_ref[...]
    else:
        s_ref[...] = st_ref[...]


def _hgrn_scratch(n_seq, n_items, length, sell):
    return [pltpu.VMEM((n_seq, N_PAIR, LANES, LANES), F32),
            pltpu.VMEM((n_items * 2 * length, W_B), F32),
            pltpu.VMEM((sell.shape[0], W_B), F32)]


def _hgrn_prompt(qb, kb, ib, gb, logf, consts, bd2, onorm_t, batch, seq):
    selb, sell, masks = consts
    steps = seq // ATT_ROWS
    n_chunk = ATT_ROWS // CHUNK
    row = pl.BlockSpec((ATT_ROWS, W_B), lambda b, j: (b * steps + j, 0))
    whole = lambda a: pl.BlockSpec(a.shape, lambda b, j: (0,) * a.ndim)
    return pl.pallas_call(
        functools.partial(_hgrn_kernel, 1, n_chunk, CHUNK, True),
        grid=(batch, steps),
        in_specs=[row, row, row, row, row, whole(selb), whole(sell), whole(masks), whole(bd2),
                  whole(onorm_t)],
        out_specs=[row, pl.BlockSpec((None, 1, N_PAIR, LANES, LANES), lambda b, j: (b, 0, 0, 0, 0))],
        out_shape=[jax.ShapeDtypeStruct((batch * seq, V_B), BF16),
                   jax.ShapeDtypeStruct((batch, 1, N_PAIR, LANES, LANES), F32)],
        scratch_shapes=_hgrn_scratch(1, n_chunk, CHUNK, sell),
        compiler_params=_cparams(("arbitrary", "arbitrary")),
        name="hgrn_prompt",
    )(qb, kb, ib, gb, logf, selb, sell, masks, bd2, onorm_t)


def _hgrn_sample(qb, kb, ib, gb, logf, s0, consts, bd2, onorm_t, row0, dec_batch, dec_seq):
    selb, sell, masks = consts
    rows = SAMPLE_SEQS * dec_seq
    row = pl.BlockSpec((rows, W_B), lambda b: (row0 // rows + b, 0))
    whole = lambda a: pl.BlockSpec(a.shape, lambda b: (0,) * a.ndim)
    state = pl.BlockSpec((SAMPLE_SEQS, N_PAIR, LANES, LANES), lambda b: (b, 0, 0, 0))
    return pl.pallas_call(
        functools.partial(_hgrn_kernel, SAMPLE_SEQS, 1, dec_seq, False),
        grid=(dec_batch // SAMPLE_SEQS,),
        in_specs=[row, row, row, row, row, state, whole(selb), whole(sell), whole(masks),
                  whole(bd2), whole(onorm_t)],
        out_specs=[pl.BlockSpec((rows, V_B), lambda b: (b, 0)), state],
        out_shape=[jax.ShapeDtypeStruct((dec_batch * dec_seq, V_B), BF16),
                   jax.ShapeDtypeStruct((dec_batch, N_PAIR, LANES, LANES), F32)],
        scratch_shapes=_hgrn_scratch(SAMPLE_SEQS, SAMPLE_SEQS, dec_seq, sell),
        compiler_params=_cparams(("arbitrary",)),
        name="hgrn_sample",
    )(qb, kb, ib, gb, logf, s0, selb, sell, masks, bd2, onorm_t)


def _outproj_kernel(prompt_tiles, x_ref, oap_ref, obp_ref, oas_ref, obs_ref, g1_ref, sh_ref, sc_ref,
                    ln_ref, w_ref, wr_ref, rb_ref, x1_out, h2_out, cls_out, ga_out, gb_out):
    def mix_of(oa_ref, ob_ref):
        return jnp.concatenate([oa_ref[hh] for hh in range(GQA_GROUP)] + [ob_ref[...]], axis=1)

    is_prompt = pl.program_id(0) < prompt_tiles
    mix = jnp.where(is_prompt, mix_of(oap_ref, obp_ref), mix_of(oas_ref, obs_ref))
    att = _dot(mix, w_ref[...])
    x = x_ref[...]
    x1 = x + g1_ref[...] * att.reshape(GRP, SUB, D_MODEL)
    x1_out[...] = x1
    ms = jnp.mean(x1 * x1, axis=-1, keepdims=True)
    h = x1 * lax.rsqrt(ms + RMS_EPS) * ln_ref[...]
    h = (h * (1.0 + sc_ref[...]) + sh_ref[...]).reshape(ROWS, D_MODEL)

    for s in range(SUBLANES):
        h2_out[pl.ds(s, ROWS, stride=SUBLANES), :] = h[:, s * LANES:(s + 1) * LANES]

    h_hi, h_lo = _split2(h)
    w_hi, w_lo = _split2(wr_ref[...])
    logits = (_dot(h_hi, w_hi) + _dot(h_lo, w_hi) + _dot(h_hi, w_lo)).T[0:N_EXPERTS]
    aff = _sigmoid(logits)
    sel = aff + rb_ref[...]
    e_idx = lax.broadcasted_iota(I32, (N_EXPERTS, ROWS), 0)
    l_idx = lax.broadcasted_iota(I32, (EXPERTS_PER_GROUP, ROWS), 0)

    def first_argmax(vals, idx, big):
        top = jnp.max(vals, axis=0, keepdims=True)
        return top, jnp.min(jnp.where(vals == top, idx, big), axis=0, keepdims=True)

    g_scores = []
    for g in range(N_GROUPS):
        sg = sel[g * EXPERTS_PER_GROUP:(g + 1) * EXPERTS_PER_GROUP]
        m1, i1 = first_argmax(sg, l_idx, EXPERTS_PER_GROUP)
        m2 = jnp.max(jnp.where(l_idx == i1, -jnp.inf, sg), axis=0, keepdims=True)
        g_scores.append(m1 + m2)
    g_score = jnp.concatenate(g_scores, axis=0)
    g_idx = lax.broadcasted_iota(I32, (N_GROUPS, ROWS), 0)
    _, g_sel = first_argmax(g_score, g_idx, N_GROUPS)
    masked = jnp.where((e_idx // EXPERTS_PER_GROUP) == g_sel, sel, -jnp.inf)
    _, e1 = first_argmax(masked, e_idx, N_EXPERTS)
    _, e2 = first_argmax(jnp.where(e_idx == e1, -jnp.inf, masked), e_idx, N_EXPERTS)
    w1 = jnp.sum(jnp.where(e_idx == e1, aff, 0.0), axis=0, keepdims=True)
    w2 = jnp.sum(jnp.where(e_idx == e2, aff, 0.0), axis=0, keepdims=True)
    den = w1 + w2
    w1 = w1 / den
    w2 = w2 / den
    lo_first = e1 < e2
    gate_a = jnp.where(lo_first, w1, w2)
    gate_b = jnp.where(lo_first, w2, w1)
    a = jnp.minimum(e1, e2) - g_sel * EXPERTS_PER_GROUP
    b = jnp.maximum(e1, e2) - g_sel * EXPERTS_PER_GROUP
    base = jnp.where(a == 0, 0, jnp.where(a == 1, 3, 5))
    cls_out[...] = g_sel * PAIRS_PER_GROUP + base + (b - a - 1)
    ga_out[...] = gate_a
    gb_out[...] = gate_b


def _outproj(layer, x4, oap4, obp, oas4, obs, mod, ln2, w_out_b, w_router_p, router_bias_c,
             tiles_per_batch):
    n_tiles = x4.shape[0]
    x_spec = pl.BlockSpec((None, GRP, SUB, D_MODEL), lambda i: (i, 0, 0, 0))
    prompt_tiles = oap4.shape[0]
    last = prompt_tiles - 1
    lane_row = pl.BlockSpec((None, 1, ROWS), lambda i: (i, 0, 0))
    return pl.pallas_call(
        functools.partial(_outproj_kernel, prompt_tiles),
        grid=(n_tiles,),
        in_specs=[
            x_spec,
            pl.BlockSpec((None, GQA_GROUP, ROWS, LANES), lambda i: (jnp.minimum(i, last), 0, 0, 0)),
            pl.BlockSpec((ROWS, V_B), lambda i: (jnp.minimum(i, last), 0)),
            pl.BlockSpec((None, GQA_GROUP, ROWS, LANES), lambda i: (0, 0, 0, 0)),
            pl.BlockSpec((ROWS, V_B), lambda i: (0, 0)),
            _mod_spec(layer, 2, tiles_per_batch), _mod_spec(layer, 3, tiles_per_batch),
            _mod_spec(layer, 4, tiles_per_batch),
            pl.BlockSpec((None, 1, D_MODEL), lambda i: (layer, 0, 0)),
            pl.BlockSpec((None, Q_A + V_B, D_MODEL), lambda i: (layer, 0, 0)),
            pl.BlockSpec((D_MODEL, LANES), lambda i: (0, 0)),
            pl.BlockSpec((N_EXPERTS, 1), lambda i: (0, 0)),
        ],
        out_specs=[
            x_spec,
            pl.BlockSpec((ROWS * SUBLANES, LANES), lambda i: (i, 0)),
            lane_row, lane_row, lane_row,
        ],
        out_shape=[
            jax.ShapeDtypeStruct(x4.shape, F32),
            jax.ShapeDtypeStruct((n_tiles * ROWS * SUBLANES, LANES), F32),
            jax.ShapeDtypeStruct((n_tiles, 1, ROWS), I32),
            jax.ShapeDtypeStruct((n_tiles, 1, ROWS), F32),
            jax.ShapeDtypeStruct((n_tiles, 1, ROWS), F32),
        ],
        compiler_params=_cparams(("arbitrary",)),
        name=f"outproj{layer}",
    )(x4, oap4, obp, oas4, obs, mod, mod, mod, ln2, w_out_b, w_router_p, router_bias_c)


def _rank_kernel(cls_ref, upper_ref, lower_ref, dest_ref, count_ref, rank_ref):
    n_tiles = cls_ref.shape[0]
    c_idx = lax.broadcasted_iota(I32, (GRP, ROWS), 0)
    upper = upper_ref[...]

    def rank_body(i, carry):
        onehot = c_idx == cls_ref[i]
        within = _dot(onehot.astype(BF16), upper)
        rank_ref[i] = jnp.sum(jnp.where(onehot, within + carry, 0.0), axis=0, keepdims=True)
        return carry + jnp.sum(onehot.astype(F32), axis=1, keepdims=True)

    count = lax.fori_loop(0, n_tiles, rank_body, jnp.zeros((GRP, 1), F32))
    count_ref[...] = jnp.broadcast_to(count, (GRP, LANES)).astype(I32)
    tiles = jnp.floor((count + (TM - 1)) * (1.0 / TM))
    first_tile = _dot(lower_ref[...], jnp.broadcast_to(tiles, (GRP, LANES)).astype(BF16))
    start = first_tile[:, 0:1] * TM

    def dest_body(i, carry):
        onehot = c_idx == cls_ref[i]
        dest = rank_ref[i] + jnp.sum(jnp.where(onehot, start, 0.0), axis=0, keepdims=True)
        dest_ref[i] = dest.astype(I32)
        return carry

    lax.fori_loop(0, n_tiles, dest_body, 0)


def _plan_kernel(n_tok, n_t, dest_ref, count_ref, src_ref, ea_ref, eb_ref, nv_ref, nt_ref):
    tile = 0
    for c in range(N_CLASS):
        g, pi = divmod(c, PAIRS_PER_GROUP)
        a, b = PAIR_TABLE[pi]
        cnt = count_ref[c, 0]
        n_c = lax.shift_right_logical(cnt + (TM - 1), TM.bit_length() - 1)
        first = tile

        def mark(t, carry, g=g, a=a, b=b, cnt=cnt, first=first):
            ea_ref[t] = g * EXPERTS_PER_GROUP + a
            eb_ref[t] = g * EXPERTS_PER_GROUP + b
            nv_ref[t] = jnp.minimum(cnt - (t - first) * TM, TM)
            return carry

        lax.fori_loop(tile, tile + n_c, mark, 0)

        def pad(p, carry):
            src_ref[p] = 0
            return carry

        lax.fori_loop(first * TM + cnt, (first + n_c) * TM, pad, 0)
        tile = tile + n_c
    nt_ref[0] = tile

    def unused(t, carry):
        ea_ref[t] = 0
        eb_ref[t] = 0
        nv_ref[t] = 0
        return carry

    lax.fori_loop(tile, n_t, unused, 0)

    def unused_rows(p, carry):
        src_ref[p] = 0
        return carry

    lax.fori_loop(tile * TM, n_t * TM, unused_rows, 0)

    def place(t, carry):
        src_ref[dest_ref[t]] = t
        return carry

    lax.fori_loop(0, n_tok, place, 0, unroll=8)


def _route_plan(cls3, upper, lower):
    n_tiles = cls3.shape[0]
    n_tok = n_tiles * ROWS
    n_t = (n_tok + N_CLASS * (TM - 1)) // TM + 2
    dest, count = pl.pallas_call(
        _rank_kernel,
        out_shape=[jax.ShapeDtypeStruct((n_tiles, 1, ROWS), I32),
                   jax.ShapeDtypeStruct((GRP, LANES), I32)],
        scratch_shapes=[pltpu.VMEM((n_tiles, 1, ROWS), F32)],
        name="moe_rank",
    )(cls3, upper, lower)
    smem = pl.BlockSpec(memory_space=pltpu.SMEM)
    return pl.pallas_call(
        functools.partial(_plan_kernel, n_tok, n_t),
        in_specs=[smem, smem],
        out_specs=[smem, smem, smem, smem, smem],
        out_shape=[jax.ShapeDtypeStruct((n_t * TM,), I32),
                   jax.ShapeDtypeStruct((n_t,), I32),
                   jax.ShapeDtypeStruct((n_t,), I32),
                   jax.ShapeDtypeStruct((n_t,), I32),
                   jax.ShapeDtypeStruct((1,), I32)],
        name="moe_plan",
    )(dest.reshape(n_tok), count)


def _tile_rows(ref, slot, r):
    return ref.at[pl.ds(pl.multiple_of((slot * TM + r) * SUBLANES, SUBLANES), SUBLANES)]


def _moe_kernel(n_tok, src_ref, ea_ref, eb_ref, nv_ref, nt_ref, ga_ref, gb_ref, h2_hbm, wga_ref,
                wua_ref, wda_ref, wgb_ref, wub_ref, wdb_ref, y_hbm, xbuf, ybuf, gbuf, gsem, ssem):
    i = pl.program_id(0)
    n_used = nt_ref[0]
    lane = lax.broadcasted_iota(I32, (1, LANES), 1)

    def token_tile(hbm, tok):
        return hbm.at[pl.ds(pl.multiple_of(tok * SUBLANES, SUBLANES), SUBLANES)]

    def start_gather(tile, slot):
        def body(r8, c):
            for u in range(DMA_UNROLL):
                r = r8 * DMA_UNROLL + u
                tok = src_ref[tile * TM + r]
                pltpu.make_async_copy(token_tile(h2_hbm, tok), _tile_rows(xbuf, slot, r),
                                      gsem.at[slot]).start(priority=u % 2)
                gbuf[pl.ds(slot * TM + r, 1), :] = jnp.where(lane == 0, ga_ref[tok], gb_ref[tok])
            return c
        lax.fori_loop(0, TM // DMA_UNROLL, body, 0)

    def wait_gather(slot):
        pltpu.make_async_copy(h2_hbm.at[pl.ds(0, TM * SUBLANES)],
                              xbuf.at[pl.ds(slot * TM * SUBLANES, TM * SUBLANES)],
                              gsem.at[slot]).wait()

    def start_scatter(tile, slot):
        n_valid = nv_ref[tile]

        def body(r8, c):
            for u in range(DMA_UNROLL):
                r = r8 * DMA_UNROLL + u
                tok = jnp.where(r < n_valid, src_ref[tile * TM + r], n_tok + slot * TM + r)
                pltpu.make_async_copy(_tile_rows(ybuf, slot, r), token_tile(y_hbm, tok),
                                      ssem.at[slot]).start(priority=u % 2)
            return c
        lax.fori_loop(0, TM // DMA_UNROLL, body, 0)

    def wait_scatter(slot):
        pltpu.make_async_copy(ybuf.at[pl.ds(slot * TM * SUBLANES, TM * SUBLANES)],
                              y_hbm.at[pl.ds(0, TM * SUBLANES)], ssem.at[slot]).wait()

    @pl.when(i == 0)
    def _():
        ybuf[...] = jnp.zeros_like(ybuf)
        spare = pltpu.make_async_copy(
            ybuf, y_hbm.at[pl.ds(n_tok * SUBLANES, 2 * TM * SUBLANES)], ssem.at[0])
        spare.start()
        spare.wait()

    @pl.when(jnp.logical_and(i == 0, n_used > 0))
    def _():
        start_gather(0, 0)

    @pl.when(i + 1 < n_used)
    def _():
        start_gather(i + 1, (i + 1) % 2)

    @pl.when(jnp.logical_and(i >= 2, i - 2 < n_used))
    def _():
        wait_scatter(i % 2)

    @pl.when(i < n_used)
    def _():
        slot = i % 2
        wait_gather(slot)
        base = slot * TM * SUBLANES
        xb = jnp.concatenate([_token_rows(xbuf, s, TM, base) for s in range(SUBLANES)],
                             axis=1).astype(BF16)
        gates = gbuf[pl.ds(slot * TM, TM), :]
        y = jnp.zeros((TM, D_MODEL), F32)
        for e, (wg, wu, wd) in enumerate(((wga_ref, wua_ref, wda_ref), (wgb_ref, wub_ref, wdb_ref))):
            gt = _dot(xb, wg[...])
            up = _dot(xb, wu[...])
            act = gt * _sigmoid(gt) * up * gates[:, e:e + 1]
            y = y + _dot(act.astype(BF16), wd[...])
        for s in range(SUBLANES):
            ybuf[pl.ds(base + s, TM, stride=SUBLANES), :] = y[:, s * LANES:(s + 1) * LANES]
        start_scatter(i, slot)


def _moe(layer, h2, gate_a, gate_b, plan, wg_b, wu_b, wd_b):
    src, ea, eb, nv, nt = plan
    n_t = ea.shape[0]
    n_tok = h2.shape[0] // SUBLANES

    def w_spec(shape, which):
        def index(i, src_ref, ea_ref, eb_ref, nv_ref, nt_ref, ga_ref, gb_ref):
            t = jnp.minimum(i, jnp.maximum(nt_ref[0] - 1, 0))
            e = (ea_ref, eb_ref)[which][t]
            return (layer, e, 0, 0)
        return pl.BlockSpec((None, None) + shape, index)

    gu = (D_MODEL, D_EXPERT)
    dn = (D_EXPERT, D_MODEL)
    grid_spec = pltpu.PrefetchScalarGridSpec(
        num_scalar_prefetch=7,
        grid=(n_t,),
        in_specs=[pl.BlockSpec(memory_space=pl.ANY),
                  w_spec(gu, 0), w_spec(gu, 0), w_spec(dn, 0),
                  w_spec(gu, 1), w_spec(gu, 1), w_spec(dn, 1)],
        out_specs=pl.BlockSpec(memory_space=pl.ANY),
        scratch_shapes=[pltpu.VMEM((2 * TM * SUBLANES, LANES), F32),
                        pltpu.VMEM((2 * TM * SUBLANES, LANES), F32),
                        pltpu.VMEM((2 * TM, LANES), F32),
                        pltpu.SemaphoreType.DMA((2,)), pltpu.SemaphoreType.DMA((2,))],
    )
    return pl.pallas_call(
        functools.partial(_moe_kernel, n_tok),
        grid_spec=grid_spec,
        out_shape=jax.ShapeDtypeStruct(((n_tok + 2 * TM) * SUBLANES, LANES), F32),
        compiler_params=_cparams(("arbitrary",)),
        name=f"moe{layer}",
    )(src, ea, eb, nv, nt, gate_a.reshape(n_tok), gate_b.reshape(n_tok), h2,
      wg_b, wu_b, wd_b, wg_b, wu_b, wd_b)


def _pair_state_in(state):
    b = state.shape[0]
    st = jnp.swapaxes(state, -1, -2).reshape(b, N_PAIR, 2, DV_B, DK_B)
    z = jnp.zeros_like(st[:, :, 0])
    top = jnp.concatenate([st[:, :, 0], z], axis=-1)
    bot = jnp.concatenate([z, st[:, :, 1]], axis=-1)
    return jnp.concatenate([top, bot], axis=-2)


def _pair_state_out(st):
    b = st.shape[0]
    h0 = st[:, :, :DV_B, :DK_B]
    h1 = st[:, :, DV_B:, DK_B:]
    out = jnp.stack([h0, h1], axis=2).reshape(b, N_HEADS_B, DV_B, DK_B)
    return jnp.swapaxes(out, -1, -2)


def kernel(x_prompt, x_sample, cache_k, cache_v, state_hgrn, c_prompt, c_sample, w_ada, b_ada,
           ln1, ln2, w_in, q_norm, k_norm, sinks, lb_param, o_norm, w_out, w_router, router_bias,
           w_e_gate, w_e_up, w_e_down):
    batch, seq, _ = x_prompt.shape
    dec_batch, dec_seq, _ = x_sample.shape
    depth = w_ada.shape[0]
    assert dec_batch == GRP and dec_seq == SUB and seq % ROWS == 0
    tiles_per_batch = seq // ROWS
    n_prompt = batch * seq
    n_tok = n_prompt + dec_batch * dec_seq
    n_tiles = n_tok // ROWS
    prompt_tiles = n_prompt // ROWS
    n_cache = cache_k.shape[2]

    seg = np.arange(Q_A) // HEAD_DIM
    bd = jnp.asarray((seg[:, None] == seg[None, :]).astype(np.float32) / HEAD_DIM, BF16)
    bd2 = bd[:LANES, :LANES]

    def hgrn_consts(length):
        selb, sell, masks = _hgrn_consts(length)
        return jnp.asarray(selb, BF16), jnp.asarray(sell, BF16), jnp.asarray(masks)

    consts_p = hgrn_consts(CHUNK)
    consts_s = hgrn_consts(dec_seq)
    upper = jnp.asarray(np.triu(np.ones((ROWS, ROWS), np.float32), 1), BF16)
    lower = jnp.asarray(np.tril(np.ones((GRP, GRP), np.float32), -1), BF16)

    wq = w_in[:, :, :Q_A].reshape(depth, D_MODEL, N_KV_A, GQA_GROUP, HEAD_DIM)
    wq = jnp.swapaxes(wq, 2, 3).reshape(depth, D_MODEL, Q_A)
    w_in_b = jnp.concatenate([wq, w_in[:, :, Q_A:]], axis=-1).astype(BF16)
    wo = w_out[:, :Q_A].reshape(depth, N_KV_A, GQA_GROUP, HEAD_DIM, D_MODEL)
    wo = jnp.swapaxes(wo, 1, 2).reshape(depth, Q_A, D_MODEL)
    w_out_b = jnp.concatenate([wo, w_out[:, Q_A:]], axis=1).astype(BF16)
    wg_b, wu_b, wd_b = w_e_gate.astype(BF16), w_e_up.astype(BF16), w_e_down.astype(BF16)
    q_norm_t = jnp.tile(q_norm, (1, N_HEADS_A)).reshape(depth, 1, Q_A)
    k_norm_t = jnp.tile(k_norm, (1, N_KV_A)).reshape(depth, 1, KV_A)
    o_norm_t = jnp.tile(o_norm, (1, N_HEADS_B)).reshape(depth, 1, V_B)
    ln1_r = ln1.reshape(depth, 1, D_MODEL)
    ln2_r = ln2.reshape(depth, 1, D_MODEL)
    w_router_p = jnp.pad(w_router, ((0, 0), (0, LANES - N_EXPERTS)))
    router_bias_c = router_bias.reshape(N_EXPERTS, 1)

    c_exp = jnp.concatenate([jnp.repeat(c_prompt, GRP, axis=0), c_sample], axis=0)
    mod = _adaln(c_exp, w_ada, b_ada).reshape(depth, batch + 1, GRP, 1, N_MOD * D_MODEL)

    cache_k2 = cache_k.reshape(depth, dec_batch, n_cache, KV_A)
    cache_v2 = cache_v.reshape(depth, dec_batch, n_cache, KV_A)

    src = (x_prompt.reshape(prompt_tiles, GRP, SUB, D_MODEL),
           x_sample.reshape(n_tiles - prompt_tiles, GRP, SUB, D_MODEL))
    kp, vp, sp, kss, vss, sss = [], [], [], [], [], []
    for l in range(depth):
        x, q4, k, v, qb, logf, kb, ib, gb = _inproj(
            l, src, n_tiles, prompt_tiles, mod, ln1_r, w_in_b, q_norm_t, k_norm_t, lb_param, bd,
            tiles_per_batch)

        oap4 = _attn_prompt(q4, k, v, sinks[l], batch, seq)
        oas4 = _attn_sample(q4, k, v, cache_k2[l], cache_v2[l], sinks[l], prompt_tiles,
                            n_prompt, dec_batch, dec_seq)
        obp, s_p = _hgrn_prompt(qb, kb, ib, gb, logf, consts_p, bd2, o_norm_t[l], batch, seq)
        obs, s_s = _hgrn_sample(qb, kb, ib, gb, logf, _pair_state_in(state_hgrn[l]), consts_s,
                                bd2, o_norm_t[l], n_prompt, dec_batch, dec_seq)

        x1, h2, cls3, gate_a, gate_b = _outproj(l, x, oap4, obp, oas4, obs, mod, ln2_r, w_out_b,
                                                w_router_p, router_bias_c, tiles_per_batch)
        y = _moe(l, h2, gate_a, gate_b, _route_plan(cls3, upper, lower), wg_b, wu_b, wd_b)
        src = (x1, y)

        kp.append(k[:n_prompt].reshape(batch, seq, N_KV_A, HEAD_DIM)[:, -WINDOW:])
        vp.append(v[:n_prompt].reshape(batch, seq, N_KV_A, HEAD_DIM)[:, -WINDOW:])
        sp.append(_pair_state_out(s_p[:, 0]))
        kss.append(k[n_prompt:].reshape(dec_batch, dec_seq, N_KV_A, HEAD_DIM))
        vss.append(v[n_prompt:].reshape(dec_batch, dec_seq, N_KV_A, HEAD_DIM))
        sss.append(_pair_state_out(s_s))

    yp, ys = _final(depth - 1, src[0], src[1], mod, prompt_tiles, tiles_per_batch)
    return (yp.reshape(batch, seq, D_MODEL), ys.reshape(dec_batch, dec_seq, D_MODEL),
            jnp.stack(kp), jnp.stack(vp), jnp.stack(sp),
            jnp.stack(kss), jnp.stack(vss), jnp.stack(sss))
```

```python
import functools

import numpy as np
import jax
import jax.numpy as jnp
from jax import lax
from jax.experimental import pallas as pl
from jax.experimental.pallas import tpu as pltpu

F32 = jnp.float32
BF16 = jnp.bfloat16
I32 = jnp.int32

D_MODEL = 1024
HEAD_DIM = 64
N_HEADS_A = 8
N_KV_A = 2
GQA_GROUP = N_HEADS_A // N_KV_A
N_HEADS_B = 8
DK_B = 64
DV_B = 64
Q_A = N_HEADS_A * HEAD_DIM
KV_A = N_KV_A * HEAD_DIM
W_B = N_HEADS_B * DK_B
V_B = N_HEADS_B * DV_B
CHUNK = 64
WINDOW = 128
N_EXPERTS = 16
N_GROUPS = 4
EXPERTS_PER_GROUP = 4
D_EXPERT = D_MODEL // 2
RMS_EPS = 1e-6
NEG_INF = -1e30
N_MOD = 6

LANES = 128
SUBLANES = 8
ROWS = 512
GRP = 32
SUB = ROWS // GRP
ATT_ROWS = 256
N_PAIR = N_HEADS_B // 2
SAMPLE_SEQS = 4
HGRN_SPAN_LIMIT = 80.0
TM = 256
PAIRS_PER_GROUP = 6
PAIR_TABLE = ((0, 1), (0, 2), (0, 3), (1, 2), (1, 3), (2, 3))
N_CLASS = N_GROUPS * PAIRS_PER_GROUP
DMA_UNROLL = 8
MOE_PIECE = 256
VMEM_LIMIT = 56 * 1024 * 1024


def _cparams(sem):
    return pltpu.CompilerParams(dimension_semantics=sem, vmem_limit_bytes=VMEM_LIMIT)


def _dot(a, b):
    return jnp.dot(a, b, preferred_element_type=F32)


def _dot_nt(a, b):
    return lax.dot_general(a, b, (((1,), (1,)), ((), ())), preferred_element_type=F32)


def _dot_tn(a, b):
    return lax.dot_general(a, b, (((0,), (0,)), ((), ())), preferred_element_type=F32)


def _sigmoid(x):
    return 1.0 / (1.0 + jnp.exp(-x))


def _split2(x):
    hi = x.astype(BF16)
    lo = (x - hi.astype(F32)).astype(BF16)
    return hi, lo


def _split3(x):
    hi = x.astype(BF16)
    r = x - hi.astype(F32)
    mid = r.astype(BF16)
    lo = (r - mid.astype(F32)).astype(BF16)
    return hi, mid, lo


def _seg_mean(sq, bd):
    hi, lo = _split2(sq)
    return _dot(hi, bd) + _dot(lo, bd)


def _token_rows(ref, s, n, base=0):
    return ref[pl.ds(base + s, n, stride=SUBLANES), :]


def _adaln_kernel(c_ref, w_ref, b_ref, o_ref):
    c = c_ref[...]
    a = c * _sigmoid(c)
    a_hi, a_lo = _split2(a)
    w_hi, w_lo = _split2(w_ref[...])
    acc = _dot(a_hi, w_hi) + _dot(a_lo, w_hi) + _dot(a_hi, w_lo)
    o_ref[...] = acc + b_ref[...]


def _adaln(c_exp, w_ada, b_ada):
    depth = w_ada.shape[0]
    rows = c_exp.shape[0]
    tn = D_MODEL
    return pl.pallas_call(
        _adaln_kernel,
        grid=(depth, N_MOD * D_MODEL // tn),
        in_specs=[
            pl.BlockSpec((rows, D_MODEL), lambda l, n: (0, 0)),
            pl.BlockSpec((None, D_MODEL, tn), lambda l, n: (l, 0, n)),
            pl.BlockSpec((None, 1, tn), lambda l, n: (l, 0, n)),
        ],
        out_specs=pl.BlockSpec((None, rows, tn), lambda l, n: (l, 0, n)),
        out_shape=jax.ShapeDtypeStruct((depth, rows, N_MOD * D_MODEL), F32),
        compiler_params=_cparams(("arbitrary", "arbitrary")),
        name="adaln",
    )(c_exp, w_ada, b_ada.reshape(depth, 1, N_MOD * D_MODEL))


def _mod_spec(layer, j, tiles_per_batch):
    return pl.BlockSpec((None, None, GRP, 1, D_MODEL),
                        lambda i, *_: (layer, i // tiles_per_batch, 0, 0, j))


def _grouped(x):
    return x.reshape(GRP, SUB, D_MODEL)


def _prompt_tile_spec(prompt_tiles, tiles_per_batch):
    def index(i):
        t = jnp.minimum(i, prompt_tiles - 1)
        return (t // tiles_per_batch, t % tiles_per_batch, 0)
    return pl.BlockSpec((None, ROWS, D_MODEL), index)


def _moe_output_tile(y_ref):
    y = jnp.concatenate([_token_rows(y_ref, s, ROWS) for s in range(SUBLANES)], axis=1)
    return y.reshape(GRP, SUB, D_MODEL)


def _inproj_kernel(layer, prompt_tiles, n_src, *refs):
    src = refs[:n_src]
    sh_ref, sc_ref, ln_ref, w_ref, qn_ref, kn_ref, lbp_ref, bd_ref = refs[n_src:n_src + 8]
    x_out, q_out, k_out, v_out, qb_out, logf_out, kb_out, ib_out, gb_out = refs[n_src + 8:]
    if layer == 0:
        xp_ref, xs_ref = src
        x = jnp.where(pl.program_id(0) < prompt_tiles, _grouped(xp_ref[...]), xs_ref[...])
    else:
        x1_ref, y_ref, g2_ref = src
        x = _grouped(x1_ref[...]) + g2_ref[...] * _moe_output_tile(y_ref)
    x_out[...] = x.reshape(ROWS, D_MODEL)
    ms = jnp.mean(x * x, axis=-1, keepdims=True)
    h = x * lax.rsqrt(ms + RMS_EPS) * ln_ref[...]
    h = h * (1.0 + sc_ref[...]) + sh_ref[...]
    hb = h.reshape(ROWS, D_MODEL).astype(BF16)
    bd = bd_ref[...]

    qa = _dot(hb, w_ref[:, 0:Q_A])
    qa = qa * lax.rsqrt(_seg_mean(qa * qa, bd) + RMS_EPS) * qn_ref[...]
    qa = (qa * (HEAD_DIM ** -0.5)).astype(BF16)
    for hh in range(GQA_GROUP):
        q_out[hh] = qa[:, hh * LANES:(hh + 1) * LANES]

    ka = _dot(hb, w_ref[:, Q_A:Q_A + KV_A])
    ka = ka * lax.rsqrt(_seg_mean(ka * ka, bd[:KV_A, :KV_A]) + RMS_EPS) * kn_ref[...]
    k_out[...] = ka
    v_out[...] = _dot(hb, w_ref[:, Q_A + KV_A:Q_A + 2 * KV_A])

    c0 = Q_A + 2 * KV_A
    qb_out[...] = (_dot(hb, w_ref[:, c0:c0 + W_B]) * (DK_B ** -0.5)).astype(BF16)

    p = lbp_ref[...]
    e = jnp.exp(p - jnp.max(p, axis=0, keepdims=True))
    sm = e / jnp.sum(e, axis=0, keepdims=True)
    cum0 = sm[0:1]
    cum = cum0
    for r in range(1, layer + 1):
        cum = cum + sm[r:r + 1]
    lb = cum - cum0

    z = _dot(hb, w_ref[:, c0 + W_B:c0 + 2 * W_B])
    pos = z >= 0.0
    t = jnp.exp(-jnp.abs(z))
    num = jnp.where(pos, jnp.log1p(lb * t), jnp.maximum(jnp.log(lb + t), z))
    logf_out[...] = num - jnp.log1p(t)
    kb_out[...] = ((1.0 - lb) * jnp.where(pos, t, 1.0) / (1.0 + t)).astype(BF16)

    c1 = c0 + 2 * W_B
    ib_out[...] = _dot(hb, w_ref[:, c1:c1 + V_B]).astype(BF16)
    gb_out[...] = _dot(hb, w_ref[:, c1 + V_B:c1 + 2 * V_B]).astype(BF16)


def _inproj(layer, src, n_tiles, prompt_tiles, mod, ln1, w_in_b, q_norm_t, k_norm_t, lb_param, bd,
            tiles_per_batch):
    depth = lb_param.shape[0]
    n = n_tiles * ROWS
    in_width = w_in_b.shape[-1]
    x_spec = pl.BlockSpec((ROWS, D_MODEL), lambda i: (i, 0))
    if layer == 0:
        src_specs = [_prompt_tile_spec(prompt_tiles, tiles_per_batch),
                     pl.BlockSpec((GRP, SUB, D_MODEL), lambda i: (0, 0, 0))]
        src_args = list(src)
    else:
        src_specs = [x_spec, pl.BlockSpec((ROWS * SUBLANES, LANES), lambda i: (i, 0)),
                     _mod_spec(layer - 1, 5, tiles_per_batch)]
        src_args = list(src) + [mod]
    row_spec = lambda w: pl.BlockSpec((ROWS, w), lambda i: (i, 0))
    return pl.pallas_call(
        functools.partial(_inproj_kernel, layer, prompt_tiles, len(src_specs)),
        grid=(n_tiles,),
        in_specs=src_specs + [
            _mod_spec(layer, 0, tiles_per_batch), _mod_spec(layer, 1, tiles_per_batch),
            pl.BlockSpec((None, 1, D_MODEL), lambda i: (layer, 0, 0)),
            pl.BlockSpec((None, D_MODEL, in_width), lambda i: (layer, 0, 0)),
            pl.BlockSpec((None, 1, Q_A), lambda i: (layer, 0, 0)),
            pl.BlockSpec((None, 1, KV_A), lambda i: (layer, 0, 0)),
            pl.BlockSpec((depth, W_B), lambda i: (0, 0)),
            pl.BlockSpec((Q_A, Q_A), lambda i: (0, 0)),
        ],
        out_specs=[
            x_spec,
            pl.BlockSpec((None, GQA_GROUP, ROWS, LANES), lambda i: (i, 0, 0, 0)),
            row_spec(KV_A), row_spec(KV_A),
            row_spec(W_B), row_spec(W_B), row_spec(W_B), row_spec(V_B), row_spec(V_B),
        ],
        out_shape=[
            jax.ShapeDtypeStruct((n, D_MODEL), F32),
            jax.ShapeDtypeStruct((n_tiles, GQA_GROUP, ROWS, LANES), BF16),
            jax.ShapeDtypeStruct((n, KV_A), F32),
            jax.ShapeDtypeStruct((n, KV_A), F32),
            jax.ShapeDtypeStruct((n, W_B), BF16),
            jax.ShapeDtypeStruct((n, W_B), F32),
            jax.ShapeDtypeStruct((n, W_B), BF16),
            jax.ShapeDtypeStruct((n, V_B), BF16),
            jax.ShapeDtypeStruct((n, V_B), BF16),
        ],
        compiler_params=_cparams(("arbitrary",)),
        name=f"inproj{layer}",
    )(*src_args, mod, mod, ln1, w_in_b, q_norm_t, k_norm_t, lb_param, bd)


def _final_kernel(prompt_tiles, x1_ref, y_ref, g2_ref, yp_ref, ys_ref):
    i = pl.program_id(0)
    x = _grouped(x1_ref[...]) + g2_ref[...] * _moe_output_tile(y_ref)

    @pl.when(i < prompt_tiles)
    def _():
        yp_ref[...] = x.reshape(ROWS, D_MODEL)

    @pl.when(i >= prompt_tiles)
    def _():
        ys_ref[...] = x


def _final(layer, x1, y, mod, batch, seq, tiles_per_batch):
    n_tiles = x1.shape[0] // ROWS
    prompt_tiles = batch * tiles_per_batch
    return pl.pallas_call(
        functools.partial(_final_kernel, prompt_tiles),
        grid=(n_tiles,),
        in_specs=[pl.BlockSpec((ROWS, D_MODEL), lambda i: (i, 0)),
                  pl.BlockSpec((ROWS * SUBLANES, LANES), lambda i: (i, 0)),
                  _mod_spec(layer, 5, tiles_per_batch)],
        out_specs=[_prompt_tile_spec(prompt_tiles, tiles_per_batch),
                   pl.BlockSpec((GRP, SUB, D_MODEL), lambda i: (0, 0, 0))],
        out_shape=[jax.ShapeDtypeStruct((batch, seq, D_MODEL), F32),
                   jax.ShapeDtypeStruct((GRP, SUB, D_MODEL), F32)],
        compiler_params=_cparams(("arbitrary",)),
        name="final",
    )(x1, y, mod)


def _attend(q, k, v, sinks_ref, q_len, k_off, k_pos0=None):
    n_q = q.shape[0]
    n_k = k.shape[0]
    row = lax.broadcasted_iota(I32, (n_q, n_k), 0)
    col = lax.broadcasted_iota(I32, (n_q, n_k), 1)
    dist = jnp.abs((row % q_len) + k_off - col).astype(F32)
    lane = lax.broadcasted_iota(I32, (n_k, LANES), 1)
    hh_col = lax.broadcasted_iota(I32, (n_q, 1), 0) // q_len
    out = jnp.zeros((n_q, LANES), F32)
    for g in range(N_KV_A):
        in_g = (lane // HEAD_DIM) == g
        kg = jnp.where(in_g, k, 0.0).astype(BF16)
        vg = jnp.where(in_g, v, 0.0).astype(BF16)
        s = _dot_nt(q, kg)
        head = (g * GQA_GROUP + 1 + hh_col).astype(F32)
        s = s - jnp.exp2(-8.0 * head / N_HEADS_A) * dist
        if k_pos0 is not None:
            s = jnp.where(col + k_pos0 >= 0, s, NEG_INF)
        sink = jnp.zeros((n_q, 1), F32)
        for j in range(GQA_GROUP):
            sink = jnp.where(hh_col == j, sinks_ref[g * GQA_GROUP + j], sink)
        m = jnp.maximum(jnp.max(s, axis=-1, keepdims=True), sink)
        p = jnp.exp(s - m)
        denom = jnp.sum(p, axis=-1, keepdims=True) + jnp.exp(sink - m)
        out = out + _dot(p.astype(BF16), vg) / denom
    return out


def _attn_prompt_kernel(sinks_ref, q_ref, kp_ref, kc_ref, vp_ref, vc_ref, o_ref):
    j = pl.program_id(1)
    kcat = jnp.concatenate([kp_ref[...], kc_ref[...]], axis=0)
    vcat = jnp.concatenate([vp_ref[...], vc_ref[...]], axis=0)
    for c in range(ATT_ROWS // CHUNK):
        q = q_ref[:, c * CHUNK:(c + 1) * CHUNK, :].reshape(GQA_GROUP * CHUNK, LANES)
        k = kcat[c * CHUNK:c * CHUNK + WINDOW + CHUNK]
        v = vcat[c * CHUNK:c * CHUNK + WINDOW + CHUNK]
        k_pos0 = j * ATT_ROWS + c * CHUNK - WINDOW
        out = _attend(q, k, v, sinks_ref, CHUNK, WINDOW, k_pos0)
        o_ref[:, c * CHUNK:(c + 1) * CHUNK, :] = out.reshape(GQA_GROUP, CHUNK, LANES).astype(BF16)


def _attn_prompt(q4, k, v, sinks, batch, seq):
    steps = seq // ATT_ROWS
    per_tile = ROWS // ATT_ROWS
    win_blocks = ATT_ROWS // WINDOW
    n_tiles = batch * seq // ROWS
    cur = pl.BlockSpec((ATT_ROWS, KV_A), lambda b, j: (b * steps + j, 0))
    prev = pl.BlockSpec(
        (WINDOW, KV_A), lambda b, j: (b * steps * win_blocks + jnp.maximum(j * win_blocks - 1, 0), 0))
    qo = pl.BlockSpec((None, GQA_GROUP, ATT_ROWS, LANES),
                      lambda b, j: ((b * steps + j) // per_tile, 0, j % per_tile, 0))
    return pl.pallas_call(
        _attn_prompt_kernel,
        grid=(batch, steps),
        in_specs=[pl.BlockSpec(memory_space=pltpu.SMEM), qo, prev, cur, prev, cur],
        out_specs=qo,
        out_shape=jax.ShapeDtypeStruct((n_tiles, GQA_GROUP, ROWS, LANES), BF16),
        compiler_params=_cparams(("arbitrary", "arbitrary")),
        name="attn_prompt",
    )(sinks, q4, k, k, v, v)


def _attn_sample_kernel(sinks_ref, q_ref, kc_ref, kn_ref, vc_ref, vn_ref, o_ref):
    n_new = kn_ref.shape[0]
    n_cache = kc_ref.shape[0]
    q = q_ref[...].reshape(GQA_GROUP * n_new, LANES)
    k = jnp.concatenate([kc_ref[...], kn_ref[...]], axis=0)
    v = jnp.concatenate([vc_ref[...], vn_ref[...]], axis=0)
    out = _attend(q, k, v, sinks_ref, n_new, n_cache)
    o_ref[...] = out.reshape(GQA_GROUP, n_new, LANES).astype(BF16)


def _attn_sample(q4, k, v, cache_k, cache_v, sinks, tile, row0, dec_batch, dec_seq):
    n_cache = cache_k.shape[1]
    q_spec = pl.BlockSpec((None, GQA_GROUP, dec_seq, LANES), lambda b: (tile, 0, b, 0))
    new = pl.BlockSpec((dec_seq, KV_A), lambda b: (row0 // dec_seq + b, 0))
    cache = pl.BlockSpec((None, n_cache, KV_A), lambda b: (b, 0, 0))
    return pl.pallas_call(
        _attn_sample_kernel,
        grid=(dec_batch,),
        in_specs=[pl.BlockSpec(memory_space=pltpu.SMEM), q_spec, cache, new, cache, new],
        out_specs=pl.BlockSpec((None, GQA_GROUP, dec_seq, LANES), lambda b: (0, 0, b, 0)),
        out_shape=jax.ShapeDtypeStruct((1, GQA_GROUP, dec_batch * dec_seq, LANES), BF16),
        compiler_params=_cparams(("arbitrary",)),
        name="attn_sample",
    )(sinks, q4, cache_k, k, cache_v, v)


def _level_sizes(length):
    sizes = []
    m = length // 2
    while m >= 1:
        sizes.append(m)
        m //= 2
    return sizes


def _hgrn_consts(length):
    t = np.arange(length)
    tri = (t[None, :] <= t[:, None]).astype(np.float32)
    after = (t[None, :] > t[:, None]).astype(np.float32)
    levels = []
    masks = []
    s = np.tile(t, 2)[None, :]
    tt = t[:, None]
    for m in _level_sizes(length):
        blk = t // m
        same = blk[None, :] == blk[:, None]
        q_rows = (blk % 2 == 1)[:, None]
        sel = np.where(q_rows, same & (t[None, :] <= t[:, None]), same & (t[None, :] > t[:, None]))
        levels.append(sel.astype(np.float32))
        masks.append(((tt // m) % 2 == 1) & ((s // m) == (tt // m) - 1))
    masks.append(s == tt)
    masks.append(s <= tt)
    base = np.concatenate([tri, after], axis=0)
    lev = np.concatenate(levels, axis=0)
    return (np.concatenate([base] * 3, axis=1), np.concatenate([lev] * 3, axis=1),
            np.stack(masks).astype(np.int32))


def _stack2(x, m0, m1):
    xb = x.astype(BF16)
    return jnp.concatenate([xb * m0, xb * m1], axis=0)


def _hgrn_intra(qp, kp, b, masks, m0, m1, arg_ref, sl):
    length = qp.shape[0]
    n_lev = len(_level_sizes(length))
    if arg_ref is None:
        mid = length // 2 - 1
        rel = b - b[mid:mid + 1]
        a = _dot_nt((qp * jnp.exp(rel)).astype(BF16), _stack2(kp * jnp.exp(-rel), m0, m1))
        a = jnp.where(masks[n_lev + 1], a, 0.0)
    else:
        a = _dot_nt(qp.astype(BF16), _stack2(kp, m0, m1))
        a = jnp.where(masks[n_lev], a, 0.0)
        for lev in range(n_lev):
            e = jnp.exp(arg_ref[lev * length:(lev + 1) * length, sl])
            pr = _dot_nt((qp * e).astype(BF16), _stack2(kp * e, m0, m1))
            a = jnp.where(masks[lev], pr, a)
    return a.astype(BF16)


def _hgrn_finish(a, qp, kp, ip, gp, base, sl, st_ref, seq, p, m0, m1, same_head, bd2, onorm):
    length = qp.shape[0]
    e_b = jnp.exp(base[0:length, sl])
    e_k = jnp.exp(base[length:2 * length, sl])
    st = st_ref[seq, p]
    o = _dot(a, _stack2(ip, m0, m1)) + _dot_nt((qp * e_b).astype(BF16), st.astype(BF16))
    upd = _dot_tn(ip.astype(BF16), (kp * e_k).astype(BF16))
    st_ref[seq, p] = st * e_b[length - 1:length, :] + jnp.where(same_head, upd, 0.0)
    ms = _seg_mean(o * o, bd2)
    o = o * lax.rsqrt(ms + RMS_EPS) * onorm[:, sl] * (gp * _sigmoid(gp))
    return o.astype(BF16)


def _hgrn_kernel(n_seq, n_chunk, length, carry_state, *refs):
    if carry_state:
        (q_ref, k_ref, i_ref, g_ref, f_ref, selb_ref, sell_ref, mask_ref, bd2_ref, on_ref,
         o_ref, s_ref, st_ref, base_ref, arg_ref, a_ref) = refs

        @pl.when(pl.program_id(1) == 0)
        def _():
            st_ref[...] = jnp.zeros_like(st_ref)
    else:
        (q_ref, k_ref, i_ref, g_ref, f_ref, s0_ref, selb_ref, sell_ref, mask_ref, bd2_ref, on_ref,
         o_ref, s_ref, st_ref, base_ref, arg_ref, a_ref) = refs
        st_ref[...] = s0_ref[...]

    bd2 = bd2_ref[...]
    onorm = on_ref[...]
    n_items = n_seq * n_chunk
    mid = length // 2 - 1
    lane = lax.broadcasted_iota(I32, (length, LANES), 1)
    m0 = (lane < DK_B).astype(BF16)
    m1 = (lane >= DK_B).astype(BF16)
    rowp = lax.broadcasted_iota(I32, (LANES, LANES), 0) // DV_B
    colp = lax.broadcasted_iota(I32, (LANES, LANES), 1) // DK_B
    same_head = rowp == colp
    masks = [mask_ref[m] != 0 for m in range(mask_ref.shape[0])]

    def pieces(rows):
        return jnp.concatenate(_split3(f_ref[rows, :]), axis=0)

    def pair(ref, rows, p):
        return ref[rows, p * LANES:(p + 1) * LANES].astype(F32)

    span = jnp.float32(0.0)
    for it in range(n_items):
        base = _dot(selb_ref[...], pieces(slice(it * length, (it + 1) * length)))
        base_ref[it * 2 * length:(it + 1) * 2 * length, :] = base
        b_mid = base[mid:mid + 1]
        span = jnp.maximum(span, jnp.max(base[0:1] - b_mid))
        span = jnp.maximum(span, jnp.max(b_mid - base[length - 1:length]))
    safe = span <= HGRN_SPAN_LIMIT

    def finish(rows, base, seq, a_of):
        outs = []
        for p in range(N_PAIR):
            sl = slice(p * LANES, (p + 1) * LANES)
            outs.append(_hgrn_finish(a_of(p), pair(q_ref, rows, p), pair(k_ref, rows, p),
                                     pair(i_ref, rows, p), pair(g_ref, rows, p), base, sl,
                                     st_ref, seq, p, m0, m1, same_head, bd2, onorm))
        o_ref[rows, :] = jnp.concatenate(outs, axis=1)

    @pl.when(safe)
    def _():
        for it in range(n_items):
            rows = slice(it * length, (it + 1) * length)
            for p in range(N_PAIR):
                sl = slice(p * LANES, (p + 1) * LANES)
                b = base_ref[it * 2 * length:it * 2 * length + length, sl]
                a_ref[it * N_PAIR + p] = _hgrn_intra(pair(q_ref, rows, p), pair(k_ref, rows, p), b,
                                                     masks, m0, m1, None, sl)
        for it in range(n_items):
            rows = slice(it * length, (it + 1) * length)
            base = base_ref[it * 2 * length:(it + 1) * 2 * length, :]
            finish(rows, base, it // n_chunk, lambda p, it=it: a_ref[it * N_PAIR + p])

    @pl.when(jnp.logical_not(safe))
    def _():
        def body(it, carry):
            rows = pl.ds(pl.multiple_of(it * length, length), length)
            arg_ref[...] = _dot(sell_ref[...], pieces(rows))
            base = base_ref[pl.ds(pl.multiple_of(it * 2 * length, 2 * length), 2 * length), :]
            seq = it // n_chunk if n_chunk > 1 and n_seq > 1 else (it if n_chunk == 1 else 0)

            def a_of(p):
                sl = slice(p * LANES, (p + 1) * LANES)
                return _hgrn_intra(pair(q_ref, rows, p), pair(k_ref, rows, p), base[0:length, sl],
                                   masks, m0, m1, arg_ref, sl)

            finish(rows, base, seq, a_of)
            return carry

        lax.fori_loop(0, n_items, body, 0)

    if carry_state:
        @pl.when(pl.program_id(1) == pl.num_programs(1) - 1)
        def _():
            s_ref[...] = st_ref[...]
    else:
        s_ref[...] = st_ref[...]


def _hgrn_scratch(n_seq, n_items, length, sell):
    return [pltpu.VMEM((n_seq, N_PAIR, LANES, LANES), F32),
            pltpu.VMEM((n_items * 2 * length, W_B), F32),
            pltpu.VMEM((sell.shape[0], W_B), F32),
            pltpu.VMEM((n_items * N_PAIR, length, 2 * length), BF16)]


def _hgrn_prompt(qb, kb, ib, gb, logf, consts, bd2, onorm_t, batch, seq):
    selb, sell, masks = consts
    steps = seq // ATT_ROWS
    n_chunk = ATT_ROWS // CHUNK
    row = pl.BlockSpec((ATT_ROWS, W_B), lambda b, j: (b * steps + j, 0))
    whole = lambda a: pl.BlockSpec(a.shape, lambda b, j: (0,) * a.ndim)
    return pl.pallas_call(
        functools.partial(_hgrn_kernel, 1, n_chunk, CHUNK, True),
        grid=(batch, steps),
        in_specs=[row, row, row, row, row, whole(selb), whole(sell), whole(masks), whole(bd2),
                  whole(onorm_t)],
        out_specs=[row, pl.BlockSpec((None, 1, N_PAIR, LANES, LANES), lambda b, j: (b, 0, 0, 0, 0))],
        out_shape=[jax.ShapeDtypeStruct((batch * seq, V_B), BF16),
                   jax.ShapeDtypeStruct((batch, 1, N_PAIR, LANES, LANES), F32)],
        scratch_shapes=_hgrn_scratch(1, n_chunk, CHUNK, sell),
        compiler_params=_cparams(("arbitrary", "arbitrary")),
        name="hgrn_prompt",
    )(qb, kb, ib, gb, logf, selb, sell, masks, bd2, onorm_t)


def _hgrn_sample(qb, kb, ib, gb, logf, s0, consts, bd2, onorm_t, row0, dec_batch, dec_seq):
    selb, sell, masks = consts
    rows = SAMPLE_SEQS * dec_seq
    row = pl.BlockSpec((rows, W_B), lambda b: (row0 // rows + b, 0))
    whole = lambda a: pl.BlockSpec(a.shape, lambda b: (0,) * a.ndim)
    state = pl.BlockSpec((SAMPLE_SEQS, N_PAIR, LANES, LANES), lambda b: (b, 0, 0, 0))
    return pl.pallas_call(
        functools.partial(_hgrn_kernel, SAMPLE_SEQS, 1, dec_seq, False),
        grid=(dec_batch // SAMPLE_SEQS,),
        in_specs=[row, row, row, row, row, state, whole(selb), whole(sell), whole(masks),
                  whole(bd2), whole(onorm_t)],
        out_specs=[pl.BlockSpec((rows, V_B), lambda b: (b, 0)), state],
        out_shape=[jax.ShapeDtypeStruct((dec_batch * dec_seq, V_B), BF16),
                   jax.ShapeDtypeStruct((dec_batch, N_PAIR, LANES, LANES), F32)],
        scratch_shapes=_hgrn_scratch(SAMPLE_SEQS, SAMPLE_SEQS, dec_seq, sell),
        compiler_params=_cparams(("arbitrary",)),
        name="hgrn_sample",
    )(qb, kb, ib, gb, logf, s0, selb, sell, masks, bd2, onorm_t)


def _outproj_kernel(prompt_tiles, x_ref, oap_ref, obp_ref, oas_ref, obs_ref, g1_ref, sh_ref, sc_ref,
                    ln_ref, w_ref, wr_ref, rb_ref, x1_out, h2_out, cls_out, ga_out, gb_out):
    def mix_of(oa_ref, ob_ref):
        return jnp.concatenate([oa_ref[hh] for hh in range(GQA_GROUP)] + [ob_ref[...]], axis=1)

    is_prompt = pl.program_id(0) < prompt_tiles
    mix = jnp.where(is_prompt, mix_of(oap_ref, obp_ref), mix_of(oas_ref, obs_ref))
    att = _dot(mix, w_ref[...])
    x1 = _grouped(x_ref[...]) + g1_ref[...] * _grouped(att)
    x1_out[...] = x1.reshape(ROWS, D_MODEL)
    ms = jnp.mean(x1 * x1, axis=-1, keepdims=True)
    h = x1 * lax.rsqrt(ms + RMS_EPS) * ln_ref[...]
    h = (h * (1.0 + sc_ref[...]) + sh_ref[...]).reshape(ROWS, D_MODEL)

    for s in range(SUBLANES):
        h2_out[pl.ds(s, ROWS, stride=SUBLANES), :] = h[:, s * LANES:(s + 1) * LANES]

    h_hi, h_lo = _split2(h)
    w_hi, w_lo = _split2(wr_ref[...])
    logits = (_dot(h_hi, w_hi) + _dot(h_lo, w_hi) + _dot(h_hi, w_lo)).T[0:N_EXPERTS]
    aff = _sigmoid(logits)
    sel = aff + rb_ref[...]
    e_idx = lax.broadcasted_iota(I32, (N_EXPERTS, ROWS), 0)
    l_idx = lax.broadcasted_iota(I32, (EXPERTS_PER_GROUP, ROWS), 0)

    def first_argmax(vals, idx, big):
        top = jnp.max(vals, axis=0, keepdims=True)
        return top, jnp.min(jnp.where(vals == top, idx, big), axis=0, keepdims=True)

    g_scores = []
    for g in range(N_GROUPS):
        sg = sel[g * EXPERTS_PER_GROUP:(g + 1) * EXPERTS_PER_GROUP]
        m1, i1 = first_argmax(sg, l_idx, EXPERTS_PER_GROUP)
        m2 = jnp.max(jnp.where(l_idx == i1, -jnp.inf, sg), axis=0, keepdims=True)
        g_scores.append(m1 + m2)
    g_score = jnp.concatenate(g_scores, axis=0)
    g_idx = lax.broadcasted_iota(I32, (N_GROUPS, ROWS), 0)
    _, g_sel = first_argmax(g_score, g_idx, N_GROUPS)
    masked = jnp.where((e_idx // EXPERTS_PER_GROUP) == g_sel, sel, -jnp.inf)
    _, e1 = first_argmax(masked, e_idx, N_EXPERTS)
    _, e2 = first_argmax(jnp.where(e_idx == e1, -jnp.inf, masked), e_idx, N_EXPERTS)
    w1 = jnp.sum(jnp.where(e_idx == e1, aff, 0.0), axis=0, keepdims=True)
    w2 = jnp.sum(jnp.where(e_idx == e2, aff, 0.0), axis=0, keepdims=True)
    den = w1 + w2
    w1 = w1 / den
    w2 = w2 / den
    lo_first = e1 < e2
    gate_a = jnp.where(lo_first, w1, w2)
    gate_b = jnp.where(lo_first, w2, w1)
    a = jnp.minimum(e1, e2) - g_sel * EXPERTS_PER_GROUP
    b = jnp.maximum(e1, e2) - g_sel * EXPERTS_PER_GROUP
    base = jnp.where(a == 0, 0, jnp.where(a == 1, 3, 5))
    cls_out[...] = g_sel * PAIRS_PER_GROUP + base + (b - a - 1)
    ga_out[...] = gate_a
    gb_out[...] = gate_b


def _outproj(layer, x4, oap4, obp, oas4, obs, mod, ln2, w_out_b, w_router_p, router_bias_c,
             tiles_per_batch):
    n_tiles = x4.shape[0] // ROWS
    x_spec = pl.BlockSpec((ROWS, D_MODEL), lambda i: (i, 0))
    prompt_tiles = oap4.shape[0]
    last = prompt_tiles - 1
    lane_row = pl.BlockSpec((None, 1, ROWS), lambda i: (i, 0, 0))
    return pl.pallas_call(
        functools.partial(_outproj_kernel, prompt_tiles),
        grid=(n_tiles,),
        in_specs=[
            x_spec,
            pl.BlockSpec((None, GQA_GROUP, ROWS, LANES), lambda i: (jnp.minimum(i, last), 0, 0, 0)),
            pl.BlockSpec((ROWS, V_B), lambda i: (jnp.minimum(i, last), 0)),
            pl.BlockSpec((None, GQA_GROUP, ROWS, LANES), lambda i: (0, 0, 0, 0)),
            pl.BlockSpec((ROWS, V_B), lambda i: (0, 0)),
            _mod_spec(layer, 2, tiles_per_batch), _mod_spec(layer, 3, tiles_per_batch),
            _mod_spec(layer, 4, tiles_per_batch),
            pl.BlockSpec((None, 1, D_MODEL), lambda i: (layer, 0, 0)),
            pl.BlockSpec((None, Q_A + V_B, D_MODEL), lambda i: (layer, 0, 0)),
            pl.BlockSpec((D_MODEL, LANES), lambda i: (0, 0)),
            pl.BlockSpec((N_EXPERTS, 1), lambda i: (0, 0)),
        ],
        out_specs=[
            x_spec,
            pl.BlockSpec((ROWS * SUBLANES, LANES), lambda i: (i, 0)),
            lane_row, lane_row, lane_row,
        ],
        out_shape=[
            jax.ShapeDtypeStruct(x4.shape, F32),
            jax.ShapeDtypeStruct((n_tiles * ROWS * SUBLANES, LANES), F32),
            jax.ShapeDtypeStruct((n_tiles, 1, ROWS), I32),
            jax.ShapeDtypeStruct((n_tiles, 1, ROWS), F32),
            jax.ShapeDtypeStruct((n_tiles, 1, ROWS), F32),
        ],
        compiler_params=_cparams(("arbitrary",)),
        name=f"outproj{layer}",
    )(x4, oap4, obp, oas4, obs, mod, mod, mod, ln2, w_out_b, w_router_p, router_bias_c)


def _rank_kernel(cls_ref, upper_ref, lower_ref, dest_ref, count_ref, rank_ref):
    n_tiles = cls_ref.shape[0]
    c_idx = lax.broadcasted_iota(I32, (GRP, ROWS), 0)
    upper = upper_ref[...]

    def rank_body(i, carry):
        onehot = c_idx == cls_ref[i]
        within = _dot(onehot.astype(BF16), upper)
        rank_ref[i] = jnp.sum(jnp.where(onehot, within + carry, 0.0), axis=0, keepdims=True)
        return carry + jnp.sum(onehot.astype(F32), axis=1, keepdims=True)

    count = lax.fori_loop(0, n_tiles, rank_body, jnp.zeros((GRP, 1), F32))
    count_ref[...] = jnp.broadcast_to(count, (GRP, LANES)).astype(I32)
    tiles = jnp.floor((count + (TM - 1)) * (1.0 / TM))
    first_tile = _dot(lower_ref[...], jnp.broadcast_to(tiles, (GRP, LANES)).astype(BF16))
    start = first_tile[:, 0:1] * TM

    def dest_body(i, carry):
        onehot = c_idx == cls_ref[i]
        dest = rank_ref[i] + jnp.sum(jnp.where(onehot, start, 0.0), axis=0, keepdims=True)
        dest_ref[i] = dest.astype(I32)
        return carry

    lax.fori_loop(0, n_tiles, dest_body, 0)


def _plan_kernel(n_tok, n_t, dest_ref, count_ref, src_ref, ea_ref, eb_ref, nv_ref, nt_ref):
    tile = 0
    for c in range(N_CLASS):
        g, pi = divmod(c, PAIRS_PER_GROUP)
        a, b = PAIR_TABLE[pi]
        cnt = count_ref[c, 0]
        n_c = lax.shift_right_logical(cnt + (TM - 1), TM.bit_length() - 1)
        first = tile

        def mark(t, carry, g=g, a=a, b=b, cnt=cnt, first=first):
            ea_ref[t] = g * EXPERTS_PER_GROUP + a
            eb_ref[t] = g * EXPERTS_PER_GROUP + b
            nv_ref[t] = jnp.minimum(cnt - (t - first) * TM, TM)
            return carry

        lax.fori_loop(tile, tile + n_c, mark, 0)

        def pad(p, carry):
            src_ref[p] = 0
            return carry

        lax.fori_loop(first * TM + cnt, (first + n_c) * TM, pad, 0)
        tile = tile + n_c
    nt_ref[0] = tile

    def unused(t, carry):
        ea_ref[t] = 0
        eb_ref[t] = 0
        nv_ref[t] = 0
        return carry

    lax.fori_loop(tile, n_t, unused, 0)

    def unused_rows(p, carry):
        src_ref[p] = 0
        return carry

    lax.fori_loop(tile * TM, n_t * TM, unused_rows, 0)

    def place(t, carry):
        src_ref[dest_ref[t]] = t
        return carry

    lax.fori_loop(0, n_tok, place, 0, unroll=8)


def _route_plan(cls3, upper, lower):
    n_tiles = cls3.shape[0]
    n_tok = n_tiles * ROWS
    n_t = (n_tok + N_CLASS * (TM - 1)) // TM + 2
    dest, count = pl.pallas_call(
        _rank_kernel,
        out_shape=[jax.ShapeDtypeStruct((n_tiles, 1, ROWS), I32),
                   jax.ShapeDtypeStruct((GRP, LANES), I32)],
        scratch_shapes=[pltpu.VMEM((n_tiles, 1, ROWS), F32)],
        name="moe_rank",
    )(cls3, upper, lower)
    smem = pl.BlockSpec(memory_space=pltpu.SMEM)
    return pl.pallas_call(
        functools.partial(_plan_kernel, n_tok, n_t),
        in_specs=[smem, smem],
        out_specs=[smem, smem, smem, smem, smem],
        out_shape=[jax.ShapeDtypeStruct((n_t * TM,), I32),
                   jax.ShapeDtypeStruct((n_t,), I32),
                   jax.ShapeDtypeStruct((n_t,), I32),
                   jax.ShapeDtypeStruct((n_t,), I32),
                   jax.ShapeDtypeStruct((1,), I32)],
        name="moe_plan",
    )(dest.reshape(n_tok), count)


def _tile_rows(ref, slot, r):
    return ref.at[pl.ds(pl.multiple_of((slot * TM + r) * SUBLANES, SUBLANES), SUBLANES)]


def _moe_kernel(n_tok, src_ref, ea_ref, eb_ref, nv_ref, nt_ref, ga_ref, gb_ref, h2_hbm, wga_ref,
                wua_ref, wda_ref, wgb_ref, wub_ref, wdb_ref, y_hbm, xbuf, ybuf, gbuf, gsem, ssem):
    i = pl.program_id(0)
    n_used = nt_ref[0]
    lane = lax.broadcasted_iota(I32, (1, LANES), 1)

    def token_tile(hbm, tok):
        return hbm.at[pl.ds(pl.multiple_of(tok * SUBLANES, SUBLANES), SUBLANES)]

    def gather_row(tile, slot, r, priority):
        tok = src_ref[tile * TM + r]
        pltpu.make_async_copy(token_tile(h2_hbm, tok), _tile_rows(xbuf, slot, r),
                              gsem.at[slot]).start(priority=priority)
        gbuf[pl.ds(slot * TM + r, 1), :] = jnp.where(lane == 0, ga_ref[tok], gb_ref[tok])

    def scatter_row(tile, slot, r, n_valid, priority):
        tok = jnp.where(r < n_valid, src_ref[tile * TM + r], n_tok + slot * TM + r)
        pltpu.make_async_copy(_tile_rows(ybuf, slot, r), token_tile(y_hbm, tok),
                              ssem.at[slot]).start(priority=priority)

    def row_loop(start_row):
        def body(r8, c):
            for u in range(DMA_UNROLL):
                start_row(r8 * DMA_UNROLL + u, u % 2)
            return c
        lax.fori_loop(0, TM // DMA_UNROLL, body, 0)

    def wait_gather(slot):
        pltpu.make_async_copy(h2_hbm.at[pl.ds(0, TM * SUBLANES)],
                              xbuf.at[pl.ds(slot * TM * SUBLANES, TM * SUBLANES)],
                              gsem.at[slot]).wait()

    def wait_scatter(slot):
        pltpu.make_async_copy(ybuf.at[pl.ds(slot * TM * SUBLANES, TM * SUBLANES)],
                              y_hbm.at[pl.ds(0, TM * SUBLANES)], ssem.at[slot]).wait()

    @pl.when(i == 0)
    def _():
        ybuf[...] = jnp.zeros_like(ybuf)
        spare = pltpu.make_async_copy(
            ybuf, y_hbm.at[pl.ds(n_tok * SUBLANES, 2 * TM * SUBLANES)], ssem.at[0])
        spare.start()
        spare.wait()
        row_loop(lambda r, pr: gather_row(0, 0, r, pr))

    @pl.when(i < n_used)
    def _():
        slot = i % 2
        other = 1 - slot
        wait_gather(slot)
        nxt = jnp.minimum(i + 1, n_used - 1)
        prev = jnp.maximum(i - 1, 0)
        prev_valid = jnp.where(i >= 1, nv_ref[prev], 0)
        starts = []
        for r in range(TM):
            starts.append(functools.partial(gather_row, nxt, other, r, r % 2))
            starts.append(functools.partial(scatter_row, prev, other, r, prev_valid, r % 2))
        pieces = 1 + 2 * (2 * (D_EXPERT // MOE_PIECE) + 1 + D_MODEL // MOE_PIECE)
        per_piece = -(-len(starts) // pieces)

        def issue():
            for _ in range(min(per_piece, len(starts))):
                starts.pop(0)()

        base = slot * TM * SUBLANES
        xb = jnp.concatenate([_token_rows(xbuf, s, TM, base) for s in range(SUBLANES)],
                             axis=1).astype(BF16)
        gates = gbuf[pl.ds(slot * TM, TM), :]
        issue()
        y = [jnp.zeros((TM, MOE_PIECE), F32) for _ in range(D_MODEL // MOE_PIECE)]
        for e, (wg, wu, wd) in enumerate(((wga_ref, wua_ref, wda_ref), (wgb_ref, wub_ref, wdb_ref))):
            gt, up = [], []
            for c in range(D_EXPERT // MOE_PIECE):
                cols = slice(c * MOE_PIECE, (c + 1) * MOE_PIECE)
                gt.append(_dot(xb, wg[:, cols]))
                issue()
                up.append(_dot(xb, wu[:, cols]))
                issue()
            gt = jnp.concatenate(gt, axis=1)
            act = (gt * _sigmoid(gt) * jnp.concatenate(up, axis=1) * gates[:, e:e + 1]).astype(BF16)
            issue()
            for c in range(D_MODEL // MOE_PIECE):
                y[c] = y[c] + _dot(act, wd[:, c * MOE_PIECE:(c + 1) * MOE_PIECE])
                issue()
        while starts:
            issue()
        y = jnp.concatenate(y, axis=1)

        @pl.when(i >= 1)
        def _():
            wait_scatter(slot)

        for s in range(SUBLANES):
            ybuf[pl.ds(base + s, TM, stride=SUBLANES), :] = y[:, s * LANES:(s + 1) * LANES]

    @pl.when(jnp.logical_and(i == n_used, n_used > 0))
    def _():
        last_slot = (i - 1) % 2
        wait_gather(i % 2)
        wait_scatter(i % 2)
        last_valid = nv_ref[i - 1]
        row_loop(lambda r, pr: scatter_row(i - 1, last_slot, r, last_valid, pr))
        wait_scatter(last_slot)


def _moe(layer, h2, gate_a, gate_b, plan, wg_b, wu_b, wd_b):
    src, ea, eb, nv, nt = plan
    n_t = ea.shape[0]
    n_tok = h2.shape[0] // SUBLANES

    def w_spec(shape, which):
        def index(i, src_ref, ea_ref, eb_ref, nv_ref, nt_ref, ga_ref, gb_ref):
            t = jnp.minimum(i, jnp.maximum(nt_ref[0] - 1, 0))
            e = (ea_ref, eb_ref)[which][t]
            return (layer, e, 0, 0)
        return pl.BlockSpec((None, None) + shape, index)

    gu = (D_MODEL, D_EXPERT)
    dn = (D_EXPERT, D_MODEL)
    grid_spec = pltpu.PrefetchScalarGridSpec(
        num_scalar_prefetch=7,
        grid=(n_t,),
        in_specs=[pl.BlockSpec(memory_space=pl.ANY),
                  w_spec(gu, 0), w_spec(gu, 0), w_spec(dn, 0),
                  w_spec(gu, 1), w_spec(gu, 1), w_spec(dn, 1)],
        out_specs=pl.BlockSpec(memory_space=pl.ANY),
        scratch_shapes=[pltpu.VMEM((2 * TM * SUBLANES, LANES), F32),
                        pltpu.VMEM((2 * TM * SUBLANES, LANES), F32),
                        pltpu.VMEM((2 * TM, LANES), F32),
                        pltpu.SemaphoreType.DMA((2,)), pltpu.SemaphoreType.DMA((2,))],
    )
    return pl.pallas_call(
        functools.partial(_moe_kernel, n_tok),
        grid_spec=grid_spec,
        out_shape=jax.ShapeDtypeStruct(((n_tok + 2 * TM) * SUBLANES, LANES), F32),
        compiler_params=_cparams(("arbitrary",)),
        name=f"moe{layer}",
    )(src, ea, eb, nv, nt, gate_a.reshape(n_tok), gate_b.reshape(n_tok), h2,
      wg_b, wu_b, wd_b, wg_b, wu_b, wd_b)


def _pair_state_in(state):
    b = state.shape[0]
    st = jnp.swapaxes(state, -1, -2).reshape(b, N_PAIR, 2, DV_B, DK_B)
    z = jnp.zeros_like(st[:, :, 0])
    top = jnp.concatenate([st[:, :, 0], z], axis=-1)
    bot = jnp.concatenate([z, st[:, :, 1]], axis=-1)
    return jnp.concatenate([top, bot], axis=-2)


def _pair_state_out(st):
    b = st.shape[0]
    h0 = st[:, :, :DV_B, :DK_B]
    h1 = st[:, :, DV_B:, DK_B:]
    out = jnp.stack([h0, h1], axis=2).reshape(b, N_HEADS_B, DV_B, DK_B)
    return jnp.swapaxes(out, -1, -2)


def kernel(x_prompt, x_sample, cache_k, cache_v, state_hgrn, c_prompt, c_sample, w_ada, b_ada,
           ln1, ln2, w_in, q_norm, k_norm, sinks, lb_param, o_norm, w_out, w_router, router_bias,
           w_e_gate, w_e_up, w_e_down):
    batch, seq, _ = x_prompt.shape
    dec_batch, dec_seq, _ = x_sample.shape
    depth = w_ada.shape[0]
    assert dec_batch == GRP and dec_seq == SUB and seq % ROWS == 0
    tiles_per_batch = seq // ROWS
    n_prompt = batch * seq
    n_tok = n_prompt + dec_batch * dec_seq
    n_tiles = n_tok // ROWS
    prompt_tiles = n_prompt // ROWS
    n_cache = cache_k.shape[2]

    seg = np.arange(Q_A) // HEAD_DIM
    bd = jnp.asarray((seg[:, None] == seg[None, :]).astype(np.float32) / HEAD_DIM, BF16)
    bd2 = bd[:LANES, :LANES]

    def hgrn_consts(length):
        selb, sell, masks = _hgrn_consts(length)
        return jnp.asarray(selb, BF16), jnp.asarray(sell, BF16), jnp.asarray(masks)

    consts_p = hgrn_consts(CHUNK)
    consts_s = hgrn_consts(dec_seq)
    upper = jnp.asarray(np.triu(np.ones((ROWS, ROWS), np.float32), 1), BF16)
    lower = jnp.asarray(np.tril(np.ones((GRP, GRP), np.float32), -1), BF16)

    wq = w_in[:, :, :Q_A].reshape(depth, D_MODEL, N_KV_A, GQA_GROUP, HEAD_DIM)
    wq = jnp.swapaxes(wq, 2, 3).reshape(depth, D_MODEL, Q_A)
    w_in_b = jnp.concatenate([wq, w_in[:, :, Q_A:]], axis=-1).astype(BF16)
    wo = w_out[:, :Q_A].reshape(depth, N_KV_A, GQA_GROUP, HEAD_DIM, D_MODEL)
    wo = jnp.swapaxes(wo, 1, 2).reshape(depth, Q_A, D_MODEL)
    w_out_b = jnp.concatenate([wo, w_out[:, Q_A:]], axis=1).astype(BF16)
    wg_b, wu_b, wd_b = w_e_gate.astype(BF16), w_e_up.astype(BF16), w_e_down.astype(BF16)
    q_norm_t = jnp.tile(q_norm, (1, N_HEADS_A)).reshape(depth, 1, Q_A)
    k_norm_t = jnp.tile(k_norm, (1, N_KV_A)).reshape(depth, 1, KV_A)
    o_norm_t = jnp.tile(o_norm, (1, N_HEADS_B)).reshape(depth, 1, V_B)
    ln1_r = ln1.reshape(depth, 1, D_MODEL)
    ln2_r = ln2.reshape(depth, 1, D_MODEL)
    w_router_p = jnp.pad(w_router, ((0, 0), (0, LANES - N_EXPERTS)))
    router_bias_c = router_bias.reshape(N_EXPERTS, 1)

    c_exp = jnp.concatenate([jnp.repeat(c_prompt, GRP, axis=0), c_sample], axis=0)
    mod = _adaln(c_exp, w_ada, b_ada).reshape(depth, batch + 1, GRP, 1, N_MOD * D_MODEL)

    cache_k2 = cache_k.reshape(depth, dec_batch, n_cache, KV_A)
    cache_v2 = cache_v.reshape(depth, dec_batch, n_cache, KV_A)

    src = (x_prompt, x_sample)
    kp, vp, sp, kss, vss, sss = [], [], [], [], [], []
    for l in range(depth):
        x, q4, k, v, qb, logf, kb, ib, gb = _inproj(
            l, src, n_tiles, prompt_tiles, mod, ln1_r, w_in_b, q_norm_t, k_norm_t, lb_param, bd,
            tiles_per_batch)

        oap4 = _attn_prompt(q4, k, v, sinks[l], batch, seq)
        oas4 = _attn_sample(q4, k, v, cache_k2[l], cache_v2[l], sinks[l], prompt_tiles,
                            n_prompt, dec_batch, dec_seq)
        obp, s_p = _hgrn_prompt(qb, kb, ib, gb, logf, consts_p, bd2, o_norm_t[l], batch, seq)
        obs, s_s = _hgrn_sample(qb, kb, ib, gb, logf, _pair_state_in(state_hgrn[l]), consts_s,
                                bd2, o_norm_t[l], n_prompt, dec_batch, dec_seq)

        x1, h2, cls3, gate_a, gate_b = _outproj(l, x, oap4, obp, oas4, obs, mod, ln2_r, w_out_b,
                                                w_router_p, router_bias_c, tiles_per_batch)
        y = _moe(l, h2, gate_a, gate_b, _route_plan(cls3, upper, lower), wg_b, wu_b, wd_b)
        src = (x1, y)

        kp.append(k[:n_prompt].reshape(batch, seq, N_KV_A, HEAD_DIM)[:, -WINDOW:])
        vp.append(v[:n_prompt].reshape(batch, seq, N_KV_A, HEAD_DIM)[:, -WINDOW:])
        sp.append(_pair_state_out(s_p[:, 0]))
        kss.append(k[n_prompt:].reshape(dec_batch, dec_seq, N_KV_A, HEAD_DIM))
        vss.append(v[n_prompt:].reshape(dec_batch, dec_seq, N_KV_A, HEAD_DIM))
        sss.append(_pair_state_out(s_s))

    yp, ys = _final(depth - 1, src[0], src[1], mod, batch, seq, tiles_per_batch)
    return (yp, ys, jnp.stack(kp), jnp.stack(vp), jnp.stack(sp),
            jnp.stack(kss), jnp.stack(vss), jnp.stack(sss))
```

```python
import functools

import numpy as np
import jax
import jax.numpy as jnp
from jax import lax
from jax.experimental import pallas as pl
from jax.experimental.pallas import tpu as pltpu

F32 = jnp.float32
BF16 = jnp.bfloat16
I32 = jnp.int32

D_MODEL = 1024
HEAD_DIM = 64
N_HEADS_A = 8
N_KV_A = 2
GQA_GROUP = N_HEADS_A // N_KV_A
N_HEADS_B = 8
DK_B = 64
DV_B = 64
Q_A = N_HEADS_A * HEAD_DIM
KV_A = N_KV_A * HEAD_DIM
W_B = N_HEADS_B * DK_B
V_B = N_HEADS_B * DV_B
CHUNK = 64
WINDOW = 128
N_EXPERTS = 16
N_GROUPS = 4
EXPERTS_PER_GROUP = 4
D_EXPERT = D_MODEL // 2
RMS_EPS = 1e-6
NEG_INF = -1e30
N_MOD = 6

LANES = 128
SUBLANES = 8
ROWS = 512
GRP = 32
SUB = ROWS // GRP
ATT_ROWS = 256
N_PAIR = N_HEADS_B // 2
SAMPLE_SEQS = 4
HGRN_SPAN_LIMIT = 80.0
TM = 256
PAIRS_PER_GROUP = 6
PAIR_TABLE = ((0, 1), (0, 2), (0, 3), (1, 2), (1, 3), (2, 3))
N_CLASS = N_GROUPS * PAIRS_PER_GROUP
DMA_UNROLL = 8
VMEM_LIMIT = 56 * 1024 * 1024


def _cparams(sem):
    return pltpu.CompilerParams(dimension_semantics=sem, vmem_limit_bytes=VMEM_LIMIT)


def _dot(a, b):
    return jnp.dot(a, b, preferred_element_type=F32)


def _dot_nt(a, b):
    return lax.dot_general(a, b, (((1,), (1,)), ((), ())), preferred_element_type=F32)


def _dot_tn(a, b):
    return lax.dot_general(a, b, (((0,), (0,)), ((), ())), preferred_element_type=F32)


def _sigmoid(x):
    return 1.0 / (1.0 + jnp.exp(-x))


def _split2(x):
    hi = x.astype(BF16)
    lo = (x - hi.astype(F32)).astype(BF16)
    return hi, lo


def _split3(x):
    hi = x.astype(BF16)
    r = x - hi.astype(F32)
    mid = r.astype(BF16)
    lo = (r - mid.astype(F32)).astype(BF16)
    return hi, mid, lo


def _seg_mean(sq, bd):
    return _dot(sq.astype(BF16), bd)


def _token_rows(ref, s, n, base=0):
    return ref[pl.ds(base + s, n, stride=SUBLANES), :]


def _adaln_kernel(c_ref, w_ref, b_ref, o_ref):
    c = c_ref[...]
    a = c * _sigmoid(c)
    a_hi, a_lo = _split2(a)
    w_hi, w_lo = _split2(w_ref[...])
    acc = _dot(a_hi, w_hi) + _dot(a_lo, w_hi) + _dot(a_hi, w_lo)
    o_ref[...] = acc + b_ref[...]


def _adaln(c_exp, w_ada, b_ada):
    depth = w_ada.shape[0]
    rows = c_exp.shape[0]
    tn = D_MODEL
    return pl.pallas_call(
        _adaln_kernel,
        grid=(depth, N_MOD * D_MODEL // tn),
        in_specs=[
            pl.BlockSpec((rows, D_MODEL), lambda l, n: (0, 0)),
            pl.BlockSpec((None, D_MODEL, tn), lambda l, n: (l, 0, n)),
            pl.BlockSpec((None, 1, tn), lambda l, n: (l, 0, n)),
        ],
        out_specs=pl.BlockSpec((None, rows, tn), lambda l, n: (l, 0, n)),
        out_shape=jax.ShapeDtypeStruct((depth, rows, N_MOD * D_MODEL), F32),
        compiler_params=_cparams(("arbitrary", "arbitrary")),
        name="adaln",
    )(c_exp, w_ada, b_ada.reshape(depth, 1, N_MOD * D_MODEL))


def _mod_spec(layer, j, tiles_per_batch):
    return pl.BlockSpec((None, None, GRP, D_MODEL),
                        lambda i, *_: (layer, i // tiles_per_batch, 0, j))


def _by_group(fn, xs, mod_refs):
    outs = []
    for g in range(GRP):
        rows = slice(g * SUB, (g + 1) * SUB)
        outs.append(fn(*[x[rows] for x in xs], *[m[g:g + 1, :] for m in mod_refs]))
    return jnp.concatenate(outs, axis=0)


def _gated_add(x, gate_ref, y):
    return _by_group(lambda xg, yg, gg: xg + gg * yg, (x, y), (gate_ref,))


def _modulated_norm(x, ln_ref, sc_ref, sh_ref):
    ms = jnp.mean(x * x, axis=-1, keepdims=True)
    xn = x * lax.rsqrt(ms + RMS_EPS) * ln_ref[...]
    return _by_group(lambda xg, sc, sh: xg * (1.0 + sc) + sh, (xn,), (sc_ref, sh_ref))


def _prompt_tile_spec(prompt_tiles, tiles_per_batch):
    def index(i):
        t = jnp.minimum(i, prompt_tiles - 1)
        return (t // tiles_per_batch, t % tiles_per_batch, 0)
    return pl.BlockSpec((None, ROWS, D_MODEL), index)


def _moe_output_tile(y_ref):
    return jnp.concatenate([_token_rows(y_ref, s, ROWS) for s in range(SUBLANES)], axis=1)


def _inproj_kernel(layer, prompt_tiles, n_src, *refs):
    src = refs[:n_src]
    sh_ref, sc_ref, ln_ref, w_ref, qn_ref, kn_ref, lbp_ref, bd_ref = refs[n_src:n_src + 8]
    x_out, q_out, k_out, v_out, qb_out, logf_out, kb_out, ib_out, gb_out = refs[n_src + 8:]
    if layer == 0:
        xp_ref, xs_ref = src
        x = jnp.where(pl.program_id(0) < prompt_tiles, xp_ref[...],
                      xs_ref[...].reshape(ROWS, D_MODEL))
    else:
        x1_ref, y_ref, g2_ref = src
        x = _gated_add(x1_ref[...], g2_ref, _moe_output_tile(y_ref))
    x_out[...] = x
    hb = _modulated_norm(x, ln_ref, sc_ref, sh_ref).astype(BF16)
    bd = bd_ref[...]

    qa = _dot(hb, w_ref[:, 0:Q_A])
    qa = qa * lax.rsqrt(_seg_mean(qa * qa, bd) + RMS_EPS) * qn_ref[...]
    qa = (qa * (HEAD_DIM ** -0.5)).astype(BF16)
    for hh in range(GQA_GROUP):
        q_out[hh] = qa[:, hh * LANES:(hh + 1) * LANES]

    ka = _dot(hb, w_ref[:, Q_A:Q_A + KV_A])
    ka = ka * lax.rsqrt(_seg_mean(ka * ka, bd[:KV_A, :KV_A]) + RMS_EPS) * kn_ref[...]
    k_out[...] = ka
    v_out[...] = _dot(hb, w_ref[:, Q_A + KV_A:Q_A + 2 * KV_A])

    c0 = Q_A + 2 * KV_A
    qb_out[...] = (_dot(hb, w_ref[:, c0:c0 + W_B]) * (DK_B ** -0.5)).astype(BF16)

    p = lbp_ref[...]
    e = jnp.exp(p - jnp.max(p, axis=0, keepdims=True))
    sm = e / jnp.sum(e, axis=0, keepdims=True)
    cum0 = sm[0:1]
    cum = cum0
    for r in range(1, layer + 1):
        cum = cum + sm[r:r + 1]
    lb = cum - cum0

    z = _dot(hb, w_ref[:, c0 + W_B:c0 + 2 * W_B])
    pos = z >= 0.0
    t = jnp.exp(-jnp.abs(z))
    num = jnp.where(pos, jnp.log1p(lb * t), jnp.maximum(jnp.log(lb + t), z))
    logf_out[...] = num - jnp.log1p(t)
    kb_out[...] = ((1.0 - lb) * jnp.where(pos, t, 1.0) / (1.0 + t)).astype(BF16)

    c1 = c0 + 2 * W_B
    ib_out[...] = _dot(hb, w_ref[:, c1:c1 + V_B]).astype(BF16)
    gb_out[...] = _dot(hb, w_ref[:, c1 + V_B:c1 + 2 * V_B]).astype(BF16)


def _inproj(layer, src, n_tiles, prompt_tiles, mod, ln1, w_in_b, q_norm_t, k_norm_t, lb_param, bd,
            tiles_per_batch):
    depth = lb_param.shape[0]
    n = n_tiles * ROWS
    in_width = w_in_b.shape[-1]
    x_spec = pl.BlockSpec((ROWS, D_MODEL), lambda i: (i, 0))
    if layer == 0:
        src_specs = [_prompt_tile_spec(prompt_tiles, tiles_per_batch),
                     pl.BlockSpec((GRP, SUB, D_MODEL), lambda i: (0, 0, 0))]
        src_args = list(src)
    else:
        src_specs = [x_spec, pl.BlockSpec((ROWS * SUBLANES, LANES), lambda i: (i, 0)),
                     _mod_spec(layer - 1, 5, tiles_per_batch)]
        src_args = list(src) + [mod]
    row_spec = lambda w: pl.BlockSpec((ROWS, w), lambda i: (i, 0))
    return pl.pallas_call(
        functools.partial(_inproj_kernel, layer, prompt_tiles, len(src_specs)),
        grid=(n_tiles,),
        in_specs=src_specs + [
            _mod_spec(layer, 0, tiles_per_batch), _mod_spec(layer, 1, tiles_per_batch),
            pl.BlockSpec((None, 1, D_MODEL), lambda i: (layer, 0, 0)),
            pl.BlockSpec((None, D_MODEL, in_width), lambda i: (layer, 0, 0)),
            pl.BlockSpec((None, 1, Q_A), lambda i: (layer, 0, 0)),
            pl.BlockSpec((None, 1, KV_A), lambda i: (layer, 0, 0)),
            pl.BlockSpec((depth, W_B), lambda i: (0, 0)),
            pl.BlockSpec((Q_A, Q_A), lambda i: (0, 0)),
        ],
        out_specs=[
            x_spec,
            pl.BlockSpec((None, GQA_GROUP, ROWS, LANES), lambda i: (i, 0, 0, 0)),
            row_spec(KV_A), row_spec(KV_A),
            row_spec(W_B), row_spec(W_B), row_spec(W_B), row_spec(V_B), row_spec(V_B),
        ],
        out_shape=[
            jax.ShapeDtypeStruct((n, D_MODEL), F32),
            jax.ShapeDtypeStruct((n_tiles, GQA_GROUP, ROWS, LANES), BF16),
            jax.ShapeDtypeStruct((n, KV_A), F32),
            jax.ShapeDtypeStruct((n, KV_A), F32),
            jax.ShapeDtypeStruct((n, W_B), BF16),
            jax.ShapeDtypeStruct((n, W_B), F32),
            jax.ShapeDtypeStruct((n, W_B), BF16),
            jax.ShapeDtypeStruct((n, V_B), BF16),
            jax.ShapeDtypeStruct((n, V_B), BF16),
        ],
        compiler_params=_cparams(("arbitrary",)),
        name=f"inproj{layer}",
    )(*src_args, mod, mod, ln1, w_in_b, q_norm_t, k_norm_t, lb_param, bd)


def _final_kernel(prompt_tiles, x1_ref, y_ref, g2_ref, yp_ref, ys_ref):
    i = pl.program_id(0)
    x = _gated_add(x1_ref[...], g2_ref, _moe_output_tile(y_ref))

    @pl.when(i < prompt_tiles)
    def _():
        yp_ref[...] = x

    @pl.when(i >= prompt_tiles)
    def _():
        ys_ref[...] = x.reshape(GRP, SUB, D_MODEL)


def _final(layer, x1, y, mod, batch, seq, tiles_per_batch):
    n_tiles = x1.shape[0] // ROWS
    prompt_tiles = batch * tiles_per_batch
    return pl.pallas_call(
        functools.partial(_final_kernel, prompt_tiles),
        grid=(n_tiles,),
        in_specs=[pl.BlockSpec((ROWS, D_MODEL), lambda i: (i, 0)),
                  pl.BlockSpec((ROWS * SUBLANES, LANES), lambda i: (i, 0)),
                  _mod_spec(layer, 5, tiles_per_batch)],
        out_specs=[_prompt_tile_spec(prompt_tiles, tiles_per_batch),
                   pl.BlockSpec((GRP, SUB, D_MODEL), lambda i: (0, 0, 0))],
        out_shape=[jax.ShapeDtypeStruct((batch, seq, D_MODEL), F32),
                   jax.ShapeDtypeStruct((GRP, SUB, D_MODEL), F32)],
        compiler_params=_cparams(("arbitrary",)),
        name="final",
    )(x1, y, mod)


def _attend(q, k, v, sinks_ref, q_len, k_off, k_pos0=None):
    n_q = q.shape[0]
    n_k = k.shape[0]
    row = lax.broadcasted_iota(I32, (n_q, n_k), 0)
    col = lax.broadcasted_iota(I32, (n_q, n_k), 1)
    dist = jnp.abs((row % q_len) + k_off - col).astype(F32)
    lane = lax.broadcasted_iota(I32, (n_k, LANES), 1)
    hh_col = lax.broadcasted_iota(I32, (n_q, 1), 0) // q_len
    out = jnp.zeros((n_q, LANES), F32)
    for g in range(N_KV_A):
        in_g = (lane // HEAD_DIM) == g
        kg = jnp.where(in_g, k, 0.0).astype(BF16)
        vg = jnp.where(in_g, v, 0.0).astype(BF16)
        s = _dot_nt(q, kg)
        head = (g * GQA_GROUP + 1 + hh_col).astype(F32)
        s = s - jnp.exp2(-8.0 * head / N_HEADS_A) * dist
        if k_pos0 is not None:
            s = jnp.where(col + k_pos0 >= 0, s, NEG_INF)
        sink = jnp.zeros((n_q, 1), F32)
        for j in range(GQA_GROUP):
            sink = jnp.where(hh_col == j, sinks_ref[g * GQA_GROUP + j], sink)
        m = jnp.maximum(jnp.max(s, axis=-1, keepdims=True), sink)
        p = jnp.exp(s - m)
        denom = jnp.sum(p, axis=-1, keepdims=True) + jnp.exp(sink - m)
        out = out + _dot(p.astype(BF16), vg) / denom
    return out


def _attn_prompt_kernel(sinks_ref, q_ref, kp_ref, kc_ref, vp_ref, vc_ref, o_ref):
    j = pl.program_id(1)
    kcat = jnp.concatenate([kp_ref[...], kc_ref[...]], axis=0)
    vcat = jnp.concatenate([vp_ref[...], vc_ref[...]], axis=0)
    for c in range(ATT_ROWS // CHUNK):
        q = q_ref[:, c * CHUNK:(c + 1) * CHUNK, :].reshape(GQA_GROUP * CHUNK, LANES)
        k = kcat[c * CHUNK:c * CHUNK + WINDOW + CHUNK]
        v = vcat[c * CHUNK:c * CHUNK + WINDOW + CHUNK]
        k_pos0 = j * ATT_ROWS + c * CHUNK - WINDOW
        out = _attend(q, k, v, sinks_ref, CHUNK, WINDOW, k_pos0)
        o_ref[:, c * CHUNK:(c + 1) * CHUNK, :] = out.reshape(GQA_GROUP, CHUNK, LANES).astype(BF16)


def _attn_prompt(q4, k, v, sinks, batch, seq):
    steps = seq // ATT_ROWS
    per_tile = ROWS // ATT_ROWS
    win_blocks = ATT_ROWS // WINDOW
    n_tiles = batch * seq // ROWS
    cur = pl.BlockSpec((ATT_ROWS, KV_A), lambda b, j: (b * steps + j, 0))
    prev = pl.BlockSpec(
        (WINDOW, KV_A), lambda b, j: (b * steps * win_blocks + jnp.maximum(j * win_blocks - 1, 0), 0))
    qo = pl.BlockSpec((None, GQA_GROUP, ATT_ROWS, LANES),
                      lambda b, j: ((b * steps + j) // per_tile, 0, j % per_tile, 0))
    return pl.pallas_call(
        _attn_prompt_kernel,
        grid=(batch, steps),
        in_specs=[pl.BlockSpec(memory_space=pltpu.SMEM), qo, prev, cur, prev, cur],
        out_specs=qo,
        out_shape=jax.ShapeDtypeStruct((n_tiles, GQA_GROUP, ROWS, LANES), BF16),
        compiler_params=_cparams(("arbitrary", "arbitrary")),
        name="attn_prompt",
    )(sinks, q4, k, k, v, v)


def _attn_sample_kernel(sinks_ref, q_ref, kc_ref, kn_ref, vc_ref, vn_ref, o_ref):
    n_new = kn_ref.shape[0]
    n_cache = kc_ref.shape[0]
    q = q_ref[...].reshape(GQA_GROUP * n_new, LANES)
    k = jnp.concatenate([kc_ref[...], kn_ref[...]], axis=0)
    v = jnp.concatenate([vc_ref[...], vn_ref[...]], axis=0)
    out = _attend(q, k, v, sinks_ref, n_new, n_cache)
    o_ref[...] = out.reshape(GQA_GROUP, n_new, LANES).astype(BF16)


def _attn_sample(q4, k, v, cache_k, cache_v, sinks, tile, row0, dec_batch, dec_seq):
    n_cache = cache_k.shape[1]
    q_spec = pl.BlockSpec((None, GQA_GROUP, dec_seq, LANES), lambda b: (tile, 0, b, 0))
    new = pl.BlockSpec((dec_seq, KV_A), lambda b: (row0 // dec_seq + b, 0))
    cache = pl.BlockSpec((None, n_cache, KV_A), lambda b: (b, 0, 0))
    return pl.pallas_call(
        _attn_sample_kernel,
        grid=(dec_batch,),
        in_specs=[pl.BlockSpec(memory_space=pltpu.SMEM), q_spec, cache, new, cache, new],
        out_specs=pl.BlockSpec((None, GQA_GROUP, dec_seq, LANES), lambda b: (0, 0, b, 0)),
        out_shape=jax.ShapeDtypeStruct((1, GQA_GROUP, dec_batch * dec_seq, LANES), BF16),
        compiler_params=_cparams(("arbitrary",)),
        name="attn_sample",
    )(sinks, q4, cache_k, k, cache_v, v)


def _level_sizes(length):
    sizes = []
    m = length // 2
    while m >= 1:
        sizes.append(m)
        m //= 2
    return sizes


def _hgrn_consts(length):
    t = np.arange(length)
    tri = (t[None, :] <= t[:, None]).astype(np.float32)
    after = (t[None, :] > t[:, None]).astype(np.float32)
    levels = []
    masks = []
    s = np.tile(t, 2)[None, :]
    tt = t[:, None]
    for m in _level_sizes(length):
        blk = t // m
        same = blk[None, :] == blk[:, None]
        q_rows = (blk % 2 == 1)[:, None]
        sel = np.where(q_rows, same & (t[None, :] <= t[:, None]), same & (t[None, :] > t[:, None]))
        levels.append(sel.astype(np.float32))
        masks.append(((tt // m) % 2 == 1) & ((s // m) == (tt // m) - 1))
    masks.append(s == tt)
    masks.append(s <= tt)
    base = np.concatenate([tri, after], axis=0)
    lev = np.concatenate(levels, axis=0)
    return (np.concatenate([base] * 3, axis=1), np.concatenate([lev] * 3, axis=1),
            np.stack(masks).astype(np.int32))


def _stack2(x, m0, m1):
    xb = x.astype(BF16)
    return jnp.concatenate([xb * m0, xb * m1], axis=0)


def _hgrn_intra(qp, kp, b, masks, m0, m1, arg_ref, sl):
    length = qp.shape[0]
    n_lev = len(_level_sizes(length))
    if arg_ref is None:
        mid = length // 2 - 1
        rel = b - b[mid:mid + 1]
        a = _dot_nt((qp * jnp.exp(rel)).astype(BF16), _stack2(kp * jnp.exp(-rel), m0, m1))
        a = jnp.where(masks[n_lev + 1], a, 0.0)
    else:
        a = _dot_nt(qp.astype(BF16), _stack2(kp, m0, m1))
        a = jnp.where(masks[n_lev], a, 0.0)
        for lev in range(n_lev):
            e = jnp.exp(arg_ref[lev * length:(lev + 1) * length, sl])
            pr = _dot_nt((qp * e).astype(BF16), _stack2(kp * e, m0, m1))
            a = jnp.where(masks[lev], pr, a)
    return a.astype(BF16)


def _hgrn_finish(a, qp, kp, ip, base, sl, st_ref, seq, p, m0, m1, same_head):
    length = qp.shape[0]
    e_b = jnp.exp(base[0:length, sl])
    e_k = jnp.exp(base[length:2 * length, sl])
    st = st_ref[seq, p]
    o = _dot(a, _stack2(ip, m0, m1)) + _dot_nt((qp * e_b).astype(BF16), st.astype(BF16))
    upd = _dot_tn(ip.astype(BF16), (kp * e_k).astype(BF16))
    st_ref[seq, p] = st * e_b[length - 1:length, :] + jnp.where(same_head, upd, 0.0)
    return o


def _hgrn_kernel(n_seq, n_chunk, length, carry_state, *refs):
    if carry_state:
        (q_ref, k_ref, i_ref, g_ref, f_ref, selb_ref, sell_ref, mask_ref, bd_ref, on_ref,
         o_ref, s_ref, st_ref, base_ref, arg_ref) = refs

        @pl.when(pl.program_id(1) == 0)
        def _():
            st_ref[...] = jnp.zeros_like(st_ref)
    else:
        (q_ref, k_ref, i_ref, g_ref, f_ref, s0_ref, selb_ref, sell_ref, mask_ref, bd_ref, on_ref,
         o_ref, s_ref, st_ref, base_ref, arg_ref) = refs
        st_ref[...] = s0_ref[...]

    bd = bd_ref[...]
    onorm = on_ref[...]
    n_items = n_seq * n_chunk
    mid = length // 2 - 1
    lane = lax.broadcasted_iota(I32, (length, LANES), 1)
    m0 = (lane < DK_B).astype(BF16)
    m1 = (lane >= DK_B).astype(BF16)
    rowp = lax.broadcasted_iota(I32, (LANES, LANES), 0) // DV_B
    colp = lax.broadcasted_iota(I32, (LANES, LANES), 1) // DK_B
    same_head = rowp == colp
    masks = [mask_ref[m] != 0 for m in range(mask_ref.shape[0])]

    def pieces(rows):
        return jnp.concatenate(_split3(f_ref[rows, :]), axis=0)

    def pair(ref, rows, p):
        return ref[rows, p * LANES:(p + 1) * LANES].astype(F32)

    span = jnp.float32(0.0)
    for it in range(n_items):
        base = _dot(selb_ref[...], pieces(slice(it * length, (it + 1) * length)))
        base_ref[it * 2 * length:(it + 1) * 2 * length, :] = base
        b_mid = base[mid:mid + 1]
        span = jnp.maximum(span, jnp.max(base[0:1] - b_mid))
        span = jnp.maximum(span, jnp.max(b_mid - base[length - 1:length]))
    safe = span <= HGRN_SPAN_LIMIT

    def finish(rows, base, seq, a_of):
        outs = []
        for p in range(N_PAIR):
            sl = slice(p * LANES, (p + 1) * LANES)
            outs.append(_hgrn_finish(a_of(p), pair(q_ref, rows, p), pair(k_ref, rows, p),
                                     pair(i_ref, rows, p), base, sl, st_ref, seq, p, m0, m1,
                                     same_head))
        o = jnp.concatenate(outs, axis=1)
        g = g_ref[rows, :].astype(F32)
        o = o * lax.rsqrt(_seg_mean(o * o, bd) + RMS_EPS) * onorm * (g * _sigmoid(g))
        o_ref[rows, :] = o.astype(BF16)

    @pl.when(safe)
    def _():
        for it in range(n_items):
            rows = slice(it * length, (it + 1) * length)
            base = base_ref[it * 2 * length:(it + 1) * 2 * length, :]

            def a_of(p, rows=rows, base=base):
                sl = slice(p * LANES, (p + 1) * LANES)
                return _hgrn_intra(pair(q_ref, rows, p), pair(k_ref, rows, p), base[0:length, sl],
                                   masks, m0, m1, None, sl)

            finish(rows, base, it // n_chunk, a_of)

    @pl.when(jnp.logical_not(safe))
    def _():
        def body(it, carry):
            rows = pl.ds(pl.multiple_of(it * length, length), length)
            arg_ref[...] = _dot(sell_ref[...], pieces(rows))
            base = base_ref[pl.ds(pl.multiple_of(it * 2 * length, 2 * length), 2 * length), :]
            seq = it // n_chunk if n_chunk > 1 and n_seq > 1 else (it if n_chunk == 1 else 0)

            def a_of(p):
                sl = slice(p * LANES, (p + 1) * LANES)
                return _hgrn_intra(pair(q_ref, rows, p), pair(k_ref, rows, p), base[0:length, sl],
                                   masks, m0, m1, arg_ref, sl)

            finish(rows, base, seq, a_of)
            return carry

        lax.fori_loop(0, n_items, body, 0)

    if carry_state:
        @pl.when(pl.program_id(1) == pl.num_programs(1) - 1)
        def _():
            s_ref[...] = st_ref[...]
    else:
        s_ref[...] = st_ref[...]


def _hgrn_scratch(n_seq, n_items, length, sell):
    return [pltpu.VMEM((n_seq, N_PAIR, LANES, LANES), F32),
            pltpu.VMEM((n_items * 2 * length, W_B), F32),
            pltpu.VMEM((sell.shape[0], W_B), F32)]


def _hgrn_prompt(qb, kb, ib, gb, logf, consts, bd, onorm_t, batch, seq):
    selb, sell, masks = consts
    steps = seq // ATT_ROWS
    n_chunk = ATT_ROWS // CHUNK
    row = pl.BlockSpec((ATT_ROWS, W_B), lambda b, j: (b * steps + j, 0))
    whole = lambda a: pl.BlockSpec(a.shape, lambda b, j: (0,) * a.ndim)
    return pl.pallas_call(
        functools.partial(_hgrn_kernel, 1, n_chunk, CHUNK, True),
        grid=(batch, steps),
        in_specs=[row, row, row, row, row, whole(selb), whole(sell), whole(masks), whole(bd),
                  whole(onorm_t)],
        out_specs=[row, pl.BlockSpec((None, 1, N_PAIR, LANES, LANES), lambda b, j: (b, 0, 0, 0, 0))],
        out_shape=[jax.ShapeDtypeStruct((batch * seq, V_B), BF16),
                   jax.ShapeDtypeStruct((batch, 1, N_PAIR, LANES, LANES), F32)],
        scratch_shapes=_hgrn_scratch(1, n_chunk, CHUNK, sell),
        compiler_params=_cparams(("arbitrary", "arbitrary")),
        name="hgrn_prompt",
    )(qb, kb, ib, gb, logf, selb, sell, masks, bd, onorm_t)


def _hgrn_sample(qb, kb, ib, gb, logf, s0, consts, bd, onorm_t, row0, dec_batch, dec_seq):
    selb, sell, masks = consts
    rows = SAMPLE_SEQS * dec_seq
    row = pl.BlockSpec((rows, W_B), lambda b: (row0 // rows + b, 0))
    whole = lambda a: pl.BlockSpec(a.shape, lambda b: (0,) * a.ndim)
    state = pl.BlockSpec((SAMPLE_SEQS, N_PAIR, LANES, LANES), lambda b: (b, 0, 0, 0))
    return pl.pallas_call(
        functools.partial(_hgrn_kernel, SAMPLE_SEQS, 1, dec_seq, False),
        grid=(dec_batch // SAMPLE_SEQS,),
        in_specs=[row, row, row, row, row, state, whole(selb), whole(sell), whole(masks),
                  whole(bd), whole(onorm_t)],
        out_specs=[pl.BlockSpec((rows, V_B), lambda b: (b, 0)), state],
        out_shape=[jax.ShapeDtypeStruct((dec_batch * dec_seq, V_B), BF16),
                   jax.ShapeDtypeStruct((dec_batch, N_PAIR, LANES, LANES), F32)],
        scratch_shapes=_hgrn_scratch(SAMPLE_SEQS, SAMPLE_SEQS, dec_seq, sell),
        compiler_params=_cparams(("arbitrary",)),
        name="hgrn_sample",
    )(qb, kb, ib, gb, logf, s0, selb, sell, masks, bd, onorm_t)


def _outproj_kernel(prompt_tiles, x_ref, oap_ref, obp_ref, oas_ref, obs_ref, g1_ref, sh_ref, sc_ref,
                    ln_ref, w_ref, wr_ref, rb_ref, x1_out, h2_out, cls_out, ga_out, gb_out):
    def mix_of(oa_ref, ob_ref):
        return jnp.concatenate([oa_ref[hh] for hh in range(GQA_GROUP)] + [ob_ref[...]], axis=1)

    is_prompt = pl.program_id(0) < prompt_tiles
    mix = jnp.where(is_prompt, mix_of(oap_ref, obp_ref), mix_of(oas_ref, obs_ref))
    att = _dot(mix, w_ref[...])
    x1 = _gated_add(x_ref[...], g1_ref, att)
    x1_out[...] = x1
    h = _modulated_norm(x1, ln_ref, sc_ref, sh_ref)

    for s in range(SUBLANES):
        h2_out[pl.ds(s, ROWS, stride=SUBLANES), :] = h[:, s * LANES:(s + 1) * LANES]

    h_hi, h_lo = _split2(h)
    w_hi, w_lo = _split2(wr_ref[...])
    hi_terms = _dot_nt(jnp.concatenate([w_hi, w_lo], axis=0), h_hi)
    logits = hi_terms[0:N_EXPERTS] + hi_terms[N_EXPERTS:] + _dot_nt(w_hi, h_lo)
    aff = _sigmoid(logits)
    sel = aff + rb_ref[...]
    e_idx = lax.broadcasted_iota(I32, (N_EXPERTS, ROWS), 0)
    l_idx = lax.broadcasted_iota(I32, (EXPERTS_PER_GROUP, ROWS), 0)

    def first_argmax(vals, idx, big):
        top = jnp.max(vals, axis=0, keepdims=True)
        return top, jnp.min(jnp.where(vals == top, idx, big), axis=0, keepdims=True)

    g_scores = []
    for g in range(N_GROUPS):
        sg = sel[g * EXPERTS_PER_GROUP:(g + 1) * EXPERTS_PER_GROUP]
        m1, i1 = first_argmax(sg, l_idx, EXPERTS_PER_GROUP)
        m2 = jnp.max(jnp.where(l_idx == i1, -jnp.inf, sg), axis=0, keepdims=True)
        g_scores.append(m1 + m2)
    g_score = jnp.concatenate(g_scores, axis=0)
    g_idx = lax.broadcasted_iota(I32, (N_GROUPS, ROWS), 0)
    _, g_sel = first_argmax(g_score, g_idx, N_GROUPS)
    masked = jnp.where((e_idx // EXPERTS_PER_GROUP) == g_sel, sel, -jnp.inf)
    _, e1 = first_argmax(masked, e_idx, N_EXPERTS)
    _, e2 = first_argmax(jnp.where(e_idx == e1, -jnp.inf, masked), e_idx, N_EXPERTS)
    w1 = jnp.sum(jnp.where(e_idx == e1, aff, 0.0), axis=0, keepdims=True)
    w2 = jnp.sum(jnp.where(e_idx == e2, aff, 0.0), axis=0, keepdims=True)
    den = w1 + w2
    w1 = w1 / den
    w2 = w2 / den
    lo_first = e1 < e2
    gate_a = jnp.where(lo_first, w1, w2)
    gate_b = jnp.where(lo_first, w2, w1)
    a = jnp.minimum(e1, e2) - g_sel * EXPERTS_PER_GROUP
    b = jnp.maximum(e1, e2) - g_sel * EXPERTS_PER_GROUP
    base = jnp.where(a == 0, 0, jnp.where(a == 1, 3, 5))
    cls_out[...] = g_sel * PAIRS_PER_GROUP + base + (b - a - 1)
    ga_out[...] = gate_a
    gb_out[...] = gate_b


def _outproj(layer, x4, oap4, obp, oas4, obs, mod, ln2, w_out_b, w_router_t, router_bias_c,
             tiles_per_batch):
    n_tiles = x4.shape[0] // ROWS
    x_spec = pl.BlockSpec((ROWS, D_MODEL), lambda i: (i, 0))
    prompt_tiles = oap4.shape[0]
    last = prompt_tiles - 1
    lane_row = pl.BlockSpec((None, 1, ROWS), lambda i: (i, 0, 0))
    return pl.pallas_call(
        functools.partial(_outproj_kernel, prompt_tiles),
        grid=(n_tiles,),
        in_specs=[
            x_spec,
            pl.BlockSpec((None, GQA_GROUP, ROWS, LANES), lambda i: (jnp.minimum(i, last), 0, 0, 0)),
            pl.BlockSpec((ROWS, V_B), lambda i: (jnp.minimum(i, last), 0)),
            pl.BlockSpec((None, GQA_GROUP, ROWS, LANES), lambda i: (0, 0, 0, 0)),
            pl.BlockSpec((ROWS, V_B), lambda i: (0, 0)),
            _mod_spec(layer, 2, tiles_per_batch), _mod_spec(layer, 3, tiles_per_batch),
            _mod_spec(layer, 4, tiles_per_batch),
            pl.BlockSpec((None, 1, D_MODEL), lambda i: (layer, 0, 0)),
            pl.BlockSpec((None, Q_A + V_B, D_MODEL), lambda i: (layer, 0, 0)),
            pl.BlockSpec((N_EXPERTS, D_MODEL), lambda i: (0, 0)),
            pl.BlockSpec((N_EXPERTS, 1), lambda i: (0, 0)),
        ],
        out_specs=[
            x_spec,
            pl.BlockSpec((ROWS * SUBLANES, LANES), lambda i: (i, 0)),
            lane_row, lane_row, lane_row,
        ],
        out_shape=[
            jax.ShapeDtypeStruct(x4.shape, F32),
            jax.ShapeDtypeStruct((n_tiles * ROWS * SUBLANES, LANES), F32),
            jax.ShapeDtypeStruct((n_tiles, 1, ROWS), I32),
            jax.ShapeDtypeStruct((n_tiles, 1, ROWS), F32),
            jax.ShapeDtypeStruct((n_tiles, 1, ROWS), F32),
        ],
        compiler_params=_cparams(("arbitrary",)),
        name=f"outproj{layer}",
    )(x4, oap4, obp, oas4, obs, mod, mod, mod, ln2, w_out_b, w_router_t, router_bias_c)


def _rank_kernel(cls_ref, upper_ref, lower_ref, dest_ref, count_ref, rank_ref):
    n_tiles = cls_ref.shape[0]
    c_idx = lax.broadcasted_iota(I32, (GRP, ROWS), 0)
    upper = upper_ref[...]

    def rank_body(i, carry):
        onehot = c_idx == cls_ref[i]
        within = _dot(onehot.astype(BF16), upper)
        rank_ref[i] = jnp.sum(jnp.where(onehot, within + carry, 0.0), axis=0, keepdims=True)
        return carry + jnp.sum(onehot.astype(F32), axis=1, keepdims=True)

    count = lax.fori_loop(0, n_tiles, rank_body, jnp.zeros((GRP, 1), F32))
    count_ref[...] = jnp.broadcast_to(count, (GRP, LANES)).astype(I32)
    tiles = jnp.floor((count + (TM - 1)) * (1.0 / TM))
    first_tile = _dot(lower_ref[...], jnp.broadcast_to(tiles, (GRP, LANES)).astype(BF16))
    start = first_tile[:, 0:1] * TM

    def dest_body(i, carry):
        onehot = c_idx == cls_ref[i]
        dest = rank_ref[i] + jnp.sum(jnp.where(onehot, start, 0.0), axis=0, keepdims=True)
        dest_ref[i] = dest.astype(I32)
        return carry

    lax.fori_loop(0, n_tiles, dest_body, 0)


def _plan_kernel(n_tok, n_t, dest_ref, count_ref, src_ref, ea_ref, eb_ref, nv_ref, nt_ref):
    tile = 0
    for c in range(N_CLASS):
        g, pi = divmod(c, PAIRS_PER_GROUP)
        a, b = PAIR_TABLE[pi]
        cnt = count_ref[c, 0]
        n_c = lax.shift_right_logical(cnt + (TM - 1), TM.bit_length() - 1)
        first = tile

        def mark(t, carry, g=g, a=a, b=b, cnt=cnt, first=first):
            ea_ref[t] = g * EXPERTS_PER_GROUP + a
            eb_ref[t] = g * EXPERTS_PER_GROUP + b
            nv_ref[t] = jnp.minimum(cnt - (t - first) * TM, TM)
            return carry

        lax.fori_loop(tile, tile + n_c, mark, 0)

        def pad(p, carry):
            src_ref[p] = 0
            return carry

        lax.fori_loop(first * TM + cnt, (first + n_c) * TM, pad, 0)
        tile = tile + n_c
    nt_ref[0] = tile

    def unused(t, carry):
        ea_ref[t] = 0
        eb_ref[t] = 0
        nv_ref[t] = 0
        return carry

    lax.fori_loop(tile, n_t, unused, 0)

    def unused_rows(p, carry):
        src_ref[p] = 0
        return carry

    lax.fori_loop(tile * TM, n_t * TM, unused_rows, 0)

    def place(t, carry):
        src_ref[dest_ref[t]] = t
        return carry

    lax.fori_loop(0, n_tok, place, 0, unroll=8)


def _route_plan(cls3, upper, lower):
    n_tiles = cls3.shape[0]
    n_tok = n_tiles * ROWS
    n_t = (n_tok + N_CLASS * (TM - 1)) // TM + 2
    dest, count = pl.pallas_call(
        _rank_kernel,
        out_shape=[jax.ShapeDtypeStruct((n_tiles, 1, ROWS), I32),
                   jax.ShapeDtypeStruct((GRP, LANES), I32)],
        scratch_shapes=[pltpu.VMEM((n_tiles, 1, ROWS), F32)],
        name="moe_rank",
    )(cls3, upper, lower)
    smem = pl.BlockSpec(memory_space=pltpu.SMEM)
    return pl.pallas_call(
        functools.partial(_plan_kernel, n_tok, n_t),
        in_specs=[smem, smem],
        out_specs=[smem, smem, smem, smem, smem],
        out_shape=[jax.ShapeDtypeStruct((n_t * TM,), I32),
                   jax.ShapeDtypeStruct((n_t,), I32),
                   jax.ShapeDtypeStruct((n_t,), I32),
                   jax.ShapeDtypeStruct((n_t,), I32),
                   jax.ShapeDtypeStruct((1,), I32)],
        name="moe_plan",
    )(dest.reshape(n_tok), count)


def _tile_rows(ref, slot, r):
    return ref.at[pl.ds(pl.multiple_of((slot * TM + r) * SUBLANES, SUBLANES), SUBLANES)]


def _moe_kernel(n_tok, src_ref, ea_ref, eb_ref, nv_ref, nt_ref, ga_ref, gb_ref, h2_hbm, wga_ref,
                wua_ref, wda_ref, wgb_ref, wub_ref, wdb_ref, y_hbm, xbuf, ybuf, gbuf, gsem, ssem):
    i = pl.program_id(0)
    n_used = nt_ref[0]
    lane = lax.broadcasted_iota(I32, (1, LANES), 1)

    def token_tile(hbm, tok):
        return hbm.at[pl.ds(pl.multiple_of(tok * SUBLANES, SUBLANES), SUBLANES)]

    def gather_row(tile, slot, r, priority):
        tok = src_ref[tile * TM + r]
        pltpu.make_async_copy(token_tile(h2_hbm, tok), _tile_rows(xbuf, slot, r),
                              gsem.at[slot]).start(priority=priority)
        gbuf[pl.ds(slot * TM + r, 1), :] = jnp.where(lane == 0, ga_ref[tok], gb_ref[tok])

    def scatter_row(tile, slot, r, n_valid, priority):
        tok = jnp.where(r < n_valid, src_ref[tile * TM + r], n_tok + slot * TM + r)
        pltpu.make_async_copy(_tile_rows(ybuf, slot, r), token_tile(y_hbm, tok),
                              ssem.at[slot]).start(priority=priority)

    def row_loop(start_row):
        def body(r8, c):
            for u in range(DMA_UNROLL):
                start_row(r8 * DMA_UNROLL + u, u % 2)
            return c
        lax.fori_loop(0, TM // DMA_UNROLL, body, 0)

    def wait_gather(slot):
        pltpu.make_async_copy(h2_hbm.at[pl.ds(0, TM * SUBLANES)],
                              xbuf.at[pl.ds(slot * TM * SUBLANES, TM * SUBLANES)],
                              gsem.at[slot]).wait()

    def wait_scatter(slot):
        pltpu.make_async_copy(ybuf.at[pl.ds(slot * TM * SUBLANES, TM * SUBLANES)],
                              y_hbm.at[pl.ds(0, TM * SUBLANES)], ssem.at[slot]).wait()

    @pl.when(i == 0)
    def _():
        ybuf[...] = jnp.zeros_like(ybuf)
        spare = pltpu.make_async_copy(
            ybuf, y_hbm.at[pl.ds(n_tok * SUBLANES, 2 * TM * SUBLANES)], ssem.at[0])
        spare.start()
        spare.wait()

    @pl.when(jnp.logical_and(i == 0, n_used > 0))
    def _():
        row_loop(lambda r, pr: gather_row(0, 0, r, pr))

    @pl.when(i + 1 < n_used)
    def _():
        row_loop(lambda r, pr: gather_row(i + 1, (i + 1) % 2, r, pr))

    @pl.when(jnp.logical_and(i >= 2, i - 2 < n_used))
    def _():
        wait_scatter(i % 2)

    @pl.when(i < n_used)
    def _():
        slot = i % 2
        wait_gather(slot)
        base = slot * TM * SUBLANES
        xb = jnp.concatenate([_token_rows(xbuf, s, TM, base) for s in range(SUBLANES)],
                             axis=1).astype(BF16)
        gates = gbuf[pl.ds(slot * TM, TM), :]
        y = jnp.zeros((TM, D_MODEL), F32)
        for e, (wg, wu, wd) in enumerate(((wga_ref, wua_ref, wda_ref), (wgb_ref, wub_ref, wdb_ref))):
            gt = _dot(xb, wg[...])
            up = _dot(xb, wu[...])
            act = gt * _sigmoid(gt) * up * gates[:, e:e + 1]
            y = y + _dot(act.astype(BF16), wd[...])
        for s in range(SUBLANES):
            ybuf[pl.ds(base + s, TM, stride=SUBLANES), :] = y[:, s * LANES:(s + 1) * LANES]
        n_valid = nv_ref[i]
        row_loop(lambda r, pr: scatter_row(i, slot, r, n_valid, pr))


def _moe(layer, h2, gate_a, gate_b, plan, wg_b, wu_b, wd_b):
    src, ea, eb, nv, nt = plan
    n_t = ea.shape[0]
    n_tok = h2.shape[0] // SUBLANES

    def w_spec(shape, which):
        def index(i, src_ref, ea_ref, eb_ref, nv_ref, nt_ref, ga_ref, gb_ref):
            t = jnp.minimum(i, jnp.maximum(nt_ref[0] - 1, 0))
            e = (ea_ref, eb_ref)[which][t]
            return (layer, e, 0, 0)
        return pl.BlockSpec((None, None) + shape, index)

    gu = (D_MODEL, D_EXPERT)
    dn = (D_EXPERT, D_MODEL)
    grid_spec = pltpu.PrefetchScalarGridSpec(
        num_scalar_prefetch=7,
        grid=(n_t,),
        in_specs=[pl.BlockSpec(memory_space=pl.ANY),
                  w_spec(gu, 0), w_spec(gu, 0), w_spec(dn, 0),
                  w_spec(gu, 1), w_spec(gu, 1), w_spec(dn, 1)],
        out_specs=pl.BlockSpec(memory_space=pl.ANY),
        scratch_shapes=[pltpu.VMEM((2 * TM * SUBLANES, LANES), F32),
                        pltpu.VMEM((2 * TM * SUBLANES, LANES), F32),
                        pltpu.VMEM((2 * TM, LANES), F32),
                        pltpu.SemaphoreType.DMA((2,)), pltpu.SemaphoreType.DMA((2,))],
    )
    return pl.pallas_call(
        functools.partial(_moe_kernel, n_tok),
        grid_spec=grid_spec,
        out_shape=jax.ShapeDtypeStruct(((n_tok + 2 * TM) * SUBLANES, LANES), F32),
        compiler_params=_cparams(("arbitrary",)),
        name=f"moe{layer}",
    )(src, ea, eb, nv, nt, gate_a.reshape(n_tok), gate_b.reshape(n_tok), h2,
      wg_b, wu_b, wd_b, wg_b, wu_b, wd_b)


def _pair_state_in(state):
    b = state.shape[0]
    st = jnp.swapaxes(state, -1, -2).reshape(b, N_PAIR, 2, DV_B, DK_B)
    z = jnp.zeros_like(st[:, :, 0])
    top = jnp.concatenate([st[:, :, 0], z], axis=-1)
    bot = jnp.concatenate([z, st[:, :, 1]], axis=-1)
    return jnp.concatenate([top, bot], axis=-2)


def _pair_state_out(st):
    b = st.shape[0]
    h0 = st[:, :, :DV_B, :DK_B]
    h1 = st[:, :, DV_B:, DK_B:]
    out = jnp.stack([h0, h1], axis=2).reshape(b, N_HEADS_B, DV_B, DK_B)
    return jnp.swapaxes(out, -1, -2)


def kernel(x_prompt, x_sample, cache_k, cache_v, state_hgrn, c_prompt, c_sample, w_ada, b_ada,
           ln1, ln2, w_in, q_norm, k_norm, sinks, lb_param, o_norm, w_out, w_router, router_bias,
           w_e_gate, w_e_up, w_e_down):
    batch, seq, _ = x_prompt.shape
    dec_batch, dec_seq, _ = x_sample.shape
    depth = w_ada.shape[0]
    assert dec_batch == GRP and dec_seq == SUB and seq % ROWS == 0
    tiles_per_batch = seq // ROWS
    n_prompt = batch * seq
    n_tok = n_prompt + dec_batch * dec_seq
    n_tiles = n_tok // ROWS
    prompt_tiles = n_prompt // ROWS
    n_cache = cache_k.shape[2]

    seg = np.arange(Q_A) // HEAD_DIM
    bd = jnp.asarray((seg[:, None] == seg[None, :]).astype(np.float32) / HEAD_DIM, BF16)

    def hgrn_consts(length):
        selb, sell, masks = _hgrn_consts(length)
        return jnp.asarray(selb, BF16), jnp.asarray(sell, BF16), jnp.asarray(masks)

    consts_p = hgrn_consts(CHUNK)
    consts_s = hgrn_consts(dec_seq)
    upper = jnp.asarray(np.triu(np.ones((ROWS, ROWS), np.float32), 1), BF16)
    lower = jnp.asarray(np.tril(np.ones((GRP, GRP), np.float32), -1), BF16)

    wq = w_in[:, :, :Q_A].reshape(depth, D_MODEL, N_KV_A, GQA_GROUP, HEAD_DIM)
    wq = jnp.swapaxes(wq, 2, 3).reshape(depth, D_MODEL, Q_A)
    w_in_b = jnp.concatenate([wq, w_in[:, :, Q_A:]], axis=-1).astype(BF16)
    wo = w_out[:, :Q_A].reshape(depth, N_KV_A, GQA_GROUP, HEAD_DIM, D_MODEL)
    wo = jnp.swapaxes(wo, 1, 2).reshape(depth, Q_A, D_MODEL)
    w_out_b = jnp.concatenate([wo, w_out[:, Q_A:]], axis=1).astype(BF16)
    wg_b, wu_b, wd_b = w_e_gate.astype(BF16), w_e_up.astype(BF16), w_e_down.astype(BF16)
    q_norm_t = jnp.tile(q_norm, (1, N_HEADS_A)).reshape(depth, 1, Q_A)
    k_norm_t = jnp.tile(k_norm, (1, N_KV_A)).reshape(depth, 1, KV_A)
    o_norm_t = jnp.tile(o_norm, (1, N_HEADS_B)).reshape(depth, 1, V_B)
    ln1_r = ln1.reshape(depth, 1, D_MODEL)
    ln2_r = ln2.reshape(depth, 1, D_MODEL)
    w_router_t = w_router.T
    router_bias_c = router_bias.reshape(N_EXPERTS, 1)

    c_exp = jnp.concatenate([jnp.repeat(c_prompt, GRP, axis=0), c_sample], axis=0)
    mod = _adaln(c_exp, w_ada, b_ada).reshape(depth, batch + 1, GRP, N_MOD * D_MODEL)

    cache_k2 = cache_k.reshape(depth, dec_batch, n_cache, KV_A)
    cache_v2 = cache_v.reshape(depth, dec_batch, n_cache, KV_A)

    src = (x_prompt, x_sample)
    kp, vp, sp, kss, vss, sss = [], [], [], [], [], []
    for l in range(depth):
        x, q4, k, v, qb, logf, kb, ib, gb = _inproj(
            l, src, n_tiles, prompt_tiles, mod, ln1_r, w_in_b, q_norm_t, k_norm_t, lb_param, bd,
            tiles_per_batch)

        oap4 = _attn_prompt(q4, k, v, sinks[l], batch, seq)
        oas4 = _attn_sample(q4, k, v, cache_k2[l], cache_v2[l], sinks[l], prompt_tiles,
                            n_prompt, dec_batch, dec_seq)
        obp, s_p = _hgrn_prompt(qb, kb, ib, gb, logf, consts_p, bd, o_norm_t[l], batch, seq)
        obs, s_s = _hgrn_sample(qb, kb, ib, gb, logf, _pair_state_in(state_hgrn[l]), consts_s,
                                bd, o_norm_t[l], n_prompt, dec_batch, dec_seq)

        x1, h2, cls3, gate_a, gate_b = _outproj(l, x, oap4, obp, oas4, obs, mod, ln2_r, w_out_b,
                                                w_router_t, router_bias_c, tiles_per_batch)
        y = _moe(l, h2, gate_a, gate_b, _route_plan(cls3, upper, lower), wg_b, wu_b, wd_b)
        src = (x1, y)

        kp.append(k[:n_prompt].reshape(batch, seq, N_KV_A, HEAD_DIM)[:, -WINDOW:])
        vp.append(v[:n_prompt].reshape(batch, seq, N_KV_A, HEAD_DIM)[:, -WINDOW:])
        sp.append(_pair_state_out(s_p[:, 0]))
        kss.append(k[n_prompt:].reshape(dec_batch, dec_seq, N_KV_A, HEAD_DIM))
        vss.append(v[n_prompt:].reshape(dec_batch, dec_seq, N_KV_A, HEAD_DIM))
        sss.append(_pair_state_out(s_s))

    yp, ys = _final(depth - 1, src[0], src[1], mod, batch, seq, tiles_per_batch)
    return (yp, ys, jnp.stack(kp), jnp.stack(vp), jnp.stack(sp),
            jnp.stack(kss), jnp.stack(vss), jnp.stack(sss))
```

```python
import functools

import numpy as np
import jax
import jax.numpy as jnp
from jax import lax
from jax.experimental import pallas as pl
from jax.experimental.pallas import tpu as pltpu

F32 = jnp.float32
BF16 = jnp.bfloat16
I32 = jnp.int32

D_MODEL = 1024
HEAD_DIM = 64
N_HEADS_A = 8
N_KV_A = 2
GQA_GROUP = N_HEADS_A // N_KV_A
N_HEADS_B = 8
DK_B = 64
DV_B = 64
Q_A = N_HEADS_A * HEAD_DIM
KV_A = N_KV_A * HEAD_DIM
W_B = N_HEADS_B * DK_B
V_B = N_HEADS_B * DV_B
CHUNK = 64
WINDOW = 128
N_EXPERTS = 16
N_GROUPS = 4
EXPERTS_PER_GROUP = 4
D_EXPERT = D_MODEL // 2
RMS_EPS = 1e-6
NEG_INF = -1e30
N_MOD = 6

LANES = 128
SUBLANES = 8
ROWS = 512
GRP = 32
SUB = ROWS // GRP
ATT_ROWS = 256
N_PAIR = N_HEADS_B // 2
SAMPLE_SEQS = 4
HGRN_SPAN_LIMIT = 80.0
TM = 256
PAIRS_PER_GROUP = 6
PAIR_TABLE = ((0, 1), (0, 2), (0, 3), (1, 2), (1, 3), (2, 3))
N_CLASS = N_GROUPS * PAIRS_PER_GROUP
DMA_UNROLL = 8
VMEM_LIMIT = 56 * 1024 * 1024


def _cparams(sem):
    return pltpu.CompilerParams(dimension_semantics=sem, vmem_limit_bytes=VMEM_LIMIT)


def _dot(a, b):
    return jnp.dot(a, b, preferred_element_type=F32)


def _dot_nt(a, b):
    return lax.dot_general(a, b, (((1,), (1,)), ((), ())), preferred_element_type=F32)


def _dot_tn(a, b):
    return lax.dot_general(a, b, (((0,), (0,)), ((), ())), preferred_element_type=F32)


def _sigmoid(x):
    return 1.0 / (1.0 + jnp.exp(-x))


def _split2(x):
    hi = x.astype(BF16)
    lo = (x - hi.astype(F32)).astype(BF16)
    return hi, lo


def _split3(x):
    hi = x.astype(BF16)
    r = x - hi.astype(F32)
    mid = r.astype(BF16)
    lo = (r - mid.astype(F32)).astype(BF16)
    return hi, mid, lo


def _seg_mean(sq, bd):
    return _dot(sq.astype(BF16), bd)


def _token_rows(ref, s, n, base=0):
    return ref[pl.ds(base + s, n, stride=SUBLANES), :]


def _adaln_kernel(c_ref, w_ref, b_ref, o_ref):
    c = c_ref[...]
    a = c * _sigmoid(c)
    a_hi, a_lo = _split2(a)
    w_hi, w_lo = _split2(w_ref[...])
    acc = _dot(a_hi, w_hi) + _dot(a_lo, w_hi) + _dot(a_hi, w_lo)
    o_ref[...] = acc + b_ref[...]


def _adaln(c_exp, w_ada, b_ada):
    depth = w_ada.shape[0]
    rows = c_exp.shape[0]
    tn = D_MODEL
    return pl.pallas_call(
        _adaln_kernel,
        grid=(depth, N_MOD * D_MODEL // tn),
        in_specs=[
            pl.BlockSpec((rows, D_MODEL), lambda l, n: (0, 0)),
            pl.BlockSpec((None, D_MODEL, tn), lambda l, n: (l, 0, n)),
            pl.BlockSpec((None, 1, tn), lambda l, n: (l, 0, n)),
        ],
        out_specs=pl.BlockSpec((None, rows, tn), lambda l, n: (l, 0, n)),
        out_shape=jax.ShapeDtypeStruct((depth, rows, N_MOD * D_MODEL), F32),
        compiler_params=_cparams(("arbitrary", "arbitrary")),
        name="adaln",
    )(c_exp, w_ada, b_ada.reshape(depth, 1, N_MOD * D_MODEL))


def _mod_spec(layer, j, tiles_per_batch):
    return pl.BlockSpec((None, None, GRP, D_MODEL),
                        lambda i, *_: (layer, i // tiles_per_batch, 0, j))


def _by_group(fn, xs, mod_refs):
    outs = []
    for g in range(GRP):
        rows = slice(g * SUB, (g + 1) * SUB)
        outs.append(fn(*[x[rows] for x in xs], *[m[g:g + 1, :] for m in mod_refs]))
    return jnp.concatenate(outs, axis=0)


def _gated_add(x, gate_ref, y):
    return _by_group(lambda xg, yg, gg: xg + gg * yg, (x, y), (gate_ref,))


def _modulated_norm(x, ln_ref, sc_ref, sh_ref):
    ms = jnp.mean(x * x, axis=-1, keepdims=True)
    xn = x * lax.rsqrt(ms + RMS_EPS) * ln_ref[...]
    return _by_group(lambda xg, sc, sh: xg * (1.0 + sc) + sh, (xn,), (sc_ref, sh_ref))


def _prompt_tile_spec(prompt_tiles, tiles_per_batch):
    def index(i):
        t = jnp.minimum(i, prompt_tiles - 1)
        return (t // tiles_per_batch, t % tiles_per_batch, 0)
    return pl.BlockSpec((None, ROWS, D_MODEL), index)


def _moe_output_tile(y_ref):
    return jnp.concatenate([_token_rows(y_ref, s, ROWS) for s in range(SUBLANES)], axis=1)


def _inproj_kernel(layer, prompt_tiles, n_src, *refs):
    src = refs[:n_src]
    sh_ref, sc_ref, ln_ref, w_ref, qn_ref, kn_ref, lbp_ref, bd_ref = refs[n_src:n_src + 8]
    x_out, q_out, k_out, v_out, qb_out, logf_out, kb_out, ib_out, gb_out = refs[n_src + 8:]
    if layer == 0:
        xp_ref, xs_ref = src
        x = jnp.where(pl.program_id(0) < prompt_tiles, xp_ref[...],
                      xs_ref[...].reshape(ROWS, D_MODEL))
    else:
        x1_ref, y_ref, g2_ref = src
        x = _gated_add(x1_ref[...], g2_ref, _moe_output_tile(y_ref))
    x_out[...] = x
    hb = _modulated_norm(x, ln_ref, sc_ref, sh_ref).astype(BF16)
    bd = bd_ref[...]

    qa = _dot(hb, w_ref[:, 0:Q_A])
    qa = qa * lax.rsqrt(_seg_mean(qa * qa, bd) + RMS_EPS) * qn_ref[...]
    qa = (qa * (HEAD_DIM ** -0.5)).astype(BF16)
    for hh in range(GQA_GROUP):
        q_out[hh] = qa[:, hh * LANES:(hh + 1) * LANES]

    ka = _dot(hb, w_ref[:, Q_A:Q_A + KV_A])
    ka = ka * lax.rsqrt(_seg_mean(ka * ka, bd[:KV_A, :KV_A]) + RMS_EPS) * kn_ref[...]
    k_out[...] = ka
    v_out[...] = _dot(hb, w_ref[:, Q_A + KV_A:Q_A + 2 * KV_A])

    c0 = Q_A + 2 * KV_A
    qb_out[...] = (_dot(hb, w_ref[:, c0:c0 + W_B]) * (DK_B ** -0.5)).astype(BF16)

    p = lbp_ref[...]
    e = jnp.exp(p - jnp.max(p, axis=0, keepdims=True))
    sm = e / jnp.sum(e, axis=0, keepdims=True)
    cum0 = sm[0:1]
    cum = cum0
    for r in range(1, layer + 1):
        cum = cum + sm[r:r + 1]
    lb = cum - cum0

    z = _dot(hb, w_ref[:, c0 + W_B:c0 + 2 * W_B])
    pos = z >= 0.0
    t = jnp.exp(-jnp.abs(z))
    num = jnp.where(pos, jnp.log1p(lb * t), jnp.maximum(jnp.log(lb + t), z))
    logf_out[...] = num - jnp.log1p(t)
    kb_out[...] = ((1.0 - lb) * jnp.where(pos, t, 1.0) / (1.0 + t)).astype(BF16)

    c1 = c0 + 2 * W_B
    ib_out[...] = _dot(hb, w_ref[:, c1:c1 + V_B]).astype(BF16)
    gb_out[...] = _dot(hb, w_ref[:, c1 + V_B:c1 + 2 * V_B]).astype(BF16)


def _inproj(layer, src, n_tiles, prompt_tiles, mod, ln1, w_in_b, q_norm_t, k_norm_t, lb_param, bd,
            tiles_per_batch):
    depth = lb_param.shape[0]
    n = n_tiles * ROWS
    in_width = w_in_b.shape[-1]
    x_spec = pl.BlockSpec((ROWS, D_MODEL), lambda i: (i, 0))
    if layer == 0:
        src_specs = [_prompt_tile_spec(prompt_tiles, tiles_per_batch),
                     pl.BlockSpec((GRP, SUB, D_MODEL), lambda i: (0, 0, 0))]
        src_args = list(src)
    else:
        src_specs = [x_spec, pl.BlockSpec((ROWS * SUBLANES, LANES), lambda i: (i, 0)),
                     _mod_spec(layer - 1, 5, tiles_per_batch)]
        src_args = list(src) + [mod]
    row_spec = lambda w: pl.BlockSpec((ROWS, w), lambda i: (i, 0))
    return pl.pallas_call(
        functools.partial(_inproj_kernel, layer, prompt_tiles, len(src_specs)),
        grid=(n_tiles,),
        in_specs=src_specs + [
            _mod_spec(layer, 0, tiles_per_batch), _mod_spec(layer, 1, tiles_per_batch),
            pl.BlockSpec((None, 1, D_MODEL), lambda i: (layer, 0, 0)),
            pl.BlockSpec((None, D_MODEL, in_width), lambda i: (layer, 0, 0)),
            pl.BlockSpec((None, 1, Q_A), lambda i: (layer, 0, 0)),
            pl.BlockSpec((None, 1, KV_A), lambda i: (layer, 0, 0)),
            pl.BlockSpec((depth, W_B), lambda i: (0, 0)),
            pl.BlockSpec((Q_A, Q_A), lambda i: (0, 0)),
        ],
        out_specs=[
            x_spec,
            pl.BlockSpec((None, GQA_GROUP, ROWS, LANES), lambda i: (i, 0, 0, 0)),
            row_spec(KV_A), row_spec(KV_A),
            row_spec(W_B), row_spec(W_B), row_spec(W_B), row_spec(V_B), row_spec(V_B),
        ],
        out_shape=[
            jax.ShapeDtypeStruct((n, D_MODEL), F32),
            jax.ShapeDtypeStruct((n_tiles, GQA_GROUP, ROWS, LANES), BF16),
            jax.ShapeDtypeStruct((n, KV_A), F32),
            jax.ShapeDtypeStruct((n, KV_A), F32),
            jax.ShapeDtypeStruct((n, W_B), BF16),
            jax.ShapeDtypeStruct((n, W_B), F32),
            jax.ShapeDtypeStruct((n, W_B), BF16),
            jax.ShapeDtypeStruct((n, V_B), BF16),
            jax.ShapeDtypeStruct((n, V_B), BF16),
        ],
        compiler_params=_cparams(("arbitrary",)),
        name=f"inproj{layer}",
    )(*src_args, mod, mod, ln1, w_in_b, q_norm_t, k_norm_t, lb_param, bd)


def _final_kernel(prompt_tiles, x1_ref, y_ref, g2_ref, yp_ref, ys_ref):
    i = pl.program_id(0)
    x = _gated_add(x1_ref[...], g2_ref, _moe_output_tile(y_ref))

    @pl.when(i < prompt_tiles)
    def _():
        yp_ref[...] = x

    @pl.when(i >= prompt_tiles)
    def _():
        ys_ref[...] = x.reshape(GRP, SUB, D_MODEL)


def _final(layer, x1, y, mod, batch, seq, tiles_per_batch):
    n_tiles = x1.shape[0] // ROWS
    prompt_tiles = batch * tiles_per_batch
    return pl.pallas_call(
        functools.partial(_final_kernel, prompt_tiles),
        grid=(n_tiles,),
        in_specs=[pl.BlockSpec((ROWS, D_MODEL), lambda i: (i, 0)),
                  pl.BlockSpec((ROWS * SUBLANES, LANES), lambda i: (i, 0)),
                  _mod_spec(layer, 5, tiles_per_batch)],
        out_specs=[_prompt_tile_spec(prompt_tiles, tiles_per_batch),
                   pl.BlockSpec((GRP, SUB, D_MODEL), lambda i: (0, 0, 0))],
        out_shape=[jax.ShapeDtypeStruct((batch, seq, D_MODEL), F32),
                   jax.ShapeDtypeStruct((GRP, SUB, D_MODEL), F32)],
        compiler_params=_cparams(("arbitrary",)),
        name="final",
    )(x1, y, mod)


def _attend(q, k, v, sinks_ref, q_len, k_off, k_pos0=None):
    n_q = q.shape[0]
    n_k = k.shape[0]
    row = lax.broadcasted_iota(I32, (n_q, n_k), 0)
    col = lax.broadcasted_iota(I32, (n_q, n_k), 1)
    dist = jnp.abs((row % q_len) + k_off - col).astype(F32)
    lane = lax.broadcasted_iota(I32, (n_k, LANES), 1)
    hh_col = lax.broadcasted_iota(I32, (n_q, 1), 0) // q_len
    out = jnp.zeros((n_q, LANES), F32)
    for g in range(N_KV_A):
        in_g = (lane // HEAD_DIM) == g
        kg = jnp.where(in_g, k, 0.0).astype(BF16)
        vg = jnp.where(in_g, v, 0.0).astype(BF16)
        s = _dot_nt(q, kg)
        head = (g * GQA_GROUP + 1 + hh_col).astype(F32)
        s = s - jnp.exp2(-8.0 * head / N_HEADS_A) * dist
        if k_pos0 is not None:
            s = jnp.where(col + k_pos0 >= 0, s, NEG_INF)
        sink = jnp.zeros((n_q, 1), F32)
        for j in range(GQA_GROUP):
            sink = jnp.where(hh_col == j, sinks_ref[g * GQA_GROUP + j], sink)
        m = jnp.maximum(jnp.max(s, axis=-1, keepdims=True), sink)
        p = jnp.exp(s - m)
        denom = jnp.sum(p, axis=-1, keepdims=True) + jnp.exp(sink - m)
        out = out + _dot(p.astype(BF16), vg) / denom
    return out


def _attn_prompt_kernel(sinks_ref, q_ref, kp_ref, kc_ref, vp_ref, vc_ref, o_ref):
    j = pl.program_id(1)
    kcat = jnp.concatenate([kp_ref[...], kc_ref[...]], axis=0)
    vcat = jnp.concatenate([vp_ref[...], vc_ref[...]], axis=0)
    for c in range(ATT_ROWS // CHUNK):
        q = q_ref[:, c * CHUNK:(c + 1) * CHUNK, :].reshape(GQA_GROUP * CHUNK, LANES)
        k = kcat[c * CHUNK:c * CHUNK + WINDOW + CHUNK]
        v = vcat[c * CHUNK:c * CHUNK + WINDOW + CHUNK]
        k_pos0 = j * ATT_ROWS + c * CHUNK - WINDOW
        out = _attend(q, k, v, sinks_ref, CHUNK, WINDOW, k_pos0)
        o_ref[:, c * CHUNK:(c + 1) * CHUNK, :] = out.reshape(GQA_GROUP, CHUNK, LANES).astype(BF16)


def _attn_prompt(q4, k, v, sinks, batch, seq):
    steps = seq // ATT_ROWS
    per_tile = ROWS // ATT_ROWS
    win_blocks = ATT_ROWS // WINDOW
    n_tiles = batch * seq // ROWS
    cur = pl.BlockSpec((ATT_ROWS, KV_A), lambda b, j: (b * steps + j, 0))
    prev = pl.BlockSpec(
        (WINDOW, KV_A), lambda b, j: (b * steps * win_blocks + jnp.maximum(j * win_blocks - 1, 0), 0))
    qo = pl.BlockSpec((None, GQA_GROUP, ATT_ROWS, LANES),
                      lambda b, j: ((b * steps + j) // per_tile, 0, j % per_tile, 0))
    return pl.pallas_call(
        _attn_prompt_kernel,
        grid=(batch, steps),
        in_specs=[pl.BlockSpec(memory_space=pltpu.SMEM), qo, prev, cur, prev, cur],
        out_specs=qo,
        out_shape=jax.ShapeDtypeStruct((n_tiles, GQA_GROUP, ROWS, LANES), BF16),
        compiler_params=_cparams(("arbitrary", "arbitrary")),
        name="attn_prompt",
    )(sinks, q4, k, k, v, v)


def _attn_sample_kernel(n_new, sinks_ref, q_ref, kc_ref, kn_ref, vc_ref, vn_ref, o_ref):
    n_cache = kc_ref.shape[1]
    for r in range(SAMPLE_SEQS):
        rows = slice(r * n_new, (r + 1) * n_new)
        q = q_ref[:, rows, :].reshape(GQA_GROUP * n_new, LANES)
        k = jnp.concatenate([kc_ref[r], kn_ref[rows, :]], axis=0)
        v = jnp.concatenate([vc_ref[r], vn_ref[rows, :]], axis=0)
        out = _attend(q, k, v, sinks_ref, n_new, n_cache)
        o_ref[:, rows, :] = out.reshape(GQA_GROUP, n_new, LANES).astype(BF16)


def _attn_sample(q4, k, v, cache_k, cache_v, sinks, tile, row0, dec_batch, dec_seq):
    n_cache = cache_k.shape[1]
    rows = SAMPLE_SEQS * dec_seq
    q_spec = pl.BlockSpec((None, GQA_GROUP, rows, LANES), lambda b: (tile, 0, b, 0))
    new = pl.BlockSpec((rows, KV_A), lambda b: (row0 // rows + b, 0))
    cache = pl.BlockSpec((SAMPLE_SEQS, n_cache, KV_A), lambda b: (b, 0, 0))
    return pl.pallas_call(
        functools.partial(_attn_sample_kernel, dec_seq),
        grid=(dec_batch // SAMPLE_SEQS,),
        in_specs=[pl.BlockSpec(memory_space=pltpu.SMEM), q_spec, cache, new, cache, new],
        out_specs=pl.BlockSpec((None, GQA_GROUP, rows, LANES), lambda b: (0, 0, b, 0)),
        out_shape=jax.ShapeDtypeStruct((1, GQA_GROUP, dec_batch * dec_seq, LANES), BF16),
        compiler_params=_cparams(("arbitrary",)),
        name="attn_sample",
    )(sinks, q4, cache_k, k, cache_v, v)


def _level_sizes(length):
    sizes = []
    m = length // 2
    while m >= 1:
        sizes.append(m)
        m //= 2
    return sizes


def _hgrn_consts(length):
    t = np.arange(length)
    tri = (t[None, :] <= t[:, None]).astype(np.float32)
    after = (t[None, :] > t[:, None]).astype(np.float32)
    levels = []
    masks = []
    s = np.tile(t, 2)[None, :]
    tt = t[:, None]
    for m in _level_sizes(length):
        blk = t // m
        same = blk[None, :] == blk[:, None]
        q_rows = (blk % 2 == 1)[:, None]
        sel = np.where(q_rows, same & (t[None, :] <= t[:, None]), same & (t[None, :] > t[:, None]))
        levels.append(sel.astype(np.float32))
        masks.append(((tt // m) % 2 == 1) & ((s // m) == (tt // m) - 1))
    masks.append(s == tt)
    masks.append(s <= tt)
    base = np.concatenate([tri, after], axis=0)
    lev = np.concatenate(levels, axis=0)
    return (np.concatenate([base] * 3, axis=1), np.concatenate([lev] * 3, axis=1),
            np.stack(masks).astype(np.int32))


def _stack2(x, m0, m1):
    xb = x.astype(BF16)
    return jnp.concatenate([xb * m0, xb * m1], axis=0)


def _hgrn_intra(qp, kp, b, masks, m0, m1, arg_ref, sl):
    length = qp.shape[0]
    n_lev = len(_level_sizes(length))
    if arg_ref is None:
        mid = length // 2 - 1
        rel = b - b[mid:mid + 1]
        a = _dot_nt((qp * jnp.exp(rel)).astype(BF16), _stack2(kp * jnp.exp(-rel), m0, m1))
        a = jnp.where(masks[n_lev + 1], a, 0.0)
    else:
        a = _dot_nt(qp.astype(BF16), _stack2(kp, m0, m1))
        a = jnp.where(masks[n_lev], a, 0.0)
        for lev in range(n_lev):
            e = jnp.exp(arg_ref[lev * length:(lev + 1) * length, sl])
            pr = _dot_nt((qp * e).astype(BF16), _stack2(kp * e, m0, m1))
            a = jnp.where(masks[lev], pr, a)
    return a.astype(BF16)


def _hgrn_finish(a, qp, kp, ip, base, sl, st_ref, seq, p, m0, m1, same_head):
    length = qp.shape[0]
    e_b = jnp.exp(base[0:length, sl])
    e_k = jnp.exp(base[length:2 * length, sl])
    st = st_ref[seq, p]
    o = _dot(jnp.concatenate([a, (qp * e_b).astype(BF16)], axis=1),
             jnp.concatenate([_stack2(ip, m0, m1), st.T.astype(BF16)], axis=0))
    upd = _dot_tn(ip.astype(BF16), (kp * e_k).astype(BF16))
    st_ref[seq, p] = st * e_b[length - 1:length, :] + jnp.where(same_head, upd, 0.0)
    return o


def _hgrn_kernel(n_seq, n_chunk, length, carry_state, *refs):
    if carry_state:
        (q_ref, k_ref, i_ref, g_ref, f_ref, selb_ref, sell_ref, mask_ref, bd_ref, on_ref,
         o_ref, s_ref, st_ref, base_ref, arg_ref) = refs

        @pl.when(pl.program_id(1) == 0)
        def _():
            st_ref[...] = jnp.zeros_like(st_ref)
    else:
        (q_ref, k_ref, i_ref, g_ref, f_ref, s0_ref, selb_ref, sell_ref, mask_ref, bd_ref, on_ref,
         o_ref, s_ref, st_ref, base_ref, arg_ref) = refs
        st_ref[...] = s0_ref[...]

    bd = bd_ref[...]
    onorm = on_ref[...]
    n_items = n_seq * n_chunk
    mid = length // 2 - 1
    lane = lax.broadcasted_iota(I32, (length, LANES), 1)
    m0 = (lane < DK_B).astype(BF16)
    m1 = (lane >= DK_B).astype(BF16)
    rowp = lax.broadcasted_iota(I32, (LANES, LANES), 0) // DV_B
    colp = lax.broadcasted_iota(I32, (LANES, LANES), 1) // DK_B
    same_head = rowp == colp
    masks = [mask_ref[m] != 0 for m in range(mask_ref.shape[0])]

    def pieces(rows):
        return jnp.concatenate(_split3(f_ref[rows, :]), axis=0)

    def pair(ref, rows, p):
        return ref[rows, p * LANES:(p + 1) * LANES].astype(F32)

    span = jnp.float32(0.0)
    for it in range(n_items):
        base = _dot(selb_ref[...], pieces(slice(it * length, (it + 1) * length)))
        base_ref[it * 2 * length:(it + 1) * 2 * length, :] = base
        b_mid = base[mid:mid + 1]
        span = jnp.maximum(span, jnp.max(base[0:1] - b_mid))
        span = jnp.maximum(span, jnp.max(b_mid - base[length - 1:length]))
    safe = span <= HGRN_SPAN_LIMIT

    def finish(rows, base, seq, a_of):
        outs = []
        for p in range(N_PAIR):
            sl = slice(p * LANES, (p + 1) * LANES)
            outs.append(_hgrn_finish(a_of(p), pair(q_ref, rows, p), pair(k_ref, rows, p),
                                     pair(i_ref, rows, p), base, sl, st_ref, seq, p, m0, m1,
                                     same_head))
        o = jnp.concatenate(outs, axis=1)
        g = g_ref[rows, :].astype(F32)
        o = o * lax.rsqrt(_seg_mean(o * o, bd) + RMS_EPS) * onorm * (g * _sigmoid(g))
        o_ref[rows, :] = o.astype(BF16)

    @pl.when(safe)
    def _():
        for it in range(n_items):
            rows = slice(it * length, (it + 1) * length)
            base = base_ref[it * 2 * length:(it + 1) * 2 * length, :]

            def a_of(p, rows=rows, base=base):
                sl = slice(p * LANES, (p + 1) * LANES)
                return _hgrn_intra(pair(q_ref, rows, p), pair(k_ref, rows, p), base[0:length, sl],
                                   masks, m0, m1, None, sl)

            finish(rows, base, it // n_chunk, a_of)

    @pl.when(jnp.logical_not(safe))
    def _():
        def body(it, carry):
            rows = pl.ds(pl.multiple_of(it * length, length), length)
            arg_ref[...] = _dot(sell_ref[...], pieces(rows))
            base = base_ref[pl.ds(pl.multiple_of(it * 2 * length, 2 * length), 2 * length), :]
            seq = it // n_chunk if n_chunk > 1 and n_seq > 1 else (it if n_chunk == 1 else 0)

            def a_of(p):
                sl = slice(p * LANES, (p + 1) * LANES)
                return _hgrn_intra(pair(q_ref, rows, p), pair(k_ref, rows, p), base[0:length, sl],
                                   masks, m0, m1, arg_ref, sl)

            finish(rows, base, seq, a_of)
            return carry

        lax.fori_loop(0, n_items, body, 0)

    if carry_state:
        @pl.when(pl.program_id(1) == pl.num_programs(1) - 1)
        def _():
            s_ref[...] = st_ref[...]
    else:
        s_ref[...] = st_ref[...]


def _hgrn_scratch(n_seq, n_items, length, sell):
    return [pltpu.VMEM((n_seq, N_PAIR, LANES, LANES), F32),
            pltpu.VMEM((n_items * 2 * length, W_B), F32),
            pltpu.VMEM((sell.shape[0], W_B), F32)]


def _hgrn_prompt(qb, kb, ib, gb, logf, consts, bd, onorm_t, batch, seq):
    selb, sell, masks = consts
    steps = seq // ATT_ROWS
    n_chunk = ATT_ROWS // CHUNK
    row = pl.BlockSpec((ATT_ROWS, W_B), lambda b, j: (b * steps + j, 0))
    whole = lambda a: pl.BlockSpec(a.shape, lambda b, j: (0,) * a.ndim)
    return pl.pallas_call(
        functools.partial(_hgrn_kernel, 1, n_chunk, CHUNK, True),
        grid=(batch, steps),
        in_specs=[row, row, row, row, row, whole(selb), whole(sell), whole(masks), whole(bd),
                  whole(onorm_t)],
        out_specs=[row, pl.BlockSpec((None, 1, N_PAIR, LANES, LANES), lambda b, j: (b, 0, 0, 0, 0))],
        out_shape=[jax.ShapeDtypeStruct((batch * seq, V_B), BF16),
                   jax.ShapeDtypeStruct((batch, 1, N_PAIR, LANES, LANES), F32)],
        scratch_shapes=_hgrn_scratch(1, n_chunk, CHUNK, sell),
        compiler_params=_cparams(("arbitrary", "arbitrary")),
        name="hgrn_prompt",
    )(qb, kb, ib, gb, logf, selb, sell, masks, bd, onorm_t)


def _hgrn_sample(qb, kb, ib, gb, logf, s0, consts, bd, onorm_t, row0, dec_batch, dec_seq):
    selb, sell, masks = consts
    rows = SAMPLE_SEQS * dec_seq
    row = pl.BlockSpec((rows, W_B), lambda b: (row0 // rows + b, 0))
    whole = lambda a: pl.BlockSpec(a.shape, lambda b: (0,) * a.ndim)
    state = pl.BlockSpec((SAMPLE_SEQS, N_PAIR, LANES, LANES), lambda b: (b, 0, 0, 0))
    return pl.pallas_call(
        functools.partial(_hgrn_kernel, SAMPLE_SEQS, 1, dec_seq, False),
        grid=(dec_batch // SAMPLE_SEQS,),
        in_specs=[row, row, row, row, row, state, whole(selb), whole(sell), whole(masks),
                  whole(bd), whole(onorm_t)],
        out_specs=[pl.BlockSpec((rows, V_B), lambda b: (b, 0)), state],
        out_shape=[jax.ShapeDtypeStruct((dec_batch * dec_seq, V_B), BF16),
                   jax.ShapeDtypeStruct((dec_batch, N_PAIR, LANES, LANES), F32)],
        scratch_shapes=_hgrn_scratch(SAMPLE_SEQS, SAMPLE_SEQS, dec_seq, sell),
        compiler_params=_cparams(("arbitrary",)),
        name="hgrn_sample",
    )(qb, kb, ib, gb, logf, s0, selb, sell, masks, bd, onorm_t)


def _outproj_kernel(prompt_tiles, x_ref, oap_ref, obp_ref, oas_ref, obs_ref, g1_ref, sh_ref, sc_ref,
                    ln_ref, w_ref, wr_ref, rb_ref, x1_out, h2_out, cls_out, ga_out, gb_out):
    def mix_of(oa_ref, ob_ref):
        return jnp.concatenate([oa_ref[hh] for hh in range(GQA_GROUP)] + [ob_ref[...]], axis=1)

    is_prompt = pl.program_id(0) < prompt_tiles
    mix = jnp.where(is_prompt, mix_of(oap_ref, obp_ref), mix_of(oas_ref, obs_ref))
    att = _dot(mix, w_ref[...])
    x1 = _gated_add(x_ref[...], g1_ref, att)
    x1_out[...] = x1
    h = _modulated_norm(x1, ln_ref, sc_ref, sh_ref)

    for s in range(SUBLANES):
        h2_out[pl.ds(s, ROWS, stride=SUBLANES), :] = h[:, s * LANES:(s + 1) * LANES]

    h_hi, h_lo = _split2(h)
    w_hi, w_lo = _split2(wr_ref[...])
    hi_terms = _dot_nt(jnp.concatenate([w_hi, w_lo], axis=0), h_hi)
    logits = hi_terms[0:N_EXPERTS] + hi_terms[N_EXPERTS:] + _dot_nt(w_hi, h_lo)
    aff = _sigmoid(logits)
    sel = aff + rb_ref[...]
    e_idx = lax.broadcasted_iota(I32, (N_EXPERTS, ROWS), 0)
    l_idx = lax.broadcasted_iota(I32, (EXPERTS_PER_GROUP, ROWS), 0)

    def first_argmax(vals, idx, big):
        top = jnp.max(vals, axis=0, keepdims=True)
        return top, jnp.min(jnp.where(vals == top, idx, big), axis=0, keepdims=True)

    g_scores = []
    for g in range(N_GROUPS):
        sg = sel[g * EXPERTS_PER_GROUP:(g + 1) * EXPERTS_PER_GROUP]
        m1, i1 = first_argmax(sg, l_idx, EXPERTS_PER_GROUP)
        m2 = jnp.max(jnp.where(l_idx == i1, -jnp.inf, sg), axis=0, keepdims=True)
        g_scores.append(m1 + m2)
    g_score = jnp.concatenate(g_scores, axis=0)
    g_idx = lax.broadcasted_iota(I32, (N_GROUPS, ROWS), 0)
    _, g_sel = first_argmax(g_score, g_idx, N_GROUPS)
    masked = jnp.where((e_idx // EXPERTS_PER_GROUP) == g_sel, sel, -jnp.inf)
    _, e1 = first_argmax(masked, e_idx, N_EXPERTS)
    _, e2 = first_argmax(jnp.where(e_idx == e1, -jnp.inf, masked), e_idx, N_EXPERTS)
    w1 = jnp.sum(jnp.where(e_idx == e1, aff, 0.0), axis=0, keepdims=True)
    w2 = jnp.sum(jnp.where(e_idx == e2, aff, 0.0), axis=0, keepdims=True)
    den = w1 + w2
    w1 = w1 / den
    w2 = w2 / den
    lo_first = e1 < e2
    gate_a = jnp.where(lo_first, w1, w2)
    gate_b = jnp.where(lo_first, w2, w1)
    a = jnp.minimum(e1, e2) - g_sel * EXPERTS_PER_GROUP
    b = jnp.maximum(e1, e2) - g_sel * EXPERTS_PER_GROUP
    base = jnp.where(a == 0, 0, jnp.where(a == 1, 3, 5))
    cls_out[...] = g_sel * PAIRS_PER_GROUP + base + (b - a - 1)
    ga_out[...] = gate_a
    gb_out[...] = gate_b


def _outproj(layer, x4, oap4, obp, oas4, obs, mod, ln2, w_out_b, w_router_t, router_bias_c,
             tiles_per_batch):
    n_tiles = x4.shape[0] // ROWS
    x_spec = pl.BlockSpec((ROWS, D_MODEL), lambda i: (i, 0))
    prompt_tiles = oap4.shape[0]
    last = prompt_tiles - 1
    lane_row = pl.BlockSpec((None, 1, ROWS), lambda i: (i, 0, 0))
    return pl.pallas_call(
        functools.partial(_outproj_kernel, prompt_tiles),
        grid=(n_tiles,),
        in_specs=[
            x_spec,
            pl.BlockSpec((None, GQA_GROUP, ROWS, LANES), lambda i: (jnp.minimum(i, last), 0, 0, 0)),
            pl.BlockSpec((ROWS, V_B), lambda i: (jnp.minimum(i, last), 0)),
            pl.BlockSpec((None, GQA_GROUP, ROWS, LANES), lambda i: (0, 0, 0, 0)),
            pl.BlockSpec((ROWS, V_B), lambda i: (0, 0)),
            _mod_spec(layer, 2, tiles_per_batch), _mod_spec(layer, 3, tiles_per_batch),
            _mod_spec(layer, 4, tiles_per_batch),
            pl.BlockSpec((None, 1, D_MODEL), lambda i: (layer, 0, 0)),
            pl.BlockSpec((None, Q_A + V_B, D_MODEL), lambda i: (layer, 0, 0)),
            pl.BlockSpec((N_EXPERTS, D_MODEL), lambda i: (0, 0)),
            pl.BlockSpec((N_EXPERTS, 1), lambda i: (0, 0)),
        ],
        out_specs=[
            x_spec,
            pl.BlockSpec((ROWS * SUBLANES, LANES), lambda i: (i, 0)),
            lane_row, lane_row, lane_row,
        ],
        out_shape=[
            jax.ShapeDtypeStruct(x4.shape, F32),
            jax.ShapeDtypeStruct((n_tiles * ROWS * SUBLANES, LANES), F32),
            jax.ShapeDtypeStruct((n_tiles, 1, ROWS), I32),
            jax.ShapeDtypeStruct((n_tiles, 1, ROWS), F32),
            jax.ShapeDtypeStruct((n_tiles, 1, ROWS), F32),
        ],
        compiler_params=_cparams(("arbitrary",)),
        name=f"outproj{layer}",
    )(x4, oap4, obp, oas4, obs, mod, mod, mod, ln2, w_out_b, w_router_t, router_bias_c)


def _rank_kernel(cls_ref, upper_ref, lower_ref, dest_ref, count_ref, rank_ref):
    n_tiles = cls_ref.shape[0]
    c_idx = lax.broadcasted_iota(I32, (GRP, ROWS), 0)
    upper = upper_ref[...]

    def rank_body(i, carry):
        onehot = c_idx == cls_ref[i]
        within = _dot(onehot.astype(BF16), upper)
        rank_ref[i] = jnp.sum(jnp.where(onehot, within + carry, 0.0), axis=0, keepdims=True)
        return carry + jnp.sum(onehot.astype(F32), axis=1, keepdims=True)

    count = lax.fori_loop(0, n_tiles, rank_body, jnp.zeros((GRP, 1), F32))
    count_ref[...] = jnp.broadcast_to(count, (GRP, LANES)).astype(I32)
    tiles = jnp.floor((count + (TM - 1)) * (1.0 / TM))
    first_tile = _dot(lower_ref[...], jnp.broadcast_to(tiles, (GRP, LANES)).astype(BF16))
    start = first_tile[:, 0:1] * TM

    def dest_body(i, carry):
        onehot = c_idx == cls_ref[i]
        dest = rank_ref[i] + jnp.sum(jnp.where(onehot, start, 0.0), axis=0, keepdims=True)
        dest_ref[i] = dest.astype(I32)
        return carry

    lax.fori_loop(0, n_tiles, dest_body, 0)


def _plan_kernel(n_tok, n_t, dest_ref, count_ref, src_ref, ea_ref, eb_ref, nv_ref, nt_ref):
    tile = 0
    for c in range(N_CLASS):
        g, pi = divmod(c, PAIRS_PER_GROUP)
        a, b = PAIR_TABLE[pi]
        cnt = count_ref[c, 0]
        n_c = lax.shift_right_logical(cnt + (TM - 1), TM.bit_length() - 1)
        first = tile

        def mark(t, carry, g=g, a=a, b=b, cnt=cnt, first=first):
            ea_ref[t] = g * EXPERTS_PER_GROUP + a
            eb_ref[t] = g * EXPERTS_PER_GROUP + b
            nv_ref[t] = jnp.minimum(cnt - (t - first) * TM, TM)
            return carry

        lax.fori_loop(tile, tile + n_c, mark, 0)

        def pad(p, carry):
            src_ref[p] = 0
            return carry

        lax.fori_loop(first * TM + cnt, (first + n_c) * TM, pad, 0)
        tile = tile + n_c
    nt_ref[0] = tile

    def unused(t, carry):
        ea_ref[t] = 0
        eb_ref[t] = 0
        nv_ref[t] = 0
        return carry

    lax.fori_loop(tile, n_t, unused, 0)

    def unused_rows(p, carry):
        src_ref[p] = 0
        return carry

    lax.fori_loop(tile * TM, n_t * TM, unused_rows, 0)

    def place(t, carry):
        src_ref[dest_ref[t]] = t
        return carry

    lax.fori_loop(0, n_tok, place, 0, unroll=8)


def _route_plan(cls3, upper, lower):
    n_tiles = cls3.shape[0]
    n_tok = n_tiles * ROWS
    n_t = (n_tok + N_CLASS * (TM - 1)) // TM + 2
    dest, count = pl.pallas_call(
        _rank_kernel,
        out_shape=[jax.ShapeDtypeStruct((n_tiles, 1, ROWS), I32),
                   jax.ShapeDtypeStruct((GRP, LANES), I32)],
        scratch_shapes=[pltpu.VMEM((n_tiles, 1, ROWS), F32)],
        name="moe_rank",
    )(cls3, upper, lower)
    smem = pl.BlockSpec(memory_space=pltpu.SMEM)
    return pl.pallas_call(
        functools.partial(_plan_kernel, n_tok, n_t),
        in_specs=[smem, smem],
        out_specs=[smem, smem, smem, smem, smem],
        out_shape=[jax.ShapeDtypeStruct((n_t * TM,), I32),
                   jax.ShapeDtypeStruct((n_t,), I32),
                   jax.ShapeDtypeStruct((n_t,), I32),
                   jax.ShapeDtypeStruct((n_t,), I32),
                   jax.ShapeDtypeStruct((1,), I32)],
        name="moe_plan",
    )(dest.reshape(n_tok), count)


def _tile_rows(ref, slot, r):
    return ref.at[pl.ds(pl.multiple_of((slot * TM + r) * SUBLANES, SUBLANES), SUBLANES)]


def _moe_kernel(n_tok, src_ref, ea_ref, eb_ref, nv_ref, nt_ref, ga_ref, gb_ref, h2_hbm, wga_ref,
                wua_ref, wda_ref, wgb_ref, wub_ref, wdb_ref, y_hbm, xbuf, ybuf, gbuf, gsem, ssem):
    i = pl.program_id(0)
    n_used = nt_ref[0]
    lane = lax.broadcasted_iota(I32, (1, LANES), 1)

    def token_tile(hbm, tok):
        return hbm.at[pl.ds(pl.multiple_of(tok * SUBLANES, SUBLANES), SUBLANES)]

    def gather_row(tile, slot, r, priority):
        tok = src_ref[tile * TM + r]
        pltpu.make_async_copy(token_tile(h2_hbm, tok), _tile_rows(xbuf, slot, r),
                              gsem.at[slot]).start(priority=priority)
        gbuf[pl.ds(slot * TM + r, 1), :] = jnp.where(lane == 0, ga_ref[tok], gb_ref[tok])

    def scatter_row(tile, slot, r, n_valid, priority):
        tok = jnp.where(r < n_valid, src_ref[tile * TM + r], n_tok + slot * TM + r)
        pltpu.make_async_copy(_tile_rows(ybuf, slot, r), token_tile(y_hbm, tok),
                              ssem.at[slot]).start(priority=priority)

    def row_loop(start_row):
        def body(r8, c):
            for u in range(DMA_UNROLL):
                start_row(r8 * DMA_UNROLL + u, u % 2)
            return c
        lax.fori_loop(0, TM // DMA_UNROLL, body, 0)

    def wait_gather(slot):
        pltpu.make_async_copy(h2_hbm.at[pl.ds(0, TM * SUBLANES)],
                              xbuf.at[pl.ds(slot * TM * SUBLANES, TM * SUBLANES)],
                              gsem.at[slot]).wait()

    def wait_scatter(slot):
        pltpu.make_async_copy(ybuf.at[pl.ds(slot * TM * SUBLANES, TM * SUBLANES)],
                              y_hbm.at[pl.ds(0, TM * SUBLANES)], ssem.at[slot]).wait()

    @pl.when(i == 0)
    def _():
        ybuf[...] = jnp.zeros_like(ybuf)
        spare = pltpu.make_async_copy(
            ybuf, y_hbm.at[pl.ds(n_tok * SUBLANES, 2 * TM * SUBLANES)], ssem.at[0])
        spare.start()
        spare.wait()

    @pl.when(jnp.logical_and(i == 0, n_used > 0))
    def _():
        row_loop(lambda r, pr: gather_row(0, 0, r, pr))

    @pl.when(i + 1 < n_used)
    def _():
        row_loop(lambda r, pr: gather_row(i + 1, (i + 1) % 2, r, pr))

    @pl.when(jnp.logical_and(i >= 2, i - 2 < n_used))
    def _():
        wait_scatter(i % 2)

    @pl.when(i < n_used)
    def _():
        slot = i % 2
        wait_gather(slot)
        base = slot * TM * SUBLANES
        xb = jnp.concatenate([_token_rows(xbuf, s, TM, base) for s in range(SUBLANES)],
                             axis=1).astype(BF16)
        gates = gbuf[pl.ds(slot * TM, TM), :]
        y = jnp.zeros((TM, D_MODEL), F32)
        for e, (wg, wu, wd) in enumerate(((wga_ref, wua_ref, wda_ref), (wgb_ref, wub_ref, wdb_ref))):
            gt = _dot(xb, wg[...])
            up = _dot(xb, wu[...])
            act = gt * _sigmoid(gt) * up * gates[:, e:e + 1]
            y = y + _dot(act.astype(BF16), wd[...])
        for s in range(SUBLANES):
            ybuf[pl.ds(base + s, TM, stride=SUBLANES), :] = y[:, s * LANES:(s + 1) * LANES]
        n_valid = nv_ref[i]
        row_loop(lambda r, pr: scatter_row(i, slot, r, n_valid, pr))


def _moe(layer, h2, gate_a, gate_b, plan, wg_b, wu_b, wd_b):
    src, ea, eb, nv, nt = plan
    n_t = ea.shape[0]
    n_tok = h2.shape[0] // SUBLANES

    def w_spec(shape, which):
        def index(i, src_ref, ea_ref, eb_ref, nv_ref, nt_ref, ga_ref, gb_ref):
            t = jnp.minimum(i, jnp.maximum(nt_ref[0] - 1, 0))
            e = (ea_ref, eb_ref)[which][t]
            return (layer, e, 0, 0)
        return pl.BlockSpec((None, None) + shape, index)

    gu = (D_MODEL, D_EXPERT)
    dn = (D_EXPERT, D_MODEL)
    grid_spec = pltpu.PrefetchScalarGridSpec(
        num_scalar_prefetch=7,
        grid=(n_t,),
        in_specs=[pl.BlockSpec(memory_space=pl.ANY),
                  w_spec(gu, 0), w_spec(gu, 0), w_spec(dn, 0),
                  w_spec(gu, 1), w_spec(gu, 1), w_spec(dn, 1)],
        out_specs=pl.BlockSpec(memory_space=pl.ANY),
        scratch_shapes=[pltpu.VMEM((2 * TM * SUBLANES, LANES), F32),
                        pltpu.VMEM((2 * TM * SUBLANES, LANES), F32),
                        pltpu.VMEM((2 * TM, LANES), F32),
                        pltpu.SemaphoreType.DMA((2,)), pltpu.SemaphoreType.DMA((2,))],
    )
    return pl.pallas_call(
        functools.partial(_moe_kernel, n_tok),
        grid_spec=grid_spec,
        out_shape=jax.ShapeDtypeStruct(((n_tok + 2 * TM) * SUBLANES, LANES), F32),
        compiler_params=_cparams(("arbitrary",)),
        name=f"moe{layer}",
    )(src, ea, eb, nv, nt, gate_a.reshape(n_tok), gate_b.reshape(n_tok), h2,
      wg_b, wu_b, wd_b, wg_b, wu_b, wd_b)


def _pair_state_in(state):
    b = state.shape[0]
    st = jnp.swapaxes(state, -1, -2).reshape(b, N_PAIR, 2, DV_B, DK_B)
    z = jnp.zeros_like(st[:, :, 0])
    top = jnp.concatenate([st[:, :, 0], z], axis=-1)
    bot = jnp.concatenate([z, st[:, :, 1]], axis=-1)
    return jnp.concatenate([top, bot], axis=-2)


def _pair_state_out(st):
    b = st.shape[0]
    h0 = st[:, :, :DV_B, :DK_B]
    h1 = st[:, :, DV_B:, DK_B:]
    out = jnp.stack([h0, h1], axis=2).reshape(b, N_HEADS_B, DV_B, DK_B)
    return jnp.swapaxes(out, -1, -2)


def kernel(x_prompt, x_sample, cache_k, cache_v, state_hgrn, c_prompt, c_sample, w_ada, b_ada,
           ln1, ln2, w_in, q_norm, k_norm, sinks, lb_param, o_norm, w_out, w_router, router_bias,
           w_e_gate, w_e_up, w_e_down):
    batch, seq, _ = x_prompt.shape
    dec_batch, dec_seq, _ = x_sample.shape
    depth = w_ada.shape[0]
    assert dec_batch == GRP and dec_seq == SUB and seq % ROWS == 0
    tiles_per_batch = seq // ROWS
    n_prompt = batch * seq
    n_tok = n_prompt + dec_batch * dec_seq
    n_tiles = n_tok // ROWS
    prompt_tiles = n_prompt // ROWS
    n_cache = cache_k.shape[2]

    seg = np.arange(Q_A) // HEAD_DIM
    bd = jnp.asarray((seg[:, None] == seg[None, :]).astype(np.float32) / HEAD_DIM, BF16)

    def hgrn_consts(length):
        selb, sell, masks = _hgrn_consts(length)
        return jnp.asarray(selb, BF16), jnp.asarray(sell, BF16), jnp.asarray(masks)

    consts_p = hgrn_consts(CHUNK)
    consts_s = hgrn_consts(dec_seq)
    upper = jnp.asarray(np.triu(np.ones((ROWS, ROWS), np.float32), 1), BF16)
    lower = jnp.asarray(np.tril(np.ones((GRP, GRP), np.float32), -1), BF16)

    wq = w_in[:, :, :Q_A].reshape(depth, D_MODEL, N_KV_A, GQA_GROUP, HEAD_DIM)
    wq = jnp.swapaxes(wq, 2, 3).reshape(depth, D_MODEL, Q_A)
    w_in_b = jnp.concatenate([wq, w_in[:, :, Q_A:]], axis=-1).astype(BF16)
    wo = w_out[:, :Q_A].reshape(depth, N_KV_A, GQA_GROUP, HEAD_DIM, D_MODEL)
    wo = jnp.swapaxes(wo, 1, 2).reshape(depth, Q_A, D_MODEL)
    w_out_b = jnp.concatenate([wo, w_out[:, Q_A:]], axis=1).astype(BF16)
    wg_b, wu_b, wd_b = w_e_gate.astype(BF16), w_e_up.astype(BF16), w_e_down.astype(BF16)
    q_norm_t = jnp.tile(q_norm, (1, N_HEADS_A)).reshape(depth, 1, Q_A)
    k_norm_t = jnp.tile(k_norm, (1, N_KV_A)).reshape(depth, 1, KV_A)
    o_norm_t = jnp.tile(o_norm, (1, N_HEADS_B)).reshape(depth, 1, V_B)
    ln1_r = ln1.reshape(depth, 1, D_MODEL)
    ln2_r = ln2.reshape(depth, 1, D_MODEL)
    w_router_t = w_router.T
    router_bias_c = router_bias.reshape(N_EXPERTS, 1)

    c_exp = jnp.concatenate([jnp.repeat(c_prompt, GRP, axis=0), c_sample], axis=0)
    mod = _adaln(c_exp, w_ada, b_ada).reshape(depth, batch + 1, GRP, N_MOD * D_MODEL)

    cache_k2 = cache_k.reshape(depth, dec_batch, n_cache, KV_A)
    cache_v2 = cache_v.reshape(depth, dec_batch, n_cache, KV_A)

    src = (x_prompt, x_sample)
    kp, vp, sp, kss, vss, sss = [], [], [], [], [], []
    for l in range(depth):
        x, q4, k, v, qb, logf, kb, ib, gb = _inproj(
            l, src, n_tiles, prompt_tiles, mod, ln1_r, w_in_b, q_norm_t, k_norm_t, lb_param, bd,
            tiles_per_batch)

        oap4 = _attn_prompt(q4, k, v, sinks[l], batch, seq)
        oas4 = _attn_sample(q4, k, v, cache_k2[l], cache_v2[l], sinks[l], prompt_tiles,
                            n_prompt, dec_batch, dec_seq)
        obp, s_p = _hgrn_prompt(qb, kb, ib, gb, logf, consts_p, bd, o_norm_t[l], batch, seq)
        obs, s_s = _hgrn_sample(qb, kb, ib, gb, logf, _pair_state_in(state_hgrn[l]), consts_s,
                                bd, o_norm_t[l], n_prompt, dec_batch, dec_seq)

        x1, h2, cls3, gate_a, gate_b = _outproj(l, x, oap4, obp, oas4, obs, mod, ln2_r, w_out_b,
                                                w_router_t, router_bias_c, tiles_per_batch)
        y = _moe(l, h2, gate_a, gate_b, _route_plan(cls3, upper, lower), wg_b, wu_b, wd_b)
        src = (x1, y)

        kp.append(k[:n_prompt].reshape(batch, seq, N_KV_A, HEAD_DIM)[:, -WINDOW:])
        vp.append(v[:n_prompt].reshape(batch, seq, N_KV_A, HEAD_DIM)[:, -WINDOW:])
        sp.append(_pair_state_out(s_p[:, 0]))
        kss.append(k[n_prompt:].reshape(dec_batch, dec_seq, N_KV_A, HEAD_DIM))
        vss.append(v[n_prompt:].reshape(dec_batch, dec_seq, N_KV_A, HEAD_DIM))
        sss.append(_pair_state_out(s_s))

    yp, ys = _final(depth - 1, src[0], src[1], mod, batch, seq, tiles_per_batch)
    return (yp, ys, jnp.stack(kp), jnp.stack(vp), jnp.stack(sp),
            jnp.stack(kss), jnp.stack(vss), jnp.stack(sss))
```

```python
import functools

import numpy as np
import jax
import jax.numpy as jnp
from jax import lax
from jax.experimental import pallas as pl
from jax.experimental.pallas import tpu as pltpu

F32 = jnp.float32
BF16 = jnp.bfloat16
I32 = jnp.int32

D_MODEL = 1024
HEAD_DIM = 64
N_HEADS_A = 8
N_KV_A = 2
GQA_GROUP = N_HEADS_A // N_KV_A
N_HEADS_B = 8
DK_B = 64
DV_B = 64
Q_A = N_HEADS_A * HEAD_DIM
KV_A = N_KV_A * HEAD_DIM
W_B = N_HEADS_B * DK_B
V_B = N_HEADS_B * DV_B
CHUNK = 64
WINDOW = 128
N_EXPERTS = 16
N_GROUPS = 4
EXPERTS_PER_GROUP = 4
D_EXPERT = D_MODEL // 2
RMS_EPS = 1e-6
NEG_INF = -1e30
N_MOD = 6

LANES = 128
SUBLANES = 8
ROWS = 512
GRP = 32
SUB = ROWS // GRP
ATT_ROWS = 256
N_PAIR = N_HEADS_B // 2
SAMPLE_SEQS = 4
HGRN_SPAN_LIMIT = 80.0
TM = 256
PAIRS_PER_GROUP = 6
PAIR_TABLE = ((0, 1), (0, 2), (0, 3), (1, 2), (1, 3), (2, 3))
N_CLASS = N_GROUPS * PAIRS_PER_GROUP
DMA_UNROLL = 8
VMEM_LIMIT = 56 * 1024 * 1024


def _cparams(sem):
    return pltpu.CompilerParams(dimension_semantics=sem, vmem_limit_bytes=VMEM_LIMIT)


def _dot(a, b):
    return jnp.dot(a, b, preferred_element_type=F32)


def _dot_nt(a, b):
    return lax.dot_general(a, b, (((1,), (1,)), ((), ())), preferred_element_type=F32)


def _dot_tn(a, b):
    return lax.dot_general(a, b, (((0,), (0,)), ((), ())), preferred_element_type=F32)


def _sigmoid(x):
    return 1.0 / (1.0 + jnp.exp(-x))


def _split2(x):
    hi = x.astype(BF16)
    lo = (x - hi.astype(F32)).astype(BF16)
    return hi, lo


def _split3(x):
    hi = x.astype(BF16)
    r = x - hi.astype(F32)
    mid = r.astype(BF16)
    lo = (r - mid.astype(F32)).astype(BF16)
    return hi, mid, lo


def _seg_mean(sq, bd):
    return _dot(sq.astype(BF16), bd)


def _token_rows(ref, s, n, base=0):
    return ref[pl.ds(base + s, n, stride=SUBLANES), :]


def _adaln_kernel(c_ref, w_ref, b_ref, o_ref):
    c = c_ref[...]
    a = c * _sigmoid(c)
    a_hi, a_lo = _split2(a)
    w_hi, w_lo = _split2(w_ref[...])
    acc = _dot(a_hi, w_hi) + _dot(a_lo, w_hi) + _dot(a_hi, w_lo)
    o_ref[...] = acc + b_ref[...]


def _adaln(c_exp, w_ada, b_ada):
    depth = w_ada.shape[0]
    rows = c_exp.shape[0]
    tn = D_MODEL
    return pl.pallas_call(
        _adaln_kernel,
        grid=(depth, N_MOD * D_MODEL // tn),
        in_specs=[
            pl.BlockSpec((rows, D_MODEL), lambda l, n: (0, 0)),
            pl.BlockSpec((None, D_MODEL, tn), lambda l, n: (l, 0, n)),
            pl.BlockSpec((None, 1, tn), lambda l, n: (l, 0, n)),
        ],
        out_specs=pl.BlockSpec((None, rows, tn), lambda l, n: (l, 0, n)),
        out_shape=jax.ShapeDtypeStruct((depth, rows, N_MOD * D_MODEL), F32),
        compiler_params=_cparams(("arbitrary", "arbitrary")),
        name="adaln",
    )(c_exp, w_ada, b_ada.reshape(depth, 1, N_MOD * D_MODEL))


def _mod_spec(layer, j, tiles_per_batch):
    return pl.BlockSpec((None, None, GRP, D_MODEL),
                        lambda i, *_: (layer, i // tiles_per_batch, 0, j))


def _by_group(fn, xs, mod_refs):
    outs = []
    for g in range(GRP):
        rows = slice(g * SUB, (g + 1) * SUB)
        outs.append(fn(*[x[rows] for x in xs], *[m[g:g + 1, :] for m in mod_refs]))
    return jnp.concatenate(outs, axis=0)


def _gated_add(x, gate_ref, y):
    return _by_group(lambda xg, yg, gg: xg + gg * yg, (x, y), (gate_ref,))


def _modulated_norm(x, ln_ref, sc_ref, sh_ref):
    ms = jnp.mean(x * x, axis=-1, keepdims=True)
    xn = x * lax.rsqrt(ms + RMS_EPS) * ln_ref[...]
    return _by_group(lambda xg, sc, sh: xg * (1.0 + sc) + sh, (xn,), (sc_ref, sh_ref))


def _prompt_tile_spec(prompt_tiles, tiles_per_batch):
    def index(i):
        t = jnp.minimum(i, prompt_tiles - 1)
        return (t // tiles_per_batch, t % tiles_per_batch, 0)
    return pl.BlockSpec((None, ROWS, D_MODEL), index)


def _moe_output_tile(y_ref):
    return jnp.concatenate([_token_rows(y_ref, s, ROWS) for s in range(SUBLANES)], axis=1)


def _inproj_kernel(layer, prompt_tiles, n_src, *refs):
    src = refs[:n_src]
    sh_ref, sc_ref, ln_ref, w_ref, qn_ref, kn_ref, lbp_ref, bd_ref = refs[n_src:n_src + 8]
    (x_out, q_out, k_out, v_out, qb_out, logf_out, kb_out, ib_out, gb_out,
     kwin_out, vwin_out) = refs[n_src + 8:]
    if layer == 0:
        xp_ref, xs_ref = src
        x = jnp.where(pl.program_id(0) < prompt_tiles, xp_ref[...],
                      xs_ref[...].reshape(ROWS, D_MODEL))
    else:
        x1_ref, y_ref, g2_ref = src
        x = _gated_add(x1_ref[...], g2_ref, _moe_output_tile(y_ref))
    x_out[...] = x
    hb = _modulated_norm(x, ln_ref, sc_ref, sh_ref).astype(BF16)
    bd = bd_ref[...]

    qa = _dot(hb, w_ref[:, 0:Q_A])
    qa = qa * lax.rsqrt(_seg_mean(qa * qa, bd) + RMS_EPS) * qn_ref[...]
    qa = (qa * (HEAD_DIM ** -0.5)).astype(BF16)
    for hh in range(GQA_GROUP):
        q_out[hh] = qa[:, hh * LANES:(hh + 1) * LANES]

    kv = _dot(hb, w_ref[:, Q_A:Q_A + 2 * KV_A])
    ka = kv[:, :KV_A]
    ka = ka * lax.rsqrt(_seg_mean(ka * ka, bd[:KV_A, :KV_A]) + RMS_EPS) * kn_ref[...]
    k_out[...] = ka
    v_out[...] = kv[:, KV_A:]

    @pl.when(pl.program_id(0) < prompt_tiles)
    def _():
        kwin_out[...] = ka[ROWS - WINDOW:]
        vwin_out[...] = kv[ROWS - WINDOW:, KV_A:]

    c0 = Q_A + 2 * KV_A
    qb_out[...] = (_dot(hb, w_ref[:, c0:c0 + W_B]) * (DK_B ** -0.5)).astype(BF16)

    p = lbp_ref[...]
    e = jnp.exp(p - jnp.max(p, axis=0, keepdims=True))
    sm = e / jnp.sum(e, axis=0, keepdims=True)
    cum0 = sm[0:1]
    cum = cum0
    for r in range(1, layer + 1):
        cum = cum + sm[r:r + 1]
    lb = cum - cum0

    z = _dot(hb, w_ref[:, c0 + W_B:c0 + 2 * W_B])
    pos = z >= 0.0
    t = jnp.exp(-jnp.abs(z))
    num = jnp.where(pos, jnp.log(1.0 + lb * t), jnp.maximum(jnp.log(lb + t), z))
    logf_out[...] = num - jnp.log(1.0 + t)
    kb_out[...] = ((1.0 - lb) * jnp.where(pos, t, 1.0) / (1.0 + t)).astype(BF16)

    c1 = c0 + 2 * W_B
    ib_out[...] = _dot(hb, w_ref[:, c1:c1 + V_B]).astype(BF16)
    gb_out[...] = _dot(hb, w_ref[:, c1 + V_B:c1 + 2 * V_B]).astype(BF16)


def _inproj(layer, src, n_tiles, prompt_tiles, mod, ln1, w_in_b, q_norm_t, k_norm_t, lb_param, bd,
            tiles_per_batch):
    depth = lb_param.shape[0]
    n = n_tiles * ROWS
    in_width = w_in_b.shape[-1]
    x_spec = pl.BlockSpec((ROWS, D_MODEL), lambda i: (i, 0))
    if layer == 0:
        src_specs = [_prompt_tile_spec(prompt_tiles, tiles_per_batch),
                     pl.BlockSpec((GRP, SUB, D_MODEL), lambda i: (0, 0, 0))]
        src_args = list(src)
    else:
        src_specs = [x_spec, pl.BlockSpec((ROWS * SUBLANES, LANES), lambda i: (i, 0)),
                     _mod_spec(layer - 1, 5, tiles_per_batch)]
        src_args = list(src) + [mod]
    row_spec = lambda w: pl.BlockSpec((ROWS, w), lambda i: (i, 0))
    win_spec = pl.BlockSpec(
        (None, WINDOW, KV_A), lambda i: (jnp.minimum(i, prompt_tiles - 1) // tiles_per_batch, 0, 0))
    return pl.pallas_call(
        functools.partial(_inproj_kernel, layer, prompt_tiles, len(src_specs)),
        grid=(n_tiles,),
        in_specs=src_specs + [
            _mod_spec(layer, 0, tiles_per_batch), _mod_spec(layer, 1, tiles_per_batch),
            pl.BlockSpec((None, 1, D_MODEL), lambda i: (layer, 0, 0)),
            pl.BlockSpec((None, D_MODEL, in_width), lambda i: (layer, 0, 0)),
            pl.BlockSpec((None, 1, Q_A), lambda i: (layer, 0, 0)),
            pl.BlockSpec((None, 1, KV_A), lambda i: (layer, 0, 0)),
            pl.BlockSpec((depth, W_B), lambda i: (0, 0)),
            pl.BlockSpec((Q_A, Q_A), lambda i: (0, 0)),
        ],
        out_specs=[
            x_spec,
            pl.BlockSpec((None, GQA_GROUP, ROWS, LANES), lambda i: (i, 0, 0, 0)),
            row_spec(KV_A), row_spec(KV_A),
            row_spec(W_B), row_spec(W_B), row_spec(W_B), row_spec(V_B), row_spec(V_B),
            win_spec, win_spec,
        ],
        out_shape=[
            jax.ShapeDtypeStruct((n, D_MODEL), F32),
            jax.ShapeDtypeStruct((n_tiles, GQA_GROUP, ROWS, LANES), BF16),
            jax.ShapeDtypeStruct((n, KV_A), F32),
            jax.ShapeDtypeStruct((n, KV_A), F32),
            jax.ShapeDtypeStruct((n, W_B), BF16),
            jax.ShapeDtypeStruct((n, W_B), F32),
            jax.ShapeDtypeStruct((n, W_B), BF16),
            jax.ShapeDtypeStruct((n, V_B), BF16),
            jax.ShapeDtypeStruct((n, V_B), BF16),
            jax.ShapeDtypeStruct((prompt_tiles // tiles_per_batch, WINDOW, KV_A), F32),
            jax.ShapeDtypeStruct((prompt_tiles // tiles_per_batch, WINDOW, KV_A), F32),
        ],
        compiler_params=_cparams(("arbitrary",)),
        name=f"inproj{layer}",
    )(*src_args, mod, mod, ln1, w_in_b, q_norm_t, k_norm_t, lb_param, bd)


def _final_kernel(prompt_tiles, x1_ref, y_ref, g2_ref, yp_ref, ys_ref):
    i = pl.program_id(0)
    x = _gated_add(x1_ref[...], g2_ref, _moe_output_tile(y_ref))

    @pl.when(i < prompt_tiles)
    def _():
        yp_ref[...] = x

    @pl.when(i >= prompt_tiles)
    def _():
        ys_ref[...] = x.reshape(GRP, SUB, D_MODEL)


def _final(layer, x1, y, mod, batch, seq, tiles_per_batch):
    n_tiles = x1.shape[0] // ROWS
    prompt_tiles = batch * tiles_per_batch
    return pl.pallas_call(
        functools.partial(_final_kernel, prompt_tiles),
        grid=(n_tiles,),
        in_specs=[pl.BlockSpec((ROWS, D_MODEL), lambda i: (i, 0)),
                  pl.BlockSpec((ROWS * SUBLANES, LANES), lambda i: (i, 0)),
                  _mod_spec(layer, 5, tiles_per_batch)],
        out_specs=[_prompt_tile_spec(prompt_tiles, tiles_per_batch),
                   pl.BlockSpec((GRP, SUB, D_MODEL), lambda i: (0, 0, 0))],
        out_shape=[jax.ShapeDtypeStruct((batch, seq, D_MODEL), F32),
                   jax.ShapeDtypeStruct((GRP, SUB, D_MODEL), F32)],
        compiler_params=_cparams(("arbitrary",)),
        name="final",
    )(x1, y, mod)


def _attend(q, k, v, sinks_ref, q_len, k_off, k_pos0=None):
    n_q = q.shape[0]
    n_k = k.shape[0]
    row = lax.broadcasted_iota(I32, (n_q, n_k), 0)
    col = lax.broadcasted_iota(I32, (n_q, n_k), 1)
    dist = jnp.abs((row % q_len) + k_off - col).astype(F32)
    lane = lax.broadcasted_iota(I32, (n_k, LANES), 1)
    hh_col = lax.broadcasted_iota(I32, (n_q, 1), 0) // q_len
    out = jnp.zeros((n_q, LANES), F32)
    for g in range(N_KV_A):
        in_g = (lane // HEAD_DIM) == g
        kg = jnp.where(in_g, k, 0.0).astype(BF16)
        vg = jnp.where(in_g, v, 0.0).astype(BF16)
        s = _dot_nt(q, kg)
        head = (g * GQA_GROUP + 1 + hh_col).astype(F32)
        s = s - jnp.exp2(-8.0 * head / N_HEADS_A) * dist
        if k_pos0 is not None:
            s = jnp.where(col + k_pos0 >= 0, s, NEG_INF)
        sink = jnp.zeros((n_q, 1), F32)
        for j in range(GQA_GROUP):
            sink = jnp.where(hh_col == j, sinks_ref[g * GQA_GROUP + j], sink)
        m = jnp.maximum(jnp.max(s, axis=-1, keepdims=True), sink)
        p = jnp.exp(s - m)
        denom = jnp.sum(p, axis=-1, keepdims=True) + jnp.exp(sink - m)
        out = out + _dot(p.astype(BF16), vg) / denom
    return out


def _attn_prompt_kernel(sinks_ref, q_ref, kp_ref, kc_ref, vp_ref, vc_ref, o_ref):
    j = pl.program_id(1)
    kcat = jnp.concatenate([kp_ref[...], kc_ref[...]], axis=0)
    vcat = jnp.concatenate([vp_ref[...], vc_ref[...]], axis=0)
    for c in range(ATT_ROWS // CHUNK):
        q = q_ref[:, c * CHUNK:(c + 1) * CHUNK, :].reshape(GQA_GROUP * CHUNK, LANES)
        k = kcat[c * CHUNK:c * CHUNK + WINDOW + CHUNK]
        v = vcat[c * CHUNK:c * CHUNK + WINDOW + CHUNK]
        k_pos0 = j * ATT_ROWS + c * CHUNK - WINDOW
        out = _attend(q, k, v, sinks_ref, CHUNK, WINDOW, k_pos0)
        o_ref[:, c * CHUNK:(c + 1) * CHUNK, :] = out.reshape(GQA_GROUP, CHUNK, LANES).astype(BF16)


def _attn_prompt(q4, k, v, sinks, batch, seq):
    steps = seq // ATT_ROWS
    per_tile = ROWS // ATT_ROWS
    win_blocks = ATT_ROWS // WINDOW
    n_tiles = batch * seq // ROWS
    cur = pl.BlockSpec((ATT_ROWS, KV_A), lambda b, j: (b * steps + j, 0))
    prev = pl.BlockSpec(
        (WINDOW, KV_A), lambda b, j: (b * steps * win_blocks + jnp.maximum(j * win_blocks - 1, 0), 0))
    qo = pl.BlockSpec((None, GQA_GROUP, ATT_ROWS, LANES),
                      lambda b, j: ((b * steps + j) // per_tile, 0, j % per_tile, 0))
    return pl.pallas_call(
        _attn_prompt_kernel,
        grid=(batch, steps),
        in_specs=[pl.BlockSpec(memory_space=pltpu.SMEM), qo, prev, cur, prev, cur],
        out_specs=qo,
        out_shape=jax.ShapeDtypeStruct((n_tiles, GQA_GROUP, ROWS, LANES), BF16),
        compiler_params=_cparams(("arbitrary", "arbitrary")),
        name="attn_prompt",
    )(sinks, q4, k, k, v, v)


def _attn_sample_kernel(n_new, sinks_ref, q_ref, kc_ref, kn_ref, vc_ref, vn_ref, o_ref):
    n_cache = kc_ref.shape[1]
    for r in range(SAMPLE_SEQS):
        rows = slice(r * n_new, (r + 1) * n_new)
        q = q_ref[:, rows, :].reshape(GQA_GROUP * n_new, LANES)
        k = jnp.concatenate([kc_ref[r], kn_ref[rows, :]], axis=0)
        v = jnp.concatenate([vc_ref[r], vn_ref[rows, :]], axis=0)
        out = _attend(q, k, v, sinks_ref, n_new, n_cache)
        o_ref[:, rows, :] = out.reshape(GQA_GROUP, n_new, LANES).astype(BF16)


def _attn_sample(q4, k, v, cache_k, cache_v, sinks, tile, row0, dec_batch, dec_seq):
    n_cache = cache_k.shape[1]
    rows = SAMPLE_SEQS * dec_seq
    q_spec = pl.BlockSpec((None, GQA_GROUP, rows, LANES), lambda b: (tile, 0, b, 0))
    new = pl.BlockSpec((rows, KV_A), lambda b: (row0 // rows + b, 0))
    cache = pl.BlockSpec((SAMPLE_SEQS, n_cache, KV_A), lambda b: (b, 0, 0))
    return pl.pallas_call(
        functools.partial(_attn_sample_kernel, dec_seq),
        grid=(dec_batch // SAMPLE_SEQS,),
        in_specs=[pl.BlockSpec(memory_space=pltpu.SMEM), q_spec, cache, new, cache, new],
        out_specs=pl.BlockSpec((None, GQA_GROUP, rows, LANES), lambda b: (0, 0, b, 0)),
        out_shape=jax.ShapeDtypeStruct((1, GQA_GROUP, dec_batch * dec_seq, LANES), BF16),
        compiler_params=_cparams(("arbitrary",)),
        name="attn_sample",
    )(sinks, q4, cache_k, k, cache_v, v)


def _level_sizes(length):
    sizes = []
    m = length // 2
    while m >= 1:
        sizes.append(m)
        m //= 2
    return sizes


def _hgrn_consts(length):
    t = np.arange(length)
    tri = (t[None, :] <= t[:, None]).astype(np.float32)
    after = (t[None, :] > t[:, None]).astype(np.float32)
    levels = []
    masks = []
    s = np.tile(t, 2)[None, :]
    tt = t[:, None]
    for m in _level_sizes(length):
        blk = t // m
        same = blk[None, :] == blk[:, None]
        q_rows = (blk % 2 == 1)[:, None]
        sel = np.where(q_rows, same & (t[None, :] <= t[:, None]), same & (t[None, :] > t[:, None]))
        levels.append(sel.astype(np.float32))
        masks.append(((tt // m) % 2 == 1) & ((s // m) == (tt // m) - 1))
    masks.append(s == tt)
    masks.append(s <= tt)
    base = np.concatenate([tri, after], axis=0)
    lev = np.concatenate(levels, axis=0)
    return (np.concatenate([base] * 3, axis=1), np.concatenate([lev] * 3, axis=1),
            np.stack(masks).astype(np.int32))


def _stack2(x, m0, m1):
    xb = x.astype(BF16)
    return jnp.concatenate([xb * m0, xb * m1], axis=0)


def _hgrn_intra(qp, kp, b, masks, m0, m1, arg_ref, sl):
    length = qp.shape[0]
    n_lev = len(_level_sizes(length))
    if arg_ref is None:
        mid = length // 2 - 1
        rel = b - b[mid:mid + 1]
        a = _dot_nt((qp * jnp.exp(rel)).astype(BF16), _stack2(kp * jnp.exp(-rel), m0, m1))
        a = jnp.where(masks[n_lev + 1], a, 0.0)
    else:
        a = _dot_nt(qp.astype(BF16), _stack2(kp, m0, m1))
        a = jnp.where(masks[n_lev], a, 0.0)
        for lev in range(n_lev):
            e = jnp.exp(arg_ref[lev * length:(lev + 1) * length, sl])
            pr = _dot_nt((qp * e).astype(BF16), _stack2(kp * e, m0, m1))
            a = jnp.where(masks[lev], pr, a)
    return a.astype(BF16)


def _hgrn_finish(a, qp, kp, ip, base, sl, st_ref, seq, p, m0, m1, same_head):
    length = qp.shape[0]
    e_b = jnp.exp(base[0:length, sl])
    e_k = jnp.exp(base[length:2 * length, sl])
    st = st_ref[seq, p]
    o = _dot(jnp.concatenate([a, (qp * e_b).astype(BF16)], axis=1),
             jnp.concatenate([_stack2(ip, m0, m1), st.T.astype(BF16)], axis=0))
    upd = _dot_tn(ip.astype(BF16), (kp * e_k).astype(BF16))
    st_ref[seq, p] = st * e_b[length - 1:length, :] + jnp.where(same_head, upd, 0.0)
    return o


def _state_to_pairs(s0_ref, st_ref, n_seq):
    zero = jnp.zeros((DK_B, DV_B), F32)
    for r in range(n_seq):
        for p in range(N_PAIR):
            top = jnp.concatenate([s0_ref[r, 2 * p], zero], axis=1)
            bot = jnp.concatenate([zero, s0_ref[r, 2 * p + 1]], axis=1)
            st_ref[r, p] = jnp.concatenate([top, bot], axis=0).T


def _pairs_to_state(st_ref, s_ref, n_seq):
    for r in range(n_seq):
        for p in range(N_PAIR):
            d = st_ref[r, p].T
            s_ref[r, 2 * p] = d[:DK_B, :DV_B]
            s_ref[r, 2 * p + 1] = d[DK_B:, DV_B:]


def _hgrn_kernel(n_seq, n_chunk, length, carry_state, *refs):
    if carry_state:
        (q_ref, k_ref, i_ref, g_ref, f_ref, selb_ref, sell_ref, mask_ref, bd_ref, on_ref,
         o_ref, s_ref, st_ref, base_ref, arg_ref) = refs

        @pl.when(pl.program_id(1) == 0)
        def _():
            st_ref[...] = jnp.zeros_like(st_ref)
    else:
        (q_ref, k_ref, i_ref, g_ref, f_ref, s0_ref, selb_ref, sell_ref, mask_ref, bd_ref, on_ref,
         o_ref, s_ref, st_ref, base_ref, arg_ref) = refs
        _state_to_pairs(s0_ref, st_ref, n_seq)

    bd = bd_ref[...]
    onorm = on_ref[...]
    n_items = n_seq * n_chunk
    mid = length // 2 - 1
    lane = lax.broadcasted_iota(I32, (length, LANES), 1)
    m0 = (lane < DK_B).astype(BF16)
    m1 = (lane >= DK_B).astype(BF16)
    rowp = lax.broadcasted_iota(I32, (LANES, LANES), 0) // DV_B
    colp = lax.broadcasted_iota(I32, (LANES, LANES), 1) // DK_B
    same_head = rowp == colp
    masks = [mask_ref[m] != 0 for m in range(mask_ref.shape[0])]

    def pieces(rows):
        return jnp.concatenate(_split3(f_ref[rows, :]), axis=0)

    def pair(ref, rows, p):
        return ref[rows, p * LANES:(p + 1) * LANES].astype(F32)

    span = jnp.float32(0.0)
    for it in range(n_items):
        base = _dot(selb_ref[...], pieces(slice(it * length, (it + 1) * length)))
        base_ref[it * 2 * length:(it + 1) * 2 * length, :] = base
        b_mid = base[mid:mid + 1]
        span = jnp.maximum(span, jnp.max(base[0:1] - b_mid))
        span = jnp.maximum(span, jnp.max(b_mid - base[length - 1:length]))
    safe = span <= HGRN_SPAN_LIMIT

    def finish(rows, base, seq, a_of):
        outs = []
        for p in range(N_PAIR):
            sl = slice(p * LANES, (p + 1) * LANES)
            outs.append(_hgrn_finish(a_of(p), pair(q_ref, rows, p), pair(k_ref, rows, p),
                                     pair(i_ref, rows, p), base, sl, st_ref, seq, p, m0, m1,
                                     same_head))
        o = jnp.concatenate(outs, axis=1)
        g = g_ref[rows, :].astype(F32)
        o = o * lax.rsqrt(_seg_mean(o * o, bd) + RMS_EPS) * onorm * (g * _sigmoid(g))
        o_ref[rows, :] = o.astype(BF16)

    @pl.when(safe)
    def _():
        for it in range(n_items):
            rows = slice(it * length, (it + 1) * length)
            base = base_ref[it * 2 * length:(it + 1) * 2 * length, :]

            def a_of(p, rows=rows, base=base):
                sl = slice(p * LANES, (p + 1) * LANES)
                return _hgrn_intra(pair(q_ref, rows, p), pair(k_ref, rows, p), base[0:length, sl],
                                   masks, m0, m1, None, sl)

            finish(rows, base, it // n_chunk, a_of)

    @pl.when(jnp.logical_not(safe))
    def _():
        def body(it, carry):
            rows = pl.ds(pl.multiple_of(it * length, length), length)
            arg_ref[...] = _dot(sell_ref[...], pieces(rows))
            base = base_ref[pl.ds(pl.multiple_of(it * 2 * length, 2 * length), 2 * length), :]
            seq = it // n_chunk if n_chunk > 1 and n_seq > 1 else (it if n_chunk == 1 else 0)

            def a_of(p):
                sl = slice(p * LANES, (p + 1) * LANES)
                return _hgrn_intra(pair(q_ref, rows, p), pair(k_ref, rows, p), base[0:length, sl],
                                   masks, m0, m1, arg_ref, sl)

            finish(rows, base, seq, a_of)
            return carry

        lax.fori_loop(0, n_items, body, 0)

    if carry_state:
        @pl.when(pl.program_id(1) == pl.num_programs(1) - 1)
        def _():
            _pairs_to_state(st_ref, s_ref, n_seq)
    else:
        _pairs_to_state(st_ref, s_ref, n_seq)


def _hgrn_scratch(n_seq, n_items, length, sell):
    return [pltpu.VMEM((n_seq, N_PAIR, LANES, LANES), F32),
            pltpu.VMEM((n_items * 2 * length, W_B), F32),
            pltpu.VMEM((sell.shape[0], W_B), F32)]


def _hgrn_prompt(qb, kb, ib, gb, logf, consts, bd, onorm_t, batch, seq):
    selb, sell, masks = consts
    steps = seq // ATT_ROWS
    n_chunk = ATT_ROWS // CHUNK
    row = pl.BlockSpec((ATT_ROWS, W_B), lambda b, j: (b * steps + j, 0))
    whole = lambda a: pl.BlockSpec(a.shape, lambda b, j: (0,) * a.ndim)
    return pl.pallas_call(
        functools.partial(_hgrn_kernel, 1, n_chunk, CHUNK, True),
        grid=(batch, steps),
        in_specs=[row, row, row, row, row, whole(selb), whole(sell), whole(masks), whole(bd),
                  whole(onorm_t)],
        out_specs=[row, pl.BlockSpec((1, N_HEADS_B, DK_B, DV_B), lambda b, j: (b, 0, 0, 0))],
        out_shape=[jax.ShapeDtypeStruct((batch * seq, V_B), BF16),
                   jax.ShapeDtypeStruct((batch, N_HEADS_B, DK_B, DV_B), F32)],
        scratch_shapes=_hgrn_scratch(1, n_chunk, CHUNK, sell),
        compiler_params=_cparams(("arbitrary", "arbitrary")),
        name="hgrn_prompt",
    )(qb, kb, ib, gb, logf, selb, sell, masks, bd, onorm_t)


def _hgrn_sample(qb, kb, ib, gb, logf, s0, consts, bd, onorm_t, row0, dec_batch, dec_seq):
    selb, sell, masks = consts
    rows = SAMPLE_SEQS * dec_seq
    row = pl.BlockSpec((rows, W_B), lambda b: (row0 // rows + b, 0))
    whole = lambda a: pl.BlockSpec(a.shape, lambda b: (0,) * a.ndim)
    state = pl.BlockSpec((SAMPLE_SEQS, N_HEADS_B, DK_B, DV_B), lambda b: (b, 0, 0, 0))
    return pl.pallas_call(
        functools.partial(_hgrn_kernel, SAMPLE_SEQS, 1, dec_seq, False),
        grid=(dec_batch // SAMPLE_SEQS,),
        in_specs=[row, row, row, row, row, state, whole(selb), whole(sell), whole(masks),
                  whole(bd), whole(onorm_t)],
        out_specs=[pl.BlockSpec((rows, V_B), lambda b: (b, 0)), state],
        out_shape=[jax.ShapeDtypeStruct((dec_batch * dec_seq, V_B), BF16),
                   jax.ShapeDtypeStruct((dec_batch, N_HEADS_B, DK_B, DV_B), F32)],
        scratch_shapes=_hgrn_scratch(SAMPLE_SEQS, SAMPLE_SEQS, dec_seq, sell),
        compiler_params=_cparams(("arbitrary",)),
        name="hgrn_sample",
    )(qb, kb, ib, gb, logf, s0, selb, sell, masks, bd, onorm_t)


def _outproj_kernel(prompt_tiles, x_ref, oap_ref, obp_ref, oas_ref, obs_ref, g1_ref, sh_ref, sc_ref,
                    ln_ref, w_ref, wr_ref, rb_ref, x1_out, h2_out, cls_out, ga_out, gb_out):
    def mix_of(oa_ref, ob_ref):
        return jnp.concatenate([oa_ref[hh] for hh in range(GQA_GROUP)] + [ob_ref[...]], axis=1)

    is_prompt = pl.program_id(0) < prompt_tiles
    mix = jnp.where(is_prompt, mix_of(oap_ref, obp_ref), mix_of(oas_ref, obs_ref))
    att = _dot(mix, w_ref[...])
    x1 = _gated_add(x_ref[...], g1_ref, att)
    x1_out[...] = x1
    h = _modulated_norm(x1, ln_ref, sc_ref, sh_ref)

    for s in range(SUBLANES):
        h2_out[pl.ds(s, ROWS, stride=SUBLANES), :] = h[:, s * LANES:(s + 1) * LANES]

    h_hi, h_lo = _split2(h)
    w_hi, w_lo = _split2(wr_ref[...])
    hi_terms = _dot_nt(jnp.concatenate([w_hi, w_lo], axis=0), h_hi)
    logits = hi_terms[0:N_EXPERTS] + hi_terms[N_EXPERTS:] + _dot_nt(w_hi, h_lo)
    aff = _sigmoid(logits)
    sel = aff + rb_ref[...]
    e_idx = lax.broadcasted_iota(I32, (N_EXPERTS, ROWS), 0)
    l_idx = lax.broadcasted_iota(I32, (EXPERTS_PER_GROUP, ROWS), 0)

    def first_argmax(vals, idx, big):
        top = jnp.max(vals, axis=0, keepdims=True)
        return top, jnp.min(jnp.where(vals == top, idx, big), axis=0, keepdims=True)

    g_scores = []
    for g in range(N_GROUPS):
        sg = sel[g * EXPERTS_PER_GROUP:(g + 1) * EXPERTS_PER_GROUP]
        m1, i1 = first_argmax(sg, l_idx, EXPERTS_PER_GROUP)
        m2 = jnp.max(jnp.where(l_idx == i1, -jnp.inf, sg), axis=0, keepdims=True)
        g_scores.append(m1 + m2)
    g_score = jnp.concatenate(g_scores, axis=0)
    g_idx = lax.broadcasted_iota(I32, (N_GROUPS, ROWS), 0)
    _, g_sel = first_argmax(g_score, g_idx, N_GROUPS)
    masked = jnp.where((e_idx // EXPERTS_PER_GROUP) == g_sel, sel, -jnp.inf)
    _, e1 = first_argmax(masked, e_idx, N_EXPERTS)
    _, e2 = first_argmax(jnp.where(e_idx == e1, -jnp.inf, masked), e_idx, N_EXPERTS)
    w1 = jnp.sum(jnp.where(e_idx == e1, aff, 0.0), axis=0, keepdims=True)
    w2 = jnp.sum(jnp.where(e_idx == e2, aff, 0.0), axis=0, keepdims=True)
    den = w1 + w2
    w1 = w1 / den
    w2 = w2 / den
    lo_first = e1 < e2
    gate_a = jnp.where(lo_first, w1, w2)
    gate_b = jnp.where(lo_first, w2, w1)
    a = jnp.minimum(e1, e2) - g_sel * EXPERTS_PER_GROUP
    b = jnp.maximum(e1, e2) - g_sel * EXPERTS_PER_GROUP
    base = jnp.where(a == 0, 0, jnp.where(a == 1, 3, 5))
    cls_out[...] = g_sel * PAIRS_PER_GROUP + base + (b - a - 1)
    ga_out[...] = gate_a
    gb_out[...] = gate_b


def _outproj(layer, x4, oap4, obp, oas4, obs, mod, ln2, w_out_b, w_router_t, router_bias_c,
             tiles_per_batch):
    n_tiles = x4.shape[0] // ROWS
    x_spec = pl.BlockSpec((ROWS, D_MODEL), lambda i: (i, 0))
    prompt_tiles = oap4.shape[0]
    last = prompt_tiles - 1
    lane_row = pl.BlockSpec((None, 1, ROWS), lambda i: (i, 0, 0))
    return pl.pallas_call(
        functools.partial(_outproj_kernel, prompt_tiles),
        grid=(n_tiles,),
        in_specs=[
            x_spec,
            pl.BlockSpec((None, GQA_GROUP, ROWS, LANES), lambda i: (jnp.minimum(i, last), 0, 0, 0)),
            pl.BlockSpec((ROWS, V_B), lambda i: (jnp.minimum(i, last), 0)),
            pl.BlockSpec((None, GQA_GROUP, ROWS, LANES), lambda i: (0, 0, 0, 0)),
            pl.BlockSpec((ROWS, V_B), lambda i: (0, 0)),
            _mod_spec(layer, 2, tiles_per_batch), _mod_spec(layer, 3, tiles_per_batch),
            _mod_spec(layer, 4, tiles_per_batch),
            pl.BlockSpec((None, 1, D_MODEL), lambda i: (layer, 0, 0)),
            pl.BlockSpec((None, Q_A + V_B, D_MODEL), lambda i: (layer, 0, 0)),
            pl.BlockSpec((N_EXPERTS, D_MODEL), lambda i: (0, 0)),
            pl.BlockSpec((N_EXPERTS, 1), lambda i: (0, 0)),
        ],
        out_specs=[
            x_spec,
            pl.BlockSpec((ROWS * SUBLANES, LANES), lambda i: (i, 0)),
            lane_row, lane_row, lane_row,
        ],
        out_shape=[
            jax.ShapeDtypeStruct(x4.shape, F32),
            jax.ShapeDtypeStruct((n_tiles * ROWS * SUBLANES, LANES), F32),
            jax.ShapeDtypeStruct((n_tiles, 1, ROWS), I32),
            jax.ShapeDtypeStruct((n_tiles, 1, ROWS), F32),
            jax.ShapeDtypeStruct((n_tiles, 1, ROWS), F32),
        ],
        compiler_params=_cparams(("arbitrary",)),
        name=f"outproj{layer}",
    )(x4, oap4, obp, oas4, obs, mod, mod, mod, ln2, w_out_b, w_router_t, router_bias_c)


def _rank_kernel(cls_ref, upper_ref, lower_ref, dest_ref, count_ref, rank_ref):
    n_tiles = cls_ref.shape[0]
    c_idx = lax.broadcasted_iota(I32, (GRP, ROWS), 0)
    upper = upper_ref[...]

    def rank_body(i, carry):
        onehot = c_idx == cls_ref[i]
        within = _dot(onehot.astype(BF16), upper)
        rank_ref[i] = jnp.sum(jnp.where(onehot, within + carry, 0.0), axis=0, keepdims=True)
        return carry + jnp.sum(onehot.astype(F32), axis=1, keepdims=True)

    count = lax.fori_loop(0, n_tiles, rank_body, jnp.zeros((GRP, 1), F32))
    count_ref[...] = jnp.broadcast_to(count, (GRP, LANES)).astype(I32)
    tiles = jnp.floor((count + (TM - 1)) * (1.0 / TM))
    first_tile = _dot(lower_ref[...], jnp.broadcast_to(tiles, (GRP, LANES)).astype(BF16))
    start = first_tile[:, 0:1] * TM

    def dest_body(i, carry):
        onehot = c_idx == cls_ref[i]
        dest = rank_ref[i] + jnp.sum(jnp.where(onehot, start, 0.0), axis=0, keepdims=True)
        dest_ref[i] = dest.astype(I32)
        return carry

    lax.fori_loop(0, n_tiles, dest_body, 0)


def _plan_kernel(n_tok, n_t, dest_ref, count_ref, src_ref, ea_ref, eb_ref, nv_ref, nt_ref):
    tile = 0
    for c in range(N_CLASS):
        g, pi = divmod(c, PAIRS_PER_GROUP)
        a, b = PAIR_TABLE[pi]
        cnt = count_ref[c, 0]
        n_c = lax.shift_right_logical(cnt + (TM - 1), TM.bit_length() - 1)
        first = tile

        def mark(t, carry, g=g, a=a, b=b, cnt=cnt, first=first):
            ea_ref[t] = g * EXPERTS_PER_GROUP + a
            eb_ref[t] = g * EXPERTS_PER_GROUP + b
            nv_ref[t] = jnp.minimum(cnt - (t - first) * TM, TM)
            return carry

        lax.fori_loop(tile, tile + n_c, mark, 0)

        def pad(p, carry):
            src_ref[p] = 0
            return carry

        lax.fori_loop(first * TM + cnt, (first + n_c) * TM, pad, 0)
        tile = tile + n_c
    nt_ref[0] = tile

    def unused(t, carry):
        ea_ref[t] = 0
        eb_ref[t] = 0
        nv_ref[t] = 0
        return carry

    lax.fori_loop(tile, n_t, unused, 0)

    def unused_rows(p, carry):
        src_ref[p] = 0
        return carry

    lax.fori_loop(tile * TM, n_t * TM, unused_rows, 0)

    def place(t, carry):
        src_ref[dest_ref[t]] = t
        return carry

    lax.fori_loop(0, n_tok, place, 0, unroll=8)


def _route_plan(cls3, upper, lower):
    n_tiles = cls3.shape[0]
    n_tok = n_tiles * ROWS
    n_t = (n_tok + N_CLASS * (TM - 1)) // TM + 2
    dest, count = pl.pallas_call(
        _rank_kernel,
        out_shape=[jax.ShapeDtypeStruct((n_tiles, 1, ROWS), I32),
                   jax.ShapeDtypeStruct((GRP, LANES), I32)],
        scratch_shapes=[pltpu.VMEM((n_tiles, 1, ROWS), F32)],
        name="moe_rank",
    )(cls3, upper, lower)
    smem = pl.BlockSpec(memory_space=pltpu.SMEM)
    return pl.pallas_call(
        functools.partial(_plan_kernel, n_tok, n_t),
        in_specs=[smem, smem],
        out_specs=[smem, smem, smem, smem, smem],
        out_shape=[jax.ShapeDtypeStruct((n_t * TM,), I32),
                   jax.ShapeDtypeStruct((n_t,), I32),
                   jax.ShapeDtypeStruct((n_t,), I32),
                   jax.ShapeDtypeStruct((n_t,), I32),
                   jax.ShapeDtypeStruct((1,), I32)],
        name="moe_plan",
    )(dest.reshape(n_tok), count)


def _tile_rows(ref, slot, r):
    return ref.at[pl.ds(pl.multiple_of((slot * TM + r) * SUBLANES, SUBLANES), SUBLANES)]


def _moe_kernel(n_tok, src_ref, ea_ref, eb_ref, nv_ref, nt_ref, ga_ref, gb_ref, h2_hbm, wga_ref,
                wua_ref, wda_ref, wgb_ref, wub_ref, wdb_ref, y_hbm, xbuf, ybuf, gbuf, gsem, ssem):
    i = pl.program_id(0)
    n_used = nt_ref[0]
    lane = lax.broadcasted_iota(I32, (1, LANES), 1)

    def token_tile(hbm, tok):
        return hbm.at[pl.ds(pl.multiple_of(tok * SUBLANES, SUBLANES), SUBLANES)]

    def gather_row(tile, slot, r, priority):
        tok = src_ref[tile * TM + r]
        pltpu.make_async_copy(token_tile(h2_hbm, tok), _tile_rows(xbuf, slot, r),
                              gsem.at[slot]).start(priority=priority)
        gbuf[pl.ds(slot * TM + r, 1), :] = jnp.where(lane == 0, ga_ref[tok], gb_ref[tok])

    def scatter_row(tile, slot, r, n_valid, priority):
        tok = jnp.where(r < n_valid, src_ref[tile * TM + r], n_tok + slot * TM + r)
        pltpu.make_async_copy(_tile_rows(ybuf, slot, r), token_tile(y_hbm, tok),
                              ssem.at[slot]).start(priority=priority)

    def row_loop(start_row):
        def body(r8, c):
            for u in range(DMA_UNROLL):
                start_row(r8 * DMA_UNROLL + u, u % 2)
            return c
        lax.fori_loop(0, TM // DMA_UNROLL, body, 0)

    def wait_gather(slot):
        pltpu.make_async_copy(h2_hbm.at[pl.ds(0, TM * SUBLANES)],
                              xbuf.at[pl.ds(slot * TM * SUBLANES, TM * SUBLANES)],
                              gsem.at[slot]).wait()

    def wait_scatter(slot):
        pltpu.make_async_copy(ybuf.at[pl.ds(slot * TM * SUBLANES, TM * SUBLANES)],
                              y_hbm.at[pl.ds(0, TM * SUBLANES)], ssem.at[slot]).wait()

    @pl.when(i == 0)
    def _():
        ybuf[...] = jnp.zeros_like(ybuf)
        spare = pltpu.make_async_copy(
            ybuf, y_hbm.at[pl.ds(n_tok * SUBLANES, 2 * TM * SUBLANES)], ssem.at[0])
        spare.start()
        spare.wait()

    @pl.when(jnp.logical_and(i == 0, n_used > 0))
    def _():
        row_loop(lambda r, pr: gather_row(0, 0, r, pr))

    @pl.when(i + 1 < n_used)
    def _():
        row_loop(lambda r, pr: gather_row(i + 1, (i + 1) % 2, r, pr))

    @pl.when(jnp.logical_and(i >= 2, i - 2 < n_used))
    def _():
        wait_scatter(i % 2)

    @pl.when(i < n_used)
    def _():
        slot = i % 2
        wait_gather(slot)
        base = slot * TM * SUBLANES
        xb = jnp.concatenate([_token_rows(xbuf, s, TM, base) for s in range(SUBLANES)],
                             axis=1).astype(BF16)
        gates = gbuf[pl.ds(slot * TM, TM), :]
        y = jnp.zeros((TM, D_MODEL), F32)
        for e, (wg, wu, wd) in enumerate(((wga_ref, wua_ref, wda_ref), (wgb_ref, wub_ref, wdb_ref))):
            gt = _dot(xb, wg[...].astype(BF16))
            up = _dot(xb, wu[...].astype(BF16))
            act = gt * _sigmoid(gt) * up * gates[:, e:e + 1]
            y = y + _dot(act.astype(BF16), wd[...].astype(BF16))
        for s in range(SUBLANES):
            ybuf[pl.ds(base + s, TM, stride=SUBLANES), :] = y[:, s * LANES:(s + 1) * LANES]
        n_valid = nv_ref[i]
        row_loop(lambda r, pr: scatter_row(i, slot, r, n_valid, pr))


def _moe(layer, h2, gate_a, gate_b, plan, w_gate, w_up, w_down):
    src, ea, eb, nv, nt = plan
    n_t = ea.shape[0]
    n_tok = h2.shape[0] // SUBLANES

    def w_spec(shape, which):
        def index(i, src_ref, ea_ref, eb_ref, nv_ref, nt_ref, ga_ref, gb_ref):
            t = jnp.minimum(i, jnp.maximum(nt_ref[0] - 1, 0))
            e = (ea_ref, eb_ref)[which][t]
            return (layer, e, 0, 0)
        return pl.BlockSpec((None, None) + shape, index)

    gu = (D_MODEL, D_EXPERT)
    dn = (D_EXPERT, D_MODEL)
    grid_spec = pltpu.PrefetchScalarGridSpec(
        num_scalar_prefetch=7,
        grid=(n_t,),
        in_specs=[pl.BlockSpec(memory_space=pl.ANY),
                  w_spec(gu, 0), w_spec(gu, 0), w_spec(dn, 0),
                  w_spec(gu, 1), w_spec(gu, 1), w_spec(dn, 1)],
        out_specs=pl.BlockSpec(memory_space=pl.ANY),
        scratch_shapes=[pltpu.VMEM((2 * TM * SUBLANES, LANES), F32),
                        pltpu.VMEM((2 * TM * SUBLANES, LANES), F32),
                        pltpu.VMEM((2 * TM, LANES), F32),
                        pltpu.SemaphoreType.DMA((2,)), pltpu.SemaphoreType.DMA((2,))],
    )
    return pl.pallas_call(
        functools.partial(_moe_kernel, n_tok),
        grid_spec=grid_spec,
        out_shape=jax.ShapeDtypeStruct(((n_tok + 2 * TM) * SUBLANES, LANES), F32),
        compiler_params=_cparams(("arbitrary",)),
        name=f"moe{layer}",
    )(src, ea, eb, nv, nt, gate_a.reshape(n_tok), gate_b.reshape(n_tok), h2,
      w_gate, w_up, w_down, w_gate, w_up, w_down)


def kernel(x_prompt, x_sample, cache_k, cache_v, state_hgrn, c_prompt, c_sample, w_ada, b_ada,
           ln1, ln2, w_in, q_norm, k_norm, sinks, lb_param, o_norm, w_out, w_router, router_bias,
           w_e_gate, w_e_up, w_e_down):
    batch, seq, _ = x_prompt.shape
    dec_batch, dec_seq, _ = x_sample.shape
    depth = w_ada.shape[0]
    assert dec_batch == GRP and dec_seq == SUB and seq % ROWS == 0
    tiles_per_batch = seq // ROWS
    n_prompt = batch * seq
    n_tok = n_prompt + dec_batch * dec_seq
    n_tiles = n_tok // ROWS
    prompt_tiles = n_prompt // ROWS
    n_cache = cache_k.shape[2]

    seg = np.arange(Q_A) // HEAD_DIM
    bd = jnp.asarray((seg[:, None] == seg[None, :]).astype(np.float32) / HEAD_DIM, BF16)

    def hgrn_consts(length):
        selb, sell, masks = _hgrn_consts(length)
        return jnp.asarray(selb, BF16), jnp.asarray(sell, BF16), jnp.asarray(masks)

    consts_p = hgrn_consts(CHUNK)
    consts_s = hgrn_consts(dec_seq)
    upper = jnp.asarray(np.triu(np.ones((ROWS, ROWS), np.float32), 1), BF16)
    lower = jnp.asarray(np.tril(np.ones((GRP, GRP), np.float32), -1), BF16)

    wq = w_in[:, :, :Q_A].reshape(depth, D_MODEL, N_KV_A, GQA_GROUP, HEAD_DIM)
    wq = jnp.swapaxes(wq, 2, 3).reshape(depth, D_MODEL, Q_A)
    w_in_b = jnp.concatenate([wq, w_in[:, :, Q_A:]], axis=-1).astype(BF16)
    wo = w_out[:, :Q_A].reshape(depth, N_KV_A, GQA_GROUP, HEAD_DIM, D_MODEL)
    wo = jnp.swapaxes(wo, 1, 2).reshape(depth, Q_A, D_MODEL)
    w_out_b = jnp.concatenate([wo, w_out[:, Q_A:]], axis=1).astype(BF16)
    q_norm_t = jnp.tile(q_norm, (1, N_HEADS_A)).reshape(depth, 1, Q_A)
    k_norm_t = jnp.tile(k_norm, (1, N_KV_A)).reshape(depth, 1, KV_A)
    o_norm_t = jnp.tile(o_norm, (1, N_HEADS_B)).reshape(depth, 1, V_B)
    ln1_r = ln1.reshape(depth, 1, D_MODEL)
    ln2_r = ln2.reshape(depth, 1, D_MODEL)
    w_router_t = w_router.T
    router_bias_c = router_bias.reshape(N_EXPERTS, 1)

    c_exp = jnp.concatenate([jnp.repeat(c_prompt, GRP, axis=0), c_sample], axis=0)
    mod = _adaln(c_exp, w_ada, b_ada).reshape(depth, batch + 1, GRP, N_MOD * D_MODEL)

    cache_k2 = cache_k.reshape(depth, dec_batch, n_cache, KV_A)
    cache_v2 = cache_v.reshape(depth, dec_batch, n_cache, KV_A)

    src = (x_prompt, x_sample)
    kp, vp, sp, kss, vss, sss = [], [], [], [], [], []
    for l in range(depth):
        x, q4, k, v, qb, logf, kb, ib, gb, kwin, vwin = _inproj(
            l, src, n_tiles, prompt_tiles, mod, ln1_r, w_in_b, q_norm_t, k_norm_t, lb_param, bd,
            tiles_per_batch)

        oap4 = _attn_prompt(q4, k, v, sinks[l], batch, seq)
        oas4 = _attn_sample(q4, k, v, cache_k2[l], cache_v2[l], sinks[l], prompt_tiles,
                            n_prompt, dec_batch, dec_seq)
        obp, s_p = _hgrn_prompt(qb, kb, ib, gb, logf, consts_p, bd, o_norm_t[l], batch, seq)
        obs, s_s = _hgrn_sample(qb, kb, ib, gb, logf, state_hgrn[l], consts_s,
                                bd, o_norm_t[l], n_prompt, dec_batch, dec_seq)

        x1, h2, cls3, gate_a, gate_b = _outproj(l, x, oap4, obp, oas4, obs, mod, ln2_r, w_out_b,
                                                w_router_t, router_bias_c, tiles_per_batch)
        y = _moe(l, h2, gate_a, gate_b, _route_plan(cls3, upper, lower), w_e_gate, w_e_up,
                 w_e_down)
        src = (x1, y)

        kp.append(kwin.reshape(batch, WINDOW, N_KV_A, HEAD_DIM))
        vp.append(vwin.reshape(batch, WINDOW, N_KV_A, HEAD_DIM))
        sp.append(s_p)
        kss.append(k[n_prompt:].reshape(dec_batch, dec_seq, N_KV_A, HEAD_DIM))
        vss.append(v[n_prompt:].reshape(dec_batch, dec_seq, N_KV_A, HEAD_DIM))
        sss.append(s_s)

    yp, ys = _final(depth - 1, src[0], src[1], mod, batch, seq, tiles_per_batch)
    return (yp, ys, jnp.stack(kp), jnp.stack(vp), jnp.stack(sp),
            jnp.stack(kss), jnp.stack(vss), jnp.stack(sss))
```

```python
import functools

import numpy as np
import jax
import jax.numpy as jnp
from jax import lax
from jax.experimental import pallas as pl
from jax.experimental.pallas import tpu as pltpu

F32 = jnp.float32
BF16 = jnp.bfloat16
I32 = jnp.int32

D_MODEL = 1024
HEAD_DIM = 64
N_HEADS_A = 8
N_KV_A = 2
GQA_GROUP = N_HEADS_A // N_KV_A
N_HEADS_B = 8
DK_B = 64
DV_B = 64
Q_A = N_HEADS_A * HEAD_DIM
KV_A = N_KV_A * HEAD_DIM
W_B = N_HEADS_B * DK_B
V_B = N_HEADS_B * DV_B
CHUNK = 64
WINDOW = 128
N_EXPERTS = 16
N_GROUPS = 4
EXPERTS_PER_GROUP = 4
D_EXPERT = D_MODEL // 2
RMS_EPS = 1e-6
NEG_INF = -1e30
N_MOD = 6

LANES = 128
SUBLANES = 8
ROWS = 512
GRP = 32
SUB = ROWS // GRP
ATT_ROWS = 256
N_PAIR = N_HEADS_B // 2
SAMPLE_SEQS = 4
HGRN_SPAN_LIMIT = 80.0
TM = 256
PAIRS_PER_GROUP = 6
PAIR_TABLE = ((0, 1), (0, 2), (0, 3), (1, 3), (1, 2), (3, 2))
N_CLASS = N_GROUPS * PAIRS_PER_GROUP
DMA_UNROLL = 8
VMEM_LIMIT = 56 * 1024 * 1024


def _cparams(sem):
    return pltpu.CompilerParams(dimension_semantics=sem, vmem_limit_bytes=VMEM_LIMIT)


def _dot(a, b):
    return jnp.dot(a, b, preferred_element_type=F32)


def _dot_nt(a, b):
    return lax.dot_general(a, b, (((1,), (1,)), ((), ())), preferred_element_type=F32)


def _dot_tn(a, b):
    return lax.dot_general(a, b, (((0,), (0,)), ((), ())), preferred_element_type=F32)


def _sigmoid(x):
    return 1.0 / (1.0 + jnp.exp(-x))


def _split2(x):
    hi = x.astype(BF16)
    lo = (x - hi.astype(F32)).astype(BF16)
    return hi, lo


def _split3(x):
    hi = x.astype(BF16)
    r = x - hi.astype(F32)
    mid = r.astype(BF16)
    lo = (r - mid.astype(F32)).astype(BF16)
    return hi, mid, lo


def _seg_mean(sq, bd):
    return _dot(sq.astype(BF16), bd)


def _token_rows(ref, s, n, base=0):
    return ref[pl.ds(base + s, n, stride=SUBLANES), :]


def _adaln_kernel(c_ref, w_ref, b_ref, o_ref):
    c = c_ref[...]
    a = c * _sigmoid(c)
    a_hi, a_lo = _split2(a)
    w_hi, w_lo = _split2(w_ref[...])
    acc = _dot(a_hi, w_hi) + _dot(a_lo, w_hi) + _dot(a_hi, w_lo)
    o_ref[...] = acc + b_ref[...]


def _adaln(c_exp, w_ada, b_ada):
    depth = w_ada.shape[0]
    rows = c_exp.shape[0]
    tn = D_MODEL
    return pl.pallas_call(
        _adaln_kernel,
        grid=(depth, N_MOD * D_MODEL // tn),
        in_specs=[
            pl.BlockSpec((rows, D_MODEL), lambda l, n: (0, 0)),
            pl.BlockSpec((None, D_MODEL, tn), lambda l, n: (l, 0, n)),
            pl.BlockSpec((None, 1, tn), lambda l, n: (l, 0, n)),
        ],
        out_specs=pl.BlockSpec((None, rows, tn), lambda l, n: (l, 0, n)),
        out_shape=jax.ShapeDtypeStruct((depth, rows, N_MOD * D_MODEL), F32),
        compiler_params=_cparams(("arbitrary", "arbitrary")),
        name="adaln",
    )(c_exp, w_ada, b_ada.reshape(depth, 1, N_MOD * D_MODEL))


def _mod_spec(layer, j, tiles_per_batch):
    return pl.BlockSpec((None, None, GRP, D_MODEL),
                        lambda i, *_: (layer, i // tiles_per_batch, 0, j))


def _by_group(fn, xs, mod_refs):
    outs = []
    for g in range(GRP):
        rows = slice(g * SUB, (g + 1) * SUB)
        outs.append(fn(*[x[rows] for x in xs], *[m[g:g + 1, :] for m in mod_refs]))
    return jnp.concatenate(outs, axis=0)


def _gated_add(x, gate_ref, y):
    return _by_group(lambda xg, yg, gg: xg + gg * yg, (x, y), (gate_ref,))


def _modulated_norm(x, ln_ref, sc_ref, sh_ref):
    ms = jnp.mean(x * x, axis=-1, keepdims=True)
    xn = x * lax.rsqrt(ms + RMS_EPS)
    scale = ln_ref[...] * (1.0 + sc_ref[...])
    return _by_group(lambda xg, sc, sh: xg * sc + sh, (xn,), (scale, sh_ref))


def _prompt_tile_spec(prompt_tiles, tiles_per_batch):
    def index(i):
        t = jnp.minimum(i, prompt_tiles - 1)
        return (t // tiles_per_batch, t % tiles_per_batch, 0)
    return pl.BlockSpec((None, ROWS, D_MODEL), index)


def _moe_output_tile(y_ref):
    return jnp.concatenate([_token_rows(y_ref, s, ROWS) for s in range(SUBLANES)], axis=1)


def _inproj_kernel(layer, prompt_tiles, n_src, *refs):
    src = refs[:n_src]
    sh_ref, sc_ref, ln_ref, w_ref, qn_ref, kn_ref, lbp_ref, bd_ref = refs[n_src:n_src + 8]
    (x_out, q_out, k_out, v_out, qb_out, logf_out, kb_out, ib_out, gb_out,
     kwin_out, vwin_out) = refs[n_src + 8:]
    if layer == 0:
        xp_ref, xs_ref = src
        x = jnp.where(pl.program_id(0) < prompt_tiles, xp_ref[...],
                      xs_ref[...].reshape(ROWS, D_MODEL))
    else:
        x1_ref, y_ref, g2_ref = src
        x = _gated_add(x1_ref[...], g2_ref, _moe_output_tile(y_ref))
    x_out[...] = x
    hb = _modulated_norm(x, ln_ref, sc_ref, sh_ref).astype(BF16)
    bd = bd_ref[...]

    c0 = Q_A + 2 * KV_A
    c1 = c0 + 2 * W_B
    qa = _dot(hb, w_ref[:, 0:Q_A])
    kv = _dot(hb, w_ref[:, Q_A:c0])
    qb = _dot(hb, w_ref[:, c0:c0 + W_B])
    z = _dot(hb, w_ref[:, c0 + W_B:c1])
    ib = _dot(hb, w_ref[:, c1:c1 + V_B])
    gb = _dot(hb, w_ref[:, c1 + V_B:c1 + 2 * V_B])

    qa = qa * lax.rsqrt(_seg_mean(qa * qa, bd) + RMS_EPS) * qn_ref[...]
    qa = (qa * (HEAD_DIM ** -0.5)).astype(BF16)
    for hh in range(GQA_GROUP):
        q_out[hh] = qa[:, hh * LANES:(hh + 1) * LANES]

    ka = kv[:, :KV_A]
    ka = ka * lax.rsqrt(_seg_mean(ka * ka, bd[:KV_A, :KV_A]) + RMS_EPS) * kn_ref[...]
    k_out[...] = ka
    v_out[...] = kv[:, KV_A:]

    @pl.when(pl.program_id(0) < prompt_tiles)
    def _():
        kwin_out[...] = ka[ROWS - WINDOW:]
        vwin_out[...] = kv[ROWS - WINDOW:, KV_A:]

    qb_out[...] = (qb * (DK_B ** -0.5)).astype(BF16)
    ib_out[...] = ib.astype(BF16)
    gb_out[...] = gb.astype(BF16)

    p = lbp_ref[...]
    e = jnp.exp(p - jnp.max(p, axis=0, keepdims=True))
    sm = e / jnp.sum(e, axis=0, keepdims=True)
    cum0 = sm[0:1]
    cum = cum0
    for r in range(1, layer + 1):
        cum = cum + sm[r:r + 1]
    lb = cum - cum0

    pos = z >= 0.0
    t = jnp.exp(-jnp.abs(z))
    num = jnp.where(pos, jnp.log(1.0 + lb * t), jnp.maximum(jnp.log(lb + t), z))
    logf_out[...] = num - jnp.log(1.0 + t)
    kb_out[...] = ((1.0 - lb) * jnp.where(pos, t, 1.0) / (1.0 + t)).astype(BF16)


def _inproj(layer, src, n_tiles, prompt_tiles, mod, ln1, w_in_b, q_norm_t, k_norm_t, lb_param, bd,
            tiles_per_batch):
    depth = lb_param.shape[0]
    n = n_tiles * ROWS
    in_width = w_in_b.shape[-1]
    x_spec = pl.BlockSpec((ROWS, D_MODEL), lambda i: (i, 0))
    if layer == 0:
        src_specs = [_prompt_tile_spec(prompt_tiles, tiles_per_batch),
                     pl.BlockSpec((GRP, SUB, D_MODEL), lambda i: (0, 0, 0))]
        src_args = list(src)
    else:
        src_specs = [x_spec, pl.BlockSpec((ROWS * SUBLANES, LANES), lambda i: (i, 0)),
                     _mod_spec(layer - 1, 5, tiles_per_batch)]
        src_args = list(src) + [mod]
    row_spec = lambda w: pl.BlockSpec((ROWS, w), lambda i: (i, 0))
    win_spec = pl.BlockSpec(
        (None, WINDOW, KV_A), lambda i: (jnp.minimum(i, prompt_tiles - 1) // tiles_per_batch, 0, 0))
    return pl.pallas_call(
        functools.partial(_inproj_kernel, layer, prompt_tiles, len(src_specs)),
        grid=(n_tiles,),
        in_specs=src_specs + [
            _mod_spec(layer, 0, tiles_per_batch), _mod_spec(layer, 1, tiles_per_batch),
            pl.BlockSpec((None, 1, D_MODEL), lambda i: (layer, 0, 0)),
            pl.BlockSpec((None, D_MODEL, in_width), lambda i: (layer, 0, 0)),
            pl.BlockSpec((None, 1, Q_A), lambda i: (layer, 0, 0)),
            pl.BlockSpec((None, 1, KV_A), lambda i: (layer, 0, 0)),
            pl.BlockSpec((depth, W_B), lambda i: (0, 0)),
            pl.BlockSpec((Q_A, Q_A), lambda i: (0, 0)),
        ],
        out_specs=[
            x_spec,
            pl.BlockSpec((None, GQA_GROUP, ROWS, LANES), lambda i: (i, 0, 0, 0)),
            row_spec(KV_A), row_spec(KV_A),
            row_spec(W_B), row_spec(W_B), row_spec(W_B), row_spec(V_B), row_spec(V_B),
            win_spec, win_spec,
        ],
        out_shape=[
            jax.ShapeDtypeStruct((n, D_MODEL), F32),
            jax.ShapeDtypeStruct((n_tiles, GQA_GROUP, ROWS, LANES), BF16),
            jax.ShapeDtypeStruct((n, KV_A), F32),
            jax.ShapeDtypeStruct((n, KV_A), F32),
            jax.ShapeDtypeStruct((n, W_B), BF16),
            jax.ShapeDtypeStruct((n, W_B), F32),
            jax.ShapeDtypeStruct((n, W_B), BF16),
            jax.ShapeDtypeStruct((n, V_B), BF16),
            jax.ShapeDtypeStruct((n, V_B), BF16),
            jax.ShapeDtypeStruct((prompt_tiles // tiles_per_batch, WINDOW, KV_A), F32),
            jax.ShapeDtypeStruct((prompt_tiles // tiles_per_batch, WINDOW, KV_A), F32),
        ],
        compiler_params=_cparams(("arbitrary",)),
        name=f"inproj{layer}",
    )(*src_args, mod, mod, ln1, w_in_b, q_norm_t, k_norm_t, lb_param, bd)


def _final_kernel(prompt_tiles, x1_ref, y_ref, g2_ref, yp_ref, ys_ref):
    i = pl.program_id(0)
    x = _gated_add(x1_ref[...], g2_ref, _moe_output_tile(y_ref))

    @pl.when(i < prompt_tiles)
    def _():
        yp_ref[...] = x

    @pl.when(i >= prompt_tiles)
    def _():
        ys_ref[...] = x.reshape(GRP, SUB, D_MODEL)


def _final(layer, x1, y, mod, batch, seq, tiles_per_batch):
    n_tiles = x1.shape[0] // ROWS
    prompt_tiles = batch * tiles_per_batch
    return pl.pallas_call(
        functools.partial(_final_kernel, prompt_tiles),
        grid=(n_tiles,),
        in_specs=[pl.BlockSpec((ROWS, D_MODEL), lambda i: (i, 0)),
                  pl.BlockSpec((ROWS * SUBLANES, LANES), lambda i: (i, 0)),
                  _mod_spec(layer, 5, tiles_per_batch)],
        out_specs=[_prompt_tile_spec(prompt_tiles, tiles_per_batch),
                   pl.BlockSpec((GRP, SUB, D_MODEL), lambda i: (0, 0, 0))],
        out_shape=[jax.ShapeDtypeStruct((batch, seq, D_MODEL), F32),
                   jax.ShapeDtypeStruct((GRP, SUB, D_MODEL), F32)],
        compiler_params=_cparams(("arbitrary",)),
        name="final",
    )(x1, y, mod)


def _attn_tables(sinks_ref, q_len, n_k, k_off):
    n_q = GQA_GROUP * q_len
    row = lax.broadcasted_iota(I32, (n_q, n_k), 0)
    col = lax.broadcasted_iota(I32, (n_q, n_k), 1)
    dist = jnp.abs((row % q_len) + k_off - col).astype(F32)
    hh_col = lax.broadcasted_iota(I32, (n_q, 1), 0) // q_len
    bias, sinks = [], []
    for g in range(N_KV_A):
        head = (g * GQA_GROUP + 1 + hh_col).astype(F32)
        bias.append(jnp.exp2(-8.0 * head / N_HEADS_A) * dist)
        sink = jnp.zeros((n_q, 1), F32)
        for j in range(GQA_GROUP):
            sink = jnp.where(hh_col == j, sinks_ref[g * GQA_GROUP + j], sink)
        sinks.append(sink)
    return bias, sinks


def _attend(q, k, v, bias, sinks, first_valid=None):
    n_q = q.shape[0]
    n_k = k.shape[0]
    lane = lax.broadcasted_iota(I32, (n_k, LANES), 1)
    out = jnp.zeros((n_q, LANES), F32)
    for g in range(N_KV_A):
        in_g = (lane // HEAD_DIM) == g
        kg = jnp.where(in_g, k, 0.0).astype(BF16)
        vg = jnp.where(in_g, v, 0.0).astype(BF16)
        s = _dot_nt(q, kg) - bias[g]
        if first_valid is not None:
            col = lax.broadcasted_iota(I32, (n_q, n_k), 1)
            s = jnp.where(col >= first_valid, s, NEG_INF)
        m = jnp.maximum(jnp.max(s, axis=-1, keepdims=True), sinks[g])
        p = jnp.exp(s - m)
        denom = jnp.sum(p, axis=-1, keepdims=True) + jnp.exp(sinks[g] - m)
        out = out + _dot(p.astype(BF16), vg) / denom
    return out


def _attn_prompt_kernel(sinks_ref, q_ref, kp_ref, kc_ref, vp_ref, vc_ref, o_ref):
    j = pl.program_id(1)
    kcat = jnp.concatenate([kp_ref[...], kc_ref[...]], axis=0)
    vcat = jnp.concatenate([vp_ref[...], vc_ref[...]], axis=0)
    bias, sinks = _attn_tables(sinks_ref, CHUNK, WINDOW + CHUNK, WINDOW)
    for c in range(ATT_ROWS // CHUNK):
        q = q_ref[:, c * CHUNK:(c + 1) * CHUNK, :].reshape(GQA_GROUP * CHUNK, LANES)
        k = kcat[c * CHUNK:c * CHUNK + WINDOW + CHUNK]
        v = vcat[c * CHUNK:c * CHUNK + WINDOW + CHUNK]
        first_valid = WINDOW - c * CHUNK - j * ATT_ROWS
        out = _attend(q, k, v, bias, sinks, first_valid)
        o_ref[:, c * CHUNK:(c + 1) * CHUNK, :] = out.reshape(GQA_GROUP, CHUNK, LANES).astype(BF16)


def _attn_prompt(q4, k, v, sinks, batch, seq):
    steps = seq // ATT_ROWS
    per_tile = ROWS // ATT_ROWS
    win_blocks = ATT_ROWS // WINDOW
    n_tiles = batch * seq // ROWS
    cur = pl.BlockSpec((ATT_ROWS, KV_A), lambda b, j: (b * steps + j, 0))
    prev = pl.BlockSpec(
        (WINDOW, KV_A), lambda b, j: (b * steps * win_blocks + jnp.maximum(j * win_blocks - 1, 0), 0))
    qo = pl.BlockSpec((None, GQA_GROUP, ATT_ROWS, LANES),
                      lambda b, j: ((b * steps + j) // per_tile, 0, j % per_tile, 0))
    return pl.pallas_call(
        _attn_prompt_kernel,
        grid=(batch, steps),
        in_specs=[pl.BlockSpec(memory_space=pltpu.SMEM), qo, prev, cur, prev, cur],
        out_specs=qo,
        out_shape=jax.ShapeDtypeStruct((n_tiles, GQA_GROUP, ROWS, LANES), BF16),
        compiler_params=_cparams(("arbitrary", "arbitrary")),
        name="attn_prompt",
    )(sinks, q4, k, k, v, v)


def _attn_sample_kernel(n_new, sinks_ref, q_ref, kc_ref, kn_ref, vc_ref, vn_ref, o_ref):
    n_cache = kc_ref.shape[1]
    bias, sinks = _attn_tables(sinks_ref, n_new, n_cache + n_new, n_cache)
    for r in range(SAMPLE_SEQS):
        rows = slice(r * n_new, (r + 1) * n_new)
        q = q_ref[:, rows, :].reshape(GQA_GROUP * n_new, LANES)
        k = jnp.concatenate([kc_ref[r], kn_ref[rows, :]], axis=0)
        v = jnp.concatenate([vc_ref[r], vn_ref[rows, :]], axis=0)
        out = _attend(q, k, v, bias, sinks)
        o_ref[:, rows, :] = out.reshape(GQA_GROUP, n_new, LANES).astype(BF16)


def _attn_sample(q4, k, v, cache_k, cache_v, sinks, tile, row0, dec_batch, dec_seq):
    n_cache = cache_k.shape[1]
    rows = SAMPLE_SEQS * dec_seq
    q_spec = pl.BlockSpec((None, GQA_GROUP, rows, LANES), lambda b: (tile, 0, b, 0))
    new = pl.BlockSpec((rows, KV_A), lambda b: (row0 // rows + b, 0))
    cache = pl.BlockSpec((SAMPLE_SEQS, n_cache, KV_A), lambda b: (b, 0, 0))
    return pl.pallas_call(
        functools.partial(_attn_sample_kernel, dec_seq),
        grid=(dec_batch // SAMPLE_SEQS,),
        in_specs=[pl.BlockSpec(memory_space=pltpu.SMEM), q_spec, cache, new, cache, new],
        out_specs=pl.BlockSpec((None, GQA_GROUP, rows, LANES), lambda b: (0, 0, b, 0)),
        out_shape=jax.ShapeDtypeStruct((1, GQA_GROUP, dec_batch * dec_seq, LANES), BF16),
        compiler_params=_cparams(("arbitrary",)),
        name="attn_sample",
    )(sinks, q4, cache_k, k, cache_v, v)


def _level_sizes(length):
    sizes = []
    m = length // 2
    while m >= 1:
        sizes.append(m)
        m //= 2
    return sizes


def _hgrn_consts(length):
    t = np.arange(length)
    tri = (t[None, :] <= t[:, None]).astype(np.float32)
    after = (t[None, :] > t[:, None]).astype(np.float32)
    levels = []
    masks = []
    s = np.tile(t, 2)[None, :]
    tt = t[:, None]
    for m in _level_sizes(length):
        blk = t // m
        same = blk[None, :] == blk[:, None]
        q_rows = (blk % 2 == 1)[:, None]
        sel = np.where(q_rows, same & (t[None, :] <= t[:, None]), same & (t[None, :] > t[:, None]))
        levels.append(sel.astype(np.float32))
        masks.append(((tt // m) % 2 == 1) & ((s // m) == (tt // m) - 1))
    masks.append(s == tt)
    masks.append(s <= tt)
    base = np.concatenate([tri, after], axis=0)
    lev = np.concatenate(levels, axis=0)
    return (np.concatenate([base] * 3, axis=1), np.concatenate([lev] * 3, axis=1),
            np.stack(masks).astype(np.int32))


def _stack2(x, m0, m1):
    xb = x.astype(BF16)
    return jnp.concatenate([xb * m0, xb * m1], axis=0)


def _hgrn_intra(qp, kp, b, masks, m0, m1, arg_ref, sl):
    length = qp.shape[0]
    n_lev = len(_level_sizes(length))
    if arg_ref is None:
        mid = length // 2 - 1
        rel = b - b[mid:mid + 1]
        a = _dot_nt((qp * jnp.exp(rel)).astype(BF16), _stack2(kp * jnp.exp(-rel), m0, m1))
        a = jnp.where(masks[n_lev + 1], a, 0.0)
    else:
        a = _dot_nt(qp.astype(BF16), _stack2(kp, m0, m1))
        a = jnp.where(masks[n_lev], a, 0.0)
        for lev in range(n_lev):
            e = jnp.exp(arg_ref[lev * length:(lev + 1) * length, sl])
            pr = _dot_nt((qp * e).astype(BF16), _stack2(kp * e, m0, m1))
            a = jnp.where(masks[lev], pr, a)
    return a.astype(BF16)


def _hgrn_finish(a, qp, kp, ip, base, sl, st_ref, seq, p, m0, m1, same_head):
    length = qp.shape[0]
    e_b = jnp.exp(base[0:length, sl])
    e_k = jnp.exp(base[length:2 * length, sl])
    st = st_ref[seq, p]
    o = _dot(jnp.concatenate([a, (qp * e_b).astype(BF16)], axis=1),
             jnp.concatenate([_stack2(ip, m0, m1), st.T.astype(BF16)], axis=0))
    upd = _dot_tn(ip.astype(BF16), (kp * e_k).astype(BF16))
    st_ref[seq, p] = st * e_b[length - 1:length, :] + jnp.where(same_head, upd, 0.0)
    return o


def _state_to_pairs(s0_ref, st_ref, n_seq):
    zero = jnp.zeros((DK_B, DV_B), F32)
    for r in range(n_seq):
        for p in range(N_PAIR):
            top = jnp.concatenate([s0_ref[r, 2 * p], zero], axis=1)
            bot = jnp.concatenate([zero, s0_ref[r, 2 * p + 1]], axis=1)
            st_ref[r, p] = jnp.concatenate([top, bot], axis=0).T


def _pairs_to_state(st_ref, s_ref, n_seq):
    for r in range(n_seq):
        for p in range(N_PAIR):
            d = st_ref[r, p].T
            s_ref[r, 2 * p] = d[:DK_B, :DV_B]
            s_ref[r, 2 * p + 1] = d[DK_B:, DV_B:]


def _hgrn_kernel(n_seq, n_chunk, length, carry_state, *refs):
    if carry_state:
        (q_ref, k_ref, i_ref, g_ref, f_ref, selb_ref, sell_ref, mask_ref, bd_ref, on_ref,
         o_ref, s_ref, st_ref, base_ref, arg_ref) = refs

        @pl.when(pl.program_id(1) == 0)
        def _():
            st_ref[...] = jnp.zeros_like(st_ref)
    else:
        (q_ref, k_ref, i_ref, g_ref, f_ref, s0_ref, selb_ref, sell_ref, mask_ref, bd_ref, on_ref,
         o_ref, s_ref, st_ref, base_ref, arg_ref) = refs
        _state_to_pairs(s0_ref, st_ref, n_seq)

    bd = bd_ref[...]
    onorm = on_ref[...]
    n_items = n_seq * n_chunk
    mid = length // 2 - 1
    lane = lax.broadcasted_iota(I32, (length, LANES), 1)
    m0 = (lane < DK_B).astype(BF16)
    m1 = (lane >= DK_B).astype(BF16)
    rowp = lax.broadcasted_iota(I32, (LANES, LANES), 0) // DV_B
    colp = lax.broadcasted_iota(I32, (LANES, LANES), 1) // DK_B
    same_head = rowp == colp
    masks = [mask_ref[m] != 0 for m in range(mask_ref.shape[0])]

    def pieces(rows):
        return jnp.concatenate(_split3(f_ref[rows, :]), axis=0)

    def pair(ref, rows, p):
        return ref[rows, p * LANES:(p + 1) * LANES].astype(F32)

    span = jnp.float32(0.0)
    for it in range(n_items):
        base = _dot(selb_ref[...], pieces(slice(it * length, (it + 1) * length)))
        base_ref[it * 2 * length:(it + 1) * 2 * length, :] = base
        b_mid = base[mid:mid + 1]
        span = jnp.maximum(span, jnp.max(base[0:1] - b_mid))
        span = jnp.maximum(span, jnp.max(b_mid - base[length - 1:length]))
    safe = span <= HGRN_SPAN_LIMIT

    def finish(rows, base, seq, a_of):
        outs = []
        for p in range(N_PAIR):
            sl = slice(p * LANES, (p + 1) * LANES)
            outs.append(_hgrn_finish(a_of(p), pair(q_ref, rows, p), pair(k_ref, rows, p),
                                     pair(i_ref, rows, p), base, sl, st_ref, seq, p, m0, m1,
                                     same_head))
        o = jnp.concatenate(outs, axis=1)
        g = g_ref[rows, :].astype(F32)
        o = o * lax.rsqrt(_seg_mean(o * o, bd) + RMS_EPS) * onorm * (g * _sigmoid(g))
        o_ref[rows, :] = o.astype(BF16)

    @pl.when(safe)
    def _():
        for it in range(n_items):
            rows = slice(it * length, (it + 1) * length)
            base = base_ref[it * 2 * length:(it + 1) * 2 * length, :]

            def a_of(p, rows=rows, base=base):
                sl = slice(p * LANES, (p + 1) * LANES)
                return _hgrn_intra(pair(q_ref, rows, p), pair(k_ref, rows, p), base[0:length, sl],
                                   masks, m0, m1, None, sl)

            finish(rows, base, it // n_chunk, a_of)

    @pl.when(jnp.logical_not(safe))
    def _():
        def body(it, carry):
            rows = pl.ds(pl.multiple_of(it * length, length), length)
            arg_ref[...] = _dot(sell_ref[...], pieces(rows))
            base = base_ref[pl.ds(pl.multiple_of(it * 2 * length, 2 * length), 2 * length), :]
            seq = it // n_chunk if n_chunk > 1 and n_seq > 1 else (it if n_chunk == 1 else 0)

            def a_of(p):
                sl = slice(p * LANES, (p + 1) * LANES)
                return _hgrn_intra(pair(q_ref, rows, p), pair(k_ref, rows, p), base[0:length, sl],
                                   masks, m0, m1, arg_ref, sl)

            finish(rows, base, seq, a_of)
            return carry

        lax.fori_loop(0, n_items, body, 0)

    if carry_state:
        @pl.when(pl.program_id(1) == pl.num_programs(1) - 1)
        def _():
            _pairs_to_state(st_ref, s_ref, n_seq)
    else:
        _pairs_to_state(st_ref, s_ref, n_seq)


def _hgrn_scratch(n_seq, n_items, length, sell):
    return [pltpu.VMEM((n_seq, N_PAIR, LANES, LANES), F32),
            pltpu.VMEM((n_items * 2 * length, W_B), F32),
            pltpu.VMEM((sell.shape[0], W_B), F32)]


def _hgrn_prompt(qb, kb, ib, gb, logf, consts, bd, onorm_t, batch, seq):
    selb, sell, masks = consts
    steps = seq // ATT_ROWS
    n_chunk = ATT_ROWS // CHUNK
    row = pl.BlockSpec((ATT_ROWS, W_B), lambda b, j: (b * steps + j, 0))
    whole = lambda a: pl.BlockSpec(a.shape, lambda b, j: (0,) * a.ndim)
    return pl.pallas_call(
        functools.partial(_hgrn_kernel, 1, n_chunk, CHUNK, True),
        grid=(batch, steps),
        in_specs=[row, row, row, row, row, whole(selb), whole(sell), whole(masks), whole(bd),
                  whole(onorm_t)],
        out_specs=[row, pl.BlockSpec((1, N_HEADS_B, DK_B, DV_B), lambda b, j: (b, 0, 0, 0))],
        out_shape=[jax.ShapeDtypeStruct((batch * seq, V_B), BF16),
                   jax.ShapeDtypeStruct((batch, N_HEADS_B, DK_B, DV_B), F32)],
        scratch_shapes=_hgrn_scratch(1, n_chunk, CHUNK, sell),
        compiler_params=_cparams(("arbitrary", "arbitrary")),
        name="hgrn_prompt",
    )(qb, kb, ib, gb, logf, selb, sell, masks, bd, onorm_t)


def _hgrn_sample(qb, kb, ib, gb, logf, s0, consts, bd, onorm_t, row0, dec_batch, dec_seq):
    selb, sell, masks = consts
    rows = SAMPLE_SEQS * dec_seq
    row = pl.BlockSpec((rows, W_B), lambda b: (row0 // rows + b, 0))
    whole = lambda a: pl.BlockSpec(a.shape, lambda b: (0,) * a.ndim)
    state = pl.BlockSpec((SAMPLE_SEQS, N_HEADS_B, DK_B, DV_B), lambda b: (b, 0, 0, 0))
    return pl.pallas_call(
        functools.partial(_hgrn_kernel, SAMPLE_SEQS, 1, dec_seq, False),
        grid=(dec_batch // SAMPLE_SEQS,),
        in_specs=[row, row, row, row, row, state, whole(selb), whole(sell), whole(masks),
                  whole(bd), whole(onorm_t)],
        out_specs=[pl.BlockSpec((rows, V_B), lambda b: (b, 0)), state],
        out_shape=[jax.ShapeDtypeStruct((dec_batch * dec_seq, V_B), BF16),
                   jax.ShapeDtypeStruct((dec_batch, N_HEADS_B, DK_B, DV_B), F32)],
        scratch_shapes=_hgrn_scratch(SAMPLE_SEQS, SAMPLE_SEQS, dec_seq, sell),
        compiler_params=_cparams(("arbitrary",)),
        name="hgrn_sample",
    )(qb, kb, ib, gb, logf, s0, selb, sell, masks, bd, onorm_t)


def _outproj_kernel(prompt_tiles, x_ref, oap_ref, obp_ref, oas_ref, obs_ref, g1_ref, sh_ref, sc_ref,
                    ln_ref, w_ref, wr_ref, rb_ref, x1_out, h2_out, cls_out, ga_out, gb_out):
    def mix_of(oa_ref, ob_ref):
        return jnp.concatenate([oa_ref[hh] for hh in range(GQA_GROUP)] + [ob_ref[...]], axis=1)

    is_prompt = pl.program_id(0) < prompt_tiles
    mix = jnp.where(is_prompt, mix_of(oap_ref, obp_ref), mix_of(oas_ref, obs_ref))
    att = _dot(mix, w_ref[...])
    x1 = _gated_add(x_ref[...], g1_ref, att)
    x1_out[...] = x1
    h = _modulated_norm(x1, ln_ref, sc_ref, sh_ref)

    for s in range(SUBLANES):
        h2_out[pl.ds(s, ROWS, stride=SUBLANES), :] = h[:, s * LANES:(s + 1) * LANES]

    h_hi, h_lo = _split2(h)
    w_hi, w_lo = _split2(wr_ref[...])
    hi_terms = _dot_nt(jnp.concatenate([w_hi, w_lo], axis=0), h_hi)
    logits = hi_terms[0:N_EXPERTS] + hi_terms[N_EXPERTS:] + _dot_nt(w_hi, h_lo)
    aff = _sigmoid(logits)
    sel = aff + rb_ref[...]
    e_idx = lax.broadcasted_iota(I32, (N_EXPERTS, ROWS), 0)
    l_idx = lax.broadcasted_iota(I32, (EXPERTS_PER_GROUP, ROWS), 0)

    def first_argmax(vals, idx, big):
        top = jnp.max(vals, axis=0, keepdims=True)
        return top, jnp.min(jnp.where(vals == top, idx, big), axis=0, keepdims=True)

    g_scores = []
    for g in range(N_GROUPS):
        sg = sel[g * EXPERTS_PER_GROUP:(g + 1) * EXPERTS_PER_GROUP]
        m1, i1 = first_argmax(sg, l_idx, EXPERTS_PER_GROUP)
        m2 = jnp.max(jnp.where(l_idx == i1, -jnp.inf, sg), axis=0, keepdims=True)
        g_scores.append(m1 + m2)
    g_score = jnp.concatenate(g_scores, axis=0)
    g_idx = lax.broadcasted_iota(I32, (N_GROUPS, ROWS), 0)
    _, g_sel = first_argmax(g_score, g_idx, N_GROUPS)
    masked = jnp.where((e_idx // EXPERTS_PER_GROUP) == g_sel, sel, -jnp.inf)
    _, e1 = first_argmax(masked, e_idx, N_EXPERTS)
    _, e2 = first_argmax(jnp.where(e_idx == e1, -jnp.inf, masked), e_idx, N_EXPERTS)
    w1 = jnp.sum(jnp.where(e_idx == e1, aff, 0.0), axis=0, keepdims=True)
    w2 = jnp.sum(jnp.where(e_idx == e2, aff, 0.0), axis=0, keepdims=True)
    den = w1 + w2
    w1 = w1 / den
    w2 = w2 / den
    lo_first = e1 < e2
    gate_lo = jnp.where(lo_first, w1, w2)
    gate_hi = jnp.where(lo_first, w2, w1)
    a = jnp.minimum(e1, e2) - g_sel * EXPERTS_PER_GROUP
    b = jnp.maximum(e1, e2) - g_sel * EXPERTS_PER_GROUP
    pair = jnp.where(a == 0, b - 1, jnp.where(a == 1, 6 - b, 5))
    cls_out[...] = g_sel * PAIRS_PER_GROUP + pair
    ga_out[...] = jnp.where(a == 2, gate_hi, gate_lo)
    gb_out[...] = jnp.where(a == 2, gate_lo, gate_hi)


def _outproj(layer, x4, oap4, obp, oas4, obs, mod, ln2, w_out_b, w_router_t, router_bias_c,
             tiles_per_batch):
    n_tiles = x4.shape[0] // ROWS
    x_spec = pl.BlockSpec((ROWS, D_MODEL), lambda i: (i, 0))
    prompt_tiles = oap4.shape[0]
    last = prompt_tiles - 1
    lane_row = pl.BlockSpec((None, 1, ROWS), lambda i: (i, 0, 0))
    return pl.pallas_call(
        functools.partial(_outproj_kernel, prompt_tiles),
        grid=(n_tiles,),
        in_specs=[
            x_spec,
            pl.BlockSpec((None, GQA_GROUP, ROWS, LANES), lambda i: (jnp.minimum(i, last), 0, 0, 0)),
            pl.BlockSpec((ROWS, V_B), lambda i: (jnp.minimum(i, last), 0)),
            pl.BlockSpec((None, GQA_GROUP, ROWS, LANES), lambda i: (0, 0, 0, 0)),
            pl.BlockSpec((ROWS, V_B), lambda i: (0, 0)),
            _mod_spec(layer, 2, tiles_per_batch), _mod_spec(layer, 3, tiles_per_batch),
            _mod_spec(layer, 4, tiles_per_batch),
            pl.BlockSpec((None, 1, D_MODEL), lambda i: (layer, 0, 0)),
            pl.BlockSpec((None, Q_A + V_B, D_MODEL), lambda i: (layer, 0, 0)),
            pl.BlockSpec((N_EXPERTS, D_MODEL), lambda i: (0, 0)),
            pl.BlockSpec((N_EXPERTS, 1), lambda i: (0, 0)),
        ],
        out_specs=[
            x_spec,
            pl.BlockSpec((ROWS * SUBLANES, LANES), lambda i: (i, 0)),
            lane_row, lane_row, lane_row,
        ],
        out_shape=[
            jax.ShapeDtypeStruct(x4.shape, F32),
            jax.ShapeDtypeStruct((n_tiles * ROWS * SUBLANES, LANES), F32),
            jax.ShapeDtypeStruct((n_tiles, 1, ROWS), I32),
            jax.ShapeDtypeStruct((n_tiles, 1, ROWS), F32),
            jax.ShapeDtypeStruct((n_tiles, 1, ROWS), F32),
        ],
        compiler_params=_cparams(("arbitrary",)),
        name=f"outproj{layer}",
    )(x4, oap4, obp, oas4, obs, mod, mod, mod, ln2, w_out_b, w_router_t, router_bias_c)


def _rank_kernel(cls_ref, upper_ref, lower_ref, dest_ref, count_ref, rank_ref):
    n_tiles = cls_ref.shape[0]
    c_idx = lax.broadcasted_iota(I32, (GRP, ROWS), 0)
    upper = upper_ref[...]

    def rank_body(i, carry):
        onehot = c_idx == cls_ref[i]
        within = _dot(onehot.astype(BF16), upper)
        rank_ref[i] = jnp.sum(jnp.where(onehot, within + carry, 0.0), axis=0, keepdims=True)
        return carry + jnp.sum(onehot.astype(F32), axis=1, keepdims=True)

    count = lax.fori_loop(0, n_tiles, rank_body, jnp.zeros((GRP, 1), F32))
    count_ref[...] = jnp.broadcast_to(count, (GRP, LANES)).astype(I32)
    tiles = jnp.floor((count + (TM - 1)) * (1.0 / TM))
    first_tile = _dot(lower_ref[...], jnp.broadcast_to(tiles, (GRP, LANES)).astype(BF16))
    start = first_tile[:, 0:1] * TM

    def dest_body(i, carry):
        onehot = c_idx == cls_ref[i]
        dest = rank_ref[i] + jnp.sum(jnp.where(onehot, start, 0.0), axis=0, keepdims=True)
        dest_ref[i] = dest.astype(I32)
        return carry

    lax.fori_loop(0, n_tiles, dest_body, 0)


def _plan_kernel(n_tok, n_t, dest_ref, count_ref, src_ref, ea_ref, eb_ref, nv_ref, nt_ref):
    def fill(first_group, end_group):
        def body(p8, carry):
            for u in range(8):
                src_ref[p8 * 8 + u] = 0
            return carry
        lax.fori_loop(first_group, end_group, body, 0)

    tile = 0
    for c in range(N_CLASS):
        g, pi = divmod(c, PAIRS_PER_GROUP)
        a, b = PAIR_TABLE[pi]
        cnt = count_ref[c, 0]
        n_c = lax.shift_right_logical(cnt + (TM - 1), TM.bit_length() - 1)
        first = tile

        def mark(t, carry, g=g, a=a, b=b, cnt=cnt, first=first):
            ea_ref[t] = g * EXPERTS_PER_GROUP + a
            eb_ref[t] = g * EXPERTS_PER_GROUP + b
            nv_ref[t] = jnp.minimum(cnt - (t - first) * TM, TM)
            return carry

        lax.fori_loop(tile, tile + n_c, mark, 0)

        fill(lax.shift_right_logical(first * TM + cnt, 3), (first + n_c) * (TM // 8))
        tile = tile + n_c
    nt_ref[0] = tile

    def unused(t, carry):
        ea_ref[t] = 0
        eb_ref[t] = 0
        nv_ref[t] = 0
        return carry

    lax.fori_loop(tile, n_t, unused, 0)

    fill(tile * (TM // 8), n_t * (TM // 8))

    def place(t, carry):
        src_ref[dest_ref[t]] = t
        return carry

    lax.fori_loop(0, n_tok, place, 0, unroll=8)


def _route_plan(cls3, upper, lower):
    n_tiles = cls3.shape[0]
    n_tok = n_tiles * ROWS
    n_t = (n_tok + N_CLASS * (TM - 1)) // TM + 2
    dest, count = pl.pallas_call(
        _rank_kernel,
        out_shape=[jax.ShapeDtypeStruct((n_tiles, 1, ROWS), I32),
                   jax.ShapeDtypeStruct((GRP, LANES), I32)],
        scratch_shapes=[pltpu.VMEM((n_tiles, 1, ROWS), F32)],
        name="moe_rank",
    )(cls3, upper, lower)
    smem = pl.BlockSpec(memory_space=pltpu.SMEM)
    return pl.pallas_call(
        functools.partial(_plan_kernel, n_tok, n_t),
        in_specs=[smem, smem],
        out_specs=[smem, smem, smem, smem, smem],
        out_shape=[jax.ShapeDtypeStruct((n_t * TM,), I32),
                   jax.ShapeDtypeStruct((n_t,), I32),
                   jax.ShapeDtypeStruct((n_t,), I32),
                   jax.ShapeDtypeStruct((n_t,), I32),
                   jax.ShapeDtypeStruct((1,), I32)],
        name="moe_plan",
    )(dest.reshape(n_tok), count)


def _tile_rows(ref, slot, r):
    return ref.at[pl.ds(pl.multiple_of((slot * TM + r) * SUBLANES, SUBLANES), SUBLANES)]


def _moe_kernel(n_tok, src_ref, ea_ref, eb_ref, nv_ref, nt_ref, ga_ref, gb_ref, h2_hbm, wga_ref,
                wua_ref, wda_ref, wgb_ref, wub_ref, wdb_ref, y_hbm, xbuf, ybuf, gbuf, gsem, ssem):
    i = pl.program_id(0)
    n_used = nt_ref[0]
    lane = lax.broadcasted_iota(I32, (1, LANES), 1)

    def token_tile(hbm, tok):
        return hbm.at[pl.ds(pl.multiple_of(tok * SUBLANES, SUBLANES), SUBLANES)]

    def gather_row(tile, slot, r, priority):
        tok = src_ref[tile * TM + r]
        pltpu.make_async_copy(token_tile(h2_hbm, tok), _tile_rows(xbuf, slot, r),
                              gsem.at[slot]).start(priority=priority)
        gbuf[pl.ds(slot * TM + r, 1), :] = jnp.where(lane == 0, ga_ref[tok], gb_ref[tok])

    def scatter_row(tile, slot, r, n_valid, priority):
        tok = jnp.where(r < n_valid, src_ref[tile * TM + r], n_tok + slot * TM + r)
        pltpu.make_async_copy(_tile_rows(ybuf, slot, r), token_tile(y_hbm, tok),
                              ssem.at[slot]).start(priority=priority)

    def row_loop(start_row):
        def body(r8, c):
            for u in range(DMA_UNROLL):
                start_row(r8 * DMA_UNROLL + u, u % 2)
            return c
        lax.fori_loop(0, TM // DMA_UNROLL, body, 0)

    def wait_gather(slot):
        pltpu.make_async_copy(h2_hbm.at[pl.ds(0, TM * SUBLANES)],
                              xbuf.at[pl.ds(slot * TM * SUBLANES, TM * SUBLANES)],
                              gsem.at[slot]).wait()

    def wait_scatter(slot):
        pltpu.make_async_copy(ybuf.at[pl.ds(slot * TM * SUBLANES, TM * SUBLANES)],
                              y_hbm.at[pl.ds(0, TM * SUBLANES)], ssem.at[slot]).wait()

    @pl.when(i == 0)
    def _():
        ybuf[...] = jnp.zeros_like(ybuf)
        spare = pltpu.make_async_copy(
            ybuf, y_hbm.at[pl.ds(n_tok * SUBLANES, 2 * TM * SUBLANES)], ssem.at[0])
        spare.start()
        spare.wait()

    @pl.when(jnp.logical_and(i == 0, n_used > 0))
    def _():
        row_loop(lambda r, pr: gather_row(0, 0, r, pr))

    @pl.when(i + 1 < n_used)
    def _():
        row_loop(lambda r, pr: gather_row(i + 1, (i + 1) % 2, r, pr))

    @pl.when(jnp.logical_and(i >= 2, i - 2 < n_used))
    def _():
        wait_scatter(i % 2)

    @pl.when(i < n_used)
    def _():
        slot = i % 2
        wait_gather(slot)
        base = slot * TM * SUBLANES
        xb = jnp.concatenate([_token_rows(xbuf, s, TM, base) for s in range(SUBLANES)],
                             axis=1).astype(BF16)
        gates = gbuf[pl.ds(slot * TM, TM), :]
        y = jnp.zeros((TM, D_MODEL), F32)
        for e, (wg, wu, wd) in enumerate(((wga_ref, wua_ref, wda_ref), (wgb_ref, wub_ref, wdb_ref))):
            gt = _dot(xb, wg[...].astype(BF16))
            up = _dot(xb, wu[...].astype(BF16))
            act = gt * _sigmoid(gt) * up * gates[:, e:e + 1]
            y = y + _dot(act.astype(BF16), wd[...].astype(BF16))
        for s in range(SUBLANES):
            ybuf[pl.ds(base + s, TM, stride=SUBLANES), :] = y[:, s * LANES:(s + 1) * LANES]
        n_valid = nv_ref[i]
        row_loop(lambda r, pr: scatter_row(i, slot, r, n_valid, pr))


def _moe(layer, h2, gate_a, gate_b, plan, w_gate, w_up, w_down):
    src, ea, eb, nv, nt = plan
    n_t = ea.shape[0]
    n_tok = h2.shape[0] // SUBLANES

    def w_spec(shape, which):
        def index(i, src_ref, ea_ref, eb_ref, nv_ref, nt_ref, ga_ref, gb_ref):
            t = jnp.minimum(i, jnp.maximum(nt_ref[0] - 1, 0))
            e = (ea_ref, eb_ref)[which][t]
            return (layer, e, 0, 0)
        return pl.BlockSpec((None, None) + shape, index)

    gu = (D_MODEL, D_EXPERT)
    dn = (D_EXPERT, D_MODEL)
    grid_spec = pltpu.PrefetchScalarGridSpec(
        num_scalar_prefetch=7,
        grid=(n_t,),
        in_specs=[pl.BlockSpec(memory_space=pl.ANY),
                  w_spec(gu, 0), w_spec(gu, 0), w_spec(dn, 0),
                  w_spec(gu, 1), w_spec(gu, 1), w_spec(dn, 1)],
        out_specs=pl.BlockSpec(memory_space=pl.ANY),
        scratch_shapes=[pltpu.VMEM((2 * TM * SUBLANES, LANES), F32),
                        pltpu.VMEM((2 * TM * SUBLANES, LANES), F32),
                        pltpu.VMEM((2 * TM, LANES), F32),
                        pltpu.SemaphoreType.DMA((2,)), pltpu.SemaphoreType.DMA((2,))],
    )
    return pl.pallas_call(
        functools.partial(_moe_kernel, n_tok),
        grid_spec=grid_spec,
        out_shape=jax.ShapeDtypeStruct(((n_tok + 2 * TM) * SUBLANES, LANES), F32),
        compiler_params=_cparams(("arbitrary",)),
        name=f"moe{layer}",
    )(src, ea, eb, nv, nt, gate_a.reshape(n_tok), gate_b.reshape(n_tok), h2,
      w_gate, w_up, w_down, w_gate, w_up, w_down)


def kernel(x_prompt, x_sample, cache_k, cache_v, state_hgrn, c_prompt, c_sample, w_ada, b_ada,
           ln1, ln2, w_in, q_norm, k_norm, sinks, lb_param, o_norm, w_out, w_router, router_bias,
           w_e_gate, w_e_up, w_e_down):
    batch, seq, _ = x_prompt.shape
    dec_batch, dec_seq, _ = x_sample.shape
    depth = w_ada.shape[0]
    assert dec_batch == GRP and dec_seq == SUB and seq % ROWS == 0
    tiles_per_batch = seq // ROWS
    n_prompt = batch * seq
    n_tok = n_prompt + dec_batch * dec_seq
    n_tiles = n_tok // ROWS
    prompt_tiles = n_prompt // ROWS
    n_cache = cache_k.shape[2]

    seg = np.arange(Q_A) // HEAD_DIM
    bd = jnp.asarray((seg[:, None] == seg[None, :]).astype(np.float32) / HEAD_DIM, BF16)

    def hgrn_consts(length):
        selb, sell, masks = _hgrn_consts(length)
        return jnp.asarray(selb, BF16), jnp.asarray(sell, BF16), jnp.asarray(masks)

    consts_p = hgrn_consts(CHUNK)
    consts_s = hgrn_consts(dec_seq)
    upper = jnp.asarray(np.triu(np.ones((ROWS, ROWS), np.float32), 1), BF16)
    lower = jnp.asarray(np.tril(np.ones((GRP, GRP), np.float32), -1), BF16)

    wq = w_in[:, :, :Q_A].reshape(depth, D_MODEL, N_KV_A, GQA_GROUP, HEAD_DIM)
    wq = jnp.swapaxes(wq, 2, 3).reshape(depth, D_MODEL, Q_A)
    w_in_b = jnp.concatenate([wq, w_in[:, :, Q_A:]], axis=-1).astype(BF16)
    wo = w_out[:, :Q_A].reshape(depth, N_KV_A, GQA_GROUP, HEAD_DIM, D_MODEL)
    wo = jnp.swapaxes(wo, 1, 2).reshape(depth, Q_A, D_MODEL)
    w_out_b = jnp.concatenate([wo, w_out[:, Q_A:]], axis=1).astype(BF16)
    q_norm_t = jnp.tile(q_norm, (1, N_HEADS_A)).reshape(depth, 1, Q_A)
    k_norm_t = jnp.tile(k_norm, (1, N_KV_A)).reshape(depth, 1, KV_A)
    o_norm_t = jnp.tile(o_norm, (1, N_HEADS_B)).reshape(depth, 1, V_B)
    ln1_r = ln1.reshape(depth, 1, D_MODEL)
    ln2_r = ln2.reshape(depth, 1, D_MODEL)
    w_router_t = w_router.T
    router_bias_c = router_bias.reshape(N_EXPERTS, 1)

    c_exp = jnp.concatenate([jnp.repeat(c_prompt, GRP, axis=0), c_sample], axis=0)
    mod = _adaln(c_exp, w_ada, b_ada).reshape(depth, batch + 1, GRP, N_MOD * D_MODEL)

    cache_k2 = cache_k.reshape(depth, dec_batch, n_cache, KV_A)
    cache_v2 = cache_v.reshape(depth, dec_batch, n_cache, KV_A)

    src = (x_prompt, x_sample)
    kp, vp, sp, kss, vss, sss = [], [], [], [], [], []
    for l in range(depth):
        x, q4, k, v, qb, logf, kb, ib, gb, kwin, vwin = _inproj(
            l, src, n_tiles, prompt_tiles, mod, ln1_r, w_in_b, q_norm_t, k_norm_t, lb_param, bd,
            tiles_per_batch)

        oap4 = _attn_prompt(q4, k, v, sinks[l], batch, seq)
        oas4 = _attn_sample(q4, k, v, cache_k2[l], cache_v2[l], sinks[l], prompt_tiles,
                            n_prompt, dec_batch, dec_seq)
        obp, s_p = _hgrn_prompt(qb, kb, ib, gb, logf, consts_p, bd, o_norm_t[l], batch, seq)
        obs, s_s = _hgrn_sample(qb, kb, ib, gb, logf, state_hgrn[l], consts_s,
                                bd, o_norm_t[l], n_prompt, dec_batch, dec_seq)

        x1, h2, cls3, gate_a, gate_b = _outproj(l, x, oap4, obp, oas4, obs, mod, ln2_r, w_out_b,
                                                w_router_t, router_bias_c, tiles_per_batch)
        y = _moe(l, h2, gate_a, gate_b, _route_plan(cls3, upper, lower), w_e_gate, w_e_up,
                 w_e_down)
        src = (x1, y)

        kp.append(kwin.reshape(batch, WINDOW, N_KV_A, HEAD_DIM))
        vp.append(vwin.reshape(batch, WINDOW, N_KV_A, HEAD_DIM))
        sp.append(s_p)
        kss.append(k[n_prompt:].reshape(dec_batch, dec_seq, N_KV_A, HEAD_DIM))
        vss.append(v[n_prompt:].reshape(dec_batch, dec_seq, N_KV_A, HEAD_DIM))
        sss.append(s_s)

    yp, ys = _final(depth - 1, src[0], src[1], mod, batch, seq, tiles_per_batch)
    return (yp, ys, jnp.stack(kp), jnp.stack(vp), jnp.stack(sp),
            jnp.stack(kss), jnp.stack(vss), jnp.stack(sss))
```

```python
import functools

import numpy as np
import jax
import jax.numpy as jnp
from jax import lax
from jax.experimental import pallas as pl
from jax.experimental.pallas import tpu as pltpu

F32 = jnp.float32
BF16 = jnp.bfloat16
I32 = jnp.int32

D_MODEL = 1024
HEAD_DIM = 64
N_HEADS_A = 8
N_KV_A = 2
GQA_GROUP = N_HEADS_A // N_KV_A
N_HEADS_B = 8
DK_B = 64
DV_B = 64
Q_A = N_HEADS_A * HEAD_DIM
KV_A = N_KV_A * HEAD_DIM
W_B = N_HEADS_B * DK_B
V_B = N_HEADS_B * DV_B
CHUNK = 64
WINDOW = 128
N_EXPERTS = 16
N_GROUPS = 4
EXPERTS_PER_GROUP = 4
D_EXPERT = D_MODEL // 2
RMS_EPS = 1e-6
NEG_INF = -1e30
N_MOD = 6

LANES = 128
SUBLANES = 8
ROWS = 512
GRP = 32
SUB = ROWS // GRP
ATT_ROWS = 256
N_PAIR = N_HEADS_B // 2
SAMPLE_SEQS = 4
HGRN_SPAN_LIMIT = 80.0
TM = 256
PAIRS_PER_GROUP = 6
PAIR_TABLE = ((0, 1), (0, 2), (0, 3), (1, 3), (1, 2), (3, 2))
N_CLASS = N_GROUPS * PAIRS_PER_GROUP
DMA_UNROLL = 8
GATHER_AHEAD = 2
GATHER_SLOTS = GATHER_AHEAD + 1
VMEM_LIMIT = 56 * 1024 * 1024


def _cparams(sem):
    return pltpu.CompilerParams(dimension_semantics=sem, vmem_limit_bytes=VMEM_LIMIT)


def _dot(a, b):
    return jnp.dot(a, b, preferred_element_type=F32)


def _dot_nt(a, b):
    return lax.dot_general(a, b, (((1,), (1,)), ((), ())), preferred_element_type=F32)


def _dot_tn(a, b):
    return lax.dot_general(a, b, (((0,), (0,)), ((), ())), preferred_element_type=F32)


def _sigmoid(x):
    return 1.0 / (1.0 + jnp.exp(-x))


def _split2(x):
    hi = x.astype(BF16)
    lo = (x - hi.astype(F32)).astype(BF16)
    return hi, lo


def _split3(x):
    hi = x.astype(BF16)
    r = x - hi.astype(F32)
    mid = r.astype(BF16)
    lo = (r - mid.astype(F32)).astype(BF16)
    return hi, mid, lo


def _seg_mean(sq, bd):
    return _dot(sq.astype(BF16), bd)


def _token_rows(ref, s, n, base=0):
    return ref[pl.ds(base + s, n, stride=SUBLANES), :]


def _adaln_kernel(c_ref, w_ref, b_ref, o_ref):
    c = c_ref[...]
    a = c * _sigmoid(c)
    a_hi, a_lo = _split2(a)
    w_hi, w_lo = _split2(w_ref[...])
    acc = _dot(a_hi, w_hi) + _dot(a_lo, w_hi) + _dot(a_hi, w_lo)
    o_ref[...] = acc + b_ref[...]


def _adaln(c_exp, w_ada, b_ada):
    depth = w_ada.shape[0]
    rows = c_exp.shape[0]
    tn = D_MODEL
    return pl.pallas_call(
        _adaln_kernel,
        grid=(depth, N_MOD * D_MODEL // tn),
        in_specs=[
            pl.BlockSpec((rows, D_MODEL), lambda l, n: (0, 0)),
            pl.BlockSpec((None, D_MODEL, tn), lambda l, n: (l, 0, n)),
            pl.BlockSpec((None, 1, tn), lambda l, n: (l, 0, n)),
        ],
        out_specs=pl.BlockSpec((None, rows, tn), lambda l, n: (l, 0, n)),
        out_shape=jax.ShapeDtypeStruct((depth, rows, N_MOD * D_MODEL), F32),
        compiler_params=_cparams(("arbitrary", "arbitrary")),
        name="adaln",
    )(c_exp, w_ada, b_ada.reshape(depth, 1, N_MOD * D_MODEL))


def _mod_spec(layer, j, tiles_per_batch):
    return pl.BlockSpec((None, None, GRP, D_MODEL),
                        lambda i, *_: (layer, i // tiles_per_batch, 0, j))


def _by_group(fn, xs, mod_refs):
    outs = []
    for g in range(GRP):
        rows = slice(g * SUB, (g + 1) * SUB)
        outs.append(fn(*[x[rows] for x in xs], *[m[g:g + 1, :] for m in mod_refs]))
    return jnp.concatenate(outs, axis=0)


def _gated_add(x, gate_ref, y):
    return _by_group(lambda xg, yg, gg: xg + gg * yg, (x, y), (gate_ref,))


def _modulated_norm(x, ln_ref, sc_ref, sh_ref):
    ms = jnp.mean(x * x, axis=-1, keepdims=True)
    xn = x * lax.rsqrt(ms + RMS_EPS)
    scale = ln_ref[...] * (1.0 + sc_ref[...])
    return _by_group(lambda xg, sc, sh: xg * sc + sh, (xn,), (scale, sh_ref))


def _prompt_tile_spec(prompt_tiles, tiles_per_batch):
    def index(i):
        t = jnp.minimum(i, prompt_tiles - 1)
        return (t // tiles_per_batch, t % tiles_per_batch, 0)
    return pl.BlockSpec((None, ROWS, D_MODEL), index)


def _moe_output_tile(y_ref):
    return jnp.concatenate([_token_rows(y_ref, s, ROWS) for s in range(SUBLANES)], axis=1)


def _inproj_kernel(layer, prompt_tiles, n_src, *refs):
    src = refs[:n_src]
    sh_ref, sc_ref, ln_ref, w_ref, qn_ref, kn_ref, lbp_ref, bd_ref = refs[n_src:n_src + 8]
    (x_out, q_out, k_out, v_out, qb_out, logf_out, kb_out, ib_out, gb_out,
     kwin_out, vwin_out) = refs[n_src + 8:]
    if layer == 0:
        xp_ref, xs_ref = src
        x = jnp.where(pl.program_id(0) < prompt_tiles, xp_ref[...],
                      xs_ref[...].reshape(ROWS, D_MODEL))
    else:
        x1_ref, y_ref, g2_ref = src
        x = _gated_add(x1_ref[...], g2_ref, _moe_output_tile(y_ref))
    x_out[...] = x
    hb = _modulated_norm(x, ln_ref, sc_ref, sh_ref).astype(BF16)
    bd = bd_ref[...]

    c0 = Q_A + 2 * KV_A
    c1 = c0 + 2 * W_B
    qa = _dot(hb, w_ref[:, 0:Q_A])
    kv = _dot(hb, w_ref[:, Q_A:c0])
    qb = _dot(hb, w_ref[:, c0:c0 + W_B])
    z = _dot(hb, w_ref[:, c0 + W_B:c1])
    ib = _dot(hb, w_ref[:, c1:c1 + V_B])
    gb = _dot(hb, w_ref[:, c1 + V_B:c1 + 2 * V_B])

    qa = qa * lax.rsqrt(_seg_mean(qa * qa, bd) + RMS_EPS) * qn_ref[...]
    qa = (qa * (HEAD_DIM ** -0.5)).astype(BF16)
    for hh in range(GQA_GROUP):
        q_out[hh] = qa[:, hh * LANES:(hh + 1) * LANES]

    ka = kv[:, :KV_A]
    ka = ka * lax.rsqrt(_seg_mean(ka * ka, bd[:KV_A, :KV_A]) + RMS_EPS) * kn_ref[...]
    k_out[...] = ka
    v_out[...] = kv[:, KV_A:]

    @pl.when(pl.program_id(0) < prompt_tiles)
    def _():
        kwin_out[...] = ka[ROWS - WINDOW:]
        vwin_out[...] = kv[ROWS - WINDOW:, KV_A:]

    qb_out[...] = (qb * (DK_B ** -0.5)).astype(BF16)
    ib_out[...] = ib.astype(BF16)
    gb_out[...] = gb.astype(BF16)

    p = lbp_ref[...]
    e = jnp.exp(p - jnp.max(p, axis=0, keepdims=True))
    sm = e / jnp.sum(e, axis=0, keepdims=True)
    cum0 = sm[0:1]
    cum = cum0
    for r in range(1, layer + 1):
        cum = cum + sm[r:r + 1]
    lb = cum - cum0

    pos = z >= 0.0
    t = jnp.exp(-jnp.abs(z))
    num = jnp.where(pos, jnp.log(1.0 + lb * t), jnp.maximum(jnp.log(lb + t), z))
    logf_out[...] = num - jnp.log(1.0 + t)
    kb_out[...] = ((1.0 - lb) * jnp.where(pos, t, 1.0) / (1.0 + t)).astype(BF16)


def _inproj(layer, src, n_tiles, prompt_tiles, mod, ln1, w_in_b, q_norm_t, k_norm_t, lb_param, bd,
            tiles_per_batch):
    depth = lb_param.shape[0]
    n = n_tiles * ROWS
    in_width = w_in_b.shape[-1]
    x_spec = pl.BlockSpec((ROWS, D_MODEL), lambda i: (i, 0))
    if layer == 0:
        src_specs = [_prompt_tile_spec(prompt_tiles, tiles_per_batch),
                     pl.BlockSpec((GRP, SUB, D_MODEL), lambda i: (0, 0, 0))]
        src_args = list(src)
    else:
        src_specs = [x_spec, pl.BlockSpec((ROWS * SUBLANES, LANES), lambda i: (i, 0)),
                     _mod_spec(layer - 1, 5, tiles_per_batch)]
        src_args = list(src) + [mod]
    row_spec = lambda w: pl.BlockSpec((ROWS, w), lambda i: (i, 0))
    win_spec = pl.BlockSpec(
        (None, WINDOW, KV_A), lambda i: (jnp.minimum(i, prompt_tiles - 1) // tiles_per_batch, 0, 0))
    return pl.pallas_call(
        functools.partial(_inproj_kernel, layer, prompt_tiles, len(src_specs)),
        grid=(n_tiles,),
        in_specs=src_specs + [
            _mod_spec(layer, 0, tiles_per_batch), _mod_spec(layer, 1, tiles_per_batch),
            pl.BlockSpec((None, 1, D_MODEL), lambda i: (layer, 0, 0)),
            pl.BlockSpec((None, D_MODEL, in_width), lambda i: (layer, 0, 0)),
            pl.BlockSpec((None, 1, Q_A), lambda i: (layer, 0, 0)),
            pl.BlockSpec((None, 1, KV_A), lambda i: (layer, 0, 0)),
            pl.BlockSpec((depth, W_B), lambda i: (0, 0)),
            pl.BlockSpec((Q_A, Q_A), lambda i: (0, 0)),
        ],
        out_specs=[
            x_spec,
            pl.BlockSpec((None, GQA_GROUP, ROWS, LANES), lambda i: (i, 0, 0, 0)),
            row_spec(KV_A), row_spec(KV_A),
            row_spec(W_B), row_spec(W_B), row_spec(W_B), row_spec(V_B), row_spec(V_B),
            win_spec, win_spec,
        ],
        out_shape=[
            jax.ShapeDtypeStruct((n, D_MODEL), F32),
            jax.ShapeDtypeStruct((n_tiles, GQA_GROUP, ROWS, LANES), BF16),
            jax.ShapeDtypeStruct((n, KV_A), F32),
            jax.ShapeDtypeStruct((n, KV_A), F32),
            jax.ShapeDtypeStruct((n, W_B), BF16),
            jax.ShapeDtypeStruct((n, W_B), F32),
            jax.ShapeDtypeStruct((n, W_B), BF16),
            jax.ShapeDtypeStruct((n, V_B), BF16),
            jax.ShapeDtypeStruct((n, V_B), BF16),
            jax.ShapeDtypeStruct((prompt_tiles // tiles_per_batch, WINDOW, KV_A), F32),
            jax.ShapeDtypeStruct((prompt_tiles // tiles_per_batch, WINDOW, KV_A), F32),
        ],
        compiler_params=_cparams(("arbitrary",)),
        name=f"inproj{layer}",
    )(*src_args, mod, mod, ln1, w_in_b, q_norm_t, k_norm_t, lb_param, bd)


def _final_kernel(prompt_tiles, x1_ref, y_ref, g2_ref, yp_ref, ys_ref):
    i = pl.program_id(0)
    x = _gated_add(x1_ref[...], g2_ref, _moe_output_tile(y_ref))

    @pl.when(i < prompt_tiles)
    def _():
        yp_ref[...] = x

    @pl.when(i >= prompt_tiles)
    def _():
        ys_ref[...] = x.reshape(GRP, SUB, D_MODEL)


def _final(layer, x1, y, mod, batch, seq, tiles_per_batch):
    n_tiles = x1.shape[0] // ROWS
    prompt_tiles = batch * tiles_per_batch
    return pl.pallas_call(
        functools.partial(_final_kernel, prompt_tiles),
        grid=(n_tiles,),
        in_specs=[pl.BlockSpec((ROWS, D_MODEL), lambda i: (i, 0)),
                  pl.BlockSpec((ROWS * SUBLANES, LANES), lambda i: (i, 0)),
                  _mod_spec(layer, 5, tiles_per_batch)],
        out_specs=[_prompt_tile_spec(prompt_tiles, tiles_per_batch),
                   pl.BlockSpec((GRP, SUB, D_MODEL), lambda i: (0, 0, 0))],
        out_shape=[jax.ShapeDtypeStruct((batch, seq, D_MODEL), F32),
                   jax.ShapeDtypeStruct((GRP, SUB, D_MODEL), F32)],
        compiler_params=_cparams(("arbitrary",)),
        name="final",
    )(x1, y, mod)


def _attn_tables(sinks_ref, q_len, n_k, k_off):
    n_q = GQA_GROUP * q_len
    row = lax.broadcasted_iota(I32, (n_q, n_k), 0)
    col = lax.broadcasted_iota(I32, (n_q, n_k), 1)
    dist = jnp.abs((row % q_len) + k_off - col).astype(F32)
    hh_col = lax.broadcasted_iota(I32, (n_q, 1), 0) // q_len
    bias, sinks = [], []
    for g in range(N_KV_A):
        head = (g * GQA_GROUP + 1 + hh_col).astype(F32)
        bias.append(jnp.exp2(-8.0 * head / N_HEADS_A) * dist)
        sink = jnp.zeros((n_q, 1), F32)
        for j in range(GQA_GROUP):
            sink = jnp.where(hh_col == j, sinks_ref[g * GQA_GROUP + j], sink)
        sinks.append(sink)
    return bias, sinks


def _attend(q, k, v, bias, sinks, first_valid=None):
    n_q = q.shape[0]
    n_k = k.shape[0]
    lane = lax.broadcasted_iota(I32, (n_k, LANES), 1)
    out = jnp.zeros((n_q, LANES), F32)
    for g in range(N_KV_A):
        in_g = (lane // HEAD_DIM) == g
        kg = jnp.where(in_g, k, 0.0).astype(BF16)
        vg = jnp.where(in_g, v, 0.0).astype(BF16)
        s = _dot_nt(q, kg) - bias[g]
        if first_valid is not None:
            col = lax.broadcasted_iota(I32, (n_q, n_k), 1)
            s = jnp.where(col >= first_valid, s, NEG_INF)
        m = jnp.maximum(jnp.max(s, axis=-1, keepdims=True), sinks[g])
        p = jnp.exp(s - m)
        denom = jnp.sum(p, axis=-1, keepdims=True) + jnp.exp(sinks[g] - m)
        out = out + _dot(p.astype(BF16), vg) / denom
    return out


def _attn_prompt_kernel(sinks_ref, q_ref, kp_ref, kc_ref, vp_ref, vc_ref, o_ref):
    j = pl.program_id(1)
    kcat = jnp.concatenate([kp_ref[...], kc_ref[...]], axis=0)
    vcat = jnp.concatenate([vp_ref[...], vc_ref[...]], axis=0)
    bias, sinks = _attn_tables(sinks_ref, CHUNK, WINDOW + CHUNK, WINDOW)
    for c in range(ATT_ROWS // CHUNK):
        q = q_ref[:, c * CHUNK:(c + 1) * CHUNK, :].reshape(GQA_GROUP * CHUNK, LANES)
        k = kcat[c * CHUNK:c * CHUNK + WINDOW + CHUNK]
        v = vcat[c * CHUNK:c * CHUNK + WINDOW + CHUNK]
        first_valid = WINDOW - c * CHUNK - j * ATT_ROWS
        out = _attend(q, k, v, bias, sinks, first_valid)
        o_ref[:, c * CHUNK:(c + 1) * CHUNK, :] = out.reshape(GQA_GROUP, CHUNK, LANES).astype(BF16)


def _attn_prompt(q4, k, v, sinks, batch, seq):
    steps = seq // ATT_ROWS
    per_tile = ROWS // ATT_ROWS
    win_blocks = ATT_ROWS // WINDOW
    n_tiles = batch * seq // ROWS
    cur = pl.BlockSpec((ATT_ROWS, KV_A), lambda b, j: (b * steps + j, 0))
    prev = pl.BlockSpec(
        (WINDOW, KV_A), lambda b, j: (b * steps * win_blocks + jnp.maximum(j * win_blocks - 1, 0), 0))
    qo = pl.BlockSpec((None, GQA_GROUP, ATT_ROWS, LANES),
                      lambda b, j: ((b * steps + j) // per_tile, 0, j % per_tile, 0))
    return pl.pallas_call(
        _attn_prompt_kernel,
        grid=(batch, steps),
        in_specs=[pl.BlockSpec(memory_space=pltpu.SMEM), qo, prev, cur, prev, cur],
        out_specs=qo,
        out_shape=jax.ShapeDtypeStruct((n_tiles, GQA_GROUP, ROWS, LANES), BF16),
        compiler_params=_cparams(("arbitrary", "arbitrary")),
        name="attn_prompt",
    )(sinks, q4, k, k, v, v)


def _attn_sample_kernel(n_new, sinks_ref, q_ref, kc_ref, kn_ref, vc_ref, vn_ref, o_ref):
    n_cache = kc_ref.shape[1]
    bias, sinks = _attn_tables(sinks_ref, n_new, n_cache + n_new, n_cache)
    for r in range(SAMPLE_SEQS):
        rows = slice(r * n_new, (r + 1) * n_new)
        q = q_ref[:, rows, :].reshape(GQA_GROUP * n_new, LANES)
        k = jnp.concatenate([kc_ref[r], kn_ref[rows, :]], axis=0)
        v = jnp.concatenate([vc_ref[r], vn_ref[rows, :]], axis=0)
        out = _attend(q, k, v, bias, sinks)
        o_ref[:, rows, :] = out.reshape(GQA_GROUP, n_new, LANES).astype(BF16)


def _attn_sample(q4, k, v, cache_k, cache_v, sinks, tile, row0, dec_batch, dec_seq):
    n_cache = cache_k.shape[1]
    rows = SAMPLE_SEQS * dec_seq
    q_spec = pl.BlockSpec((None, GQA_GROUP, rows, LANES), lambda b: (tile, 0, b, 0))
    new = pl.BlockSpec((rows, KV_A), lambda b: (row0 // rows + b, 0))
    cache = pl.BlockSpec((SAMPLE_SEQS, n_cache, KV_A), lambda b: (b, 0, 0))
    return pl.pallas_call(
        functools.partial(_attn_sample_kernel, dec_seq),
        grid=(dec_batch // SAMPLE_SEQS,),
        in_specs=[pl.BlockSpec(memory_space=pltpu.SMEM), q_spec, cache, new, cache, new],
        out_specs=pl.BlockSpec((None, GQA_GROUP, rows, LANES), lambda b: (0, 0, b, 0)),
        out_shape=jax.ShapeDtypeStruct((1, GQA_GROUP, dec_batch * dec_seq, LANES), BF16),
        compiler_params=_cparams(("arbitrary",)),
        name="attn_sample",
    )(sinks, q4, cache_k, k, cache_v, v)


def _level_sizes(length):
    sizes = []
    m = length // 2
    while m >= 1:
        sizes.append(m)
        m //= 2
    return sizes


def _hgrn_consts(length):
    t = np.arange(length)
    tri = (t[None, :] <= t[:, None]).astype(np.float32)
    after = (t[None, :] > t[:, None]).astype(np.float32)
    levels = []
    masks = []
    s = np.tile(t, 2)[None, :]
    tt = t[:, None]
    for m in _level_sizes(length):
        blk = t // m
        same = blk[None, :] == blk[:, None]
        q_rows = (blk % 2 == 1)[:, None]
        sel = np.where(q_rows, same & (t[None, :] <= t[:, None]), same & (t[None, :] > t[:, None]))
        levels.append(sel.astype(np.float32))
        masks.append(((tt // m) % 2 == 1) & ((s // m) == (tt // m) - 1))
    masks.append(s == tt)
    masks.append(s <= tt)
    base = np.concatenate([tri, after], axis=0)
    lev = np.concatenate(levels, axis=0)
    return (np.concatenate([base] * 3, axis=1), np.concatenate([lev] * 3, axis=1),
            np.stack(masks).astype(np.int32))


def _stack2(x, m0, m1):
    xb = x.astype(BF16)
    return jnp.concatenate([xb * m0, xb * m1], axis=0)


def _hgrn_intra(qp, kp, b, masks, m0, m1, arg_ref, sl):
    length = qp.shape[0]
    n_lev = len(_level_sizes(length))
    if arg_ref is None:
        mid = length // 2 - 1
        rel = b - b[mid:mid + 1]
        a = _dot_nt((qp * jnp.exp(rel)).astype(BF16), _stack2(kp * jnp.exp(-rel), m0, m1))
        a = jnp.where(masks[n_lev + 1], a, 0.0)
    else:
        a = _dot_nt(qp.astype(BF16), _stack2(kp, m0, m1))
        a = jnp.where(masks[n_lev], a, 0.0)
        for lev in range(n_lev):
            e = jnp.exp(arg_ref[lev * length:(lev + 1) * length, sl])
            pr = _dot_nt((qp * e).astype(BF16), _stack2(kp * e, m0, m1))
            a = jnp.where(masks[lev], pr, a)
    return a.astype(BF16)


def _hgrn_finish(a, qp, kp, ip, base, sl, st_ref, seq, p, m0, m1, same_head):
    length = qp.shape[0]
    e_b = jnp.exp(base[0:length, sl])
    e_k = jnp.exp(base[length:2 * length, sl])
    st = st_ref[seq, p]
    o = _dot(jnp.concatenate([a, (qp * e_b).astype(BF16)], axis=1),
             jnp.concatenate([_stack2(ip, m0, m1), st.T.astype(BF16)], axis=0))
    upd = _dot_tn(ip.astype(BF16), (kp * e_k).astype(BF16))
    st_ref[seq, p] = st * e_b[length - 1:length, :] + jnp.where(same_head, upd, 0.0)
    return o


def _state_to_pairs(s0_ref, st_ref, n_seq):
    zero = jnp.zeros((DK_B, DV_B), F32)
    for r in range(n_seq):
        for p in range(N_PAIR):
            top = jnp.concatenate([s0_ref[r, 2 * p], zero], axis=1)
            bot = jnp.concatenate([zero, s0_ref[r, 2 * p + 1]], axis=1)
            st_ref[r, p] = jnp.concatenate([top, bot], axis=0).T


def _pairs_to_state(st_ref, s_ref, n_seq):
    for r in range(n_seq):
        for p in range(N_PAIR):
            d = st_ref[r, p].T
            s_ref[r, 2 * p] = d[:DK_B, :DV_B]
            s_ref[r, 2 * p + 1] = d[DK_B:, DV_B:]


def _hgrn_kernel(n_seq, n_chunk, length, carry_state, *refs):
    if carry_state:
        (q_ref, k_ref, i_ref, g_ref, f_ref, selb_ref, sell_ref, mask_ref, bd_ref, on_ref,
         o_ref, s_ref, st_ref, base_ref, arg_ref) = refs

        @pl.when(pl.program_id(1) == 0)
        def _():
            st_ref[...] = jnp.zeros_like(st_ref)
    else:
        (q_ref, k_ref, i_ref, g_ref, f_ref, s0_ref, selb_ref, sell_ref, mask_ref, bd_ref, on_ref,
         o_ref, s_ref, st_ref, base_ref, arg_ref) = refs
        _state_to_pairs(s0_ref, st_ref, n_seq)

    bd = bd_ref[...]
    onorm = on_ref[...]
    n_items = n_seq * n_chunk
    mid = length // 2 - 1
    lane = lax.broadcasted_iota(I32, (length, LANES), 1)
    m0 = (lane < DK_B).astype(BF16)
    m1 = (lane >= DK_B).astype(BF16)
    rowp = lax.broadcasted_iota(I32, (LANES, LANES), 0) // DV_B
    colp = lax.broadcasted_iota(I32, (LANES, LANES), 1) // DK_B
    same_head = rowp == colp
    masks = [mask_ref[m] != 0 for m in range(mask_ref.shape[0])]

    def pieces(rows):
        return jnp.concatenate(_split3(f_ref[rows, :]), axis=0)

    def pair(ref, rows, p):
        return ref[rows, p * LANES:(p + 1) * LANES].astype(F32)

    span = jnp.float32(0.0)
    for it in range(n_items):
        base = _dot(selb_ref[...], pieces(slice(it * length, (it + 1) * length)))
        base_ref[it * 2 * length:(it + 1) * 2 * length, :] = base
        b_mid = base[mid:mid + 1]
        span = jnp.maximum(span, jnp.max(base[0:1] - b_mid))
        span = jnp.maximum(span, jnp.max(b_mid - base[length - 1:length]))
    safe = span <= HGRN_SPAN_LIMIT

    def finish(rows, base, seq, a_of):
        outs = []
        for p in range(N_PAIR):
            sl = slice(p * LANES, (p + 1) * LANES)
            outs.append(_hgrn_finish(a_of(p), pair(q_ref, rows, p), pair(k_ref, rows, p),
                                     pair(i_ref, rows, p), base, sl, st_ref, seq, p, m0, m1,
                                     same_head))
        o = jnp.concatenate(outs, axis=1)
        g = g_ref[rows, :].astype(F32)
        o = o * lax.rsqrt(_seg_mean(o * o, bd) + RMS_EPS) * onorm * (g * _sigmoid(g))
        o_ref[rows, :] = o.astype(BF16)

    @pl.when(safe)
    def _():
        for it in range(n_items):
            rows = slice(it * length, (it + 1) * length)
            base = base_ref[it * 2 * length:(it + 1) * 2 * length, :]

            def a_of(p, rows=rows, base=base):
                sl = slice(p * LANES, (p + 1) * LANES)
                return _hgrn_intra(pair(q_ref, rows, p), pair(k_ref, rows, p), base[0:length, sl],
                                   masks, m0, m1, None, sl)

            finish(rows, base, it // n_chunk, a_of)

    @pl.when(jnp.logical_not(safe))
    def _():
        def body(it, carry):
            rows = pl.ds(pl.multiple_of(it * length, length), length)
            arg_ref[...] = _dot(sell_ref[...], pieces(rows))
            base = base_ref[pl.ds(pl.multiple_of(it * 2 * length, 2 * length), 2 * length), :]
            seq = it // n_chunk if n_chunk > 1 and n_seq > 1 else (it if n_chunk == 1 else 0)

            def a_of(p):
                sl = slice(p * LANES, (p + 1) * LANES)
                return _hgrn_intra(pair(q_ref, rows, p), pair(k_ref, rows, p), base[0:length, sl],
                                   masks, m0, m1, arg_ref, sl)

            finish(rows, base, seq, a_of)
            return carry

        lax.fori_loop(0, n_items, body, 0)

    if carry_state:
        @pl.when(pl.program_id(1) == pl.num_programs(1) - 1)
        def _():
            _pairs_to_state(st_ref, s_ref, n_seq)
    else:
        _pairs_to_state(st_ref, s_ref, n_seq)


def _hgrn_scratch(n_seq, n_items, length, sell):
    return [pltpu.VMEM((n_seq, N_PAIR, LANES, LANES), F32),
            pltpu.VMEM((n_items * 2 * length, W_B), F32),
            pltpu.VMEM((sell.shape[0], W_B), F32)]


def _hgrn_prompt(qb, kb, ib, gb, logf, consts, bd, onorm_t, batch, seq):
    selb, sell, masks = consts
    steps = seq // ATT_ROWS
    n_chunk = ATT_ROWS // CHUNK
    row = pl.BlockSpec((ATT_ROWS, W_B), lambda b, j: (b * steps + j, 0))
    whole = lambda a: pl.BlockSpec(a.shape, lambda b, j: (0,) * a.ndim)
    return pl.pallas_call(
        functools.partial(_hgrn_kernel, 1, n_chunk, CHUNK, True),
        grid=(batch, steps),
        in_specs=[row, row, row, row, row, whole(selb), whole(sell), whole(masks), whole(bd),
                  whole(onorm_t)],
        out_specs=[row, pl.BlockSpec((1, N_HEADS_B, DK_B, DV_B), lambda b, j: (b, 0, 0, 0))],
        out_shape=[jax.ShapeDtypeStruct((batch * seq, V_B), BF16),
                   jax.ShapeDtypeStruct((batch, N_HEADS_B, DK_B, DV_B), F32)],
        scratch_shapes=_hgrn_scratch(1, n_chunk, CHUNK, sell),
        compiler_params=_cparams(("arbitrary", "arbitrary")),
        name="hgrn_prompt",
    )(qb, kb, ib, gb, logf, selb, sell, masks, bd, onorm_t)


def _hgrn_sample(qb, kb, ib, gb, logf, s0, consts, bd, onorm_t, row0, dec_batch, dec_seq):
    selb, sell, masks = consts
    rows = SAMPLE_SEQS * dec_seq
    row = pl.BlockSpec((rows, W_B), lambda b: (row0 // rows + b, 0))
    whole = lambda a: pl.BlockSpec(a.shape, lambda b: (0,) * a.ndim)
    state = pl.BlockSpec((SAMPLE_SEQS, N_HEADS_B, DK_B, DV_B), lambda b: (b, 0, 0, 0))
    return pl.pallas_call(
        functools.partial(_hgrn_kernel, SAMPLE_SEQS, 1, dec_seq, False),
        grid=(dec_batch // SAMPLE_SEQS,),
        in_specs=[row, row, row, row, row, state, whole(selb), whole(sell), whole(masks),
                  whole(bd), whole(onorm_t)],
        out_specs=[pl.BlockSpec((rows, V_B), lambda b: (b, 0)), state],
        out_shape=[jax.ShapeDtypeStruct((dec_batch * dec_seq, V_B), BF16),
                   jax.ShapeDtypeStruct((dec_batch, N_HEADS_B, DK_B, DV_B), F32)],
        scratch_shapes=_hgrn_scratch(SAMPLE_SEQS, SAMPLE_SEQS, dec_seq, sell),
        compiler_params=_cparams(("arbitrary",)),
        name="hgrn_sample",
    )(qb, kb, ib, gb, logf, s0, selb, sell, masks, bd, onorm_t)


def _outproj_kernel(prompt_tiles, x_ref, oap_ref, obp_ref, oas_ref, obs_ref, g1_ref, sh_ref, sc_ref,
                    ln_ref, w_ref, wr_ref, rb_ref, x1_out, h2_out, cls_out, ga_out, gb_out):
    def mix_of(oa_ref, ob_ref):
        return jnp.concatenate([oa_ref[hh] for hh in range(GQA_GROUP)] + [ob_ref[...]], axis=1)

    is_prompt = pl.program_id(0) < prompt_tiles
    mix = jnp.where(is_prompt, mix_of(oap_ref, obp_ref), mix_of(oas_ref, obs_ref))
    att = _dot(mix, w_ref[...])
    x1 = _gated_add(x_ref[...], g1_ref, att)
    x1_out[...] = x1
    h = _modulated_norm(x1, ln_ref, sc_ref, sh_ref)

    for s in range(SUBLANES):
        h2_out[pl.ds(s, ROWS, stride=SUBLANES), :] = h[:, s * LANES:(s + 1) * LANES]

    h_hi, h_lo = _split2(h)
    w_hi, w_lo = _split2(wr_ref[...])
    hi_terms = _dot_nt(jnp.concatenate([w_hi, w_lo], axis=0), h_hi)
    logits = hi_terms[0:N_EXPERTS] + hi_terms[N_EXPERTS:] + _dot_nt(w_hi, h_lo)
    aff = _sigmoid(logits)
    sel = aff + rb_ref[...]
    e_idx = lax.broadcasted_iota(I32, (N_EXPERTS, ROWS), 0)
    l_idx = lax.broadcasted_iota(I32, (EXPERTS_PER_GROUP, ROWS), 0)

    def first_argmax(vals, idx, big):
        top = jnp.max(vals, axis=0, keepdims=True)
        return top, jnp.min(jnp.where(vals == top, idx, big), axis=0, keepdims=True)

    g_scores = []
    for g in range(N_GROUPS):
        sg = sel[g * EXPERTS_PER_GROUP:(g + 1) * EXPERTS_PER_GROUP]
        m1, i1 = first_argmax(sg, l_idx, EXPERTS_PER_GROUP)
        m2 = jnp.max(jnp.where(l_idx == i1, -jnp.inf, sg), axis=0, keepdims=True)
        g_scores.append(m1 + m2)
    g_score = jnp.concatenate(g_scores, axis=0)
    g_idx = lax.broadcasted_iota(I32, (N_GROUPS, ROWS), 0)
    _, g_sel = first_argmax(g_score, g_idx, N_GROUPS)
    masked = jnp.where((e_idx // EXPERTS_PER_GROUP) == g_sel, sel, -jnp.inf)
    _, e1 = first_argmax(masked, e_idx, N_EXPERTS)
    _, e2 = first_argmax(jnp.where(e_idx == e1, -jnp.inf, masked), e_idx, N_EXPERTS)
    w1 = jnp.sum(jnp.where(e_idx == e1, aff, 0.0), axis=0, keepdims=True)
    w2 = jnp.sum(jnp.where(e_idx == e2, aff, 0.0), axis=0, keepdims=True)
    den = w1 + w2
    w1 = w1 / den
    w2 = w2 / den
    lo_first = e1 < e2
    gate_lo = jnp.where(lo_first, w1, w2)
    gate_hi = jnp.where(lo_first, w2, w1)
    a = jnp.minimum(e1, e2) - g_sel * EXPERTS_PER_GROUP
    b = jnp.maximum(e1, e2) - g_sel * EXPERTS_PER_GROUP
    pair = jnp.where(a == 0, b - 1, jnp.where(a == 1, 6 - b, 5))
    cls_out[...] = g_sel * PAIRS_PER_GROUP + pair
    ga_out[...] = jnp.where(a == 2, gate_hi, gate_lo)
    gb_out[...] = jnp.where(a == 2, gate_lo, gate_hi)


def _outproj(layer, x4, oap4, obp, oas4, obs, mod, ln2, w_out_b, w_router_t, router_bias_c,
             tiles_per_batch):
    n_tiles = x4.shape[0] // ROWS
    x_spec = pl.BlockSpec((ROWS, D_MODEL), lambda i: (i, 0))
    prompt_tiles = oap4.shape[0]
    last = prompt_tiles - 1
    lane_row = pl.BlockSpec((None, 1, ROWS), lambda i: (i, 0, 0))
    return pl.pallas_call(
        functools.partial(_outproj_kernel, prompt_tiles),
        grid=(n_tiles,),
        in_specs=[
            x_spec,
            pl.BlockSpec((None, GQA_GROUP, ROWS, LANES), lambda i: (jnp.minimum(i, last), 0, 0, 0)),
            pl.BlockSpec((ROWS, V_B), lambda i: (jnp.minimum(i, last), 0)),
            pl.BlockSpec((None, GQA_GROUP, ROWS, LANES), lambda i: (0, 0, 0, 0)),
            pl.BlockSpec((ROWS, V_B), lambda i: (0, 0)),
            _mod_spec(layer, 2, tiles_per_batch), _mod_spec(layer, 3, tiles_per_batch),
            _mod_spec(layer, 4, tiles_per_batch),
            pl.BlockSpec((None, 1, D_MODEL), lambda i: (layer, 0, 0)),
            pl.BlockSpec((None, Q_A + V_B, D_MODEL), lambda i: (layer, 0, 0)),
            pl.BlockSpec((N_EXPERTS, D_MODEL), lambda i: (0, 0)),
            pl.BlockSpec((N_EXPERTS, 1), lambda i: (0, 0)),
        ],
        out_specs=[
            x_spec,
            pl.BlockSpec((ROWS * SUBLANES, LANES), lambda i: (i, 0)),
            lane_row, lane_row, lane_row,
        ],
        out_shape=[
            jax.ShapeDtypeStruct(x4.shape, F32),
            jax.ShapeDtypeStruct((n_tiles * ROWS * SUBLANES, LANES), F32),
            jax.ShapeDtypeStruct((n_tiles, 1, ROWS), I32),
            jax.ShapeDtypeStruct((n_tiles, 1, ROWS), F32),
            jax.ShapeDtypeStruct((n_tiles, 1, ROWS), F32),
        ],
        compiler_params=_cparams(("arbitrary",)),
        name=f"outproj{layer}",
    )(x4, oap4, obp, oas4, obs, mod, mod, mod, ln2, w_out_b, w_router_t, router_bias_c)


def _rank_kernel(cls_ref, upper_ref, lower_ref, dest_ref, count_ref, rank_ref):
    n_tiles = cls_ref.shape[0]
    c_idx = lax.broadcasted_iota(I32, (GRP, ROWS), 0)
    upper = upper_ref[...]

    def rank_body(i, carry):
        onehot = c_idx == cls_ref[i]
        within = _dot(onehot.astype(BF16), upper)
        rank_ref[i] = jnp.sum(jnp.where(onehot, within + carry, 0.0), axis=0, keepdims=True)
        return carry + jnp.sum(onehot.astype(F32), axis=1, keepdims=True)

    count = lax.fori_loop(0, n_tiles, rank_body, jnp.zeros((GRP, 1), F32))
    count_ref[...] = jnp.broadcast_to(count, (GRP, LANES)).astype(I32)
    tiles = jnp.floor((count + (TM - 1)) * (1.0 / TM))
    first_tile = _dot(lower_ref[...], jnp.broadcast_to(tiles, (GRP, LANES)).astype(BF16))
    start = first_tile[:, 0:1] * TM

    def dest_body(i, carry):
        onehot = c_idx == cls_ref[i]
        dest = rank_ref[i] + jnp.sum(jnp.where(onehot, start, 0.0), axis=0, keepdims=True)
        dest_ref[i] = dest.astype(I32)
        return carry

    lax.fori_loop(0, n_tiles, dest_body, 0)


def _plan_kernel(n_tok, n_t, dest_ref, count_ref, src_ref, ea_ref, eb_ref, nv_ref, nt_ref):
    def fill(first_group, end_group):
        def body(p8, carry):
            for u in range(8):
                src_ref[p8 * 8 + u] = 0
            return carry
        lax.fori_loop(first_group, end_group, body, 0)

    tile = 0
    for c in range(N_CLASS):
        g, pi = divmod(c, PAIRS_PER_GROUP)
        a, b = PAIR_TABLE[pi]
        cnt = count_ref[c, 0]
        n_c = lax.shift_right_logical(cnt + (TM - 1), TM.bit_length() - 1)
        first = tile

        def mark(t, carry, g=g, a=a, b=b, cnt=cnt, first=first):
            ea_ref[t] = g * EXPERTS_PER_GROUP + a
            eb_ref[t] = g * EXPERTS_PER_GROUP + b
            nv_ref[t] = jnp.minimum(cnt - (t - first) * TM, TM)
            return carry

        lax.fori_loop(tile, tile + n_c, mark, 0)

        fill(lax.shift_right_logical(first * TM + cnt, 3), (first + n_c) * (TM // 8))
        tile = tile + n_c
    nt_ref[0] = tile

    def unused(t, carry):
        ea_ref[t] = 0
        eb_ref[t] = 0
        nv_ref[t] = 0
        return carry

    lax.fori_loop(tile, n_t, unused, 0)

    fill(tile * (TM // 8), n_t * (TM // 8))

    def place(t, carry):
        src_ref[dest_ref[t]] = t
        return carry

    lax.fori_loop(0, n_tok, place, 0, unroll=8)


def _route_plan(cls3, upper, lower):
    n_tiles = cls3.shape[0]
    n_tok = n_tiles * ROWS
    n_t = (n_tok + N_CLASS * (TM - 1)) // TM + 2
    dest, count = pl.pallas_call(
        _rank_kernel,
        out_shape=[jax.ShapeDtypeStruct((n_tiles, 1, ROWS), I32),
                   jax.ShapeDtypeStruct((GRP, LANES), I32)],
        scratch_shapes=[pltpu.VMEM((n_tiles, 1, ROWS), F32)],
        name="moe_rank",
    )(cls3, upper, lower)
    smem = pl.BlockSpec(memory_space=pltpu.SMEM)
    return pl.pallas_call(
        functools.partial(_plan_kernel, n_tok, n_t),
        in_specs=[smem, smem],
        out_specs=[smem, smem, smem, smem, smem],
        out_shape=[jax.ShapeDtypeStruct((n_t * TM,), I32),
                   jax.ShapeDtypeStruct((n_t,), I32),
                   jax.ShapeDtypeStruct((n_t,), I32),
                   jax.ShapeDtypeStruct((n_t,), I32),
                   jax.ShapeDtypeStruct((1,), I32)],
        name="moe_plan",
    )(dest.reshape(n_tok), count)


def _tile_rows(ref, slot, r):
    return ref.at[pl.ds(pl.multiple_of((slot * TM + r) * SUBLANES, SUBLANES), SUBLANES)]


def _moe_kernel(n_tok, src_ref, ea_ref, eb_ref, nv_ref, nt_ref, ga_ref, gb_ref, h2_hbm, wga_ref,
                wua_ref, wda_ref, wgb_ref, wub_ref, wdb_ref, y_hbm, xbuf, ybuf, gbuf, gsem, ssem):
    i = pl.program_id(0)
    n_used = nt_ref[0]
    lane = lax.broadcasted_iota(I32, (1, LANES), 1)

    def token_tile(hbm, tok):
        return hbm.at[pl.ds(pl.multiple_of(tok * SUBLANES, SUBLANES), SUBLANES)]

    def gather_row(tile, slot, r, priority):
        tok = src_ref[tile * TM + r]
        pltpu.make_async_copy(token_tile(h2_hbm, tok), _tile_rows(xbuf, slot, r),
                              gsem.at[slot]).start(priority=priority)
        gbuf[pl.ds(slot * TM + r, 1), :] = jnp.where(lane == 0, ga_ref[tok], gb_ref[tok])

    def scatter_row(tile, slot, r, n_valid, priority):
        tok = jnp.where(r < n_valid, src_ref[tile * TM + r], n_tok + slot * TM + r)
        pltpu.make_async_copy(_tile_rows(ybuf, slot, r), token_tile(y_hbm, tok),
                              ssem.at[slot]).start(priority=priority)

    def row_loop(start_row):
        def body(r8, c):
            for u in range(DMA_UNROLL):
                start_row(r8 * DMA_UNROLL + u, u % 2)
            return c
        lax.fori_loop(0, TM // DMA_UNROLL, body, 0)

    def wait_gather(slot):
        pltpu.make_async_copy(h2_hbm.at[pl.ds(0, TM * SUBLANES)],
                              xbuf.at[pl.ds(slot * TM * SUBLANES, TM * SUBLANES)],
                              gsem.at[slot]).wait()

    def wait_scatter(slot):
        pltpu.make_async_copy(ybuf.at[pl.ds(slot * TM * SUBLANES, TM * SUBLANES)],
                              y_hbm.at[pl.ds(0, TM * SUBLANES)], ssem.at[slot]).wait()

    @pl.when(i == 0)
    def _():
        ybuf[...] = jnp.zeros_like(ybuf)
        spare = pltpu.make_async_copy(
            ybuf, y_hbm.at[pl.ds(n_tok * SUBLANES, 2 * TM * SUBLANES)], ssem.at[0])
        spare.start()
        spare.wait()

    @pl.when(i == 0)
    def _():
        for t in range(GATHER_AHEAD):
            @pl.when(t < n_used)
            def _():
                row_loop(lambda r, pr: gather_row(t, t, r, pr))

    @pl.when(i + GATHER_AHEAD < n_used)
    def _():
        nxt = i + GATHER_AHEAD
        row_loop(lambda r, pr: gather_row(nxt, nxt % GATHER_SLOTS, r, pr))

    @pl.when(jnp.logical_and(i >= 2, i - 2 < n_used))
    def _():
        wait_scatter(i % 2)

    @pl.when(i < n_used)
    def _():
        slot = i % 2
        xslot = i % GATHER_SLOTS
        wait_gather(xslot)
        base = xslot * TM * SUBLANES
        xb = jnp.concatenate([_token_rows(xbuf, s, TM, base) for s in range(SUBLANES)],
                             axis=1).astype(BF16)
        gates = gbuf[pl.ds(xslot * TM, TM), :]
        y = jnp.zeros((TM, D_MODEL), F32)
        for e, (wg, wu, wd) in enumerate(((wga_ref, wua_ref, wda_ref), (wgb_ref, wub_ref, wdb_ref))):
            gt = _dot(xb, wg[...].astype(BF16))
            up = _dot(xb, wu[...].astype(BF16))
            act = gt * _sigmoid(gt) * up * gates[:, e:e + 1]
            y = y + _dot(act.astype(BF16), wd[...].astype(BF16))
        ybase = slot * TM * SUBLANES
        for s in range(SUBLANES):
            ybuf[pl.ds(ybase + s, TM, stride=SUBLANES), :] = y[:, s * LANES:(s + 1) * LANES]
        n_valid = nv_ref[i]
        row_loop(lambda r, pr: scatter_row(i, slot, r, n_valid, pr))


def _moe(layer, h2, gate_a, gate_b, plan, w_gate, w_up, w_down):
    src, ea, eb, nv, nt = plan
    n_t = ea.shape[0]
    n_tok = h2.shape[0] // SUBLANES

    def w_spec(shape, which):
        def index(i, src_ref, ea_ref, eb_ref, nv_ref, nt_ref, ga_ref, gb_ref):
            t = jnp.minimum(i, jnp.maximum(nt_ref[0] - 1, 0))
            e = (ea_ref, eb_ref)[which][t]
            return (layer, e, 0, 0)
        return pl.BlockSpec((None, None) + shape, index)

    gu = (D_MODEL, D_EXPERT)
    dn = (D_EXPERT, D_MODEL)
    grid_spec = pltpu.PrefetchScalarGridSpec(
        num_scalar_prefetch=7,
        grid=(n_t,),
        in_specs=[pl.BlockSpec(memory_space=pl.ANY),
                  w_spec(gu, 0), w_spec(gu, 0), w_spec(dn, 0),
                  w_spec(gu, 1), w_spec(gu, 1), w_spec(dn, 1)],
        out_specs=pl.BlockSpec(memory_space=pl.ANY),
        scratch_shapes=[pltpu.VMEM((GATHER_SLOTS * TM * SUBLANES, LANES), F32),
                        pltpu.VMEM((2 * TM * SUBLANES, LANES), F32),
                        pltpu.VMEM((GATHER_SLOTS * TM, LANES), F32),
                        pltpu.SemaphoreType.DMA((GATHER_SLOTS,)), pltpu.SemaphoreType.DMA((2,))],
    )
    return pl.pallas_call(
        functools.partial(_moe_kernel, n_tok),
        grid_spec=grid_spec,
        out_shape=jax.ShapeDtypeStruct(((n_tok + 2 * TM) * SUBLANES, LANES), F32),
        compiler_params=_cparams(("arbitrary",)),
        name=f"moe{layer}",
    )(src, ea, eb, nv, nt, gate_a.reshape(n_tok), gate_b.reshape(n_tok), h2,
      w_gate, w_up, w_down, w_gate, w_up, w_down)


def kernel(x_prompt, x_sample, cache_k, cache_v, state_hgrn, c_prompt, c_sample, w_ada, b_ada,
           ln1, ln2, w_in, q_norm, k_norm, sinks, lb_param, o_norm, w_out, w_router, router_bias,
           w_e_gate, w_e_up, w_e_down):
    batch, seq, _ = x_prompt.shape
    dec_batch, dec_seq, _ = x_sample.shape
    depth = w_ada.shape[0]
    assert dec_batch == GRP and dec_seq == SUB and seq % ROWS == 0
    tiles_per_batch = seq // ROWS
    n_prompt = batch * seq
    n_tok = n_prompt + dec_batch * dec_seq
    n_tiles = n_tok // ROWS
    prompt_tiles = n_prompt // ROWS
    n_cache = cache_k.shape[2]

    seg = np.arange(Q_A) // HEAD_DIM
    bd = jnp.asarray((seg[:, None] == seg[None, :]).astype(np.float32) / HEAD_DIM, BF16)

    def hgrn_consts(length):
        selb, sell, masks = _hgrn_consts(length)
        return jnp.asarray(selb, BF16), jnp.asarray(sell, BF16), jnp.asarray(masks)

    consts_p = hgrn_consts(CHUNK)
    consts_s = hgrn_consts(dec_seq)
    upper = jnp.asarray(np.triu(np.ones((ROWS, ROWS), np.float32), 1), BF16)
    lower = jnp.asarray(np.tril(np.ones((GRP, GRP), np.float32), -1), BF16)

    wq = w_in[:, :, :Q_A].reshape(depth, D_MODEL, N_KV_A, GQA_GROUP, HEAD_DIM)
    wq = jnp.swapaxes(wq, 2, 3).reshape(depth, D_MODEL, Q_A)
    w_in_b = jnp.concatenate([wq, w_in[:, :, Q_A:]], axis=-1).astype(BF16)
    wo = w_out[:, :Q_A].reshape(depth, N_KV_A, GQA_GROUP, HEAD_DIM, D_MODEL)
    wo = jnp.swapaxes(wo, 1, 2).reshape(depth, Q_A, D_MODEL)
    w_out_b = jnp.concatenate([wo, w_out[:, Q_A:]], axis=1).astype(BF16)
    q_norm_t = jnp.tile(q_norm, (1, N_HEADS_A)).reshape(depth, 1, Q_A)
    k_norm_t = jnp.tile(k_norm, (1, N_KV_A)).reshape(depth, 1, KV_A)
    o_norm_t = jnp.tile(o_norm, (1, N_HEADS_B)).reshape(depth, 1, V_B)
    ln1_r = ln1.reshape(depth, 1, D_MODEL)
    ln2_r = ln2.reshape(depth, 1, D_MODEL)
    w_router_t = w_router.T
    router_bias_c = router_bias.reshape(N_EXPERTS, 1)

    c_exp = jnp.concatenate([jnp.repeat(c_prompt, GRP, axis=0), c_sample], axis=0)
    mod = _adaln(c_exp, w_ada, b_ada).reshape(depth, batch + 1, GRP, N_MOD * D_MODEL)

    cache_k2 = cache_k.reshape(depth, dec_batch, n_cache, KV_A)
    cache_v2 = cache_v.reshape(depth, dec_batch, n_cache, KV_A)

    src = (x_prompt, x_sample)
    kp, vp, sp, kss, vss, sss = [], [], [], [], [], []
    for l in range(depth):
        x, q4, k, v, qb, logf, kb, ib, gb, kwin, vwin = _inproj(
            l, src, n_tiles, prompt_tiles, mod, ln1_r, w_in_b, q_norm_t, k_norm_t, lb_param, bd,
            tiles_per_batch)

        oap4 = _attn_prompt(q4, k, v, sinks[l], batch, seq)
        oas4 = _attn_sample(q4, k, v, cache_k2[l], cache_v2[l], sinks[l], prompt_tiles,
                            n_prompt, dec_batch, dec_seq)
        obp, s_p = _hgrn_prompt(qb, kb, ib, gb, logf, consts_p, bd, o_norm_t[l], batch, seq)
        obs, s_s = _hgrn_sample(qb, kb, ib, gb, logf, state_hgrn[l], consts_s,
                                bd, o_norm_t[l], n_prompt, dec_batch, dec_seq)

        x1, h2, cls3, gate_a, gate_b = _outproj(l, x, oap4, obp, oas4, obs, mod, ln2_r, w_out_b,
                                                w_router_t, router_bias_c, tiles_per_batch)
        y = _moe(l, h2, gate_a, gate_b, _route_plan(cls3, upper, lower), w_e_gate, w_e_up,
                 w_e_down)
        src = (x1, y)

        kp.append(kwin.reshape(batch, WINDOW, N_KV_A, HEAD_DIM))
        vp.append(vwin.reshape(batch, WINDOW, N_KV_A, HEAD_DIM))
        sp.append(s_p)
        kss.append(k[n_prompt:].reshape(dec_batch, dec_seq, N_KV_A, HEAD_DIM))
        vss.append(v[n_prompt:].reshape(dec_batch, dec_seq, N_KV_A, HEAD_DIM))
        sss.append(s_s)

    yp, ys = _final(depth - 1, src[0], src[1], mod, batch, seq, tiles_per_batch)
    return (yp, ys, jnp.stack(kp), jnp.stack(vp), jnp.stack(sp),
            jnp.stack(kss), jnp.stack(vss), jnp.stack(sss))
```

```python
import functools

import numpy as np
import jax
import jax.numpy as jnp
from jax import lax
from jax.experimental import pallas as pl
from jax.experimental.pallas import tpu as pltpu

F32 = jnp.float32
BF16 = jnp.bfloat16
I32 = jnp.int32

D_MODEL = 1024
HEAD_DIM = 64
N_HEADS_A = 8
N_KV_A = 2
GQA_GROUP = N_HEADS_A // N_KV_A
N_HEADS_B = 8
DK_B = 64
DV_B = 64
Q_A = N_HEADS_A * HEAD_DIM
KV_A = N_KV_A * HEAD_DIM
W_B = N_HEADS_B * DK_B
V_B = N_HEADS_B * DV_B
CHUNK = 64
WINDOW = 128
N_EXPERTS = 16
N_GROUPS = 4
EXPERTS_PER_GROUP = 4
D_EXPERT = D_MODEL // 2
RMS_EPS = 1e-6
NEG_INF = -1e30
N_MOD = 6

LANES = 128
SUBLANES = 8
ROWS = 512
GRP = 32
SUB = ROWS // GRP
ATT_ROWS = 256
N_PAIR = N_HEADS_B // 2
SAMPLE_SEQS = 4
HGRN_SPAN_LIMIT = 80.0
TM = 256
PAIRS_PER_GROUP = 6
PAIR_TABLE = ((0, 1), (0, 2), (0, 3), (1, 3), (1, 2), (3, 2))
N_CLASS = N_GROUPS * PAIRS_PER_GROUP
DMA_UNROLL = 8
GATHER_AHEAD = 3
GATHER_SLOTS = GATHER_AHEAD + 1
VMEM_LIMIT = 56 * 1024 * 1024


def _cparams(sem):
    return pltpu.CompilerParams(dimension_semantics=sem, vmem_limit_bytes=VMEM_LIMIT)


def _dot(a, b):
    return jnp.dot(a, b, preferred_element_type=F32)


def _dot_nt(a, b):
    return lax.dot_general(a, b, (((1,), (1,)), ((), ())), preferred_element_type=F32)


def _dot_tn(a, b):
    return lax.dot_general(a, b, (((0,), (0,)), ((), ())), preferred_element_type=F32)


def _sigmoid(x):
    return 1.0 / (1.0 + jnp.exp(-x))


def _split2(x):
    hi = x.astype(BF16)
    lo = (x - hi.astype(F32)).astype(BF16)
    return hi, lo


def _split3(x):
    hi = x.astype(BF16)
    r = x - hi.astype(F32)
    mid = r.astype(BF16)
    lo = (r - mid.astype(F32)).astype(BF16)
    return hi, mid, lo


def _seg_mean(sq, bd):
    return _dot(sq.astype(BF16), bd)


def _token_rows(ref, s, n, base=0):
    return ref[pl.ds(base + s, n, stride=SUBLANES), :]


def _adaln_kernel(n_prompt, c_ref, w_ref, b_ref, o_ref):
    c = c_ref[...]
    a = c * _sigmoid(c)
    a_hi, a_lo = _split2(a)
    w_hi, w_lo = _split2(w_ref[...])
    acc = _dot(a_hi, w_hi) + _dot(a_lo, w_hi) + _dot(a_hi, w_lo) + b_ref[...]
    for b in range(n_prompt):
        o_ref[b * GRP:(b + 1) * GRP, :] = jnp.broadcast_to(acc[b:b + 1], (GRP, acc.shape[1]))
    o_ref[n_prompt * GRP:, :] = acc[n_prompt:]


def _adaln(c_prompt, c_sample, w_ada, b_ada):
    depth = w_ada.shape[0]
    n_prompt = c_prompt.shape[0]
    c_all = jnp.concatenate([c_prompt, c_sample], axis=0)
    rows = n_prompt * GRP + c_sample.shape[0]
    tn = D_MODEL
    return pl.pallas_call(
        functools.partial(_adaln_kernel, n_prompt),
        grid=(depth, N_MOD * D_MODEL // tn),
        in_specs=[
            pl.BlockSpec(c_all.shape, lambda l, n: (0, 0)),
            pl.BlockSpec((None, D_MODEL, tn), lambda l, n: (l, 0, n)),
            pl.BlockSpec((None, 1, tn), lambda l, n: (l, 0, n)),
        ],
        out_specs=pl.BlockSpec((None, rows, tn), lambda l, n: (l, 0, n)),
        out_shape=jax.ShapeDtypeStruct((depth, rows, N_MOD * D_MODEL), F32),
        compiler_params=_cparams(("arbitrary", "arbitrary")),
        name="adaln",
    )(c_all, w_ada, b_ada.reshape(depth, 1, N_MOD * D_MODEL))


def _mod_spec(layer, j, tiles_per_batch):
    return pl.BlockSpec((None, None, GRP, D_MODEL),
                        lambda i, *_: (layer, i // tiles_per_batch, 0, j))


def _by_group(fn, xs, mod_refs):
    outs = []
    for g in range(GRP):
        rows = slice(g * SUB, (g + 1) * SUB)
        outs.append(fn(*[x[rows] for x in xs], *[m[g:g + 1, :] for m in mod_refs]))
    return jnp.concatenate(outs, axis=0)


def _gated_add(x, gate_ref, y):
    return _by_group(lambda xg, yg, gg: xg + gg * yg, (x, y), (gate_ref,))


def _modulated_norm(x, ln_ref, sc_ref, sh_ref):
    ms = jnp.mean(x * x, axis=-1, keepdims=True)
    xn = x * lax.rsqrt(ms + RMS_EPS)
    scale = ln_ref[...] * (1.0 + sc_ref[...])
    return _by_group(lambda xg, sc, sh: xg * sc + sh, (xn,), (scale, sh_ref))


def _prompt_tile_spec(prompt_tiles, tiles_per_batch):
    def index(i):
        t = jnp.minimum(i, prompt_tiles - 1)
        return (t // tiles_per_batch, t % tiles_per_batch, 0)
    return pl.BlockSpec((None, ROWS, D_MODEL), index)


def _moe_output_tile(y_ref):
    return jnp.concatenate([_token_rows(y_ref, s, ROWS) for s in range(SUBLANES)], axis=1)


def _inproj_kernel(layer, prompt_tiles, n_src, *refs):
    src = refs[:n_src]
    sh_ref, sc_ref, ln_ref, w_ref, qn_ref, kn_ref, lbp_ref, bd_ref = refs[n_src:n_src + 8]
    (x_out, q_out, k_out, v_out, qb_out, logf_out, kb_out, ib_out, gb_out,
     kwin_out, vwin_out) = refs[n_src + 8:]
    if layer == 0:
        xp_ref, xs_ref = src
        x = jnp.where(pl.program_id(0) < prompt_tiles, xp_ref[...],
                      xs_ref[...].reshape(ROWS, D_MODEL))
    else:
        x1_ref, y_ref, g2_ref = src
        x = _gated_add(x1_ref[...], g2_ref, _moe_output_tile(y_ref))
    x_out[...] = x
    hb = _modulated_norm(x, ln_ref, sc_ref, sh_ref).astype(BF16)
    bd = bd_ref[...]

    c0 = Q_A + 2 * KV_A
    c1 = c0 + 2 * W_B
    qa = _dot(hb, w_ref[:, 0:Q_A])
    kv = _dot(hb, w_ref[:, Q_A:c0])
    qb = _dot(hb, w_ref[:, c0:c0 + W_B])
    z = _dot(hb, w_ref[:, c0 + W_B:c1])
    ib = _dot(hb, w_ref[:, c1:c1 + V_B])
    gb = _dot(hb, w_ref[:, c1 + V_B:c1 + 2 * V_B])

    qa = qa * lax.rsqrt(_seg_mean(qa * qa, bd) + RMS_EPS) * qn_ref[...]
    qa = (qa * (HEAD_DIM ** -0.5)).astype(BF16)
    for hh in range(GQA_GROUP):
        q_out[hh] = qa[:, hh * LANES:(hh + 1) * LANES]

    ka = kv[:, :KV_A]
    ka = ka * lax.rsqrt(_seg_mean(ka * ka, bd[:KV_A, :KV_A]) + RMS_EPS) * kn_ref[...]
    k_out[...] = ka
    v_out[...] = kv[:, KV_A:]

    @pl.when(pl.program_id(0) < prompt_tiles)
    def _():
        kwin_out[...] = ka[ROWS - WINDOW:]
        vwin_out[...] = kv[ROWS - WINDOW:, KV_A:]

    qb_out[...] = (qb * (DK_B ** -0.5)).astype(BF16)
    ib_out[...] = ib.astype(BF16)
    gb_out[...] = gb.astype(BF16)

    p = lbp_ref[...]
    e = jnp.exp(p - jnp.max(p, axis=0, keepdims=True))
    sm = e / jnp.sum(e, axis=0, keepdims=True)
    cum0 = sm[0:1]
    cum = cum0
    for r in range(1, layer + 1):
        cum = cum + sm[r:r + 1]
    lb = cum - cum0

    pos = z >= 0.0
    t = jnp.exp(-jnp.abs(z))
    num = jnp.where(pos, jnp.log(1.0 + lb * t), jnp.maximum(jnp.log(lb + t), z))
    logf_out[...] = num - jnp.log(1.0 + t)
    kb_out[...] = ((1.0 - lb) * jnp.where(pos, t, 1.0) / (1.0 + t)).astype(BF16)


def _inproj(layer, src, n_tiles, prompt_tiles, mod, ln1, w_in_b, q_norm_t, k_norm_t, lb_param, bd,
            tiles_per_batch):
    depth = lb_param.shape[0]
    n = n_tiles * ROWS
    in_width = w_in_b.shape[-1]
    x_spec = pl.BlockSpec((ROWS, D_MODEL), lambda i: (i, 0))
    if layer == 0:
        src_specs = [_prompt_tile_spec(prompt_tiles, tiles_per_batch),
                     pl.BlockSpec((GRP, SUB, D_MODEL), lambda i: (0, 0, 0))]
        src_args = list(src)
    else:
        src_specs = [x_spec, pl.BlockSpec((ROWS * SUBLANES, LANES), lambda i: (i, 0)),
                     _mod_spec(layer - 1, 5, tiles_per_batch)]
        src_args = list(src) + [mod]
    row_spec = lambda w: pl.BlockSpec((ROWS, w), lambda i: (i, 0))
    win_spec = pl.BlockSpec(
        (None, WINDOW, KV_A), lambda i: (jnp.minimum(i, prompt_tiles - 1) // tiles_per_batch, 0, 0))
    return pl.pallas_call(
        functools.partial(_inproj_kernel, layer, prompt_tiles, len(src_specs)),
        grid=(n_tiles,),
        in_specs=src_specs + [
            _mod_spec(layer, 0, tiles_per_batch), _mod_spec(layer, 1, tiles_per_batch),
            pl.BlockSpec((None, 1, D_MODEL), lambda i: (layer, 0, 0)),
            pl.BlockSpec((None, D_MODEL, in_width), lambda i: (layer, 0, 0)),
            pl.BlockSpec((None, 1, Q_A), lambda i: (layer, 0, 0)),
            pl.BlockSpec((None, 1, KV_A), lambda i: (layer, 0, 0)),
            pl.BlockSpec((depth, W_B), lambda i: (0, 0)),
            pl.BlockSpec((Q_A, Q_A), lambda i: (0, 0)),
        ],
        out_specs=[
            x_spec,
            pl.BlockSpec((None, GQA_GROUP, ROWS, LANES), lambda i: (i, 0, 0, 0)),
            row_spec(KV_A), row_spec(KV_A),
            row_spec(W_B), row_spec(W_B), row_spec(W_B), row_spec(V_B), row_spec(V_B),
            win_spec, win_spec,
        ],
        out_shape=[
            jax.ShapeDtypeStruct((n, D_MODEL), F32),
            jax.ShapeDtypeStruct((n_tiles, GQA_GROUP, ROWS, LANES), BF16),
            jax.ShapeDtypeStruct((n, KV_A), F32),
            jax.ShapeDtypeStruct((n, KV_A), F32),
            jax.ShapeDtypeStruct((n, W_B), BF16),
            jax.ShapeDtypeStruct((n, W_B), F32),
            jax.ShapeDtypeStruct((n, W_B), BF16),
            jax.ShapeDtypeStruct((n, V_B), BF16),
            jax.ShapeDtypeStruct((n, V_B), BF16),
            jax.ShapeDtypeStruct((prompt_tiles // tiles_per_batch, WINDOW, KV_A), F32),
            jax.ShapeDtypeStruct((prompt_tiles // tiles_per_batch, WINDOW, KV_A), F32),
        ],
        compiler_params=_cparams(("arbitrary",)),
        name=f"inproj{layer}",
    )(*src_args, mod, mod, ln1, w_in_b, q_norm_t, k_norm_t, lb_param, bd)


def _final_kernel(prompt_tiles, x1_ref, y_ref, g2_ref, yp_ref, ys_ref):
    i = pl.program_id(0)
    x = _gated_add(x1_ref[...], g2_ref, _moe_output_tile(y_ref))

    @pl.when(i < prompt_tiles)
    def _():
        yp_ref[...] = x

    @pl.when(i >= prompt_tiles)
    def _():
        ys_ref[...] = x.reshape(GRP, SUB, D_MODEL)


def _final(layer, x1, y, mod, batch, seq, tiles_per_batch):
    n_tiles = x1.shape[0] // ROWS
    prompt_tiles = batch * tiles_per_batch
    return pl.pallas_call(
        functools.partial(_final_kernel, prompt_tiles),
        grid=(n_tiles,),
        in_specs=[pl.BlockSpec((ROWS, D_MODEL), lambda i: (i, 0)),
                  pl.BlockSpec((ROWS * SUBLANES, LANES), lambda i: (i, 0)),
                  _mod_spec(layer, 5, tiles_per_batch)],
        out_specs=[_prompt_tile_spec(prompt_tiles, tiles_per_batch),
                   pl.BlockSpec((GRP, SUB, D_MODEL), lambda i: (0, 0, 0))],
        out_shape=[jax.ShapeDtypeStruct((batch, seq, D_MODEL), F32),
                   jax.ShapeDtypeStruct((GRP, SUB, D_MODEL), F32)],
        compiler_params=_cparams(("arbitrary",)),
        name="final",
    )(x1, y, mod)


def _attn_tables(sinks_ref, q_len, n_k, k_off):
    n_q = GQA_GROUP * q_len
    row = lax.broadcasted_iota(I32, (n_q, n_k), 0)
    col = lax.broadcasted_iota(I32, (n_q, n_k), 1)
    dist = jnp.abs((row % q_len) + k_off - col).astype(F32)
    hh_col = lax.broadcasted_iota(I32, (n_q, 1), 0) // q_len
    bias, sinks = [], []
    for g in range(N_KV_A):
        head = (g * GQA_GROUP + 1 + hh_col).astype(F32)
        bias.append(jnp.exp2(-8.0 * head / N_HEADS_A) * dist)
        sink = jnp.zeros((n_q, 1), F32)
        for j in range(GQA_GROUP):
            sink = jnp.where(hh_col == j, sinks_ref[g * GQA_GROUP + j], sink)
        sinks.append(sink)
    return bias, sinks


def _attend(q, k, v, bias, sinks, first_valid=None):
    n_q = q.shape[0]
    n_k = k.shape[0]
    lane = lax.broadcasted_iota(I32, (n_k, LANES), 1)
    out = jnp.zeros((n_q, LANES), F32)
    for g in range(N_KV_A):
        in_g = (lane // HEAD_DIM) == g
        kg = jnp.where(in_g, k, 0.0).astype(BF16)
        vg = jnp.where(in_g, v, 0.0).astype(BF16)
        s = _dot_nt(q, kg) - bias[g]
        if first_valid is not None:
            col = lax.broadcasted_iota(I32, (n_q, n_k), 1)
            s = jnp.where(col >= first_valid, s, NEG_INF)
        m = jnp.maximum(jnp.max(s, axis=-1, keepdims=True), sinks[g])
        p = jnp.exp(s - m)
        denom = jnp.sum(p, axis=-1, keepdims=True) + jnp.exp(sinks[g] - m)
        out = out + _dot(p.astype(BF16), vg) / denom
    return out


def _attn_prompt_kernel(sinks_ref, q_ref, kp_ref, kc_ref, vp_ref, vc_ref, o_ref):
    j = pl.program_id(1)
    kcat = jnp.concatenate([kp_ref[...], kc_ref[...]], axis=0)
    vcat = jnp.concatenate([vp_ref[...], vc_ref[...]], axis=0)
    bias, sinks = _attn_tables(sinks_ref, CHUNK, WINDOW + CHUNK, WINDOW)

    for c in range(ATT_ROWS // CHUNK):
        q = q_ref[:, c * CHUNK:(c + 1) * CHUNK, :].reshape(GQA_GROUP * CHUNK, LANES)
        k = kcat[c * CHUNK:c * CHUNK + WINDOW + CHUNK]
        v = vcat[c * CHUNK:c * CHUNK + WINDOW + CHUNK]
        first_valid = WINDOW - c * CHUNK - j * ATT_ROWS
        out = _attend(q, k, v, bias, sinks, first_valid)
        o_ref[:, c * CHUNK:(c + 1) * CHUNK, :] = out.reshape(GQA_GROUP, CHUNK, LANES).astype(BF16)


def _attn_prompt(q4, k, v, sinks, batch, seq):
    steps = seq // ATT_ROWS
    per_tile = ROWS // ATT_ROWS
    win_blocks = ATT_ROWS // WINDOW
    n_tiles = batch * seq // ROWS
    cur = pl.BlockSpec((ATT_ROWS, KV_A), lambda b, j: (b * steps + j, 0))
    prev = pl.BlockSpec(
        (WINDOW, KV_A), lambda b, j: (b * steps * win_blocks + jnp.maximum(j * win_blocks - 1, 0), 0))
    qo = pl.BlockSpec((None, GQA_GROUP, ATT_ROWS, LANES),
                      lambda b, j: ((b * steps + j) // per_tile, 0, j % per_tile, 0))
    return pl.pallas_call(
        _attn_prompt_kernel,
        grid=(batch, steps),
        in_specs=[pl.BlockSpec(memory_space=pltpu.SMEM), qo, prev, cur, prev, cur],
        out_specs=qo,
        out_shape=jax.ShapeDtypeStruct((n_tiles, GQA_GROUP, ROWS, LANES), BF16),
        compiler_params=_cparams(("arbitrary", "arbitrary")),
        name="attn_prompt",
    )(sinks, q4, k, k, v, v)


def _attn_sample_kernel(n_new, sinks_ref, q_ref, kc_ref, kn_ref, vc_ref, vn_ref, o_ref):
    n_cache = kc_ref.shape[1]
    bias, sinks = _attn_tables(sinks_ref, n_new, n_cache + n_new, n_cache)
    for r in range(SAMPLE_SEQS):
        rows = slice(r * n_new, (r + 1) * n_new)
        q = q_ref[:, rows, :].reshape(GQA_GROUP * n_new, LANES)
        k = jnp.concatenate([kc_ref[r], kn_ref[rows, :]], axis=0)
        v = jnp.concatenate([vc_ref[r], vn_ref[rows, :]], axis=0)
        out = _attend(q, k, v, bias, sinks)
        o_ref[:, rows, :] = out.reshape(GQA_GROUP, n_new, LANES).astype(BF16)


def _attn_sample(q4, k, v, cache_k, cache_v, sinks, tile, row0, dec_batch, dec_seq):
    n_cache = cache_k.shape[1]
    rows = SAMPLE_SEQS * dec_seq
    q_spec = pl.BlockSpec((None, GQA_GROUP, rows, LANES), lambda b: (tile, 0, b, 0))
    new = pl.BlockSpec((rows, KV_A), lambda b: (row0 // rows + b, 0))
    cache = pl.BlockSpec((SAMPLE_SEQS, n_cache, KV_A), lambda b: (b, 0, 0))
    return pl.pallas_call(
        functools.partial(_attn_sample_kernel, dec_seq),
        grid=(dec_batch // SAMPLE_SEQS,),
        in_specs=[pl.BlockSpec(memory_space=pltpu.SMEM), q_spec, cache, new, cache, new],
        out_specs=pl.BlockSpec((None, GQA_GROUP, rows, LANES), lambda b: (0, 0, b, 0)),
        out_shape=jax.ShapeDtypeStruct((1, GQA_GROUP, dec_batch * dec_seq, LANES), BF16),
        compiler_params=_cparams(("arbitrary",)),
        name="attn_sample",
    )(sinks, q4, cache_k, k, cache_v, v)


def _level_sizes(length):
    sizes = []
    m = length // 2
    while m >= 1:
        sizes.append(m)
        m //= 2
    return sizes


def _hgrn_consts(length):
    t = np.arange(length)
    tri = (t[None, :] <= t[:, None]).astype(np.float32)
    after = (t[None, :] > t[:, None]).astype(np.float32)
    levels = []
    masks = []
    s = np.tile(t, 2)[None, :]
    tt = t[:, None]
    for m in _level_sizes(length):
        blk = t // m
        same = blk[None, :] == blk[:, None]
        q_rows = (blk % 2 == 1)[:, None]
        sel = np.where(q_rows, same & (t[None, :] <= t[:, None]), same & (t[None, :] > t[:, None]))
        levels.append(sel.astype(np.float32))
        masks.append(((tt // m) % 2 == 1) & ((s // m) == (tt // m) - 1))
    masks.append(s == tt)
    masks.append(s <= tt)
    base = np.concatenate([tri, after], axis=0)
    lev = np.concatenate(levels, axis=0)
    return (np.concatenate([base] * 3, axis=1), np.concatenate([lev] * 3, axis=1),
            np.stack(masks).astype(np.int32))


def _stack2(x, m0, m1):
    xb = x.astype(BF16)
    return jnp.concatenate([xb * m0, xb * m1], axis=0)


def _hgrn_intra(qp, kp, b, masks, m0, m1, arg_ref, sl):
    length = qp.shape[0]
    n_lev = len(_level_sizes(length))
    if arg_ref is None:
        mid = length // 2 - 1
        rel = b - b[mid:mid + 1]
        a = _dot_nt((qp * jnp.exp(rel)).astype(BF16), _stack2(kp * jnp.exp(-rel), m0, m1))
        a = jnp.where(masks[n_lev + 1], a, 0.0)
    else:
        a = _dot_nt(qp.astype(BF16), _stack2(kp, m0, m1))
        a = jnp.where(masks[n_lev], a, 0.0)
        for lev in range(n_lev):
            e = jnp.exp(arg_ref[lev * length:(lev + 1) * length, sl])
            pr = _dot_nt((qp * e).astype(BF16), _stack2(kp * e, m0, m1))
            a = jnp.where(masks[lev], pr, a)
    return a.astype(BF16)


def _hgrn_finish(a, qp, kp, ip, base, sl, st_ref, seq, p, m0, m1, same_head):
    length = qp.shape[0]
    e_b = jnp.exp(base[0:length, sl])
    e_k = jnp.exp(base[length:2 * length, sl])
    st = st_ref[seq, p]
    o = _dot(jnp.concatenate([a, (qp * e_b).astype(BF16)], axis=1),
             jnp.concatenate([_stack2(ip, m0, m1), st.T.astype(BF16)], axis=0))
    upd = _dot_tn(ip.astype(BF16), (kp * e_k).astype(BF16))
    st_ref[seq, p] = st * e_b[length - 1:length, :] + jnp.where(same_head, upd, 0.0)
    return o


def _state_to_pairs(s0_ref, st_ref, n_seq):
    zero = jnp.zeros((DK_B, DV_B), F32)
    for r in range(n_seq):
        for p in range(N_PAIR):
            top = jnp.concatenate([s0_ref[r, 2 * p], zero], axis=1)
            bot = jnp.concatenate([zero, s0_ref[r, 2 * p + 1]], axis=1)
            st_ref[r, p] = jnp.concatenate([top, bot], axis=0).T


def _pairs_to_state(st_ref, s_ref, n_seq):
    for r in range(n_seq):
        for p in range(N_PAIR):
            d = st_ref[r, p].T
            s_ref[r, 2 * p] = d[:DK_B, :DV_B]
            s_ref[r, 2 * p + 1] = d[DK_B:, DV_B:]


def _hgrn_kernel(n_seq, n_chunk, length, carry_state, *refs):
    if carry_state:
        (q_ref, k_ref, i_ref, g_ref, f_ref, selb_ref, sell_ref, mask_ref, bd_ref, on_ref,
         o_ref, s_ref, st_ref, base_ref, arg_ref) = refs

        @pl.when(pl.program_id(1) == 0)
        def _():
            st_ref[...] = jnp.zeros_like(st_ref)
    else:
        (q_ref, k_ref, i_ref, g_ref, f_ref, s0_ref, selb_ref, sell_ref, mask_ref, bd_ref, on_ref,
         o_ref, s_ref, st_ref, base_ref, arg_ref) = refs
        _state_to_pairs(s0_ref, st_ref, n_seq)

    bd = bd_ref[...]
    onorm = on_ref[...]
    n_items = n_seq * n_chunk
    mid = length // 2 - 1
    lane = lax.broadcasted_iota(I32, (length, LANES), 1)
    m0 = (lane < DK_B).astype(BF16)
    m1 = (lane >= DK_B).astype(BF16)
    rowp = lax.broadcasted_iota(I32, (LANES, LANES), 0) // DV_B
    colp = lax.broadcasted_iota(I32, (LANES, LANES), 1) // DK_B
    same_head = rowp == colp
    masks = [mask_ref[m] != 0 for m in range(mask_ref.shape[0])]

    def pieces(rows):
        return jnp.concatenate(_split3(f_ref[rows, :]), axis=0)

    def pair(ref, rows, p):
        return ref[rows, p * LANES:(p + 1) * LANES].astype(F32)

    span = jnp.float32(0.0)
    for it in range(n_items):
        base = _dot(selb_ref[...], pieces(slice(it * length, (it + 1) * length)))
        base_ref[it * 2 * length:(it + 1) * 2 * length, :] = base
        b_mid = base[mid:mid + 1]
        span = jnp.maximum(span, jnp.max(base[0:1] - b_mid))
        span = jnp.maximum(span, jnp.max(b_mid - base[length - 1:length]))
    safe = span <= HGRN_SPAN_LIMIT

    def finish(rows, base, seq, a_of):
        outs = []
        for p in range(N_PAIR):
            sl = slice(p * LANES, (p + 1) * LANES)
            outs.append(_hgrn_finish(a_of(p), pair(q_ref, rows, p), pair(k_ref, rows, p),
                                     pair(i_ref, rows, p), base, sl, st_ref, seq, p, m0, m1,
                                     same_head))
        o = jnp.concatenate(outs, axis=1)
        g = g_ref[rows, :].astype(F32)
        o = o * lax.rsqrt(_seg_mean(o * o, bd) + RMS_EPS) * onorm * (g * _sigmoid(g))
        o_ref[rows, :] = o.astype(BF16)

    @pl.when(safe)
    def _():
        for it in range(n_items):
            rows = slice(it * length, (it + 1) * length)
            base = base_ref[it * 2 * length:(it + 1) * 2 * length, :]

            def a_of(p, rows=rows, base=base):
                sl = slice(p * LANES, (p + 1) * LANES)
                return _hgrn_intra(pair(q_ref, rows, p), pair(k_ref, rows, p), base[0:length, sl],
                                   masks, m0, m1, None, sl)

            finish(rows, base, it // n_chunk, a_of)

    @pl.when(jnp.logical_not(safe))
    def _():
        def body(it, carry):
            rows = pl.ds(pl.multiple_of(it * length, length), length)
            arg_ref[...] = _dot(sell_ref[...], pieces(rows))
            base = base_ref[pl.ds(pl.multiple_of(it * 2 * length, 2 * length), 2 * length), :]
            seq = it // n_chunk if n_chunk > 1 and n_seq > 1 else (it if n_chunk == 1 else 0)

            def a_of(p):
                sl = slice(p * LANES, (p + 1) * LANES)
                return _hgrn_intra(pair(q_ref, rows, p), pair(k_ref, rows, p), base[0:length, sl],
                                   masks, m0, m1, arg_ref, sl)

            finish(rows, base, seq, a_of)
            return carry

        lax.fori_loop(0, n_items, body, 0)

    if carry_state:
        @pl.when(pl.program_id(1) == pl.num_programs(1) - 1)
        def _():
            _pairs_to_state(st_ref, s_ref, n_seq)
    else:
        _pairs_to_state(st_ref, s_ref, n_seq)


def _hgrn_scratch(n_seq, n_items, length, sell):
    return [pltpu.VMEM((n_seq, N_PAIR, LANES, LANES), F32),
            pltpu.VMEM((n_items * 2 * length, W_B), F32),
            pltpu.VMEM((sell.shape[0], W_B), F32)]


def _hgrn_prompt(qb, kb, ib, gb, logf, consts, bd, onorm_t, batch, seq):
    selb, sell, masks = consts
    steps = seq // ATT_ROWS
    n_chunk = ATT_ROWS // CHUNK
    row = pl.BlockSpec((ATT_ROWS, W_B), lambda b, j: (b * steps + j, 0))
    whole = lambda a: pl.BlockSpec(a.shape, lambda b, j: (0,) * a.ndim)
    return pl.pallas_call(
        functools.partial(_hgrn_kernel, 1, n_chunk, CHUNK, True),
        grid=(batch, steps),
        in_specs=[row, row, row, row, row, whole(selb), whole(sell), whole(masks), whole(bd),
                  whole(onorm_t)],
        out_specs=[row, pl.BlockSpec((1, N_HEADS_B, DK_B, DV_B), lambda b, j: (b, 0, 0, 0))],
        out_shape=[jax.ShapeDtypeStruct((batch * seq, V_B), BF16),
                   jax.ShapeDtypeStruct((batch, N_HEADS_B, DK_B, DV_B), F32)],
        scratch_shapes=_hgrn_scratch(1, n_chunk, CHUNK, sell),
        compiler_params=_cparams(("arbitrary", "arbitrary")),
        name="hgrn_prompt",
    )(qb, kb, ib, gb, logf, selb, sell, masks, bd, onorm_t)


def _hgrn_sample(qb, kb, ib, gb, logf, s0, consts, bd, onorm_t, row0, dec_batch, dec_seq):
    selb, sell, masks = consts
    rows = SAMPLE_SEQS * dec_seq
    row = pl.BlockSpec((rows, W_B), lambda b: (row0 // rows + b, 0))
    whole = lambda a: pl.BlockSpec(a.shape, lambda b: (0,) * a.ndim)
    state = pl.BlockSpec((SAMPLE_SEQS, N_HEADS_B, DK_B, DV_B), lambda b: (b, 0, 0, 0))
    return pl.pallas_call(
        functools.partial(_hgrn_kernel, SAMPLE_SEQS, 1, dec_seq, False),
        grid=(dec_batch // SAMPLE_SEQS,),
        in_specs=[row, row, row, row, row, state, whole(selb), whole(sell), whole(masks),
                  whole(bd), whole(onorm_t)],
        out_specs=[pl.BlockSpec((rows, V_B), lambda b: (b, 0)), state],
        out_shape=[jax.ShapeDtypeStruct((dec_batch * dec_seq, V_B), BF16),
                   jax.ShapeDtypeStruct((dec_batch, N_HEADS_B, DK_B, DV_B), F32)],
        scratch_shapes=_hgrn_scratch(SAMPLE_SEQS, SAMPLE_SEQS, dec_seq, sell),
        compiler_params=_cparams(("arbitrary",)),
        name="hgrn_sample",
    )(qb, kb, ib, gb, logf, s0, selb, sell, masks, bd, onorm_t)


def _outproj_kernel(prompt_tiles, x_ref, oap_ref, obp_ref, oas_ref, obs_ref, g1_ref, sh_ref, sc_ref,
                    ln_ref, w_ref, wr_ref, rb_ref, x1_out, h2_out, cls_out, ga_out, gb_out):
    def mix_of(oa_ref, ob_ref):
        return jnp.concatenate([oa_ref[hh] for hh in range(GQA_GROUP)] + [ob_ref[...]], axis=1)

    is_prompt = pl.program_id(0) < prompt_tiles
    mix = jnp.where(is_prompt, mix_of(oap_ref, obp_ref), mix_of(oas_ref, obs_ref))
    att = _dot(mix, w_ref[...])
    x1 = _gated_add(x_ref[...], g1_ref, att)
    x1_out[...] = x1
    h = _modulated_norm(x1, ln_ref, sc_ref, sh_ref)

    for s in range(SUBLANES):
        h2_out[pl.ds(s, ROWS, stride=SUBLANES), :] = h[:, s * LANES:(s + 1) * LANES]

    h_hi, h_lo = _split2(h)
    w_hi, w_lo = _split2(wr_ref[...])
    hi_terms = _dot_nt(jnp.concatenate([w_hi, w_lo], axis=0), h_hi)
    logits = hi_terms[0:N_EXPERTS] + hi_terms[N_EXPERTS:] + _dot_nt(w_hi, h_lo)
    aff = _sigmoid(logits)
    sel = aff + rb_ref[...]
    e_idx = lax.broadcasted_iota(I32, (N_EXPERTS, ROWS), 0)
    l_idx = lax.broadcasted_iota(I32, (EXPERTS_PER_GROUP, ROWS), 0)

    def first_argmax(vals, idx, big):
        top = jnp.max(vals, axis=0, keepdims=True)
        return top, jnp.min(jnp.where(vals == top, idx, big), axis=0, keepdims=True)

    g_scores = []
    for g in range(N_GROUPS):
        sg = sel[g * EXPERTS_PER_GROUP:(g + 1) * EXPERTS_PER_GROUP]
        m1, i1 = first_argmax(sg, l_idx, EXPERTS_PER_GROUP)
        m2 = jnp.max(jnp.where(l_idx == i1, -jnp.inf, sg), axis=0, keepdims=True)
        g_scores.append(m1 + m2)
    g_score = jnp.concatenate(g_scores, axis=0)
    g_idx = lax.broadcasted_iota(I32, (N_GROUPS, ROWS), 0)
    _, g_sel = first_argmax(g_score, g_idx, N_GROUPS)
    masked = jnp.where((e_idx // EXPERTS_PER_GROUP) == g_sel, sel, -jnp.inf)
    _, e1 = first_argmax(masked, e_idx, N_EXPERTS)
    _, e2 = first_argmax(jnp.where(e_idx == e1, -jnp.inf, masked), e_idx, N_EXPERTS)
    w1 = jnp.sum(jnp.where(e_idx == e1, aff, 0.0), axis=0, keepdims=True)
    w2 = jnp.sum(jnp.where(e_idx == e2, aff, 0.0), axis=0, keepdims=True)
    den = w1 + w2
    w1 = w1 / den
    w2 = w2 / den
    lo_first = e1 < e2
    gate_lo = jnp.where(lo_first, w1, w2)
    gate_hi = jnp.where(lo_first, w2, w1)
    a = jnp.minimum(e1, e2) - g_sel * EXPERTS_PER_GROUP
    b = jnp.maximum(e1, e2) - g_sel * EXPERTS_PER_GROUP
    pair = jnp.where(a == 0, b - 1, jnp.where(a == 1, 6 - b, 5))
    cls_out[...] = g_sel * PAIRS_PER_GROUP + pair
    ga_out[...] = jnp.where(a == 2, gate_hi, gate_lo)
    gb_out[...] = jnp.where(a == 2, gate_lo, gate_hi)


def _outproj(layer, x4, oap4, obp, oas4, obs, mod, ln2, w_out_b, w_router_t, router_bias_c,
             tiles_per_batch):
    n_tiles = x4.shape[0] // ROWS
    x_spec = pl.BlockSpec((ROWS, D_MODEL), lambda i: (i, 0))
    prompt_tiles = oap4.shape[0]
    last = prompt_tiles - 1
    lane_row = pl.BlockSpec((None, 1, ROWS), lambda i: (i, 0, 0))
    return pl.pallas_call(
        functools.partial(_outproj_kernel, prompt_tiles),
        grid=(n_tiles,),
        in_specs=[
            x_spec,
            pl.BlockSpec((None, GQA_GROUP, ROWS, LANES), lambda i: (jnp.minimum(i, last), 0, 0, 0)),
            pl.BlockSpec((ROWS, V_B), lambda i: (jnp.minimum(i, last), 0)),
            pl.BlockSpec((None, GQA_GROUP, ROWS, LANES), lambda i: (0, 0, 0, 0)),
            pl.BlockSpec((ROWS, V_B), lambda i: (0, 0)),
            _mod_spec(layer, 2, tiles_per_batch), _mod_spec(layer, 3, tiles_per_batch),
            _mod_spec(layer, 4, tiles_per_batch),
            pl.BlockSpec((None, 1, D_MODEL), lambda i: (layer, 0, 0)),
            pl.BlockSpec((None, Q_A + V_B, D_MODEL), lambda i: (layer, 0, 0)),
            pl.BlockSpec((N_EXPERTS, D_MODEL), lambda i: (0, 0)),
            pl.BlockSpec((N_EXPERTS, 1), lambda i: (0, 0)),
        ],
        out_specs=[
            x_spec,
            pl.BlockSpec((ROWS * SUBLANES, LANES), lambda i: (i, 0)),
            lane_row, lane_row, lane_row,
        ],
        out_shape=[
            jax.ShapeDtypeStruct(x4.shape, F32),
            jax.ShapeDtypeStruct((n_tiles * ROWS * SUBLANES, LANES), F32),
            jax.ShapeDtypeStruct((n_tiles, 1, ROWS), I32),
            jax.ShapeDtypeStruct((n_tiles, 1, ROWS), F32),
            jax.ShapeDtypeStruct((n_tiles, 1, ROWS), F32),
        ],
        compiler_params=_cparams(("arbitrary",)),
        name=f"outproj{layer}",
    )(x4, oap4, obp, oas4, obs, mod, mod, mod, ln2, w_out_b, w_router_t, router_bias_c)


def _rank_kernel(cls_ref, upper_ref, lower_ref, dest_ref, count_ref, rank_ref):
    n_tiles = cls_ref.shape[0]
    c_idx = lax.broadcasted_iota(I32, (GRP, ROWS), 0)
    upper = upper_ref[...]

    def rank_body(i, carry):
        onehot = c_idx == cls_ref[i]
        within = _dot(onehot.astype(BF16), upper)
        rank_ref[i] = jnp.sum(jnp.where(onehot, within + carry, 0.0), axis=0, keepdims=True)
        return carry + jnp.sum(onehot.astype(F32), axis=1, keepdims=True)

    count = lax.fori_loop(0, n_tiles, rank_body, jnp.zeros((GRP, 1), F32))
    count_ref[...] = jnp.broadcast_to(count, (GRP, LANES)).astype(I32)
    tiles = jnp.floor((count + (TM - 1)) * (1.0 / TM))
    first_tile = _dot(lower_ref[...], jnp.broadcast_to(tiles, (GRP, LANES)).astype(BF16))
    start = first_tile[:, 0:1] * TM

    def dest_body(i, carry):
        onehot = c_idx == cls_ref[i]
        dest = rank_ref[i] + jnp.sum(jnp.where(onehot, start, 0.0), axis=0, keepdims=True)
        dest_ref[i] = dest.astype(I32)
        return carry

    lax.fori_loop(0, n_tiles, dest_body, 0)


def _plan_kernel(n_tok, n_t, dest_ref, count_ref, src_ref, ea_ref, eb_ref, nv_ref, nt_ref):
    def fill(first_group, end_group):
        def body(p8, carry):
            for u in range(8):
                src_ref[p8 * 8 + u] = 0
            return carry
        lax.fori_loop(first_group, end_group, body, 0)

    tile = 0
    for c in range(N_CLASS):
        g, pi = divmod(c, PAIRS_PER_GROUP)
        a, b = PAIR_TABLE[pi]
        cnt = count_ref[c, 0]
        n_c = lax.shift_right_logical(cnt + (TM - 1), TM.bit_length() - 1)
        first = tile

        def mark(t, carry, g=g, a=a, b=b, cnt=cnt, first=first):
            ea_ref[t] = g * EXPERTS_PER_GROUP + a
            eb_ref[t] = g * EXPERTS_PER_GROUP + b
            nv_ref[t] = jnp.minimum(cnt - (t - first) * TM, TM)
            return carry

        lax.fori_loop(tile, tile + n_c, mark, 0)

        fill(lax.shift_right_logical(first * TM + cnt, 3), (first + n_c) * (TM // 8))
        tile = tile + n_c
    nt_ref[0] = tile

    def unused(t, carry):
        ea_ref[t] = 0
        eb_ref[t] = 0
        nv_ref[t] = 0
        return carry

    lax.fori_loop(tile, n_t, unused, 0)

    fill(tile * (TM // 8), n_t * (TM // 8))

    def place(t, carry):
        src_ref[dest_ref[t]] = t
        return carry

    lax.fori_loop(0, n_tok, place, 0, unroll=8)


def _route_plan(cls3, upper, lower):
    n_tiles = cls3.shape[0]
    n_tok = n_tiles * ROWS
    n_t = (n_tok + N_CLASS * (TM - 1)) // TM + 2
    dest, count = pl.pallas_call(
        _rank_kernel,
        out_shape=[jax.ShapeDtypeStruct((n_tiles, 1, ROWS), I32),
                   jax.ShapeDtypeStruct((GRP, LANES), I32)],
        scratch_shapes=[pltpu.VMEM((n_tiles, 1, ROWS), F32)],
        name="moe_rank",
    )(cls3, upper, lower)
    smem = pl.BlockSpec(memory_space=pltpu.SMEM)
    return pl.pallas_call(
        functools.partial(_plan_kernel, n_tok, n_t),
        in_specs=[smem, smem],
        out_specs=[smem, smem, smem, smem, smem],
        out_shape=[jax.ShapeDtypeStruct((n_t * TM,), I32),
                   jax.ShapeDtypeStruct((n_t,), I32),
                   jax.ShapeDtypeStruct((n_t,), I32),
                   jax.ShapeDtypeStruct((n_t,), I32),
                   jax.ShapeDtypeStruct((1,), I32)],
        name="moe_plan",
    )(dest.reshape(n_tok), count)


def _tile_rows(ref, slot, r):
    return ref.at[pl.ds(pl.multiple_of((slot * TM + r) * SUBLANES, SUBLANES), SUBLANES)]


def _moe_kernel(n_tok, src_ref, ea_ref, eb_ref, nv_ref, nt_ref, ga_ref, gb_ref, h2_hbm, wga_ref,
                wua_ref, wda_ref, wgb_ref, wub_ref, wdb_ref, y_hbm, xbuf, ybuf, gbuf, gsem, ssem):
    i = pl.program_id(0)
    n_used = nt_ref[0]
    lane = lax.broadcasted_iota(I32, (1, LANES), 1)

    def token_tile(hbm, tok):
        return hbm.at[pl.ds(pl.multiple_of(tok * SUBLANES, SUBLANES), SUBLANES)]

    def gather_row(tile, slot, r, priority):
        tok = src_ref[tile * TM + r]
        pltpu.make_async_copy(token_tile(h2_hbm, tok), _tile_rows(xbuf, slot, r),
                              gsem.at[slot]).start(priority=priority)
        gbuf[pl.ds(slot * TM + r, 1), :] = jnp.where(lane == 0, ga_ref[tok], gb_ref[tok])

    def scatter_row(tile, slot, r, n_valid, priority):
        tok = jnp.where(r < n_valid, src_ref[tile * TM + r], n_tok + slot * TM + r)
        pltpu.make_async_copy(_tile_rows(ybuf, slot, r), token_tile(y_hbm, tok),
                              ssem.at[slot]).start(priority=priority)

    def row_loop(start_row):
        def body(r8, c):
            for u in range(DMA_UNROLL):
                start_row(r8 * DMA_UNROLL + u, u % 2)
            return c
        lax.fori_loop(0, TM // DMA_UNROLL, body, 0)

    def wait_gather(slot):
        pltpu.make_async_copy(h2_hbm.at[pl.ds(0, TM * SUBLANES)],
                              xbuf.at[pl.ds(slot * TM * SUBLANES, TM * SUBLANES)],
                              gsem.at[slot]).wait()

    def wait_scatter(slot):
        pltpu.make_async_copy(ybuf.at[pl.ds(slot * TM * SUBLANES, TM * SUBLANES)],
                              y_hbm.at[pl.ds(0, TM * SUBLANES)], ssem.at[slot]).wait()

    @pl.when(i == 0)
    def _():
        ybuf[...] = jnp.zeros_like(ybuf)
        spare = pltpu.make_async_copy(
            ybuf, y_hbm.at[pl.ds(n_tok * SUBLANES, 2 * TM * SUBLANES)], ssem.at[0])
        spare.start()
        spare.wait()

    @pl.when(i == 0)
    def _():
        for t in range(GATHER_AHEAD):
            @pl.when(t < n_used)
            def _():
                row_loop(lambda r, pr: gather_row(t, t, r, pr))

    @pl.when(i + GATHER_AHEAD < n_used)
    def _():
        nxt = i + GATHER_AHEAD
        row_loop(lambda r, pr: gather_row(nxt, nxt % GATHER_SLOTS, r, pr))

    @pl.when(jnp.logical_and(i >= 2, i - 2 < n_used))
    def _():
        wait_scatter(i % 2)

    @pl.when(i < n_used)
    def _():
        slot = i % 2
        xslot = i % GATHER_SLOTS
        wait_gather(xslot)
        base = xslot * TM * SUBLANES
        xb = jnp.concatenate([_token_rows(xbuf, s, TM, base) for s in range(SUBLANES)],
                             axis=1).astype(BF16)
        gates = gbuf[pl.ds(xslot * TM, TM), :]
        y = jnp.zeros((TM, D_MODEL), F32)
        for e, (wg, wu, wd) in enumerate(((wga_ref, wua_ref, wda_ref), (wgb_ref, wub_ref, wdb_ref))):
            gt = _dot(xb, wg[...].astype(BF16))
            up = _dot(xb, wu[...].astype(BF16))
            act = gt * _sigmoid(gt) * up * gates[:, e:e + 1]
            y = y + _dot(act.astype(BF16), wd[...].astype(BF16))
        ybase = slot * TM * SUBLANES
        for s in range(SUBLANES):
            ybuf[pl.ds(ybase + s, TM, stride=SUBLANES), :] = y[:, s * LANES:(s + 1) * LANES]
        n_valid = nv_ref[i]
        row_loop(lambda r, pr: scatter_row(i, slot, r, n_valid, pr))


def _moe(layer, h2, gate_a, gate_b, plan, w_gate, w_up, w_down):
    src, ea, eb, nv, nt = plan
    n_t = ea.shape[0]
    n_tok = h2.shape[0] // SUBLANES

    def w_spec(shape, which):
        def index(i, src_ref, ea_ref, eb_ref, nv_ref, nt_ref, ga_ref, gb_ref):
            t = jnp.minimum(i, jnp.maximum(nt_ref[0] - 1, 0))
            e = (ea_ref, eb_ref)[which][t]
            return (layer, e, 0, 0)
        return pl.BlockSpec((None, None) + shape, index)

    gu = (D_MODEL, D_EXPERT)
    dn = (D_EXPERT, D_MODEL)
    grid_spec = pltpu.PrefetchScalarGridSpec(
        num_scalar_prefetch=7,
        grid=(n_t,),
        in_specs=[pl.BlockSpec(memory_space=pl.ANY),
                  w_spec(gu, 0), w_spec(gu, 0), w_spec(dn, 0),
                  w_spec(gu, 1), w_spec(gu, 1), w_spec(dn, 1)],
        out_specs=pl.BlockSpec(memory_space=pl.ANY),
        scratch_shapes=[pltpu.VMEM((GATHER_SLOTS * TM * SUBLANES, LANES), F32),
                        pltpu.VMEM((2 * TM * SUBLANES, LANES), F32),
                        pltpu.VMEM((GATHER_SLOTS * TM, LANES), F32),
                        pltpu.SemaphoreType.DMA((GATHER_SLOTS,)), pltpu.SemaphoreType.DMA((2,))],
    )
    return pl.pallas_call(
        functools.partial(_moe_kernel, n_tok),
        grid_spec=grid_spec,
        out_shape=jax.ShapeDtypeStruct(((n_tok + 2 * TM) * SUBLANES, LANES), F32),
        compiler_params=_cparams(("arbitrary",)),
        name=f"moe{layer}",
    )(src, ea, eb, nv, nt, gate_a.reshape(n_tok), gate_b.reshape(n_tok), h2,
      w_gate, w_up, w_down, w_gate, w_up, w_down)


def kernel(x_prompt, x_sample, cache_k, cache_v, state_hgrn, c_prompt, c_sample, w_ada, b_ada,
           ln1, ln2, w_in, q_norm, k_norm, sinks, lb_param, o_norm, w_out, w_router, router_bias,
           w_e_gate, w_e_up, w_e_down):
    batch, seq, _ = x_prompt.shape
    dec_batch, dec_seq, _ = x_sample.shape
    depth = w_ada.shape[0]
    assert dec_batch == GRP and dec_seq == SUB and seq % ROWS == 0
    tiles_per_batch = seq // ROWS
    n_prompt = batch * seq
    n_tok = n_prompt + dec_batch * dec_seq
    n_tiles = n_tok // ROWS
    prompt_tiles = n_prompt // ROWS
    n_cache = cache_k.shape[2]

    seg = np.arange(Q_A) // HEAD_DIM
    bd = jnp.asarray((seg[:, None] == seg[None, :]).astype(np.float32) / HEAD_DIM, BF16)

    def hgrn_consts(length):
        selb, sell, masks = _hgrn_consts(length)
        return jnp.asarray(selb, BF16), jnp.asarray(sell, BF16), jnp.asarray(masks)

    consts_p = hgrn_consts(CHUNK)
    consts_s = hgrn_consts(dec_seq)
    upper = jnp.asarray(np.triu(np.ones((ROWS, ROWS), np.float32), 1), BF16)
    lower = jnp.asarray(np.tril(np.ones((GRP, GRP), np.float32), -1), BF16)

    wq = w_in[:, :, :Q_A].reshape(depth, D_MODEL, N_KV_A, GQA_GROUP, HEAD_DIM)
    wq = jnp.swapaxes(wq, 2, 3).reshape(depth, D_MODEL, Q_A)
    w_in_b = jnp.concatenate([wq, w_in[:, :, Q_A:]], axis=-1).astype(BF16)
    wo = w_out[:, :Q_A].reshape(depth, N_KV_A, GQA_GROUP, HEAD_DIM, D_MODEL)
    wo = jnp.swapaxes(wo, 1, 2).reshape(depth, Q_A, D_MODEL)
    w_out_b = jnp.concatenate([wo, w_out[:, Q_A:]], axis=1).astype(BF16)
    q_norm_t = jnp.tile(q_norm, (1, N_HEADS_A)).reshape(depth, 1, Q_A)
    k_norm_t = jnp.tile(k_norm, (1, N_KV_A)).reshape(depth, 1, KV_A)
    o_norm_t = jnp.tile(o_norm, (1, N_HEADS_B)).reshape(depth, 1, V_B)
    ln1_r = ln1.reshape(depth, 1, D_MODEL)
    ln2_r = ln2.reshape(depth, 1, D_MODEL)
    w_router_t = w_router.T
    router_bias_c = router_bias.reshape(N_EXPERTS, 1)

    mod = _adaln(c_prompt, c_sample, w_ada, b_ada).reshape(depth, batch + 1, GRP, N_MOD * D_MODEL)

    cache_k2 = cache_k.reshape(depth, dec_batch, n_cache, KV_A)
    cache_v2 = cache_v.reshape(depth, dec_batch, n_cache, KV_A)

    src = (x_prompt, x_sample)
    kp, vp, sp, kss, vss, sss = [], [], [], [], [], []
    for l in range(depth):
        x, q4, k, v, qb, logf, kb, ib, gb, kwin, vwin = _inproj(
            l, src, n_tiles, prompt_tiles, mod, ln1_r, w_in_b, q_norm_t, k_norm_t, lb_param, bd,
            tiles_per_batch)

        oap4 = _attn_prompt(q4, k, v, sinks[l], batch, seq)
        oas4 = _attn_sample(q4, k, v, cache_k2[l], cache_v2[l], sinks[l], prompt_tiles,
                            n_prompt, dec_batch, dec_seq)
        obp, s_p = _hgrn_prompt(qb, kb, ib, gb, logf, consts_p, bd, o_norm_t[l], batch, seq)
        obs, s_s = _hgrn_sample(qb, kb, ib, gb, logf, state_hgrn[l], consts_s,
                                bd, o_norm_t[l], n_prompt, dec_batch, dec_seq)

        x1, h2, cls3, gate_a, gate_b = _outproj(l, x, oap4, obp, oas4, obs, mod, ln2_r, w_out_b,
                                                w_router_t, router_bias_c, tiles_per_batch)
        y = _moe(l, h2, gate_a, gate_b, _route_plan(cls3, upper, lower), w_e_gate, w_e_up,
                 w_e_down)
        src = (x1, y)

        kp.append(kwin.reshape(batch, WINDOW, N_KV_A, HEAD_DIM))
        vp.append(vwin.reshape(batch, WINDOW, N_KV_A, HEAD_DIM))
        sp.append(s_p)
        kss.append(k[n_prompt:].reshape(dec_batch, dec_seq, N_KV_A, HEAD_DIM))
        vss.append(v[n_prompt:].reshape(dec_batch, dec_seq, N_KV_A, HEAD_DIM))
        sss.append(s_s)

    yp, ys = _final(depth - 1, src[0], src[1], mod, batch, seq, tiles_per_batch)
    return (yp, ys, jnp.stack(kp), jnp.stack(vp), jnp.stack(sp),
            jnp.stack(kss), jnp.stack(vss), jnp.stack(sss))
```

```python
import functools

import numpy as np
import jax
import jax.numpy as jnp
from jax import lax
from jax.experimental import pallas as pl
from jax.experimental.pallas import tpu as pltpu

F32 = jnp.float32
BF16 = jnp.bfloat16
I32 = jnp.int32

D_MODEL = 1024
HEAD_DIM = 64
N_HEADS_A = 8
N_KV_A = 2
GQA_GROUP = N_HEADS_A // N_KV_A
N_HEADS_B = 8
DK_B = 64
DV_B = 64
Q_A = N_HEADS_A * HEAD_DIM
KV_A = N_KV_A * HEAD_DIM
W_B = N_HEADS_B * DK_B
V_B = N_HEADS_B * DV_B
CHUNK = 64
WINDOW = 128
N_EXPERTS = 16
N_GROUPS = 4
EXPERTS_PER_GROUP = 4
D_EXPERT = D_MODEL // 2
RMS_EPS = 1e-6
NEG_INF = -1e30
N_MOD = 6

LANES = 128
SUBLANES = 8
ROWS = 512
GRP = 32
SUB = ROWS // GRP
ATT_ROWS = 256
N_PAIR = N_HEADS_B // 2
SAMPLE_SEQS = 4
HGRN_SPAN_LIMIT = 80.0
TM = 256
PAIRS_PER_GROUP = 6
PAIR_TABLE = ((0, 1), (0, 2), (0, 3), (1, 3), (1, 2), (3, 2))
N_CLASS = N_GROUPS * PAIRS_PER_GROUP
ROW_GROUP = 8
GROUP_SHIFT = ROW_GROUP.bit_length() - 1
GATHER_AHEAD = 3
GATHER_SLOTS = GATHER_AHEAD + 1
VMEM_LIMIT = 56 * 1024 * 1024


def _cparams(sem):
    return pltpu.CompilerParams(dimension_semantics=sem, vmem_limit_bytes=VMEM_LIMIT)


def _dot(a, b):
    return jnp.dot(a, b, preferred_element_type=F32)


def _dot_nt(a, b):
    return lax.dot_general(a, b, (((1,), (1,)), ((), ())), preferred_element_type=F32)


def _dot_tn(a, b):
    return lax.dot_general(a, b, (((0,), (0,)), ((), ())), preferred_element_type=F32)


def _sigmoid(x):
    return 1.0 / (1.0 + jnp.exp(-x))


def _split2(x):
    hi = x.astype(BF16)
    lo = (x - hi.astype(F32)).astype(BF16)
    return hi, lo


def _split3(x):
    hi = x.astype(BF16)
    r = x - hi.astype(F32)
    mid = r.astype(BF16)
    lo = (r - mid.astype(F32)).astype(BF16)
    return hi, mid, lo


def _seg_mean(sq, bd):
    return _dot(sq.astype(BF16), bd)


def _token_rows(ref, s, n, base=0):
    return ref[pl.ds(base + s, n, stride=SUBLANES), :]


def _adaln_kernel(n_prompt, c_ref, w_ref, b_ref, o_ref):
    c = c_ref[...]
    a = c * _sigmoid(c)
    a_hi, a_lo = _split2(a)
    w_hi, w_lo = _split2(w_ref[...])
    acc = _dot(a_hi, w_hi) + _dot(a_lo, w_hi) + _dot(a_hi, w_lo) + b_ref[...]
    for b in range(n_prompt):
        o_ref[b * GRP:(b + 1) * GRP, :] = jnp.broadcast_to(acc[b:b + 1], (GRP, acc.shape[1]))
    o_ref[n_prompt * GRP:, :] = acc[n_prompt:]


def _adaln(c_prompt, c_sample, w_ada, b_ada):
    depth = w_ada.shape[0]
    n_prompt = c_prompt.shape[0]
    c_all = jnp.concatenate([c_prompt, c_sample], axis=0)
    rows = n_prompt * GRP + c_sample.shape[0]
    tn = D_MODEL
    return pl.pallas_call(
        functools.partial(_adaln_kernel, n_prompt),
        grid=(depth, N_MOD * D_MODEL // tn),
        in_specs=[
            pl.BlockSpec(c_all.shape, lambda l, n: (0, 0)),
            pl.BlockSpec((None, D_MODEL, tn), lambda l, n: (l, 0, n)),
            pl.BlockSpec((None, 1, tn), lambda l, n: (l, 0, n)),
        ],
        out_specs=pl.BlockSpec((None, rows, tn), lambda l, n: (l, 0, n)),
        out_shape=jax.ShapeDtypeStruct((depth, rows, N_MOD * D_MODEL), F32),
        compiler_params=_cparams(("arbitrary", "arbitrary")),
        name="adaln",
    )(c_all, w_ada, b_ada.reshape(depth, 1, N_MOD * D_MODEL))


def _mod_spec(layer, j, tiles_per_batch):
    return pl.BlockSpec((None, None, GRP, D_MODEL),
                        lambda i, *_: (layer, i // tiles_per_batch, 0, j))


def _by_group(fn, xs, mod_refs):
    outs = []
    for g in range(GRP):
        rows = slice(g * SUB, (g + 1) * SUB)
        outs.append(fn(*[x[rows] for x in xs], *[m[g:g + 1, :] for m in mod_refs]))
    return jnp.concatenate(outs, axis=0)


def _gated_add(x, gate_ref, y):
    return _by_group(lambda xg, yg, gg: xg + gg * yg, (x, y), (gate_ref,))


def _modulated_norm(x, ln_ref, sc_ref, sh_ref):
    ms = jnp.mean(x * x, axis=-1, keepdims=True)
    xn = x * lax.rsqrt(ms + RMS_EPS)
    scale = ln_ref[...] * (1.0 + sc_ref[...])
    return _by_group(lambda xg, sc, sh: xg * sc + sh, (xn,), (scale, sh_ref))


def _prompt_tile_spec(prompt_tiles, tiles_per_batch):
    def index(i):
        t = jnp.minimum(i, prompt_tiles - 1)
        return (t // tiles_per_batch, t % tiles_per_batch, 0)
    return pl.BlockSpec((None, ROWS, D_MODEL), index)


def _moe_output_tile(y_ref):
    return jnp.concatenate([_token_rows(y_ref, s, ROWS) for s in range(SUBLANES)], axis=1)


def _inproj_kernel(layer, prompt_tiles, n_src, *refs):
    src = refs[:n_src]
    sh_ref, sc_ref, ln_ref, w_ref, qn_ref, kn_ref, lbp_ref, bd_ref = refs[n_src:n_src + 8]
    (x_out, q_out, k_out, v_out, qb_out, logf_out, kb_out, ib_out, gb_out,
     kwin_out, vwin_out) = refs[n_src + 8:]
    if layer == 0:
        xp_ref, xs_ref = src
        x = jnp.where(pl.program_id(0) < prompt_tiles, xp_ref[...],
                      xs_ref[...].reshape(ROWS, D_MODEL))
    else:
        x1_ref, y_ref, g2_ref = src
        x = _gated_add(x1_ref[...], g2_ref, _moe_output_tile(y_ref))
    x_out[...] = x
    hb = _modulated_norm(x, ln_ref, sc_ref, sh_ref).astype(BF16)
    bd = bd_ref[...]

    c0 = Q_A + 2 * KV_A
    c1 = c0 + 2 * W_B
    qa = _dot(hb, w_ref[:, 0:Q_A])
    kv = _dot(hb, w_ref[:, Q_A:c0])
    qb = _dot(hb, w_ref[:, c0:c0 + W_B])
    z = _dot(hb, w_ref[:, c0 + W_B:c1])
    ib = _dot(hb, w_ref[:, c1:c1 + V_B])
    gb = _dot(hb, w_ref[:, c1 + V_B:c1 + 2 * V_B])

    qa = qa * lax.rsqrt(_seg_mean(qa * qa, bd) + RMS_EPS) * qn_ref[...]
    qa = (qa * (HEAD_DIM ** -0.5)).astype(BF16)
    for hh in range(GQA_GROUP):
        q_out[hh] = qa[:, hh * LANES:(hh + 1) * LANES]

    ka = kv[:, :KV_A]
    ka = ka * lax.rsqrt(_seg_mean(ka * ka, bd[:KV_A, :KV_A]) + RMS_EPS) * kn_ref[...]
    k_out[...] = ka
    v_out[...] = kv[:, KV_A:]

    @pl.when(pl.program_id(0) < prompt_tiles)
    def _():
        kwin_out[...] = ka[ROWS - WINDOW:]
        vwin_out[...] = kv[ROWS - WINDOW:, KV_A:]

    qb_out[...] = (qb * (DK_B ** -0.5)).astype(BF16)
    ib_out[...] = ib.astype(BF16)
    gb_out[...] = gb.astype(BF16)

    p = lbp_ref[...]
    e = jnp.exp(p - jnp.max(p, axis=0, keepdims=True))
    sm = e / jnp.sum(e, axis=0, keepdims=True)
    cum0 = sm[0:1]
    cum = cum0
    for r in range(1, layer + 1):
        cum = cum + sm[r:r + 1]
    lb = cum - cum0

    pos = z >= 0.0
    t = jnp.exp(-jnp.abs(z))
    num = jnp.where(pos, jnp.log(1.0 + lb * t), jnp.maximum(jnp.log(lb + t), z))
    logf_out[...] = num - jnp.log(1.0 + t)
    kb_out[...] = ((1.0 - lb) * jnp.where(pos, t, 1.0) / (1.0 + t)).astype(BF16)


def _inproj(layer, src, n_tiles, prompt_tiles, mod, ln1, w_in_b, q_norm_t, k_norm_t, lb_param, bd,
            tiles_per_batch):
    depth = lb_param.shape[0]
    n = n_tiles * ROWS
    in_width = w_in_b.shape[-1]
    x_spec = pl.BlockSpec((ROWS, D_MODEL), lambda i: (i, 0))
    if layer == 0:
        src_specs = [_prompt_tile_spec(prompt_tiles, tiles_per_batch),
                     pl.BlockSpec((GRP, SUB, D_MODEL), lambda i: (0, 0, 0))]
        src_args = list(src)
    else:
        src_specs = [x_spec, pl.BlockSpec((ROWS * SUBLANES, LANES), lambda i: (i, 0)),
                     _mod_spec(layer - 1, 5, tiles_per_batch)]
        src_args = list(src) + [mod]
    row_spec = lambda w: pl.BlockSpec((ROWS, w), lambda i: (i, 0))
    win_spec = pl.BlockSpec(
        (None, WINDOW, KV_A), lambda i: (jnp.minimum(i, prompt_tiles - 1) // tiles_per_batch, 0, 0))
    return pl.pallas_call(
        functools.partial(_inproj_kernel, layer, prompt_tiles, len(src_specs)),
        grid=(n_tiles,),
        in_specs=src_specs + [
            _mod_spec(layer, 0, tiles_per_batch), _mod_spec(layer, 1, tiles_per_batch),
            pl.BlockSpec((None, 1, D_MODEL), lambda i: (layer, 0, 0)),
            pl.BlockSpec((None, D_MODEL, in_width), lambda i: (layer, 0, 0)),
            pl.BlockSpec((None, 1, Q_A), lambda i: (layer, 0, 0)),
            pl.BlockSpec((None, 1, KV_A), lambda i: (layer, 0, 0)),
            pl.BlockSpec((depth, W_B), lambda i: (0, 0)),
            pl.BlockSpec((Q_A, Q_A), lambda i: (0, 0)),
        ],
        out_specs=[
            x_spec,
            pl.BlockSpec((None, GQA_GROUP, ROWS, LANES), lambda i: (i, 0, 0, 0)),
            row_spec(KV_A), row_spec(KV_A),
            row_spec(W_B), row_spec(W_B), row_spec(W_B), row_spec(V_B), row_spec(V_B),
            win_spec, win_spec,
        ],
        out_shape=[
            jax.ShapeDtypeStruct((n, D_MODEL), F32),
            jax.ShapeDtypeStruct((n_tiles, GQA_GROUP, ROWS, LANES), BF16),
            jax.ShapeDtypeStruct((n, KV_A), F32),
            jax.ShapeDtypeStruct((n, KV_A), F32),
            jax.ShapeDtypeStruct((n, W_B), BF16),
            jax.ShapeDtypeStruct((n, W_B), F32),
            jax.ShapeDtypeStruct((n, W_B), BF16),
            jax.ShapeDtypeStruct((n, V_B), BF16),
            jax.ShapeDtypeStruct((n, V_B), BF16),
            jax.ShapeDtypeStruct((prompt_tiles // tiles_per_batch, WINDOW, KV_A), F32),
            jax.ShapeDtypeStruct((prompt_tiles // tiles_per_batch, WINDOW, KV_A), F32),
        ],
        compiler_params=_cparams(("arbitrary",)),
        name=f"inproj{layer}",
    )(*src_args, mod, mod, ln1, w_in_b, q_norm_t, k_norm_t, lb_param, bd)


def _final_kernel(prompt_tiles, x1_ref, y_ref, g2_ref, yp_ref, ys_ref):
    i = pl.program_id(0)
    x = _gated_add(x1_ref[...], g2_ref, _moe_output_tile(y_ref))

    @pl.when(i < prompt_tiles)
    def _():
        yp_ref[...] = x

    @pl.when(i >= prompt_tiles)
    def _():
        ys_ref[...] = x.reshape(GRP, SUB, D_MODEL)


def _final(layer, x1, y, mod, batch, seq, tiles_per_batch):
    n_tiles = x1.shape[0] // ROWS
    prompt_tiles = batch * tiles_per_batch
    return pl.pallas_call(
        functools.partial(_final_kernel, prompt_tiles),
        grid=(n_tiles,),
        in_specs=[pl.BlockSpec((ROWS, D_MODEL), lambda i: (i, 0)),
                  pl.BlockSpec((ROWS * SUBLANES, LANES), lambda i: (i, 0)),
                  _mod_spec(layer, 5, tiles_per_batch)],
        out_specs=[_prompt_tile_spec(prompt_tiles, tiles_per_batch),
                   pl.BlockSpec((GRP, SUB, D_MODEL), lambda i: (0, 0, 0))],
        out_shape=[jax.ShapeDtypeStruct((batch, seq, D_MODEL), F32),
                   jax.ShapeDtypeStruct((GRP, SUB, D_MODEL), F32)],
        compiler_params=_cparams(("arbitrary",)),
        name="final",
    )(x1, y, mod)


def _attn_tables(sinks_ref, q_len, n_k, k_off):
    n_q = GQA_GROUP * q_len
    row = lax.broadcasted_iota(I32, (n_q, n_k), 0)
    col = lax.broadcasted_iota(I32, (n_q, n_k), 1)
    dist = jnp.abs((row % q_len) + k_off - col).astype(F32)
    hh_col = lax.broadcasted_iota(I32, (n_q, 1), 0) // q_len
    bias, sinks = [], []
    for g in range(N_KV_A):
        head = (g * GQA_GROUP + 1 + hh_col).astype(F32)
        bias.append(jnp.exp2(-8.0 * head / N_HEADS_A) * dist)
        sink = jnp.zeros((n_q, 1), F32)
        for j in range(GQA_GROUP):
            sink = jnp.where(hh_col == j, sinks_ref[g * GQA_GROUP + j], sink)
        sinks.append(sink)
    return bias, sinks


def _attend(q, k, v, bias, sinks, first_valid=None):
    n_q = q.shape[0]
    n_k = k.shape[0]
    lane = lax.broadcasted_iota(I32, (n_k, LANES), 1)
    out = jnp.zeros((n_q, LANES), F32)
    for g in range(N_KV_A):
        in_g = (lane // HEAD_DIM) == g
        kg = jnp.where(in_g, k, 0.0).astype(BF16)
        vg = jnp.where(in_g, v, 0.0).astype(BF16)
        s = _dot_nt(q, kg) - bias[g]
        if first_valid is not None:
            col = lax.broadcasted_iota(I32, (n_q, n_k), 1)
            s = jnp.where(col >= first_valid, s, NEG_INF)
        m = jnp.maximum(jnp.max(s, axis=-1, keepdims=True), sinks[g])
        p = jnp.exp(s - m)
        denom = jnp.sum(p, axis=-1, keepdims=True) + jnp.exp(sinks[g] - m)
        out = out + _dot(p.astype(BF16), vg) / denom
    return out


def _attn_prompt_kernel(sinks_ref, q_ref, kp_ref, kc_ref, vp_ref, vc_ref, o_ref):
    j = pl.program_id(1)
    kcat = jnp.concatenate([kp_ref[...], kc_ref[...]], axis=0)
    vcat = jnp.concatenate([vp_ref[...], vc_ref[...]], axis=0)
    bias, sinks = _attn_tables(sinks_ref, CHUNK, WINDOW + CHUNK, WINDOW)

    for c in range(ATT_ROWS // CHUNK):
        q = q_ref[:, c * CHUNK:(c + 1) * CHUNK, :].reshape(GQA_GROUP * CHUNK, LANES)
        k = kcat[c * CHUNK:c * CHUNK + WINDOW + CHUNK]
        v = vcat[c * CHUNK:c * CHUNK + WINDOW + CHUNK]
        first_valid = WINDOW - c * CHUNK - j * ATT_ROWS
        out = _attend(q, k, v, bias, sinks, first_valid)
        o_ref[:, c * CHUNK:(c + 1) * CHUNK, :] = out.reshape(GQA_GROUP, CHUNK, LANES).astype(BF16)


def _attn_prompt(q4, k, v, sinks, batch, seq):
    steps = seq // ATT_ROWS
    per_tile = ROWS // ATT_ROWS
    win_blocks = ATT_ROWS // WINDOW
    n_tiles = batch * seq // ROWS
    cur = pl.BlockSpec((ATT_ROWS, KV_A), lambda b, j: (b * steps + j, 0))
    prev = pl.BlockSpec(
        (WINDOW, KV_A), lambda b, j: (b * steps * win_blocks + jnp.maximum(j * win_blocks - 1, 0), 0))
    qo = pl.BlockSpec((None, GQA_GROUP, ATT_ROWS, LANES),
                      lambda b, j: ((b * steps + j) // per_tile, 0, j % per_tile, 0))
    return pl.pallas_call(
        _attn_prompt_kernel,
        grid=(batch, steps),
        in_specs=[pl.BlockSpec(memory_space=pltpu.SMEM), qo, prev, cur, prev, cur],
        out_specs=qo,
        out_shape=jax.ShapeDtypeStruct((n_tiles, GQA_GROUP, ROWS, LANES), BF16),
        compiler_params=_cparams(("arbitrary", "arbitrary")),
        name="attn_prompt",
    )(sinks, q4, k, k, v, v)


def _attn_sample_kernel(n_new, sinks_ref, q_ref, kc_ref, kn_ref, vc_ref, vn_ref, o_ref):
    n_cache = kc_ref.shape[1]
    bias, sinks = _attn_tables(sinks_ref, n_new, n_cache + n_new, n_cache)
    for r in range(SAMPLE_SEQS):
        rows = slice(r * n_new, (r + 1) * n_new)
        q = q_ref[:, rows, :].reshape(GQA_GROUP * n_new, LANES)
        k = jnp.concatenate([kc_ref[r], kn_ref[rows, :]], axis=0)
        v = jnp.concatenate([vc_ref[r], vn_ref[rows, :]], axis=0)
        out = _attend(q, k, v, bias, sinks)
        o_ref[:, rows, :] = out.reshape(GQA_GROUP, n_new, LANES).astype(BF16)


def _attn_sample(q4, k, v, cache_k, cache_v, sinks, tile, row0, dec_batch, dec_seq):
    n_cache = cache_k.shape[1]
    rows = SAMPLE_SEQS * dec_seq
    q_spec = pl.BlockSpec((None, GQA_GROUP, rows, LANES), lambda b: (tile, 0, b, 0))
    new = pl.BlockSpec((rows, KV_A), lambda b: (row0 // rows + b, 0))
    cache = pl.BlockSpec((SAMPLE_SEQS, n_cache, KV_A), lambda b: (b, 0, 0))
    return pl.pallas_call(
        functools.partial(_attn_sample_kernel, dec_seq),
        grid=(dec_batch // SAMPLE_SEQS,),
        in_specs=[pl.BlockSpec(memory_space=pltpu.SMEM), q_spec, cache, new, cache, new],
        out_specs=pl.BlockSpec((None, GQA_GROUP, rows, LANES), lambda b: (0, 0, b, 0)),
        out_shape=jax.ShapeDtypeStruct((1, GQA_GROUP, dec_batch * dec_seq, LANES), BF16),
        compiler_params=_cparams(("arbitrary",)),
        name="attn_sample",
    )(sinks, q4, cache_k, k, cache_v, v)


def _level_sizes(length):
    sizes = []
    m = length // 2
    while m >= 1:
        sizes.append(m)
        m //= 2
    return sizes


def _hgrn_consts(length):
    t = np.arange(length)
    tri = (t[None, :] <= t[:, None]).astype(np.float32)
    after = (t[None, :] > t[:, None]).astype(np.float32)
    levels = []
    masks = []
    s = np.tile(t, 2)[None, :]
    tt = t[:, None]
    for m in _level_sizes(length):
        blk = t // m
        same = blk[None, :] == blk[:, None]
        q_rows = (blk % 2 == 1)[:, None]
        sel = np.where(q_rows, same & (t[None, :] <= t[:, None]), same & (t[None, :] > t[:, None]))
        levels.append(sel.astype(np.float32))
        masks.append(((tt // m) % 2 == 1) & ((s // m) == (tt // m) - 1))
    masks.append(s == tt)
    masks.append(s <= tt)
    base = np.concatenate([tri, after], axis=0)
    lev = np.concatenate(levels, axis=0)
    return (np.concatenate([base] * 3, axis=1), np.concatenate([lev] * 3, axis=1),
            np.stack(masks).astype(np.int32))


def _stack2(x, m0, m1):
    xb = x.astype(BF16)
    return jnp.concatenate([xb * m0, xb * m1], axis=0)


def _hgrn_intra(qp, kp, b, masks, m0, m1, arg_ref, sl):
    length = qp.shape[0]
    n_lev = len(_level_sizes(length))
    if arg_ref is None:
        mid = length // 2 - 1
        rel = b - b[mid:mid + 1]
        a = _dot_nt((qp * jnp.exp(rel)).astype(BF16), _stack2(kp * jnp.exp(-rel), m0, m1))
        a = jnp.where(masks[n_lev + 1], a, 0.0)
    else:
        a = _dot_nt(qp.astype(BF16), _stack2(kp, m0, m1))
        a = jnp.where(masks[n_lev], a, 0.0)
        for lev in range(n_lev):
            e = jnp.exp(arg_ref[lev * length:(lev + 1) * length, sl])
            pr = _dot_nt((qp * e).astype(BF16), _stack2(kp * e, m0, m1))
            a = jnp.where(masks[lev], pr, a)
    return a.astype(BF16)


def _hgrn_finish(a, qp, kp, ip, base, sl, st_ref, seq, p, m0, m1, same_head):
    length = qp.shape[0]
    e_b = jnp.exp(base[0:length, sl])
    e_k = jnp.exp(base[length:2 * length, sl])
    st = st_ref[seq, p]
    o = _dot(jnp.concatenate([a, (qp * e_b).astype(BF16)], axis=1),
             jnp.concatenate([_stack2(ip, m0, m1), st.T.astype(BF16)], axis=0))
    upd = _dot_tn(ip.astype(BF16), (kp * e_k).astype(BF16))
    st_ref[seq, p] = st * e_b[length - 1:length, :] + jnp.where(same_head, upd, 0.0)
    return o


def _state_to_pairs(s0_ref, st_ref, n_seq):
    zero = jnp.zeros((DK_B, DV_B), F32)
    for r in range(n_seq):
        for p in range(N_PAIR):
            top = jnp.concatenate([s0_ref[r, 2 * p], zero], axis=1)
            bot = jnp.concatenate([zero, s0_ref[r, 2 * p + 1]], axis=1)
            st_ref[r, p] = jnp.concatenate([top, bot], axis=0).T


def _pairs_to_state(st_ref, s_ref, n_seq):
    for r in range(n_seq):
        for p in range(N_PAIR):
            d = st_ref[r, p].T
            s_ref[r, 2 * p] = d[:DK_B, :DV_B]
            s_ref[r, 2 * p + 1] = d[DK_B:, DV_B:]


def _hgrn_kernel(n_seq, n_chunk, length, carry_state, *refs):
    if carry_state:
        (q_ref, k_ref, i_ref, g_ref, f_ref, selb_ref, sell_ref, mask_ref, bd_ref, on_ref,
         o_ref, s_ref, st_ref, base_ref, arg_ref) = refs

        @pl.when(pl.program_id(1) == 0)
        def _():
            st_ref[...] = jnp.zeros_like(st_ref)
    else:
        (q_ref, k_ref, i_ref, g_ref, f_ref, s0_ref, selb_ref, sell_ref, mask_ref, bd_ref, on_ref,
         o_ref, s_ref, st_ref, base_ref, arg_ref) = refs
        _state_to_pairs(s0_ref, st_ref, n_seq)

    bd = bd_ref[...]
    onorm = on_ref[...]
    n_items = n_seq * n_chunk
    mid = length // 2 - 1
    lane = lax.broadcasted_iota(I32, (length, LANES), 1)
    m0 = (lane < DK_B).astype(BF16)
    m1 = (lane >= DK_B).astype(BF16)
    rowp = lax.broadcasted_iota(I32, (LANES, LANES), 0) // DV_B
    colp = lax.broadcasted_iota(I32, (LANES, LANES), 1) // DK_B
    same_head = rowp == colp
    masks = [mask_ref[m] != 0 for m in range(mask_ref.shape[0])]

    def pieces(rows):
        return jnp.concatenate(_split3(f_ref[rows, :]), axis=0)

    def pair(ref, rows, p):
        return ref[rows, p * LANES:(p + 1) * LANES].astype(F32)

    span = jnp.float32(0.0)
    for it in range(n_items):
        base = _dot(selb_ref[...], pieces(slice(it * length, (it + 1) * length)))
        base_ref[it * 2 * length:(it + 1) * 2 * length, :] = base
        b_mid = base[mid:mid + 1]
        span = jnp.maximum(span, jnp.max(base[0:1] - b_mid))
        span = jnp.maximum(span, jnp.max(b_mid - base[length - 1:length]))
    safe = span <= HGRN_SPAN_LIMIT

    def finish(rows, base, seq, a_of):
        outs = []
        for p in range(N_PAIR):
            sl = slice(p * LANES, (p + 1) * LANES)
            outs.append(_hgrn_finish(a_of(p), pair(q_ref, rows, p), pair(k_ref, rows, p),
                                     pair(i_ref, rows, p), base, sl, st_ref, seq, p, m0, m1,
                                     same_head))
        o = jnp.concatenate(outs, axis=1)
        g = g_ref[rows, :].astype(F32)
        o = o * lax.rsqrt(_seg_mean(o * o, bd) + RMS_EPS) * onorm * (g * _sigmoid(g))
        o_ref[rows, :] = o.astype(BF16)

    @pl.when(safe)
    def _():
        for it in range(n_items):
            rows = slice(it * length, (it + 1) * length)
            base = base_ref[it * 2 * length:(it + 1) * 2 * length, :]

            def a_of(p, rows=rows, base=base):
                sl = slice(p * LANES, (p + 1) * LANES)
                return _hgrn_intra(pair(q_ref, rows, p), pair(k_ref, rows, p), base[0:length, sl],
                                   masks, m0, m1, None, sl)

            finish(rows, base, it // n_chunk, a_of)

    @pl.when(jnp.logical_not(safe))
    def _():
        def body(it, carry):
            rows = pl.ds(pl.multiple_of(it * length, length), length)
            arg_ref[...] = _dot(sell_ref[...], pieces(rows))
            base = base_ref[pl.ds(pl.multiple_of(it * 2 * length, 2 * length), 2 * length), :]
            seq = it // n_chunk if n_chunk > 1 and n_seq > 1 else (it if n_chunk == 1 else 0)

            def a_of(p):
                sl = slice(p * LANES, (p + 1) * LANES)
                return _hgrn_intra(pair(q_ref, rows, p), pair(k_ref, rows, p), base[0:length, sl],
                                   masks, m0, m1, arg_ref, sl)

            finish(rows, base, seq, a_of)
            return carry

        lax.fori_loop(0, n_items, body, 0)

    if carry_state:
        @pl.when(pl.program_id(1) == pl.num_programs(1) - 1)
        def _():
            _pairs_to_state(st_ref, s_ref, n_seq)
    else:
        _pairs_to_state(st_ref, s_ref, n_seq)


def _hgrn_scratch(n_seq, n_items, length, sell):
    return [pltpu.VMEM((n_seq, N_PAIR, LANES, LANES), F32),
            pltpu.VMEM((n_items * 2 * length, W_B), F32),
            pltpu.VMEM((sell.shape[0], W_B), F32)]


def _hgrn_prompt(qb, kb, ib, gb, logf, consts, bd, onorm_t, batch, seq):
    selb, sell, masks = consts
    steps = seq // ATT_ROWS
    n_chunk = ATT_ROWS // CHUNK
    row = pl.BlockSpec((ATT_ROWS, W_B), lambda b, j: (b * steps + j, 0))
    whole = lambda a: pl.BlockSpec(a.shape, lambda b, j: (0,) * a.ndim)
    return pl.pallas_call(
        functools.partial(_hgrn_kernel, 1, n_chunk, CHUNK, True),
        grid=(batch, steps),
        in_specs=[row, row, row, row, row, whole(selb), whole(sell), whole(masks), whole(bd),
                  whole(onorm_t)],
        out_specs=[row, pl.BlockSpec((1, N_HEADS_B, DK_B, DV_B), lambda b, j: (b, 0, 0, 0))],
        out_shape=[jax.ShapeDtypeStruct((batch * seq, V_B), BF16),
                   jax.ShapeDtypeStruct((batch, N_HEADS_B, DK_B, DV_B), F32)],
        scratch_shapes=_hgrn_scratch(1, n_chunk, CHUNK, sell),
        compiler_params=_cparams(("arbitrary", "arbitrary")),
        name="hgrn_prompt",
    )(qb, kb, ib, gb, logf, selb, sell, masks, bd, onorm_t)


def _hgrn_sample(qb, kb, ib, gb, logf, s0, consts, bd, onorm_t, row0, dec_batch, dec_seq):
    selb, sell, masks = consts
    rows = SAMPLE_SEQS * dec_seq
    row = pl.BlockSpec((rows, W_B), lambda b: (row0 // rows + b, 0))
    whole = lambda a: pl.BlockSpec(a.shape, lambda b: (0,) * a.ndim)
    state = pl.BlockSpec((SAMPLE_SEQS, N_HEADS_B, DK_B, DV_B), lambda b: (b, 0, 0, 0))
    return pl.pallas_call(
        functools.partial(_hgrn_kernel, SAMPLE_SEQS, 1, dec_seq, False),
        grid=(dec_batch // SAMPLE_SEQS,),
        in_specs=[row, row, row, row, row, state, whole(selb), whole(sell), whole(masks),
                  whole(bd), whole(onorm_t)],
        out_specs=[pl.BlockSpec((rows, V_B), lambda b: (b, 0)), state],
        out_shape=[jax.ShapeDtypeStruct((dec_batch * dec_seq, V_B), BF16),
                   jax.ShapeDtypeStruct((dec_batch, N_HEADS_B, DK_B, DV_B), F32)],
        scratch_shapes=_hgrn_scratch(SAMPLE_SEQS, SAMPLE_SEQS, dec_seq, sell),
        compiler_params=_cparams(("arbitrary",)),
        name="hgrn_sample",
    )(qb, kb, ib, gb, logf, s0, selb, sell, masks, bd, onorm_t)


def _outproj_kernel(prompt_tiles, x_ref, oap_ref, obp_ref, oas_ref, obs_ref, g1_ref, sh_ref, sc_ref,
                    ln_ref, w_ref, wr_ref, rb_ref, x1_out, h2_out, cls_out, ga_out, gb_out):
    def mix_of(oa_ref, ob_ref):
        return jnp.concatenate([oa_ref[hh] for hh in range(GQA_GROUP)] + [ob_ref[...]], axis=1)

    is_prompt = pl.program_id(0) < prompt_tiles
    mix = jnp.where(is_prompt, mix_of(oap_ref, obp_ref), mix_of(oas_ref, obs_ref))
    att = _dot(mix, w_ref[...])
    x1 = _gated_add(x_ref[...], g1_ref, att)
    x1_out[...] = x1
    h = _modulated_norm(x1, ln_ref, sc_ref, sh_ref)

    for s in range(SUBLANES):
        h2_out[pl.ds(s, ROWS, stride=SUBLANES), :] = h[:, s * LANES:(s + 1) * LANES]

    h_hi, h_lo = _split2(h)
    w_hi, w_lo = _split2(wr_ref[...])
    hi_terms = _dot_nt(jnp.concatenate([w_hi, w_lo], axis=0), h_hi)
    logits = hi_terms[0:N_EXPERTS] + hi_terms[N_EXPERTS:] + _dot_nt(w_hi, h_lo)
    aff = _sigmoid(logits)
    sel = aff + rb_ref[...]
    e_idx = lax.broadcasted_iota(I32, (N_EXPERTS, ROWS), 0)
    l_idx = lax.broadcasted_iota(I32, (EXPERTS_PER_GROUP, ROWS), 0)

    def first_argmax(vals, idx, big):
        top = jnp.max(vals, axis=0, keepdims=True)
        return top, jnp.min(jnp.where(vals == top, idx, big), axis=0, keepdims=True)

    g_scores = []
    for g in range(N_GROUPS):
        sg = sel[g * EXPERTS_PER_GROUP:(g + 1) * EXPERTS_PER_GROUP]
        m1, i1 = first_argmax(sg, l_idx, EXPERTS_PER_GROUP)
        m2 = jnp.max(jnp.where(l_idx == i1, -jnp.inf, sg), axis=0, keepdims=True)
        g_scores.append(m1 + m2)
    g_score = jnp.concatenate(g_scores, axis=0)
    g_idx = lax.broadcasted_iota(I32, (N_GROUPS, ROWS), 0)
    _, g_sel = first_argmax(g_score, g_idx, N_GROUPS)
    masked = jnp.where((e_idx // EXPERTS_PER_GROUP) == g_sel, sel, -jnp.inf)
    _, e1 = first_argmax(masked, e_idx, N_EXPERTS)
    _, e2 = first_argmax(jnp.where(e_idx == e1, -jnp.inf, masked), e_idx, N_EXPERTS)
    w1 = jnp.sum(jnp.where(e_idx == e1, aff, 0.0), axis=0, keepdims=True)
    w2 = jnp.sum(jnp.where(e_idx == e2, aff, 0.0), axis=0, keepdims=True)
    den = w1 + w2
    w1 = w1 / den
    w2 = w2 / den
    lo_first = e1 < e2
    gate_lo = jnp.where(lo_first, w1, w2)
    gate_hi = jnp.where(lo_first, w2, w1)
    a = jnp.minimum(e1, e2) - g_sel * EXPERTS_PER_GROUP
    b = jnp.maximum(e1, e2) - g_sel * EXPERTS_PER_GROUP
    pair = jnp.where(a == 0, b - 1, jnp.where(a == 1, 6 - b, 5))
    cls_out[...] = g_sel * PAIRS_PER_GROUP + pair
    ga_out[...] = jnp.where(a == 2, gate_hi, gate_lo)
    gb_out[...] = jnp.where(a == 2, gate_lo, gate_hi)


def _outproj(layer, x4, oap4, obp, oas4, obs, mod, ln2, w_out_b, w_router_t, router_bias_c,
             tiles_per_batch):
    n_tiles = x4.shape[0] // ROWS
    x_spec = pl.BlockSpec((ROWS, D_MODEL), lambda i: (i, 0))
    prompt_tiles = oap4.shape[0]
    last = prompt_tiles - 1
    lane_row = pl.BlockSpec((None, 1, ROWS), lambda i: (i, 0, 0))
    return pl.pallas_call(
        functools.partial(_outproj_kernel, prompt_tiles),
        grid=(n_tiles,),
        in_specs=[
            x_spec,
            pl.BlockSpec((None, GQA_GROUP, ROWS, LANES), lambda i: (jnp.minimum(i, last), 0, 0, 0)),
            pl.BlockSpec((ROWS, V_B), lambda i: (jnp.minimum(i, last), 0)),
            pl.BlockSpec((None, GQA_GROUP, ROWS, LANES), lambda i: (0, 0, 0, 0)),
            pl.BlockSpec((ROWS, V_B), lambda i: (0, 0)),
            _mod_spec(layer, 2, tiles_per_batch), _mod_spec(layer, 3, tiles_per_batch),
            _mod_spec(layer, 4, tiles_per_batch),
            pl.BlockSpec((None, 1, D_MODEL), lambda i: (layer, 0, 0)),
            pl.BlockSpec((None, Q_A + V_B, D_MODEL), lambda i: (layer, 0, 0)),
            pl.BlockSpec((N_EXPERTS, D_MODEL), lambda i: (0, 0)),
            pl.BlockSpec((N_EXPERTS, 1), lambda i: (0, 0)),
        ],
        out_specs=[
            x_spec,
            pl.BlockSpec((ROWS * SUBLANES, LANES), lambda i: (i, 0)),
            lane_row, lane_row, lane_row,
        ],
        out_shape=[
            jax.ShapeDtypeStruct(x4.shape, F32),
            jax.ShapeDtypeStruct((n_tiles * ROWS * SUBLANES, LANES), F32),
            jax.ShapeDtypeStruct((n_tiles, 1, ROWS), I32),
            jax.ShapeDtypeStruct((n_tiles, 1, ROWS), F32),
            jax.ShapeDtypeStruct((n_tiles, 1, ROWS), F32),
        ],
        compiler_params=_cparams(("arbitrary",)),
        name=f"outproj{layer}",
    )(x4, oap4, obp, oas4, obs, mod, mod, mod, ln2, w_out_b, w_router_t, router_bias_c)


def _rank_kernel(cls_ref, upper_ref, lower_ref, dest_ref, count_ref, rank_ref):
    n_tiles = cls_ref.shape[0]
    c_idx = lax.broadcasted_iota(I32, (GRP, ROWS), 0)
    upper = upper_ref[...]

    def rank_body(i, carry):
        onehot = c_idx == cls_ref[i]
        within = _dot(onehot.astype(BF16), upper)
        rank_ref[i] = jnp.sum(jnp.where(onehot, within + carry, 0.0), axis=0, keepdims=True)
        return carry + jnp.sum(onehot.astype(F32), axis=1, keepdims=True)

    count = lax.fori_loop(0, n_tiles, rank_body, jnp.zeros((GRP, 1), F32))
    count_ref[...] = jnp.broadcast_to(count, (GRP, LANES)).astype(I32)
    tiles = jnp.floor((count + (TM - 1)) * (1.0 / TM))
    first_tile = _dot(lower_ref[...], jnp.broadcast_to(tiles, (GRP, LANES)).astype(BF16))
    start = first_tile[:, 0:1] * TM

    def dest_body(i, carry):
        onehot = c_idx == cls_ref[i]
        dest = rank_ref[i] + jnp.sum(jnp.where(onehot, start, 0.0), axis=0, keepdims=True)
        dest_ref[i] = dest.astype(I32)
        return carry

    lax.fori_loop(0, n_tiles, dest_body, 0)


def _plan_kernel(n_tok, n_t, dest_ref, count_ref, src_ref, dst_ref, ea_ref, eb_ref, nv_ref, nt_ref):
    groups = TM // ROW_GROUP

    def fill(first_group, end_group):
        def body(p8, carry):
            tile_parity = lax.shift_right_logical(p8, groups.bit_length() - 1) & 1
            spare = n_tok + tile_parity * TM + (p8 & (groups - 1)) * ROW_GROUP
            for u in range(ROW_GROUP):
                src_ref[p8 * ROW_GROUP + u] = 0
                dst_ref[p8 * ROW_GROUP + u] = spare + u
            return carry
        lax.fori_loop(first_group, end_group, body, 0)

    tile = 0
    for c in range(N_CLASS):
        g, pi = divmod(c, PAIRS_PER_GROUP)
        a, b = PAIR_TABLE[pi]
        cnt = count_ref[c, 0]
        n_c = lax.shift_right_logical(cnt + (TM - 1), TM.bit_length() - 1)
        first = tile

        def mark(t, carry, g=g, a=a, b=b, cnt=cnt, first=first):
            ea_ref[t] = g * EXPERTS_PER_GROUP + a
            eb_ref[t] = g * EXPERTS_PER_GROUP + b
            nv_ref[t] = jnp.minimum(cnt - (t - first) * TM, TM)
            return carry

        lax.fori_loop(tile, tile + n_c, mark, 0)

        fill(lax.shift_right_logical(first * TM + cnt, GROUP_SHIFT), (first + n_c) * groups)
        tile = tile + n_c
    nt_ref[0] = tile

    def unused(t, carry):
        ea_ref[t] = 0
        eb_ref[t] = 0
        nv_ref[t] = 0
        return carry

    lax.fori_loop(tile, n_t, unused, 0)

    fill(tile * groups, n_t * groups)

    def place(t, carry):
        d = dest_ref[t]
        src_ref[d] = t
        dst_ref[d] = t
        return carry

    lax.fori_loop(0, n_tok, place, 0, unroll=8)


def _route_plan(cls3, upper, lower):
    n_tiles = cls3.shape[0]
    n_tok = n_tiles * ROWS
    n_t = (n_tok + N_CLASS * (TM - 1)) // TM + 2
    dest, count = pl.pallas_call(
        _rank_kernel,
        out_shape=[jax.ShapeDtypeStruct((n_tiles, 1, ROWS), I32),
                   jax.ShapeDtypeStruct((GRP, LANES), I32)],
        scratch_shapes=[pltpu.VMEM((n_tiles, 1, ROWS), F32)],
        name="moe_rank",
    )(cls3, upper, lower)
    smem = pl.BlockSpec(memory_space=pltpu.SMEM)
    return pl.pallas_call(
        functools.partial(_plan_kernel, n_tok, n_t),
        in_specs=[smem, smem],
        out_specs=[smem, smem, smem, smem, smem, smem],
        out_shape=[jax.ShapeDtypeStruct((n_t * TM,), I32),
                   jax.ShapeDtypeStruct((n_t * TM,), I32),
                   jax.ShapeDtypeStruct((n_t,), I32),
                   jax.ShapeDtypeStruct((n_t,), I32),
                   jax.ShapeDtypeStruct((n_t,), I32),
                   jax.ShapeDtypeStruct((1,), I32)],
        name="moe_plan",
    )(dest.reshape(n_tok), count)


def _tile_rows(ref, slot, r):
    return ref.at[pl.ds(pl.multiple_of((slot * TM + r) * SUBLANES, SUBLANES), SUBLANES)]


def _moe_kernel(n_tok, src_ref, dst_ref, ea_ref, eb_ref, nv_ref, nt_ref, ga_ref, gb_ref, h2_hbm,
                wga_ref, wua_ref, wda_ref, wgb_ref, wub_ref, wdb_ref, y_hbm, xbuf, ybuf, gbuf, gsem,
                ssem):
    i = pl.program_id(0)
    n_used = nt_ref[0]
    lane = lax.broadcasted_iota(I32, (1, LANES), 1)
    group_rows = ROW_GROUP * SUBLANES

    def n_groups(tile):
        return lax.shift_right_logical(nv_ref[tile] + (ROW_GROUP - 1), GROUP_SHIFT)

    def token_tile(hbm, tok):
        return hbm.at[pl.ds(pl.multiple_of(tok * SUBLANES, SUBLANES), SUBLANES)]

    def gather_row(tile, slot, r, priority):
        tok = src_ref[tile * TM + r]
        pltpu.make_async_copy(token_tile(h2_hbm, tok), _tile_rows(xbuf, slot, r),
                              gsem.at[slot]).start(priority=priority)
        gbuf[pl.ds(slot * TM + r, 1), :] = jnp.where(lane == 0, ga_ref[tok], gb_ref[tok])

    def scatter_row(tile, slot, r, priority):
        pltpu.make_async_copy(_tile_rows(ybuf, slot, r), token_tile(y_hbm, dst_ref[tile * TM + r]),
                              ssem.at[slot]).start(priority=priority)

    def row_loop(start_row, groups):
        def body(g, c):
            for u in range(ROW_GROUP):
                start_row(g * ROW_GROUP + u, u % 2)
            return c
        lax.fori_loop(0, groups, body, 0)

    def wait_groups(hbm, buf, slot, sem, groups, to_hbm):
        def body(g, c):
            rows = buf.at[pl.ds(pl.multiple_of((slot * TM + g * ROW_GROUP) * SUBLANES, group_rows),
                                group_rows)]
            block = hbm.at[pl.ds(0, group_rows)]
            copy = (pltpu.make_async_copy(rows, block, sem.at[slot]) if to_hbm
                    else pltpu.make_async_copy(block, rows, sem.at[slot]))
            copy.wait()
            return c
        lax.fori_loop(0, groups, body, 0)

    @pl.when(i == 0)
    def _():
        xbuf[...] = jnp.zeros_like(xbuf)
        gbuf[...] = jnp.zeros_like(gbuf)
        ybuf[...] = jnp.zeros_like(ybuf)
        spare = pltpu.make_async_copy(
            ybuf, y_hbm.at[pl.ds(n_tok * SUBLANES, 2 * TM * SUBLANES)], ssem.at[0])
        spare.start()
        spare.wait()
        for t in range(GATHER_AHEAD):
            @pl.when(t < n_used)
            def _():
                row_loop(lambda r, pr: gather_row(t, t, r, pr), n_groups(t))

    @pl.when(i + GATHER_AHEAD < n_used)
    def _():
        nxt = i + GATHER_AHEAD
        row_loop(lambda r, pr: gather_row(nxt, nxt % GATHER_SLOTS, r, pr), n_groups(nxt))

    @pl.when(jnp.logical_and(i >= 2, i - 2 < n_used))
    def _():
        wait_groups(y_hbm, ybuf, i % 2, ssem, n_groups(i - 2), True)

    @pl.when(i < n_used)
    def _():
        slot = i % 2
        xslot = i % GATHER_SLOTS
        groups = n_groups(i)
        wait_groups(h2_hbm, xbuf, xslot, gsem, groups, False)
        base = xslot * TM * SUBLANES
        xb = jnp.concatenate([_token_rows(xbuf, s, TM, base) for s in range(SUBLANES)],
                             axis=1).astype(BF16)
        gates = gbuf[pl.ds(xslot * TM, TM), :]
        y = jnp.zeros((TM, D_MODEL), F32)
        for e, (wg, wu, wd) in enumerate(((wga_ref, wua_ref, wda_ref), (wgb_ref, wub_ref, wdb_ref))):
            gt = _dot(xb, wg[...].astype(BF16))
            up = _dot(xb, wu[...].astype(BF16))
            act = gt * _sigmoid(gt) * up * gates[:, e:e + 1]
            y = y + _dot(act.astype(BF16), wd[...].astype(BF16))
        ybase = slot * TM * SUBLANES
        for s in range(SUBLANES):
            ybuf[pl.ds(ybase + s, TM, stride=SUBLANES), :] = y[:, s * LANES:(s + 1) * LANES]
        row_loop(lambda r, pr: scatter_row(i, slot, r, pr), groups)


def _moe(layer, h2, gate_a, gate_b, plan, w_gate, w_up, w_down):
    src, dst, ea, eb, nv, nt = plan
    n_t = ea.shape[0]
    n_tok = h2.shape[0] // SUBLANES

    def w_spec(shape, which):
        def index(i, src_ref, dst_ref, ea_ref, eb_ref, nv_ref, nt_ref, ga_ref, gb_ref):
            t = jnp.minimum(i, jnp.maximum(nt_ref[0] - 1, 0))
            e = (ea_ref, eb_ref)[which][t]
            return (layer, e, 0, 0)
        return pl.BlockSpec((None, None) + shape, index)

    gu = (D_MODEL, D_EXPERT)
    dn = (D_EXPERT, D_MODEL)
    grid_spec = pltpu.PrefetchScalarGridSpec(
        num_scalar_prefetch=8,
        grid=(n_t,),
        in_specs=[pl.BlockSpec(memory_space=pl.ANY),
                  w_spec(gu, 0), w_spec(gu, 0), w_spec(dn, 0),
                  w_spec(gu, 1), w_spec(gu, 1), w_spec(dn, 1)],
        out_specs=pl.BlockSpec(memory_space=pl.ANY),
        scratch_shapes=[pltpu.VMEM((GATHER_SLOTS * TM * SUBLANES, LANES), F32),
                        pltpu.VMEM((2 * TM * SUBLANES, LANES), F32),
                        pltpu.VMEM((GATHER_SLOTS * TM, LANES), F32),
                        pltpu.SemaphoreType.DMA((GATHER_SLOTS,)), pltpu.SemaphoreType.DMA((2,))],
    )
    return pl.pallas_call(
        functools.partial(_moe_kernel, n_tok),
        grid_spec=grid_spec,
        out_shape=jax.ShapeDtypeStruct(((n_tok + 2 * TM) * SUBLANES, LANES), F32),
        compiler_params=_cparams(("arbitrary",)),
        name=f"moe{layer}",
    )(src, dst, ea, eb, nv, nt, gate_a.reshape(n_tok), gate_b.reshape(n_tok), h2,
      w_gate, w_up, w_down, w_gate, w_up, w_down)


def kernel(x_prompt, x_sample, cache_k, cache_v, state_hgrn, c_prompt, c_sample, w_ada, b_ada,
           ln1, ln2, w_in, q_norm, k_norm, sinks, lb_param, o_norm, w_out, w_router, router_bias,
           w_e_gate, w_e_up, w_e_down):
    batch, seq, _ = x_prompt.shape
    dec_batch, dec_seq, _ = x_sample.shape
    depth = w_ada.shape[0]
    assert dec_batch == GRP and dec_seq == SUB and seq % ROWS == 0
    tiles_per_batch = seq // ROWS
    n_prompt = batch * seq
    n_tok = n_prompt + dec_batch * dec_seq
    n_tiles = n_tok // ROWS
    prompt_tiles = n_prompt // ROWS
    n_cache = cache_k.shape[2]

    seg = np.arange(Q_A) // HEAD_DIM
    bd = jnp.asarray((seg[:, None] == seg[None, :]).astype(np.float32) / HEAD_DIM, BF16)

    def hgrn_consts(length):
        selb, sell, masks = _hgrn_consts(length)
        return jnp.asarray(selb, BF16), jnp.asarray(sell, BF16), jnp.asarray(masks)

    consts_p = hgrn_consts(CHUNK)
    consts_s = hgrn_consts(dec_seq)
    upper = jnp.asarray(np.triu(np.ones((ROWS, ROWS), np.float32), 1), BF16)
    lower = jnp.asarray(np.tril(np.ones((GRP, GRP), np.float32), -1), BF16)

    wq = w_in[:, :, :Q_A].reshape(depth, D_MODEL, N_KV_A, GQA_GROUP, HEAD_DIM)
    wq = jnp.swapaxes(wq, 2, 3).reshape(depth, D_MODEL, Q_A)
    w_in_b = jnp.concatenate([wq, w_in[:, :, Q_A:]], axis=-1).astype(BF16)
    wo = w_out[:, :Q_A].reshape(depth, N_KV_A, GQA_GROUP, HEAD_DIM, D_MODEL)
    wo = jnp.swapaxes(wo, 1, 2).reshape(depth, Q_A, D_MODEL)
    w_out_b = jnp.concatenate([wo, w_out[:, Q_A:]], axis=1).astype(BF16)
    q_norm_t = jnp.tile(q_norm, (1, N_HEADS_A)).reshape(depth, 1, Q_A)
    k_norm_t = jnp.tile(k_norm, (1, N_KV_A)).reshape(depth, 1, KV_A)
    o_norm_t = jnp.tile(o_norm, (1, N_HEADS_B)).reshape(depth, 1, V_B)
    ln1_r = ln1.reshape(depth, 1, D_MODEL)
    ln2_r = ln2.reshape(depth, 1, D_MODEL)
    w_router_t = w_router.T
    router_bias_c = router_bias.reshape(N_EXPERTS, 1)

    mod = _adaln(c_prompt, c_sample, w_ada, b_ada).reshape(depth, batch + 1, GRP, N_MOD * D_MODEL)

    cache_k2 = cache_k.reshape(depth, dec_batch, n_cache, KV_A)
    cache_v2 = cache_v.reshape(depth, dec_batch, n_cache, KV_A)

    src = (x_prompt, x_sample)
    kp, vp, sp, kss, vss, sss = [], [], [], [], [], []
    for l in range(depth):
        x, q4, k, v, qb, logf, kb, ib, gb, kwin, vwin = _inproj(
            l, src, n_tiles, prompt_tiles, mod, ln1_r, w_in_b, q_norm_t, k_norm_t, lb_param, bd,
            tiles_per_batch)

        oap4 = _attn_prompt(q4, k, v, sinks[l], batch, seq)
        oas4 = _attn_sample(q4, k, v, cache_k2[l], cache_v2[l], sinks[l], prompt_tiles,
                            n_prompt, dec_batch, dec_seq)
        obp, s_p = _hgrn_prompt(qb, kb, ib, gb, logf, consts_p, bd, o_norm_t[l], batch, seq)
        obs, s_s = _hgrn_sample(qb, kb, ib, gb, logf, state_hgrn[l], consts_s,
                                bd, o_norm_t[l], n_prompt, dec_batch, dec_seq)

        x1, h2, cls3, gate_a, gate_b = _outproj(l, x, oap4, obp, oas4, obs, mod, ln2_r, w_out_b,
                                                w_router_t, router_bias_c, tiles_per_batch)
        y = _moe(l, h2, gate_a, gate_b, _route_plan(cls3, upper, lower), w_e_gate, w_e_up,
                 w_e_down)
        src = (x1, y)

        kp.append(kwin.reshape(batch, WINDOW, N_KV_A, HEAD_DIM))
        vp.append(vwin.reshape(batch, WINDOW, N_KV_A, HEAD_DIM))
        sp.append(s_p)
        kss.append(k[n_prompt:].reshape(dec_batch, dec_seq, N_KV_A, HEAD_DIM))
        vss.append(v[n_prompt:].reshape(dec_batch, dec_seq, N_KV_A, HEAD_DIM))
        sss.append(s_s)

    yp, ys = _final(depth - 1, src[0], src[1], mod, batch, seq, tiles_per_batch)
    return (yp, ys, jnp.stack(kp), jnp.stack(vp), jnp.stack(sp),
            jnp.stack(kss), jnp.stack(vss), jnp.stack(sss))
```

```python
import functools

import numpy as np
import jax
import jax.numpy as jnp
from jax import lax
from jax.experimental import pallas as pl
from jax.experimental.pallas import tpu as pltpu

F32 = jnp.float32
BF16 = jnp.bfloat16
I32 = jnp.int32

D_MODEL = 1024
HEAD_DIM = 64
N_HEADS_A = 8
N_KV_A = 2
GQA_GROUP = N_HEADS_A // N_KV_A
N_HEADS_B = 8
DK_B = 64
DV_B = 64
Q_A = N_HEADS_A * HEAD_DIM
KV_A = N_KV_A * HEAD_DIM
W_B = N_HEADS_B * DK_B
V_B = N_HEADS_B * DV_B
CHUNK = 64
WINDOW = 128
N_EXPERTS = 16
N_GROUPS = 4
EXPERTS_PER_GROUP = 4
D_EXPERT = D_MODEL // 2
RMS_EPS = 1e-6
NEG_INF = -1e30
N_MOD = 6

LANES = 128
SUBLANES = 8
ROWS = 512
GRP = 32
SUB = ROWS // GRP
ATT_ROWS = 256
N_PAIR = N_HEADS_B // 2
SAMPLE_SEQS = 4
HGRN_SPAN_LIMIT = 80.0
TM = 256
PAIRS_PER_GROUP = 6
PAIR_TABLE = ((0, 1), (0, 2), (0, 3), (1, 3), (1, 2), (3, 2))
N_CLASS = N_GROUPS * PAIRS_PER_GROUP
ROW_GROUP = 8
GROUP_SHIFT = ROW_GROUP.bit_length() - 1
GATHER_AHEAD = 3
GATHER_SLOTS = GATHER_AHEAD + 1
VMEM_LIMIT = 56 * 1024 * 1024


def _cparams(sem):
    return pltpu.CompilerParams(dimension_semantics=sem, vmem_limit_bytes=VMEM_LIMIT)


def _dot(a, b):
    return jnp.dot(a, b, preferred_element_type=F32)


def _dot_nt(a, b):
    return lax.dot_general(a, b, (((1,), (1,)), ((), ())), preferred_element_type=F32)


def _dot_tn(a, b):
    return lax.dot_general(a, b, (((0,), (0,)), ((), ())), preferred_element_type=F32)


def _sigmoid(x):
    return 1.0 / (1.0 + jnp.exp(-x))


def _split2(x):
    hi = x.astype(BF16)
    lo = (x - hi.astype(F32)).astype(BF16)
    return hi, lo


def _split3(x):
    hi = x.astype(BF16)
    r = x - hi.astype(F32)
    mid = r.astype(BF16)
    lo = (r - mid.astype(F32)).astype(BF16)
    return hi, mid, lo


def _seg_mean(sq, bd):
    return _dot(sq.astype(BF16), bd)


def _head(x, j):
    return x[:, j * HEAD_DIM:(j + 1) * HEAD_DIM]


def _token_rows(ref, s, n, base=0):
    return ref[pl.ds(base + s, n, stride=SUBLANES), :]


def _adaln_kernel(n_prompt, c_ref, w_ref, b_ref, o_ref):
    c = c_ref[...]
    a = c * _sigmoid(c)
    a_hi, a_lo = _split2(a)
    w_hi, w_lo = _split2(w_ref[...])
    acc = _dot(a_hi, w_hi) + _dot(a_lo, w_hi) + _dot(a_hi, w_lo) + b_ref[...]
    for b in range(n_prompt):
        o_ref[b * GRP:(b + 1) * GRP, :] = jnp.broadcast_to(acc[b:b + 1], (GRP, acc.shape[1]))
    o_ref[n_prompt * GRP:, :] = acc[n_prompt:]


def _adaln(c_prompt, c_sample, w_ada, b_ada):
    depth = w_ada.shape[0]
    n_prompt = c_prompt.shape[0]
    c_all = jnp.concatenate([c_prompt, c_sample], axis=0)
    rows = n_prompt * GRP + c_sample.shape[0]
    tn = D_MODEL
    return pl.pallas_call(
        functools.partial(_adaln_kernel, n_prompt),
        grid=(depth, N_MOD * D_MODEL // tn),
        in_specs=[
            pl.BlockSpec(c_all.shape, lambda l, n: (0, 0)),
            pl.BlockSpec((None, D_MODEL, tn), lambda l, n: (l, 0, n)),
            pl.BlockSpec((None, 1, tn), lambda l, n: (l, 0, n)),
        ],
        out_specs=pl.BlockSpec((None, rows, tn), lambda l, n: (l, 0, n)),
        out_shape=jax.ShapeDtypeStruct((depth, rows, N_MOD * D_MODEL), F32),
        compiler_params=_cparams(("arbitrary", "arbitrary")),
        name="adaln",
    )(c_all, w_ada, b_ada.reshape(depth, 1, N_MOD * D_MODEL))


def _mod_spec(layer, j, tiles_per_batch):
    return pl.BlockSpec((None, None, GRP, D_MODEL),
                        lambda i, *_: (layer, i // tiles_per_batch, 0, j))


def _by_group(fn, xs, mod_refs):
    outs = []
    for g in range(GRP):
        rows = slice(g * SUB, (g + 1) * SUB)
        outs.append(fn(*[x[rows] for x in xs], *[m[g:g + 1, :] for m in mod_refs]))
    return jnp.concatenate(outs, axis=0)


def _gated_add(x, gate_ref, y):
    return _by_group(lambda xg, yg, gg: xg + gg * yg, (x, y), (gate_ref,))


def _modulated_norm(x, ln_ref, sc_ref, sh_ref):
    ms = jnp.mean(x * x, axis=-1, keepdims=True)
    xn = x * lax.rsqrt(ms + RMS_EPS)
    scale = ln_ref[...] * (1.0 + sc_ref[...])
    return _by_group(lambda xg, sc, sh: xg * sc + sh, (xn,), (scale, sh_ref))


def _prompt_tile_spec(prompt_tiles, tiles_per_batch):
    def index(i):
        t = jnp.minimum(i, prompt_tiles - 1)
        return (t // tiles_per_batch, t % tiles_per_batch, 0)
    return pl.BlockSpec((None, ROWS, D_MODEL), index)


def _moe_output_tile(y_ref):
    return jnp.concatenate([_token_rows(y_ref, s, ROWS) for s in range(SUBLANES)], axis=1)


def _inproj_kernel(layer, prompt_tiles, n_src, *refs):
    src = refs[:n_src]
    sh_ref, sc_ref, ln_ref, w_ref, qn_ref, kn_ref, lbp_ref, bd_ref = refs[n_src:n_src + 8]
    (x_out, q_out, k_out, v_out, qb_out, logf_out, kb_out, ib_out, gb_out,
     kwin_out, vwin_out) = refs[n_src + 8:]
    if layer == 0:
        xp_ref, xs_ref = src
        x = jnp.where(pl.program_id(0) < prompt_tiles, xp_ref[...],
                      xs_ref[...].reshape(ROWS, D_MODEL))
    else:
        x1_ref, y_ref, g2_ref = src
        x = _gated_add(x1_ref[...], g2_ref, _moe_output_tile(y_ref))
    x_out[...] = x
    hb = _modulated_norm(x, ln_ref, sc_ref, sh_ref).astype(BF16)
    bd = bd_ref[...]

    c0 = Q_A + 2 * KV_A
    c1 = c0 + 2 * W_B
    qa = _dot(hb, w_ref[:, 0:Q_A])
    kv = _dot(hb, w_ref[:, Q_A:c0])
    qb = _dot(hb, w_ref[:, c0:c0 + W_B])
    z = _dot(hb, w_ref[:, c0 + W_B:c1])
    ib = _dot(hb, w_ref[:, c1:c1 + V_B])
    gb = _dot(hb, w_ref[:, c1 + V_B:c1 + 2 * V_B])

    qa = qa * lax.rsqrt(_seg_mean(qa * qa, bd) + RMS_EPS) * qn_ref[...]
    qa = (qa * (HEAD_DIM ** -0.5)).astype(BF16)
    for hh in range(GQA_GROUP):
        q_out[hh] = jnp.concatenate([_head(qa, g * GQA_GROUP + hh) for g in range(N_KV_A)], axis=1)

    ka = kv[:, :KV_A]
    ka = ka * lax.rsqrt(_seg_mean(ka * ka, bd[:KV_A, :KV_A]) + RMS_EPS) * kn_ref[...]
    k_out[...] = ka
    v_out[...] = kv[:, KV_A:]

    @pl.when(pl.program_id(0) < prompt_tiles)
    def _():
        kwin_out[...] = ka[ROWS - WINDOW:]
        vwin_out[...] = kv[ROWS - WINDOW:, KV_A:]

    qb_out[...] = (qb * (DK_B ** -0.5)).astype(BF16)
    ib_out[...] = ib.astype(BF16)
    gb_out[...] = gb.astype(BF16)

    p = lbp_ref[...]
    e = jnp.exp(p - jnp.max(p, axis=0, keepdims=True))
    sm = e / jnp.sum(e, axis=0, keepdims=True)
    cum0 = sm[0:1]
    cum = cum0
    for r in range(1, layer + 1):
        cum = cum + sm[r:r + 1]
    lb = cum - cum0

    pos = z >= 0.0
    t = jnp.exp(-jnp.abs(z))
    num = jnp.where(pos, jnp.log(1.0 + lb * t), jnp.maximum(jnp.log(lb + t), z))
    logf_out[...] = num - jnp.log(1.0 + t)
    kb_out[...] = ((1.0 - lb) * jnp.where(pos, t, 1.0) / (1.0 + t)).astype(BF16)


def _inproj(layer, src, n_tiles, prompt_tiles, mod, ln1, w_in_b, q_norm_t, k_norm_t, lb_param, bd,
            tiles_per_batch):
    depth = lb_param.shape[0]
    n = n_tiles * ROWS
    in_width = w_in_b.shape[-1]
    x_spec = pl.BlockSpec((ROWS, D_MODEL), lambda i: (i, 0))
    if layer == 0:
        src_specs = [_prompt_tile_spec(prompt_tiles, tiles_per_batch),
                     pl.BlockSpec((GRP, SUB, D_MODEL), lambda i: (0, 0, 0))]
        src_args = list(src)
    else:
        src_specs = [x_spec, pl.BlockSpec((ROWS * SUBLANES, LANES), lambda i: (i, 0)),
                     _mod_spec(layer - 1, 5, tiles_per_batch)]
        src_args = list(src) + [mod]
    row_spec = lambda w: pl.BlockSpec((ROWS, w), lambda i: (i, 0))
    win_spec = pl.BlockSpec(
        (None, WINDOW, KV_A), lambda i: (jnp.minimum(i, prompt_tiles - 1) // tiles_per_batch, 0, 0))
    return pl.pallas_call(
        functools.partial(_inproj_kernel, layer, prompt_tiles, len(src_specs)),
        grid=(n_tiles,),
        in_specs=src_specs + [
            _mod_spec(layer, 0, tiles_per_batch), _mod_spec(layer, 1, tiles_per_batch),
            pl.BlockSpec((None, 1, D_MODEL), lambda i: (layer, 0, 0)),
            pl.BlockSpec((None, D_MODEL, in_width), lambda i: (layer, 0, 0)),
            pl.BlockSpec((None, 1, Q_A), lambda i: (layer, 0, 0)),
            pl.BlockSpec((None, 1, KV_A), lambda i: (layer, 0, 0)),
            pl.BlockSpec((depth, W_B), lambda i: (0, 0)),
            pl.BlockSpec((Q_A, Q_A), lambda i: (0, 0)),
        ],
        out_specs=[
            x_spec,
            pl.BlockSpec((None, GQA_GROUP, ROWS, LANES), lambda i: (i, 0, 0, 0)),
            row_spec(KV_A), row_spec(KV_A),
            row_spec(W_B), row_spec(W_B), row_spec(W_B), row_spec(V_B), row_spec(V_B),
            win_spec, win_spec,
        ],
        out_shape=[
            jax.ShapeDtypeStruct((n, D_MODEL), F32),
            jax.ShapeDtypeStruct((n_tiles, GQA_GROUP, ROWS, LANES), BF16),
            jax.ShapeDtypeStruct((n, KV_A), F32),
            jax.ShapeDtypeStruct((n, KV_A), F32),
            jax.ShapeDtypeStruct((n, W_B), BF16),
            jax.ShapeDtypeStruct((n, W_B), F32),
            jax.ShapeDtypeStruct((n, W_B), BF16),
            jax.ShapeDtypeStruct((n, V_B), BF16),
            jax.ShapeDtypeStruct((n, V_B), BF16),
            jax.ShapeDtypeStruct((prompt_tiles // tiles_per_batch, WINDOW, KV_A), F32),
            jax.ShapeDtypeStruct((prompt_tiles // tiles_per_batch, WINDOW, KV_A), F32),
        ],
        compiler_params=_cparams(("arbitrary",)),
        name=f"inproj{layer}",
    )(*src_args, mod, mod, ln1, w_in_b, q_norm_t, k_norm_t, lb_param, bd)


def _final_kernel(prompt_tiles, x1_ref, y_ref, g2_ref, yp_ref, ys_ref):
    i = pl.program_id(0)
    x = _gated_add(x1_ref[...], g2_ref, _moe_output_tile(y_ref))

    @pl.when(i < prompt_tiles)
    def _():
        yp_ref[...] = x

    @pl.when(i >= prompt_tiles)
    def _():
        ys_ref[...] = x.reshape(GRP, SUB, D_MODEL)


def _final(layer, x1, y, mod, batch, seq, tiles_per_batch):
    n_tiles = x1.shape[0] // ROWS
    prompt_tiles = batch * tiles_per_batch
    return pl.pallas_call(
        functools.partial(_final_kernel, prompt_tiles),
        grid=(n_tiles,),
        in_specs=[pl.BlockSpec((ROWS, D_MODEL), lambda i: (i, 0)),
                  pl.BlockSpec((ROWS * SUBLANES, LANES), lambda i: (i, 0)),
                  _mod_spec(layer, 5, tiles_per_batch)],
        out_specs=[_prompt_tile_spec(prompt_tiles, tiles_per_batch),
                   pl.BlockSpec((GRP, SUB, D_MODEL), lambda i: (0, 0, 0))],
        out_shape=[jax.ShapeDtypeStruct((batch, seq, D_MODEL), F32),
                   jax.ShapeDtypeStruct((GRP, SUB, D_MODEL), F32)],
        compiler_params=_cparams(("arbitrary",)),
        name="final",
    )(x1, y, mod)


def _attn_tables(sinks_ref, q_len, n_k, k_off):
    n_q = GQA_GROUP * q_len
    row = lax.broadcasted_iota(I32, (n_q, n_k), 0)
    col = lax.broadcasted_iota(I32, (n_q, n_k), 1)
    dist = jnp.abs((row % q_len) + k_off - col).astype(F32)
    hh_col = lax.broadcasted_iota(I32, (n_q, 1), 0) // q_len
    bias, sinks = [], []
    for g in range(N_KV_A):
        head = (g * GQA_GROUP + 1 + hh_col).astype(F32)
        bias.append(jnp.exp2(-8.0 * head / N_HEADS_A) * dist)
        sink = jnp.zeros((n_q, 1), F32)
        for j in range(GQA_GROUP):
            sink = jnp.where(hh_col == j, sinks_ref[g * GQA_GROUP + j], sink)
        sinks.append(sink)
    return bias, sinks


def _attend(q, k, v, bias, sinks, first_valid=None):
    n_q = q.shape[0]
    n_k = k.shape[0]
    lane = lax.broadcasted_iota(I32, (n_k, LANES), 1)
    out = jnp.zeros((n_q, LANES), F32)
    for g in range(N_KV_A):
        in_g = (lane // HEAD_DIM) == g
        kg = jnp.where(in_g, k, 0.0).astype(BF16)
        vg = jnp.where(in_g, v, 0.0).astype(BF16)
        s = _dot_nt(q, kg) - bias[g]
        if first_valid is not None:
            col = lax.broadcasted_iota(I32, (n_q, n_k), 1)
            s = jnp.where(col >= first_valid, s, NEG_INF)
        m = jnp.maximum(jnp.max(s, axis=-1, keepdims=True), sinks[g])
        p = jnp.exp(s - m)
        denom = jnp.sum(p, axis=-1, keepdims=True) + jnp.exp(sinks[g] - m)
        out = out + _dot(p.astype(BF16), vg) / denom
    return out


def _attn_prompt_kernel(sinks_ref, q_ref, kp_ref, kc_ref, vp_ref, vc_ref, o_ref):
    j = pl.program_id(1)
    kcat = jnp.concatenate([kp_ref[...], kc_ref[...]], axis=0)
    vcat = jnp.concatenate([vp_ref[...], vc_ref[...]], axis=0)
    bias, sinks = _attn_tables(sinks_ref, CHUNK, WINDOW + CHUNK, WINDOW)

    for c in range(ATT_ROWS // CHUNK):
        q = q_ref[:, c * CHUNK:(c + 1) * CHUNK, :].reshape(GQA_GROUP * CHUNK, LANES)
        k = kcat[c * CHUNK:c * CHUNK + WINDOW + CHUNK]
        v = vcat[c * CHUNK:c * CHUNK + WINDOW + CHUNK]
        first_valid = WINDOW - c * CHUNK - j * ATT_ROWS
        out = _attend(q, k, v, bias, sinks, first_valid)
        o_ref[:, c * CHUNK:(c + 1) * CHUNK, :] = out.reshape(GQA_GROUP, CHUNK, LANES).astype(BF16)


def _attn_prompt(q4, k, v, sinks, batch, seq):
    steps = seq // ATT_ROWS
    per_tile = ROWS // ATT_ROWS
    win_blocks = ATT_ROWS // WINDOW
    n_tiles = batch * seq // ROWS
    cur = pl.BlockSpec((ATT_ROWS, KV_A), lambda b, j: (b * steps + j, 0))
    prev = pl.BlockSpec(
        (WINDOW, KV_A), lambda b, j: (b * steps * win_blocks + jnp.maximum(j * win_blocks - 1, 0), 0))
    qo = pl.BlockSpec((None, GQA_GROUP, ATT_ROWS, LANES),
                      lambda b, j: ((b * steps + j) // per_tile, 0, j % per_tile, 0))
    return pl.pallas_call(
        _attn_prompt_kernel,
        grid=(batch, steps),
        in_specs=[pl.BlockSpec(memory_space=pltpu.SMEM), qo, prev, cur, prev, cur],
        out_specs=qo,
        out_shape=jax.ShapeDtypeStruct((n_tiles, GQA_GROUP, ROWS, LANES), BF16),
        compiler_params=_cparams(("arbitrary", "arbitrary")),
        name="attn_prompt",
    )(sinks, q4, k, k, v, v)


def _attn_sample_kernel(n_new, sinks_ref, q_ref, kc_ref, kn_ref, vc_ref, vn_ref, o_ref):
    n_cache = kc_ref.shape[1]
    bias, sinks = _attn_tables(sinks_ref, n_new, n_cache + n_new, n_cache)
    for r in range(SAMPLE_SEQS):
        rows = slice(r * n_new, (r + 1) * n_new)
        q = q_ref[:, rows, :].reshape(GQA_GROUP * n_new, LANES)
        k = jnp.concatenate([kc_ref[r], kn_ref[rows, :]], axis=0)
        v = jnp.concatenate([vc_ref[r], vn_ref[rows, :]], axis=0)
        out = _attend(q, k, v, bias, sinks)
        o_ref[:, rows, :] = out.reshape(GQA_GROUP, n_new, LANES).astype(BF16)


def _attn_sample(q4, k, v, cache_k, cache_v, sinks, tile, row0, dec_batch, dec_seq):
    n_cache = cache_k.shape[1]
    rows = SAMPLE_SEQS * dec_seq
    q_spec = pl.BlockSpec((None, GQA_GROUP, rows, LANES), lambda b: (tile, 0, b, 0))
    new = pl.BlockSpec((rows, KV_A), lambda b: (row0 // rows + b, 0))
    cache = pl.BlockSpec((SAMPLE_SEQS, n_cache, KV_A), lambda b: (b, 0, 0))
    return pl.pallas_call(
        functools.partial(_attn_sample_kernel, dec_seq),
        grid=(dec_batch // SAMPLE_SEQS,),
        in_specs=[pl.BlockSpec(memory_space=pltpu.SMEM), q_spec, cache, new, cache, new],
        out_specs=pl.BlockSpec((None, GQA_GROUP, rows, LANES), lambda b: (0, 0, b, 0)),
        out_shape=jax.ShapeDtypeStruct((1, GQA_GROUP, dec_batch * dec_seq, LANES), BF16),
        compiler_params=_cparams(("arbitrary",)),
        name="attn_sample",
    )(sinks, q4, cache_k, k, cache_v, v)


def _level_sizes(length):
    sizes = []
    m = length // 2
    while m >= 1:
        sizes.append(m)
        m //= 2
    return sizes


def _hgrn_consts(length):
    t = np.arange(length)
    tri = (t[None, :] <= t[:, None]).astype(np.float32)
    after = (t[None, :] > t[:, None]).astype(np.float32)
    levels = []
    masks = []
    s = np.tile(t, 2)[None, :]
    tt = t[:, None]
    for m in _level_sizes(length):
        blk = t // m
        same = blk[None, :] == blk[:, None]
        q_rows = (blk % 2 == 1)[:, None]
        sel = np.where(q_rows, same & (t[None, :] <= t[:, None]), same & (t[None, :] > t[:, None]))
        levels.append(sel.astype(np.float32))
        masks.append(((tt // m) % 2 == 1) & ((s // m) == (tt // m) - 1))
    masks.append(s == tt)
    masks.append(s <= tt)
    base = np.concatenate([tri, after], axis=0)
    lev = np.concatenate(levels, axis=0)
    return (np.concatenate([base] * 3, axis=1), np.concatenate([lev] * 3, axis=1),
            np.stack(masks).astype(np.int32))


def _stack2(x, m0, m1):
    xb = x.astype(BF16)
    return jnp.concatenate([xb * m0, xb * m1], axis=0)


def _hgrn_intra(qp, kp, b, masks, m0, m1, arg_ref, sl):
    length = qp.shape[0]
    n_lev = len(_level_sizes(length))
    if arg_ref is None:
        mid = length // 2 - 1
        rel = b - b[mid:mid + 1]
        a = _dot_nt((qp * jnp.exp(rel)).astype(BF16), _stack2(kp * jnp.exp(-rel), m0, m1))
        a = jnp.where(masks[n_lev + 1], a, 0.0)
    else:
        a = _dot_nt(qp.astype(BF16), _stack2(kp, m0, m1))
        a = jnp.where(masks[n_lev], a, 0.0)
        for lev in range(n_lev):
            e = jnp.exp(arg_ref[lev * length:(lev + 1) * length, sl])
            pr = _dot_nt((qp * e).astype(BF16), _stack2(kp * e, m0, m1))
            a = jnp.where(masks[lev], pr, a)
    return a.astype(BF16)


def _hgrn_finish(a, qp, kp, ip, base, sl, st_ref, seq, p, m0, m1, same_head):
    length = qp.shape[0]
    e_b = jnp.exp(base[0:length, sl])
    e_k = jnp.exp(base[length:2 * length, sl])
    st = st_ref[seq, p]
    o = _dot(jnp.concatenate([a, (qp * e_b).astype(BF16)], axis=1),
             jnp.concatenate([_stack2(ip, m0, m1), st.T.astype(BF16)], axis=0))
    upd = _dot_tn(ip.astype(BF16), (kp * e_k).astype(BF16))
    st_ref[seq, p] = st * e_b[length - 1:length, :] + jnp.where(same_head, upd, 0.0)
    return o


def _state_to_pairs(s0_ref, st_ref, n_seq):
    zero = jnp.zeros((DK_B, DV_B), F32)
    for r in range(n_seq):
        for p in range(N_PAIR):
            top = jnp.concatenate([s0_ref[r, 2 * p], zero], axis=1)
            bot = jnp.concatenate([zero, s0_ref[r, 2 * p + 1]], axis=1)
            st_ref[r, p] = jnp.concatenate([top, bot], axis=0).T


def _pairs_to_state(st_ref, s_ref, n_seq):
    for r in range(n_seq):
        for p in range(N_PAIR):
            d = st_ref[r, p].T
            s_ref[r, 2 * p] = d[:DK_B, :DV_B]
            s_ref[r, 2 * p + 1] = d[DK_B:, DV_B:]


def _hgrn_kernel(n_seq, n_chunk, length, carry_state, *refs):
    if carry_state:
        (q_ref, k_ref, i_ref, g_ref, f_ref, selb_ref, sell_ref, mask_ref, bd_ref, on_ref,
         o_ref, s_ref, st_ref, base_ref, arg_ref) = refs

        @pl.when(pl.program_id(1) == 0)
        def _():
            st_ref[...] = jnp.zeros_like(st_ref)
    else:
        (q_ref, k_ref, i_ref, g_ref, f_ref, s0_ref, selb_ref, sell_ref, mask_ref, bd_ref, on_ref,
         o_ref, s_ref, st_ref, base_ref, arg_ref) = refs
        _state_to_pairs(s0_ref, st_ref, n_seq)

    bd = bd_ref[...]
    onorm = on_ref[...]
    n_items = n_seq * n_chunk
    mid = length // 2 - 1
    lane = lax.broadcasted_iota(I32, (length, LANES), 1)
    m0 = (lane < DK_B).astype(BF16)
    m1 = (lane >= DK_B).astype(BF16)
    rowp = lax.broadcasted_iota(I32, (LANES, LANES), 0) // DV_B
    colp = lax.broadcasted_iota(I32, (LANES, LANES), 1) // DK_B
    same_head = rowp == colp
    masks = [mask_ref[m] != 0 for m in range(mask_ref.shape[0])]

    def pieces(rows):
        return jnp.concatenate(_split3(f_ref[rows, :]), axis=0)

    def pair(ref, rows, p):
        return ref[rows, p * LANES:(p + 1) * LANES].astype(F32)

    span = jnp.float32(0.0)
    for it in range(n_items):
        base = _dot(selb_ref[...], pieces(slice(it * length, (it + 1) * length)))
        base_ref[it * 2 * length:(it + 1) * 2 * length, :] = base
        b_mid = base[mid:mid + 1]
        span = jnp.maximum(span, jnp.max(base[0:1] - b_mid))
        span = jnp.maximum(span, jnp.max(b_mid - base[length - 1:length]))
    safe = span <= HGRN_SPAN_LIMIT

    def finish(rows, base, seq, a_of):
        outs = []
        for p in range(N_PAIR):
            sl = slice(p * LANES, (p + 1) * LANES)
            outs.append(_hgrn_finish(a_of(p), pair(q_ref, rows, p), pair(k_ref, rows, p),
                                     pair(i_ref, rows, p), base, sl, st_ref, seq, p, m0, m1,
                                     same_head))
        o = jnp.concatenate(outs, axis=1)
        g = g_ref[rows, :].astype(F32)
        o = o * lax.rsqrt(_seg_mean(o * o, bd) + RMS_EPS) * onorm * (g * _sigmoid(g))
        o_ref[rows, :] = o.astype(BF16)

    @pl.when(safe)
    def _():
        for it in range(n_items):
            rows = slice(it * length, (it + 1) * length)
            base = base_ref[it * 2 * length:(it + 1) * 2 * length, :]

            def a_of(p, rows=rows, base=base):
                sl = slice(p * LANES, (p + 1) * LANES)
                return _hgrn_intra(pair(q_ref, rows, p), pair(k_ref, rows, p), base[0:length, sl],
                                   masks, m0, m1, None, sl)

            finish(rows, base, it // n_chunk, a_of)

    @pl.when(jnp.logical_not(safe))
    def _():
        def body(it, carry):
            rows = pl.ds(pl.multiple_of(it * length, length), length)
            arg_ref[...] = _dot(sell_ref[...], pieces(rows))
            base = base_ref[pl.ds(pl.multiple_of(it * 2 * length, 2 * length), 2 * length), :]
            seq = it // n_chunk if n_chunk > 1 and n_seq > 1 else (it if n_chunk == 1 else 0)

            def a_of(p):
                sl = slice(p * LANES, (p + 1) * LANES)
                return _hgrn_intra(pair(q_ref, rows, p), pair(k_ref, rows, p), base[0:length, sl],
                                   masks, m0, m1, arg_ref, sl)

            finish(rows, base, seq, a_of)
            return carry

        lax.fori_loop(0, n_items, body, 0)

    if carry_state:
        @pl.when(pl.program_id(1) == pl.num_programs(1) - 1)
        def _():
            _pairs_to_state(st_ref, s_ref, n_seq)
    else:
        _pairs_to_state(st_ref, s_ref, n_seq)


def _hgrn_scratch(n_seq, n_items, length, sell):
    return [pltpu.VMEM((n_seq, N_PAIR, LANES, LANES), F32),
            pltpu.VMEM((n_items * 2 * length, W_B), F32),
            pltpu.VMEM((sell.shape[0], W_B), F32)]


def _hgrn_prompt(qb, kb, ib, gb, logf, consts, bd, onorm_t, batch, seq):
    selb, sell, masks = consts
    steps = seq // ATT_ROWS
    n_chunk = ATT_ROWS // CHUNK
    row = pl.BlockSpec((ATT_ROWS, W_B), lambda b, j: (b * steps + j, 0))
    whole = lambda a: pl.BlockSpec(a.shape, lambda b, j: (0,) * a.ndim)
    return pl.pallas_call(
        functools.partial(_hgrn_kernel, 1, n_chunk, CHUNK, True),
        grid=(batch, steps),
        in_specs=[row, row, row, row, row, whole(selb), whole(sell), whole(masks), whole(bd),
                  whole(onorm_t)],
        out_specs=[row, pl.BlockSpec((1, N_HEADS_B, DK_B, DV_B), lambda b, j: (b, 0, 0, 0))],
        out_shape=[jax.ShapeDtypeStruct((batch * seq, V_B), BF16),
                   jax.ShapeDtypeStruct((batch, N_HEADS_B, DK_B, DV_B), F32)],
        scratch_shapes=_hgrn_scratch(1, n_chunk, CHUNK, sell),
        compiler_params=_cparams(("arbitrary", "arbitrary")),
        name="hgrn_prompt",
    )(qb, kb, ib, gb, logf, selb, sell, masks, bd, onorm_t)


def _hgrn_sample(qb, kb, ib, gb, logf, s0, consts, bd, onorm_t, row0, dec_batch, dec_seq):
    selb, sell, masks = consts
    rows = SAMPLE_SEQS * dec_seq
    row = pl.BlockSpec((rows, W_B), lambda b: (row0 // rows + b, 0))
    whole = lambda a: pl.BlockSpec(a.shape, lambda b: (0,) * a.ndim)
    state = pl.BlockSpec((SAMPLE_SEQS, N_HEADS_B, DK_B, DV_B), lambda b: (b, 0, 0, 0))
    return pl.pallas_call(
        functools.partial(_hgrn_kernel, SAMPLE_SEQS, 1, dec_seq, False),
        grid=(dec_batch // SAMPLE_SEQS,),
        in_specs=[row, row, row, row, row, state, whole(selb), whole(sell), whole(masks),
                  whole(bd), whole(onorm_t)],
        out_specs=[pl.BlockSpec((rows, V_B), lambda b: (b, 0)), state],
        out_shape=[jax.ShapeDtypeStruct((dec_batch * dec_seq, V_B), BF16),
                   jax.ShapeDtypeStruct((dec_batch, N_HEADS_B, DK_B, DV_B), F32)],
        scratch_shapes=_hgrn_scratch(SAMPLE_SEQS, SAMPLE_SEQS, dec_seq, sell),
        compiler_params=_cparams(("arbitrary",)),
        name="hgrn_sample",
    )(qb, kb, ib, gb, logf, s0, selb, sell, masks, bd, onorm_t)


def _outproj_kernel(prompt_tiles, x_ref, oap_ref, obp_ref, oas_ref, obs_ref, g1_ref, sh_ref, sc_ref,
                    ln_ref, w_ref, wr_ref, rb_ref, x1_out, h2_out, cls_out, ga_out, gb_out):
    def mix_of(oa_ref, ob_ref):
        heads = [_head(oa_ref[h % GQA_GROUP], h // GQA_GROUP) for h in range(N_HEADS_A)]
        return jnp.concatenate(heads + [ob_ref[...]], axis=1)

    is_prompt = pl.program_id(0) < prompt_tiles
    mix = jnp.where(is_prompt, mix_of(oap_ref, obp_ref), mix_of(oas_ref, obs_ref))
    att = _dot(mix, w_ref[...])
    x1 = _gated_add(x_ref[...], g1_ref, att)
    x1_out[...] = x1
    h = _modulated_norm(x1, ln_ref, sc_ref, sh_ref)

    for s in range(SUBLANES):
        h2_out[pl.ds(s, ROWS, stride=SUBLANES), :] = h[:, s * LANES:(s + 1) * LANES]

    h_hi, h_lo = _split2(h)
    w_hi, w_lo = _split2(wr_ref[...])
    hi_terms = _dot_nt(jnp.concatenate([w_hi, w_lo], axis=0), h_hi)
    logits = hi_terms[0:N_EXPERTS] + hi_terms[N_EXPERTS:] + _dot_nt(w_hi, h_lo)
    aff = _sigmoid(logits)
    sel = aff + rb_ref[...]
    e_idx = lax.broadcasted_iota(I32, (N_EXPERTS, ROWS), 0)
    l_idx = lax.broadcasted_iota(I32, (EXPERTS_PER_GROUP, ROWS), 0)

    def first_argmax(vals, idx, big):
        top = jnp.max(vals, axis=0, keepdims=True)
        return top, jnp.min(jnp.where(vals == top, idx, big), axis=0, keepdims=True)

    g_scores = []
    for g in range(N_GROUPS):
        sg = sel[g * EXPERTS_PER_GROUP:(g + 1) * EXPERTS_PER_GROUP]
        m1, i1 = first_argmax(sg, l_idx, EXPERTS_PER_GROUP)
        m2 = jnp.max(jnp.where(l_idx == i1, -jnp.inf, sg), axis=0, keepdims=True)
        g_scores.append(m1 + m2)
    g_score = jnp.concatenate(g_scores, axis=0)
    g_idx = lax.broadcasted_iota(I32, (N_GROUPS, ROWS), 0)
    _, g_sel = first_argmax(g_score, g_idx, N_GROUPS)
    masked = jnp.where((e_idx // EXPERTS_PER_GROUP) == g_sel, sel, -jnp.inf)
    _, e1 = first_argmax(masked, e_idx, N_EXPERTS)
    _, e2 = first_argmax(jnp.where(e_idx == e1, -jnp.inf, masked), e_idx, N_EXPERTS)
    w1 = jnp.sum(jnp.where(e_idx == e1, aff, 0.0), axis=0, keepdims=True)
    w2 = jnp.sum(jnp.where(e_idx == e2, aff, 0.0), axis=0, keepdims=True)
    den = w1 + w2
    w1 = w1 / den
    w2 = w2 / den
    lo_first = e1 < e2
    gate_lo = jnp.where(lo_first, w1, w2)
    gate_hi = jnp.where(lo_first, w2, w1)
    a = jnp.minimum(e1, e2) - g_sel * EXPERTS_PER_GROUP
    b = jnp.maximum(e1, e2) - g_sel * EXPERTS_PER_GROUP
    pair = jnp.where(a == 0, b - 1, jnp.where(a == 1, 6 - b, 5))
    cls_out[...] = g_sel * PAIRS_PER_GROUP + pair
    ga_out[...] = jnp.where(a == 2, gate_hi, gate_lo)
    gb_out[...] = jnp.where(a == 2, gate_lo, gate_hi)


def _outproj(layer, x4, oap4, obp, oas4, obs, mod, ln2, w_out_b, w_router_t, router_bias_c,
             tiles_per_batch):
    n_tiles = x4.shape[0] // ROWS
    x_spec = pl.BlockSpec((ROWS, D_MODEL), lambda i: (i, 0))
    prompt_tiles = oap4.shape[0]
    last = prompt_tiles - 1
    lane_row = pl.BlockSpec((None, 1, ROWS), lambda i: (i, 0, 0))
    return pl.pallas_call(
        functools.partial(_outproj_kernel, prompt_tiles),
        grid=(n_tiles,),
        in_specs=[
            x_spec,
            pl.BlockSpec((None, GQA_GROUP, ROWS, LANES), lambda i: (jnp.minimum(i, last), 0, 0, 0)),
            pl.BlockSpec((ROWS, V_B), lambda i: (jnp.minimum(i, last), 0)),
            pl.BlockSpec((None, GQA_GROUP, ROWS, LANES), lambda i: (0, 0, 0, 0)),
            pl.BlockSpec((ROWS, V_B), lambda i: (0, 0)),
            _mod_spec(layer, 2, tiles_per_batch), _mod_spec(layer, 3, tiles_per_batch),
            _mod_spec(layer, 4, tiles_per_batch),
            pl.BlockSpec((None, 1, D_MODEL), lambda i: (layer, 0, 0)),
            pl.BlockSpec((None, Q_A + V_B, D_MODEL), lambda i: (layer, 0, 0)),
            pl.BlockSpec((N_EXPERTS, D_MODEL), lambda i: (0, 0)),
            pl.BlockSpec((N_EXPERTS, 1), lambda i: (0, 0)),
        ],
        out_specs=[
            x_spec,
            pl.BlockSpec((ROWS * SUBLANES, LANES), lambda i: (i, 0)),
            lane_row, lane_row, lane_row,
        ],
        out_shape=[
            jax.ShapeDtypeStruct(x4.shape, F32),
            jax.ShapeDtypeStruct((n_tiles * ROWS * SUBLANES, LANES), F32),
            jax.ShapeDtypeStruct((n_tiles, 1, ROWS), I32),
            jax.ShapeDtypeStruct((n_tiles, 1, ROWS), F32),
            jax.ShapeDtypeStruct((n_tiles, 1, ROWS), F32),
        ],
        compiler_params=_cparams(("arbitrary",)),
        name=f"outproj{layer}",
    )(x4, oap4, obp, oas4, obs, mod, mod, mod, ln2, w_out_b, w_router_t, router_bias_c)


def _rank_kernel(cls_ref, upper_ref, lower_ref, dest_ref, count_ref, rank_ref):
    n_tiles = cls_ref.shape[0]
    c_idx = lax.broadcasted_iota(I32, (GRP, ROWS), 0)
    upper = upper_ref[...]

    def rank_body(i, carry):
        onehot = c_idx == cls_ref[i]
        within = _dot(onehot.astype(BF16), upper)
        rank_ref[i] = jnp.sum(jnp.where(onehot, within + carry, 0.0), axis=0, keepdims=True)
        return carry + jnp.sum(onehot.astype(F32), axis=1, keepdims=True)

    count = lax.fori_loop(0, n_tiles, rank_body, jnp.zeros((GRP, 1), F32))
    count_ref[...] = jnp.broadcast_to(count, (GRP, LANES)).astype(I32)
    tiles = jnp.floor((count + (TM - 1)) * (1.0 / TM))
    first_tile = _dot(lower_ref[...], jnp.broadcast_to(tiles, (GRP, LANES)).astype(BF16))
    start = first_tile[:, 0:1] * TM

    def dest_body(i, carry):
        onehot = c_idx == cls_ref[i]
        dest = rank_ref[i] + jnp.sum(jnp.where(onehot, start, 0.0), axis=0, keepdims=True)
        dest_ref[i] = dest.astype(I32)
        return carry

    lax.fori_loop(0, n_tiles, dest_body, 0)


def _plan_kernel(n_tok, n_t, dest_ref, count_ref, src_ref, dst_ref, ea_ref, eb_ref, nv_ref, nt_ref):
    groups = TM // ROW_GROUP

    def fill(first_group, end_group):
        def body(p8, carry):
            tile_parity = lax.shift_right_logical(p8, groups.bit_length() - 1) & 1
            spare = n_tok + tile_parity * TM + (p8 & (groups - 1)) * ROW_GROUP
            for u in range(ROW_GROUP):
                src_ref[p8 * ROW_GROUP + u] = 0
                dst_ref[p8 * ROW_GROUP + u] = spare + u
            return carry
        lax.fori_loop(first_group, end_group, body, 0)

    tile = 0
    for c in range(N_CLASS):
        g, pi = divmod(c, PAIRS_PER_GROUP)
        a, b = PAIR_TABLE[pi]
        cnt = count_ref[c, 0]
        n_c = lax.shift_right_logical(cnt + (TM - 1), TM.bit_length() - 1)
        first = tile

        def mark(t, carry, g=g, a=a, b=b, cnt=cnt, first=first):
            ea_ref[t] = g * EXPERTS_PER_GROUP + a
            eb_ref[t] = g * EXPERTS_PER_GROUP + b
            nv_ref[t] = jnp.minimum(cnt - (t - first) * TM, TM)
            return carry

        lax.fori_loop(tile, tile + n_c, mark, 0)

        fill(lax.shift_right_logical(first * TM + cnt, GROUP_SHIFT), (first + n_c) * groups)
        tile = tile + n_c
    nt_ref[0] = tile

    def unused(t, carry):
        ea_ref[t] = 0
        eb_ref[t] = 0
        nv_ref[t] = 0
        return carry

    lax.fori_loop(tile, n_t, unused, 0)

    fill(tile * groups, n_t * groups)

    def place(t, carry):
        d = dest_ref[t]
        src_ref[d] = t
        dst_ref[d] = t
        return carry

    lax.fori_loop(0, n_tok, place, 0, unroll=8)


def _route_plan(cls3, upper, lower):
    n_tiles = cls3.shape[0]
    n_tok = n_tiles * ROWS
    n_t = (n_tok + N_CLASS * (TM - 1)) // TM + 2
    dest, count = pl.pallas_call(
        _rank_kernel,
        out_shape=[jax.ShapeDtypeStruct((n_tiles, 1, ROWS), I32),
                   jax.ShapeDtypeStruct((GRP, LANES), I32)],
        scratch_shapes=[pltpu.VMEM((n_tiles, 1, ROWS), F32)],
        name="moe_rank",
    )(cls3, upper, lower)
    smem = pl.BlockSpec(memory_space=pltpu.SMEM)
    return pl.pallas_call(
        functools.partial(_plan_kernel, n_tok, n_t),
        in_specs=[smem, smem],
        out_specs=[smem, smem, smem, smem, smem, smem],
        out_shape=[jax.ShapeDtypeStruct((n_t * TM,), I32),
                   jax.ShapeDtypeStruct((n_t * TM,), I32),
                   jax.ShapeDtypeStruct((n_t,), I32),
                   jax.ShapeDtypeStruct((n_t,), I32),
                   jax.ShapeDtypeStruct((n_t,), I32),
                   jax.ShapeDtypeStruct((1,), I32)],
        name="moe_plan",
    )(dest.reshape(n_tok), count)


def _tile_rows(ref, slot, r):
    return ref.at[pl.ds(pl.multiple_of((slot * TM + r) * SUBLANES, SUBLANES), SUBLANES)]


def _moe_kernel(n_tok, src_ref, dst_ref, ea_ref, eb_ref, nv_ref, nt_ref, ga_ref, gb_ref, h2_hbm,
                wga_ref, wua_ref, wda_ref, wgb_ref, wub_ref, wdb_ref, y_hbm, xbuf, ybuf, gbuf, gsem,
                ssem):
    i = pl.program_id(0)
    n_used = nt_ref[0]
    lane = lax.broadcasted_iota(I32, (1, LANES), 1)
    group_rows = ROW_GROUP * SUBLANES

    def n_groups(tile):
        return lax.shift_right_logical(nv_ref[tile] + (ROW_GROUP - 1), GROUP_SHIFT)

    def token_tile(hbm, tok):
        return hbm.at[pl.ds(pl.multiple_of(tok * SUBLANES, SUBLANES), SUBLANES)]

    def gather_row(tile, slot, r, priority):
        tok = src_ref[tile * TM + r]
        pltpu.make_async_copy(token_tile(h2_hbm, tok), _tile_rows(xbuf, slot, r),
                              gsem.at[slot]).start(priority=priority)
        gbuf[pl.ds(slot * TM + r, 1), :] = jnp.where(lane == 0, ga_ref[tok], gb_ref[tok])

    def scatter_row(tile, slot, r, priority):
        pltpu.make_async_copy(_tile_rows(ybuf, slot, r), token_tile(y_hbm, dst_ref[tile * TM + r]),
                              ssem.at[slot]).start(priority=priority)

    def row_loop(start_row, groups):
        def body(g, c):
            for u in range(ROW_GROUP):
                start_row(g * ROW_GROUP + u, u % 2)
            return c
        lax.fori_loop(0, groups, body, 0)

    def wait_groups(hbm, buf, slot, sem, groups, to_hbm):
        def body(g, c):
            rows = buf.at[pl.ds(pl.multiple_of((slot * TM + g * ROW_GROUP) * SUBLANES, group_rows),
                                group_rows)]
            block = hbm.at[pl.ds(0, group_rows)]
            copy = (pltpu.make_async_copy(rows, block, sem.at[slot]) if to_hbm
                    else pltpu.make_async_copy(block, rows, sem.at[slot]))
            copy.wait()
            return c
        lax.fori_loop(0, groups, body, 0)

    @pl.when(i == 0)
    def _():
        xbuf[...] = jnp.zeros_like(xbuf)
        gbuf[...] = jnp.zeros_like(gbuf)
        ybuf[...] = jnp.zeros_like(ybuf)
        spare = pltpu.make_async_copy(
            ybuf, y_hbm.at[pl.ds(n_tok * SUBLANES, 2 * TM * SUBLANES)], ssem.at[0])
        spare.start()
        spare.wait()
        for t in range(GATHER_AHEAD):
            @pl.when(t < n_used)
            def _():
                row_loop(lambda r, pr: gather_row(t, t, r, pr), n_groups(t))

    @pl.when(i + GATHER_AHEAD < n_used)
    def _():
        nxt = i + GATHER_AHEAD
        row_loop(lambda r, pr: gather_row(nxt, nxt % GATHER_SLOTS, r, pr), n_groups(nxt))

    @pl.when(jnp.logical_and(i >= 2, i - 2 < n_used))
    def _():
        wait_groups(y_hbm, ybuf, i % 2, ssem, n_groups(i - 2), True)

    @pl.when(i < n_used)
    def _():
        slot = i % 2
        xslot = i % GATHER_SLOTS
        groups = n_groups(i)
        wait_groups(h2_hbm, xbuf, xslot, gsem, groups, False)
        base = xslot * TM * SUBLANES
        xb = jnp.concatenate([_token_rows(xbuf, s, TM, base) for s in range(SUBLANES)],
                             axis=1).astype(BF16)
        gates = gbuf[pl.ds(xslot * TM, TM), :]
        y = jnp.zeros((TM, D_MODEL), F32)
        for e, (wg, wu, wd) in enumerate(((wga_ref, wua_ref, wda_ref), (wgb_ref, wub_ref, wdb_ref))):
            gt = _dot(xb, wg[...].astype(BF16))
            up = _dot(xb, wu[...].astype(BF16))
            act = gt * _sigmoid(gt) * up * gates[:, e:e + 1]
            y = y + _dot(act.astype(BF16), wd[...].astype(BF16))
        ybase = slot * TM * SUBLANES
        for s in range(SUBLANES):
            ybuf[pl.ds(ybase + s, TM, stride=SUBLANES), :] = y[:, s * LANES:(s + 1) * LANES]
        row_loop(lambda r, pr: scatter_row(i, slot, r, pr), groups)


def _moe(layer, h2, gate_a, gate_b, plan, w_gate, w_up, w_down):
    src, dst, ea, eb, nv, nt = plan
    n_t = ea.shape[0]
    n_tok = h2.shape[0] // SUBLANES

    def w_spec(shape, which):
        def index(i, src_ref, dst_ref, ea_ref, eb_ref, nv_ref, nt_ref, ga_ref, gb_ref):
            t = jnp.minimum(i, jnp.maximum(nt_ref[0] - 1, 0))
            e = (ea_ref, eb_ref)[which][t]
            return (layer, e, 0, 0)
        return pl.BlockSpec((None, None) + shape, index)

    gu = (D_MODEL, D_EXPERT)
    dn = (D_EXPERT, D_MODEL)
    grid_spec = pltpu.PrefetchScalarGridSpec(
        num_scalar_prefetch=8,
        grid=(n_t,),
        in_specs=[pl.BlockSpec(memory_space=pl.ANY),
                  w_spec(gu, 0), w_spec(gu, 0), w_spec(dn, 0),
                  w_spec(gu, 1), w_spec(gu, 1), w_spec(dn, 1)],
        out_specs=pl.BlockSpec(memory_space=pl.ANY),
        scratch_shapes=[pltpu.VMEM((GATHER_SLOTS * TM * SUBLANES, LANES), F32),
                        pltpu.VMEM((2 * TM * SUBLANES, LANES), F32),
                        pltpu.VMEM((GATHER_SLOTS * TM, LANES), F32),
                        pltpu.SemaphoreType.DMA((GATHER_SLOTS,)), pltpu.SemaphoreType.DMA((2,))],
    )
    return pl.pallas_call(
        functools.partial(_moe_kernel, n_tok),
        grid_spec=grid_spec,
        out_shape=jax.ShapeDtypeStruct(((n_tok + 2 * TM) * SUBLANES, LANES), F32),
        compiler_params=_cparams(("arbitrary",)),
        name=f"moe{layer}",
    )(src, dst, ea, eb, nv, nt, gate_a.reshape(n_tok), gate_b.reshape(n_tok), h2,
      w_gate, w_up, w_down, w_gate, w_up, w_down)


def kernel(x_prompt, x_sample, cache_k, cache_v, state_hgrn, c_prompt, c_sample, w_ada, b_ada,
           ln1, ln2, w_in, q_norm, k_norm, sinks, lb_param, o_norm, w_out, w_router, router_bias,
           w_e_gate, w_e_up, w_e_down):
    batch, seq, _ = x_prompt.shape
    dec_batch, dec_seq, _ = x_sample.shape
    depth = w_ada.shape[0]
    assert dec_batch == GRP and dec_seq == SUB and seq % ROWS == 0
    tiles_per_batch = seq // ROWS
    n_prompt = batch * seq
    n_tok = n_prompt + dec_batch * dec_seq
    n_tiles = n_tok // ROWS
    prompt_tiles = n_prompt // ROWS
    n_cache = cache_k.shape[2]

    seg = np.arange(Q_A) // HEAD_DIM
    bd = jnp.asarray((seg[:, None] == seg[None, :]).astype(np.float32) / HEAD_DIM, BF16)

    def hgrn_consts(length):
        selb, sell, masks = _hgrn_consts(length)
        return jnp.asarray(selb, BF16), jnp.asarray(sell, BF16), jnp.asarray(masks)

    consts_p = hgrn_consts(CHUNK)
    consts_s = hgrn_consts(dec_seq)
    upper = jnp.asarray(np.triu(np.ones((ROWS, ROWS), np.float32), 1), BF16)
    lower = jnp.asarray(np.tril(np.ones((GRP, GRP), np.float32), -1), BF16)

    w_in_b = w_in.astype(BF16)
    w_out_b = w_out.astype(BF16)
    q_norm_t = jnp.tile(q_norm, (1, N_HEADS_A)).reshape(depth, 1, Q_A)
    k_norm_t = jnp.tile(k_norm, (1, N_KV_A)).reshape(depth, 1, KV_A)
    o_norm_t = jnp.tile(o_norm, (1, N_HEADS_B)).reshape(depth, 1, V_B)
    ln1_r = ln1.reshape(depth, 1, D_MODEL)
    ln2_r = ln2.reshape(depth, 1, D_MODEL)
    w_router_t = w_router.T
    router_bias_c = router_bias.reshape(N_EXPERTS, 1)

    mod = _adaln(c_prompt, c_sample, w_ada, b_ada).reshape(depth, batch + 1, GRP, N_MOD * D_MODEL)

    cache_k2 = cache_k.reshape(depth, dec_batch, n_cache, KV_A)
    cache_v2 = cache_v.reshape(depth, dec_batch, n_cache, KV_A)

    src = (x_prompt, x_sample)
    kp, vp, sp, kss, vss, sss = [], [], [], [], [], []
    for l in range(depth):
        x, q4, k, v, qb, logf, kb, ib, gb, kwin, vwin = _inproj(
            l, src, n_tiles, prompt_tiles, mod, ln1_r, w_in_b, q_norm_t, k_norm_t, lb_param, bd,
            tiles_per_batch)

        oap4 = _attn_prompt(q4, k, v, sinks[l], batch, seq)
        oas4 = _attn_sample(q4, k, v, cache_k2[l], cache_v2[l], sinks[l], prompt_tiles,
                            n_prompt, dec_batch, dec_seq)
        obp, s_p = _hgrn_prompt(qb, kb, ib, gb, logf, consts_p, bd, o_norm_t[l], batch, seq)
        obs, s_s = _hgrn_sample(qb, kb, ib, gb, logf, state_hgrn[l], consts_s,
                                bd, o_norm_t[l], n_prompt, dec_batch, dec_seq)

        x1, h2, cls3, gate_a, gate_b = _outproj(l, x, oap4, obp, oas4, obs, mod, ln2_r, w_out_b,
                                                w_router_t, router_bias_c, tiles_per_batch)
        y = _moe(l, h2, gate_a, gate_b, _route_plan(cls3, upper, lower), w_e_gate, w_e_up,
                 w_e_down)
        src = (x1, y)

        kp.append(kwin.reshape(batch, WINDOW, N_KV_A, HEAD_DIM))
        vp.append(vwin.reshape(batch, WINDOW, N_KV_A, HEAD_DIM))
        sp.append(s_p)
        kss.append(k[n_prompt:].reshape(dec_batch, dec_seq, N_KV_A, HEAD_DIM))
        vss.append(v[n_prompt:].reshape(dec_batch, dec_seq, N_KV_A, HEAD_DIM))
        sss.append(s_s)

    yp, ys = _final(depth - 1, src[0], src[1], mod, batch, seq, tiles_per_batch)
    return (yp, ys, jnp.stack(kp), jnp.stack(vp), jnp.stack(sp),
            jnp.stack(kss), jnp.stack(vss), jnp.stack(sss))
```

```python
import functools

import numpy as np
import jax
import jax.numpy as jnp
from jax import lax
from jax.experimental import pallas as pl
from jax.experimental.pallas import tpu as pltpu

F32 = jnp.float32
BF16 = jnp.bfloat16
I32 = jnp.int32

D_MODEL = 1024
HEAD_DIM = 64
N_HEADS_A = 8
N_KV_A = 2
GQA_GROUP = N_HEADS_A // N_KV_A
N_HEADS_B = 8
DK_B = 64
DV_B = 64
Q_A = N_HEADS_A * HEAD_DIM
KV_A = N_KV_A * HEAD_DIM
W_B = N_HEADS_B * DK_B
V_B = N_HEADS_B * DV_B
CHUNK = 64
WINDOW = 128
N_EXPERTS = 16
N_GROUPS = 4
EXPERTS_PER_GROUP = 4
D_EXPERT = D_MODEL // 2
RMS_EPS = 1e-6
NEG_INF = -1e30
N_MOD = 6

LANES = 128
SUBLANES = 8
ROWS = 512
GRP = 32
SUB = ROWS // GRP
ATT_ROWS = 512
HGRN_ROWS = 1024
N_PAIR = N_HEADS_B // 2
SAMPLE_SEQS = 4
HGRN_SPAN_LIMIT = 80.0
TM = 256
PAIRS_PER_GROUP = 6
PAIR_TABLE = ((0, 1), (0, 2), (0, 3), (1, 3), (1, 2), (3, 2))
N_CLASS = N_GROUPS * PAIRS_PER_GROUP
ROW_GROUP = 8
GROUP_SHIFT = ROW_GROUP.bit_length() - 1
GATHER_AHEAD = 3
GATHER_SLOTS = GATHER_AHEAD + 1
VMEM_LIMIT = 56 * 1024 * 1024


def _cparams(sem):
    return pltpu.CompilerParams(dimension_semantics=sem, vmem_limit_bytes=VMEM_LIMIT)


def _dot(a, b):
    return jnp.dot(a, b, preferred_element_type=F32)


def _dot_nt(a, b):
    return lax.dot_general(a, b, (((1,), (1,)), ((), ())), preferred_element_type=F32)


def _dot_tn(a, b):
    return lax.dot_general(a, b, (((0,), (0,)), ((), ())), preferred_element_type=F32)


def _sigmoid(x):
    return 1.0 / (1.0 + jnp.exp(-x))


def _split2(x):
    hi = x.astype(BF16)
    lo = (x - hi.astype(F32)).astype(BF16)
    return hi, lo


def _split3(x):
    hi = x.astype(BF16)
    r = x - hi.astype(F32)
    mid = r.astype(BF16)
    lo = (r - mid.astype(F32)).astype(BF16)
    return hi, mid, lo


def _seg_mean(sq, bd):
    return _dot(sq.astype(BF16), bd)


def _head(x, j):
    return x[:, j * HEAD_DIM:(j + 1) * HEAD_DIM]


def _token_rows(ref, s, n, base=0):
    return ref[pl.ds(base + s, n, stride=SUBLANES), :]


def _adaln_kernel(n_prompt, c_ref, w_ref, b_ref, o_ref):
    c = c_ref[...]
    a = c * _sigmoid(c)
    a_hi, a_lo = _split2(a)
    w_hi, w_lo = _split2(w_ref[...])
    acc = _dot(a_hi, w_hi) + _dot(a_lo, w_hi) + _dot(a_hi, w_lo) + b_ref[...]
    for b in range(n_prompt):
        o_ref[b * GRP:(b + 1) * GRP, :] = jnp.broadcast_to(acc[b:b + 1], (GRP, acc.shape[1]))
    o_ref[n_prompt * GRP:, :] = acc[n_prompt:]


def _adaln(c_prompt, c_sample, w_ada, b_ada):
    depth = w_ada.shape[0]
    n_prompt = c_prompt.shape[0]
    c_all = jnp.concatenate([c_prompt, c_sample], axis=0)
    rows = n_prompt * GRP + c_sample.shape[0]
    tn = D_MODEL
    return pl.pallas_call(
        functools.partial(_adaln_kernel, n_prompt),
        grid=(depth, N_MOD * D_MODEL // tn),
        in_specs=[
            pl.BlockSpec(c_all.shape, lambda l, n: (0, 0)),
            pl.BlockSpec((None, D_MODEL, tn), lambda l, n: (l, 0, n)),
            pl.BlockSpec((None, 1, tn), lambda l, n: (l, 0, n)),
        ],
        out_specs=pl.BlockSpec((None, rows, tn), lambda l, n: (l, 0, n)),
        out_shape=jax.ShapeDtypeStruct((depth, rows, N_MOD * D_MODEL), F32),
        compiler_params=_cparams(("arbitrary", "arbitrary")),
        name="adaln",
    )(c_all, w_ada, b_ada.reshape(depth, 1, N_MOD * D_MODEL))


def _mod_spec(layer, j, tiles_per_batch):
    return pl.BlockSpec((None, None, GRP, D_MODEL),
                        lambda i, *_: (layer, i // tiles_per_batch, 0, j))


def _by_group(fn, xs, mod_refs):
    outs = []
    for g in range(GRP):
        rows = slice(g * SUB, (g + 1) * SUB)
        outs.append(fn(*[x[rows] for x in xs], *[m[g:g + 1, :] for m in mod_refs]))
    return jnp.concatenate(outs, axis=0)


def _gated_add(x, gate_ref, y):
    return _by_group(lambda xg, yg, gg: xg + gg * yg, (x, y), (gate_ref,))


def _modulated_norm(x, ln_ref, sc_ref, sh_ref):
    ms = jnp.mean(x * x, axis=-1, keepdims=True)
    xn = x * lax.rsqrt(ms + RMS_EPS)
    scale = ln_ref[...] * (1.0 + sc_ref[...])
    return _by_group(lambda xg, sc, sh: xg * sc + sh, (xn,), (scale, sh_ref))


def _prompt_tile_spec(prompt_tiles, tiles_per_batch):
    def index(i):
        t = jnp.minimum(i, prompt_tiles - 1)
        return (t // tiles_per_batch, t % tiles_per_batch, 0)
    return pl.BlockSpec((None, ROWS, D_MODEL), index)


def _moe_output_tile(y_ref):
    return jnp.concatenate([_token_rows(y_ref, s, ROWS) for s in range(SUBLANES)], axis=1)


def _inproj_kernel(layer, prompt_tiles, n_src, *refs):
    src = refs[:n_src]
    sh_ref, sc_ref, ln_ref, w_ref, qn_ref, kn_ref, lbp_ref, bd_ref = refs[n_src:n_src + 8]
    (x_out, q_out, k_out, v_out, qb_out, logf_out, kb_out, ib_out, gb_out,
     kwin_out, vwin_out) = refs[n_src + 8:]
    if layer == 0:
        xp_ref, xs_ref = src
        x = jnp.where(pl.program_id(0) < prompt_tiles, xp_ref[...],
                      xs_ref[...].reshape(ROWS, D_MODEL))
    else:
        x1_ref, y_ref, g2_ref = src
        x = _gated_add(x1_ref[...], g2_ref, _moe_output_tile(y_ref))
    x_out[...] = x
    hb = _modulated_norm(x, ln_ref, sc_ref, sh_ref).astype(BF16)
    bd = bd_ref[...]

    c0 = Q_A + 2 * KV_A
    c1 = c0 + 2 * W_B
    qa = _dot(hb, w_ref[:, 0:Q_A])
    kv = _dot(hb, w_ref[:, Q_A:c0])
    qb = _dot(hb, w_ref[:, c0:c0 + W_B])
    z = _dot(hb, w_ref[:, c0 + W_B:c1])
    ib = _dot(hb, w_ref[:, c1:c1 + V_B])
    gb = _dot(hb, w_ref[:, c1 + V_B:c1 + 2 * V_B])

    qa = qa * lax.rsqrt(_seg_mean(qa * qa, bd) + RMS_EPS) * qn_ref[...]
    qa = (qa * (HEAD_DIM ** -0.5)).astype(BF16)
    for hh in range(GQA_GROUP):
        q_out[hh] = jnp.concatenate([_head(qa, g * GQA_GROUP + hh) for g in range(N_KV_A)], axis=1)

    ka = kv[:, :KV_A]
    ka = ka * lax.rsqrt(_seg_mean(ka * ka, bd[:KV_A, :KV_A]) + RMS_EPS) * kn_ref[...]
    k_out[...] = ka
    v_out[...] = kv[:, KV_A:]

    @pl.when(pl.program_id(0) < prompt_tiles)
    def _():
        kwin_out[...] = ka[ROWS - WINDOW:]
        vwin_out[...] = kv[ROWS - WINDOW:, KV_A:]

    qb_out[...] = (qb * (DK_B ** -0.5)).astype(BF16)
    ib_out[...] = ib.astype(BF16)
    gb_out[...] = gb.astype(BF16)

    p = lbp_ref[...]
    e = jnp.exp(p - jnp.max(p, axis=0, keepdims=True))
    sm = e / jnp.sum(e, axis=0, keepdims=True)
    cum0 = sm[0:1]
    cum = cum0
    for r in range(1, layer + 1):
        cum = cum + sm[r:r + 1]
    lb = cum - cum0

    pos = z >= 0.0
    t = jnp.exp(-jnp.abs(z))
    num = jnp.where(pos, jnp.log(1.0 + lb * t), jnp.maximum(jnp.log(lb + t), z))
    logf_out[...] = num - jnp.log(1.0 + t)
    kb_out[...] = ((1.0 - lb) * jnp.where(pos, t, 1.0) / (1.0 + t)).astype(BF16)


def _inproj(layer, src, n_tiles, prompt_tiles, mod, ln1, w_in_b, q_norm_t, k_norm_t, lb_param, bd,
            tiles_per_batch):
    depth = lb_param.shape[0]
    n = n_tiles * ROWS
    in_width = w_in_b.shape[-1]
    x_spec = pl.BlockSpec((ROWS, D_MODEL), lambda i: (i, 0))
    if layer == 0:
        src_specs = [_prompt_tile_spec(prompt_tiles, tiles_per_batch),
                     pl.BlockSpec((GRP, SUB, D_MODEL), lambda i: (0, 0, 0))]
        src_args = list(src)
    else:
        src_specs = [x_spec, pl.BlockSpec((ROWS * SUBLANES, LANES), lambda i: (i, 0)),
                     _mod_spec(layer - 1, 5, tiles_per_batch)]
        src_args = list(src) + [mod]
    row_spec = lambda w: pl.BlockSpec((ROWS, w), lambda i: (i, 0))
    win_spec = pl.BlockSpec(
        (None, WINDOW, KV_A), lambda i: (jnp.minimum(i, prompt_tiles - 1) // tiles_per_batch, 0, 0))
    return pl.pallas_call(
        functools.partial(_inproj_kernel, layer, prompt_tiles, len(src_specs)),
        grid=(n_tiles,),
        in_specs=src_specs + [
            _mod_spec(layer, 0, tiles_per_batch), _mod_spec(layer, 1, tiles_per_batch),
            pl.BlockSpec((None, 1, D_MODEL), lambda i: (layer, 0, 0)),
            pl.BlockSpec((None, D_MODEL, in_width), lambda i: (layer, 0, 0)),
            pl.BlockSpec((None, 1, Q_A), lambda i: (layer, 0, 0)),
            pl.BlockSpec((None, 1, KV_A), lambda i: (layer, 0, 0)),
            pl.BlockSpec((depth, W_B), lambda i: (0, 0)),
            pl.BlockSpec((Q_A, Q_A), lambda i: (0, 0)),
        ],
        out_specs=[
            x_spec,
            pl.BlockSpec((None, GQA_GROUP, ROWS, LANES), lambda i: (i, 0, 0, 0)),
            row_spec(KV_A), row_spec(KV_A),
            row_spec(W_B), row_spec(W_B), row_spec(W_B), row_spec(V_B), row_spec(V_B),
            win_spec, win_spec,
        ],
        out_shape=[
            jax.ShapeDtypeStruct((n, D_MODEL), F32),
            jax.ShapeDtypeStruct((n_tiles, GQA_GROUP, ROWS, LANES), BF16),
            jax.ShapeDtypeStruct((n, KV_A), F32),
            jax.ShapeDtypeStruct((n, KV_A), F32),
            jax.ShapeDtypeStruct((n, W_B), BF16),
            jax.ShapeDtypeStruct((n, W_B), F32),
            jax.ShapeDtypeStruct((n, W_B), BF16),
            jax.ShapeDtypeStruct((n, V_B), BF16),
            jax.ShapeDtypeStruct((n, V_B), BF16),
            jax.ShapeDtypeStruct((prompt_tiles // tiles_per_batch, WINDOW, KV_A), F32),
            jax.ShapeDtypeStruct((prompt_tiles // tiles_per_batch, WINDOW, KV_A), F32),
        ],
        compiler_params=_cparams(("arbitrary",)),
        name=f"inproj{layer}",
    )(*src_args, mod, mod, ln1, w_in_b, q_norm_t, k_norm_t, lb_param, bd)


def _final_kernel(prompt_tiles, x1_ref, y_ref, g2_ref, yp_ref, ys_ref):
    i = pl.program_id(0)
    x = _gated_add(x1_ref[...], g2_ref, _moe_output_tile(y_ref))

    @pl.when(i < prompt_tiles)
    def _():
        yp_ref[...] = x

    @pl.when(i >= prompt_tiles)
    def _():
        ys_ref[...] = x.reshape(GRP, SUB, D_MODEL)


def _final(layer, x1, y, mod, batch, seq, tiles_per_batch):
    n_tiles = x1.shape[0] // ROWS
    prompt_tiles = batch * tiles_per_batch
    return pl.pallas_call(
        functools.partial(_final_kernel, prompt_tiles),
        grid=(n_tiles,),
        in_specs=[pl.BlockSpec((ROWS, D_MODEL), lambda i: (i, 0)),
                  pl.BlockSpec((ROWS * SUBLANES, LANES), lambda i: (i, 0)),
                  _mod_spec(layer, 5, tiles_per_batch)],
        out_specs=[_prompt_tile_spec(prompt_tiles, tiles_per_batch),
                   pl.BlockSpec((GRP, SUB, D_MODEL), lambda i: (0, 0, 0))],
        out_shape=[jax.ShapeDtypeStruct((batch, seq, D_MODEL), F32),
                   jax.ShapeDtypeStruct((GRP, SUB, D_MODEL), F32)],
        compiler_params=_cparams(("arbitrary",)),
        name="final",
    )(x1, y, mod)


def _attn_tables(sinks_ref, q_len, n_k, k_off):
    n_q = GQA_GROUP * q_len
    row = lax.broadcasted_iota(I32, (n_q, n_k), 0)
    col = lax.broadcasted_iota(I32, (n_q, n_k), 1)
    dist = jnp.abs((row % q_len) + k_off - col).astype(F32)
    hh_col = lax.broadcasted_iota(I32, (n_q, 1), 0) // q_len
    bias, sinks = [], []
    for g in range(N_KV_A):
        head = (g * GQA_GROUP + 1 + hh_col).astype(F32)
        bias.append(jnp.exp2(-8.0 * head / N_HEADS_A) * dist)
        sink = jnp.zeros((n_q, 1), F32)
        for j in range(GQA_GROUP):
            sink = jnp.where(hh_col == j, sinks_ref[g * GQA_GROUP + j], sink)
        sinks.append(sink)
    return bias, sinks


def _attend(q, k, v, bias, sinks, first_valid=None):
    n_q = q.shape[0]
    n_k = k.shape[0]
    lane = lax.broadcasted_iota(I32, (n_k, LANES), 1)
    out = jnp.zeros((n_q, LANES), F32)
    for g in range(N_KV_A):
        in_g = (lane // HEAD_DIM) == g
        kg = jnp.where(in_g, k, 0.0).astype(BF16)
        vg = jnp.where(in_g, v, 0.0).astype(BF16)
        s = _dot_nt(q, kg) - bias[g]
        if first_valid is not None:
            col = lax.broadcasted_iota(I32, (n_q, n_k), 1)
            s = jnp.where(col >= first_valid, s, NEG_INF)
        m = jnp.maximum(jnp.max(s, axis=-1, keepdims=True), sinks[g])
        p = jnp.exp(s - m)
        denom = jnp.sum(p, axis=-1, keepdims=True) + jnp.exp(sinks[g] - m)
        out = out + _dot(p.astype(BF16), vg) / denom
    return out


def _attn_prompt_kernel(sinks_ref, q_ref, kp_ref, kc_ref, vp_ref, vc_ref, o_ref):
    j = pl.program_id(1)
    kcat = jnp.concatenate([kp_ref[...], kc_ref[...]], axis=0)
    vcat = jnp.concatenate([vp_ref[...], vc_ref[...]], axis=0)
    bias, sinks = _attn_tables(sinks_ref, CHUNK, WINDOW + CHUNK, WINDOW)

    for c in range(ATT_ROWS // CHUNK):
        q = q_ref[:, c * CHUNK:(c + 1) * CHUNK, :].reshape(GQA_GROUP * CHUNK, LANES)
        k = kcat[c * CHUNK:c * CHUNK + WINDOW + CHUNK]
        v = vcat[c * CHUNK:c * CHUNK + WINDOW + CHUNK]
        first_valid = WINDOW - c * CHUNK - j * ATT_ROWS
        out = _attend(q, k, v, bias, sinks, first_valid)
        o_ref[:, c * CHUNK:(c + 1) * CHUNK, :] = out.reshape(GQA_GROUP, CHUNK, LANES).astype(BF16)


def _attn_prompt(q4, k, v, sinks, batch, seq):
    steps = seq // ATT_ROWS
    per_tile = ROWS // ATT_ROWS
    win_blocks = ATT_ROWS // WINDOW
    n_tiles = batch * seq // ROWS
    cur = pl.BlockSpec((ATT_ROWS, KV_A), lambda b, j: (b * steps + j, 0))
    prev = pl.BlockSpec(
        (WINDOW, KV_A), lambda b, j: (b * steps * win_blocks + jnp.maximum(j * win_blocks - 1, 0), 0))
    qo = pl.BlockSpec((None, GQA_GROUP, ATT_ROWS, LANES),
                      lambda b, j: ((b * steps + j) // per_tile, 0, j % per_tile, 0))
    return pl.pallas_call(
        _attn_prompt_kernel,
        grid=(batch, steps),
        in_specs=[pl.BlockSpec(memory_space=pltpu.SMEM), qo, prev, cur, prev, cur],
        out_specs=qo,
        out_shape=jax.ShapeDtypeStruct((n_tiles, GQA_GROUP, ROWS, LANES), BF16),
        compiler_params=_cparams(("arbitrary", "arbitrary")),
        name="attn_prompt",
    )(sinks, q4, k, k, v, v)


def _attn_sample_kernel(n_new, sinks_ref, q_ref, kc_ref, kn_ref, vc_ref, vn_ref, o_ref):
    n_cache = kc_ref.shape[1]
    bias, sinks = _attn_tables(sinks_ref, n_new, n_cache + n_new, n_cache)
    for r in range(SAMPLE_SEQS):
        rows = slice(r * n_new, (r + 1) * n_new)
        q = q_ref[:, rows, :].reshape(GQA_GROUP * n_new, LANES)
        k = jnp.concatenate([kc_ref[r], kn_ref[rows, :]], axis=0)
        v = jnp.concatenate([vc_ref[r], vn_ref[rows, :]], axis=0)
        out = _attend(q, k, v, bias, sinks)
        o_ref[:, rows, :] = out.reshape(GQA_GROUP, n_new, LANES).astype(BF16)


def _attn_sample(q4, k, v, cache_k, cache_v, sinks, tile, row0, dec_batch, dec_seq):
    n_cache = cache_k.shape[1]
    rows = SAMPLE_SEQS * dec_seq
    q_spec = pl.BlockSpec((None, GQA_GROUP, rows, LANES), lambda b: (tile, 0, b, 0))
    new = pl.BlockSpec((rows, KV_A), lambda b: (row0 // rows + b, 0))
    cache = pl.BlockSpec((SAMPLE_SEQS, n_cache, KV_A), lambda b: (b, 0, 0))
    return pl.pallas_call(
        functools.partial(_attn_sample_kernel, dec_seq),
        grid=(dec_batch // SAMPLE_SEQS,),
        in_specs=[pl.BlockSpec(memory_space=pltpu.SMEM), q_spec, cache, new, cache, new],
        out_specs=pl.BlockSpec((None, GQA_GROUP, rows, LANES), lambda b: (0, 0, b, 0)),
        out_shape=jax.ShapeDtypeStruct((1, GQA_GROUP, dec_batch * dec_seq, LANES), BF16),
        compiler_params=_cparams(("arbitrary",)),
        name="attn_sample",
    )(sinks, q4, cache_k, k, cache_v, v)


def _level_sizes(length):
    sizes = []
    m = length // 2
    while m >= 1:
        sizes.append(m)
        m //= 2
    return sizes


def _hgrn_consts(length):
    t = np.arange(length)
    tri = (t[None, :] <= t[:, None]).astype(np.float32)
    after = (t[None, :] > t[:, None]).astype(np.float32)
    levels = []
    masks = []
    s = np.tile(t, 2)[None, :]
    tt = t[:, None]
    for m in _level_sizes(length):
        blk = t // m
        same = blk[None, :] == blk[:, None]
        q_rows = (blk % 2 == 1)[:, None]
        sel = np.where(q_rows, same & (t[None, :] <= t[:, None]), same & (t[None, :] > t[:, None]))
        levels.append(sel.astype(np.float32))
        masks.append(((tt // m) % 2 == 1) & ((s // m) == (tt // m) - 1))
    masks.append(s == tt)
    masks.append(s <= tt)
    base = np.concatenate([tri, after], axis=0)
    lev = np.concatenate(levels, axis=0)
    return (np.concatenate([base] * 3, axis=1), np.concatenate([lev] * 3, axis=1),
            np.stack(masks).astype(np.int32))


def _stack2(x, m0, m1):
    xb = x.astype(BF16)
    return jnp.concatenate([xb * m0, xb * m1], axis=0)


def _hgrn_intra(qp, kp, b, masks, m0, m1, arg_ref, sl):
    length = qp.shape[0]
    n_lev = len(_level_sizes(length))
    if arg_ref is None:
        mid = length // 2 - 1
        rel = b - b[mid:mid + 1]
        a = _dot_nt((qp * jnp.exp(rel)).astype(BF16), _stack2(kp * jnp.exp(-rel), m0, m1))
        a = jnp.where(masks[n_lev + 1], a, 0.0)
    else:
        a = _dot_nt(qp.astype(BF16), _stack2(kp, m0, m1))
        a = jnp.where(masks[n_lev], a, 0.0)
        for lev in range(n_lev):
            e = jnp.exp(arg_ref[lev * length:(lev + 1) * length, sl])
            pr = _dot_nt((qp * e).astype(BF16), _stack2(kp * e, m0, m1))
            a = jnp.where(masks[lev], pr, a)
    return a.astype(BF16)


def _hgrn_finish(a, qp, kp, ip, base, sl, st_ref, seq, p, m0, m1, same_head):
    length = qp.shape[0]
    e_b = jnp.exp(base[0:length, sl])
    e_k = jnp.exp(base[length:2 * length, sl])
    st = st_ref[seq, p]
    o = _dot(jnp.concatenate([a, (qp * e_b).astype(BF16)], axis=1),
             jnp.concatenate([_stack2(ip, m0, m1), st.T.astype(BF16)], axis=0))
    upd = _dot_tn(ip.astype(BF16), (kp * e_k).astype(BF16))
    st_ref[seq, p] = st * e_b[length - 1:length, :] + jnp.where(same_head, upd, 0.0)
    return o


def _state_to_pairs(s0_ref, st_ref, n_seq):
    zero = jnp.zeros((DK_B, DV_B), F32)
    for r in range(n_seq):
        for p in range(N_PAIR):
            top = jnp.concatenate([s0_ref[r, 2 * p], zero], axis=1)
            bot = jnp.concatenate([zero, s0_ref[r, 2 * p + 1]], axis=1)
            st_ref[r, p] = jnp.concatenate([top, bot], axis=0).T


def _pairs_to_state(st_ref, s_ref, n_seq):
    for r in range(n_seq):
        for p in range(N_PAIR):
            d = st_ref[r, p].T
            s_ref[r, 2 * p] = d[:DK_B, :DV_B]
            s_ref[r, 2 * p + 1] = d[DK_B:, DV_B:]


def _hgrn_kernel(n_seq, n_chunk, length, carry_state, *refs):
    if carry_state:
        (q_ref, k_ref, i_ref, g_ref, f_ref, selb_ref, sell_ref, mask_ref, bd_ref, on_ref,
         o_ref, s_ref, st_ref, base_ref, arg_ref) = refs

        @pl.when(pl.program_id(1) == 0)
        def _():
            st_ref[...] = jnp.zeros_like(st_ref)
    else:
        (q_ref, k_ref, i_ref, g_ref, f_ref, s0_ref, selb_ref, sell_ref, mask_ref, bd_ref, on_ref,
         o_ref, s_ref, st_ref, base_ref, arg_ref) = refs
        _state_to_pairs(s0_ref, st_ref, n_seq)

    bd = bd_ref[...]
    onorm = on_ref[...]
    n_items = n_seq * n_chunk
    mid = length // 2 - 1
    lane = lax.broadcasted_iota(I32, (length, LANES), 1)
    m0 = (lane < DK_B).astype(BF16)
    m1 = (lane >= DK_B).astype(BF16)
    rowp = lax.broadcasted_iota(I32, (LANES, LANES), 0) // DV_B
    colp = lax.broadcasted_iota(I32, (LANES, LANES), 1) // DK_B
    same_head = rowp == colp
    masks = [mask_ref[m] != 0 for m in range(mask_ref.shape[0])]

    def pieces(rows):
        return jnp.concatenate(_split3(f_ref[rows, :]), axis=0)

    def pair(ref, rows, p):
        return ref[rows, p * LANES:(p + 1) * LANES].astype(F32)

    span = jnp.float32(0.0)
    for it in range(n_items):
        base = _dot(selb_ref[...], pieces(slice(it * length, (it + 1) * length)))
        base_ref[it * 2 * length:(it + 1) * 2 * length, :] = base
        b_mid = base[mid:mid + 1]
        span = jnp.maximum(span, jnp.max(base[0:1] - b_mid))
        span = jnp.maximum(span, jnp.max(b_mid - base[length - 1:length]))
    safe = span <= HGRN_SPAN_LIMIT

    def finish(rows, base, seq, a_of):
        outs = []
        for p in range(N_PAIR):
            sl = slice(p * LANES, (p + 1) * LANES)
            outs.append(_hgrn_finish(a_of(p), pair(q_ref, rows, p), pair(k_ref, rows, p),
                                     pair(i_ref, rows, p), base, sl, st_ref, seq, p, m0, m1,
                                     same_head))
        o = jnp.concatenate(outs, axis=1)
        g = g_ref[rows, :].astype(F32)
        o = o * lax.rsqrt(_seg_mean(o * o, bd) + RMS_EPS) * onorm * (g * _sigmoid(g))
        o_ref[rows, :] = o.astype(BF16)

    @pl.when(safe)
    def _():
        for it in range(n_items):
            rows = slice(it * length, (it + 1) * length)
            base = base_ref[it * 2 * length:(it + 1) * 2 * length, :]

            def a_of(p, rows=rows, base=base):
                sl = slice(p * LANES, (p + 1) * LANES)
                return _hgrn_intra(pair(q_ref, rows, p), pair(k_ref, rows, p), base[0:length, sl],
                                   masks, m0, m1, None, sl)

            finish(rows, base, it // n_chunk, a_of)

    @pl.when(jnp.logical_not(safe))
    def _():
        def body(it, carry):
            rows = pl.ds(pl.multiple_of(it * length, length), length)
            arg_ref[...] = _dot(sell_ref[...], pieces(rows))
            base = base_ref[pl.ds(pl.multiple_of(it * 2 * length, 2 * length), 2 * length), :]
            seq = it // n_chunk if n_chunk > 1 and n_seq > 1 else (it if n_chunk == 1 else 0)

            def a_of(p):
                sl = slice(p * LANES, (p + 1) * LANES)
                return _hgrn_intra(pair(q_ref, rows, p), pair(k_ref, rows, p), base[0:length, sl],
                                   masks, m0, m1, arg_ref, sl)

            finish(rows, base, seq, a_of)
            return carry

        lax.fori_loop(0, n_items, body, 0)

    if carry_state:
        @pl.when(pl.program_id(1) == pl.num_programs(1) - 1)
        def _():
            _pairs_to_state(st_ref, s_ref, n_seq)
    else:
        _pairs_to_state(st_ref, s_ref, n_seq)


def _hgrn_scratch(n_seq, n_items, length, sell):
    return [pltpu.VMEM((n_seq, N_PAIR, LANES, LANES), F32),
            pltpu.VMEM((n_items * 2 * length, W_B), F32),
            pltpu.VMEM((sell.shape[0], W_B), F32)]


def _hgrn_prompt(qb, kb, ib, gb, logf, consts, bd, onorm_t, batch, seq):
    selb, sell, masks = consts
    steps = seq // HGRN_ROWS
    n_chunk = HGRN_ROWS // CHUNK
    row = pl.BlockSpec((HGRN_ROWS, W_B), lambda b, j: (b * steps + j, 0))
    whole = lambda a: pl.BlockSpec(a.shape, lambda b, j: (0,) * a.ndim)
    return pl.pallas_call(
        functools.partial(_hgrn_kernel, 1, n_chunk, CHUNK, True),
        grid=(batch, steps),
        in_specs=[row, row, row, row, row, whole(selb), whole(sell), whole(masks), whole(bd),
                  whole(onorm_t)],
        out_specs=[row, pl.BlockSpec((1, N_HEADS_B, DK_B, DV_B), lambda b, j: (b, 0, 0, 0))],
        out_shape=[jax.ShapeDtypeStruct((batch * seq, V_B), BF16),
                   jax.ShapeDtypeStruct((batch, N_HEADS_B, DK_B, DV_B), F32)],
        scratch_shapes=_hgrn_scratch(1, n_chunk, CHUNK, sell),
        compiler_params=_cparams(("arbitrary", "arbitrary")),
        name="hgrn_prompt",
    )(qb, kb, ib, gb, logf, selb, sell, masks, bd, onorm_t)


def _hgrn_sample(qb, kb, ib, gb, logf, s0, consts, bd, onorm_t, row0, dec_batch, dec_seq):
    selb, sell, masks = consts
    rows = SAMPLE_SEQS * dec_seq
    row = pl.BlockSpec((rows, W_B), lambda b: (row0 // rows + b, 0))
    whole = lambda a: pl.BlockSpec(a.shape, lambda b: (0,) * a.ndim)
    state = pl.BlockSpec((SAMPLE_SEQS, N_HEADS_B, DK_B, DV_B), lambda b: (b, 0, 0, 0))
    return pl.pallas_call(
        functools.partial(_hgrn_kernel, SAMPLE_SEQS, 1, dec_seq, False),
        grid=(dec_batch // SAMPLE_SEQS,),
        in_specs=[row, row, row, row, row, state, whole(selb), whole(sell), whole(masks),
                  whole(bd), whole(onorm_t)],
        out_specs=[pl.BlockSpec((rows, V_B), lambda b: (b, 0)), state],
        out_shape=[jax.ShapeDtypeStruct((dec_batch * dec_seq, V_B), BF16),
                   jax.ShapeDtypeStruct((dec_batch, N_HEADS_B, DK_B, DV_B), F32)],
        scratch_shapes=_hgrn_scratch(SAMPLE_SEQS, SAMPLE_SEQS, dec_seq, sell),
        compiler_params=_cparams(("arbitrary",)),
        name="hgrn_sample",
    )(qb, kb, ib, gb, logf, s0, selb, sell, masks, bd, onorm_t)


def _outproj_kernel(prompt_tiles, x_ref, oap_ref, obp_ref, oas_ref, obs_ref, g1_ref, sh_ref, sc_ref,
                    ln_ref, w_ref, wr_ref, rb_ref, x1_out, h2_out, cls_out, ga_out, gb_out):
    def mix_of(oa_ref, ob_ref):
        heads = [_head(oa_ref[h % GQA_GROUP], h // GQA_GROUP) for h in range(N_HEADS_A)]
        return jnp.concatenate(heads + [ob_ref[...]], axis=1)

    is_prompt = pl.program_id(0) < prompt_tiles
    mix = jnp.where(is_prompt, mix_of(oap_ref, obp_ref), mix_of(oas_ref, obs_ref))
    att = _dot(mix, w_ref[...])
    x1 = _gated_add(x_ref[...], g1_ref, att)
    x1_out[...] = x1
    h = _modulated_norm(x1, ln_ref, sc_ref, sh_ref)

    for s in range(SUBLANES):
        h2_out[pl.ds(s, ROWS, stride=SUBLANES), :] = h[:, s * LANES:(s + 1) * LANES]

    h_hi, h_lo = _split2(h)
    w_hi, w_lo = _split2(wr_ref[...])
    hi_terms = _dot_nt(jnp.concatenate([w_hi, w_lo], axis=0), h_hi)
    logits = hi_terms[0:N_EXPERTS] + hi_terms[N_EXPERTS:] + _dot_nt(w_hi, h_lo)
    aff = _sigmoid(logits)
    sel = aff + rb_ref[...]
    e_idx = lax.broadcasted_iota(I32, (N_EXPERTS, ROWS), 0)
    l_idx = lax.broadcasted_iota(I32, (EXPERTS_PER_GROUP, ROWS), 0)

    def first_argmax(vals, idx, big):
        top = jnp.max(vals, axis=0, keepdims=True)
        return top, jnp.min(jnp.where(vals == top, idx, big), axis=0, keepdims=True)

    g_scores = []
    for g in range(N_GROUPS):
        sg = sel[g * EXPERTS_PER_GROUP:(g + 1) * EXPERTS_PER_GROUP]
        m1, i1 = first_argmax(sg, l_idx, EXPERTS_PER_GROUP)
        m2 = jnp.max(jnp.where(l_idx == i1, -jnp.inf, sg), axis=0, keepdims=True)
        g_scores.append(m1 + m2)
    g_score = jnp.concatenate(g_scores, axis=0)
    g_idx = lax.broadcasted_iota(I32, (N_GROUPS, ROWS), 0)
    _, g_sel = first_argmax(g_score, g_idx, N_GROUPS)
    masked = jnp.where((e_idx // EXPERTS_PER_GROUP) == g_sel, sel, -jnp.inf)
    _, e1 = first_argmax(masked, e_idx, N_EXPERTS)
    _, e2 = first_argmax(jnp.where(e_idx == e1, -jnp.inf, masked), e_idx, N_EXPERTS)
    w1 = jnp.sum(jnp.where(e_idx == e1, aff, 0.0), axis=0, keepdims=True)
    w2 = jnp.sum(jnp.where(e_idx == e2, aff, 0.0), axis=0, keepdims=True)
    den = w1 + w2
    w1 = w1 / den
    w2 = w2 / den
    lo_first = e1 < e2
    gate_lo = jnp.where(lo_first, w1, w2)
    gate_hi = jnp.where(lo_first, w2, w1)
    a = jnp.minimum(e1, e2) - g_sel * EXPERTS_PER_GROUP
    b = jnp.maximum(e1, e2) - g_sel * EXPERTS_PER_GROUP
    pair = jnp.where(a == 0, b - 1, jnp.where(a == 1, 6 - b, 5))
    cls_out[...] = g_sel * PAIRS_PER_GROUP + pair
    ga_out[...] = jnp.where(a == 2, gate_hi, gate_lo)
    gb_out[...] = jnp.where(a == 2, gate_lo, gate_hi)


def _outproj(layer, x4, oap4, obp, oas4, obs, mod, ln2, w_out_b, w_router_t, router_bias_c,
             tiles_per_batch):
    n_tiles = x4.shape[0] // ROWS
    x_spec = pl.BlockSpec((ROWS, D_MODEL), lambda i: (i, 0))
    prompt_tiles = oap4.shape[0]
    last = prompt_tiles - 1
    lane_row = pl.BlockSpec((None, 1, ROWS), lambda i: (i, 0, 0))
    return pl.pallas_call(
        functools.partial(_outproj_kernel, prompt_tiles),
        grid=(n_tiles,),
        in_specs=[
            x_spec,
            pl.BlockSpec((None, GQA_GROUP, ROWS, LANES), lambda i: (jnp.minimum(i, last), 0, 0, 0)),
            pl.BlockSpec((ROWS, V_B), lambda i: (jnp.minimum(i, last), 0)),
            pl.BlockSpec((None, GQA_GROUP, ROWS, LANES), lambda i: (0, 0, 0, 0)),
            pl.BlockSpec((ROWS, V_B), lambda i: (0, 0)),
            _mod_spec(layer, 2, tiles_per_batch), _mod_spec(layer, 3, tiles_per_batch),
            _mod_spec(layer, 4, tiles_per_batch),
            pl.BlockSpec((None, 1, D_MODEL), lambda i: (layer, 0, 0)),
            pl.BlockSpec((None, Q_A + V_B, D_MODEL), lambda i: (layer, 0, 0)),
            pl.BlockSpec((N_EXPERTS, D_MODEL), lambda i: (0, 0)),
            pl.BlockSpec((N_EXPERTS, 1), lambda i: (0, 0)),
        ],
        out_specs=[
            x_spec,
            pl.BlockSpec((ROWS * SUBLANES, LANES), lambda i: (i, 0)),
            lane_row, lane_row, lane_row,
        ],
        out_shape=[
            jax.ShapeDtypeStruct(x4.shape, F32),
            jax.ShapeDtypeStruct((n_tiles * ROWS * SUBLANES, LANES), F32),
            jax.ShapeDtypeStruct((n_tiles, 1, ROWS), I32),
            jax.ShapeDtypeStruct((n_tiles, 1, ROWS), F32),
            jax.ShapeDtypeStruct((n_tiles, 1, ROWS), F32),
        ],
        compiler_params=_cparams(("arbitrary",)),
        name=f"outproj{layer}",
    )(x4, oap4, obp, oas4, obs, mod, mod, mod, ln2, w_out_b, w_router_t, router_bias_c)


def _rank_kernel(cls_ref, upper_ref, lower_ref, dest_ref, count_ref, rank_ref):
    n_tiles = cls_ref.shape[0]
    c_idx = lax.broadcasted_iota(I32, (GRP, ROWS), 0)
    upper = upper_ref[...]

    def rank_body(i, carry):
        onehot = c_idx == cls_ref[i]
        within = _dot(onehot.astype(BF16), upper)
        rank_ref[i] = jnp.sum(jnp.where(onehot, within + carry, 0.0), axis=0, keepdims=True)
        return carry + jnp.sum(onehot.astype(F32), axis=1, keepdims=True)

    count = lax.fori_loop(0, n_tiles, rank_body, jnp.zeros((GRP, 1), F32))
    count_ref[...] = jnp.broadcast_to(count, (GRP, LANES)).astype(I32)
    tiles = jnp.floor((count + (TM - 1)) * (1.0 / TM))
    first_tile = _dot(lower_ref[...], jnp.broadcast_to(tiles, (GRP, LANES)).astype(BF16))
    start = first_tile[:, 0:1] * TM

    def dest_body(i, carry):
        onehot = c_idx == cls_ref[i]
        dest = rank_ref[i] + jnp.sum(jnp.where(onehot, start, 0.0), axis=0, keepdims=True)
        dest_ref[i] = dest.astype(I32)
        return carry

    lax.fori_loop(0, n_tiles, dest_body, 0)


def _plan_kernel(n_tok, n_t, dest_ref, count_ref, src_ref, dst_ref, ea_ref, eb_ref, nv_ref, nt_ref):
    groups = TM // ROW_GROUP

    def fill(first_group, end_group):
        def body(p8, carry):
            tile_parity = lax.shift_right_logical(p8, groups.bit_length() - 1) & 1
            spare = n_tok + tile_parity * TM + (p8 & (groups - 1)) * ROW_GROUP
            for u in range(ROW_GROUP):
                src_ref[p8 * ROW_GROUP + u] = 0
                dst_ref[p8 * ROW_GROUP + u] = spare + u
            return carry
        lax.fori_loop(first_group, end_group, body, 0)

    tile = 0
    for c in range(N_CLASS):
        g, pi = divmod(c, PAIRS_PER_GROUP)
        a, b = PAIR_TABLE[pi]
        cnt = count_ref[c, 0]
        n_c = lax.shift_right_logical(cnt + (TM - 1), TM.bit_length() - 1)
        first = tile

        def mark(t, carry, g=g, a=a, b=b, cnt=cnt, first=first):
            ea_ref[t] = g * EXPERTS_PER_GROUP + a
            eb_ref[t] = g * EXPERTS_PER_GROUP + b
            nv_ref[t] = jnp.minimum(cnt - (t - first) * TM, TM)
            return carry

        lax.fori_loop(tile, tile + n_c, mark, 0)

        fill(lax.shift_right_logical(first * TM + cnt, GROUP_SHIFT), (first + n_c) * groups)
        tile = tile + n_c
    nt_ref[0] = tile

    def unused(t, carry):
        ea_ref[t] = 0
        eb_ref[t] = 0
        nv_ref[t] = 0
        return carry

    lax.fori_loop(tile, n_t, unused, 0)

    fill(tile * groups, n_t * groups)

    def place(t, carry):
        d = dest_ref[t]
        src_ref[d] = t
        dst_ref[d] = t
        return carry

    lax.fori_loop(0, n_tok, place, 0, unroll=8)


def _route_plan(cls3, upper, lower):
    n_tiles = cls3.shape[0]
    n_tok = n_tiles * ROWS
    n_t = (n_tok + N_CLASS * (TM - 1)) // TM + 2
    dest, count = pl.pallas_call(
        _rank_kernel,
        out_shape=[jax.ShapeDtypeStruct((n_tiles, 1, ROWS), I32),
                   jax.ShapeDtypeStruct((GRP, LANES), I32)],
        scratch_shapes=[pltpu.VMEM((n_tiles, 1, ROWS), F32)],
        name="moe_rank",
    )(cls3, upper, lower)
    smem = pl.BlockSpec(memory_space=pltpu.SMEM)
    return pl.pallas_call(
        functools.partial(_plan_kernel, n_tok, n_t),
        in_specs=[smem, smem],
        out_specs=[smem, smem, smem, smem, smem, smem],
        out_shape=[jax.ShapeDtypeStruct((n_t * TM,), I32),
                   jax.ShapeDtypeStruct((n_t * TM,), I32),
                   jax.ShapeDtypeStruct((n_t,), I32),
                   jax.ShapeDtypeStruct((n_t,), I32),
                   jax.ShapeDtypeStruct((n_t,), I32),
                   jax.ShapeDtypeStruct((1,), I32)],
        name="moe_plan",
    )(dest.reshape(n_tok), count)


def _tile_rows(ref, slot, r):
    return ref.at[pl.ds(pl.multiple_of((slot * TM + r) * SUBLANES, SUBLANES), SUBLANES)]


def _moe_kernel(n_tok, src_ref, dst_ref, ea_ref, eb_ref, nv_ref, nt_ref, ga_ref, gb_ref, h2_hbm,
                wga_ref, wua_ref, wda_ref, wgb_ref, wub_ref, wdb_ref, y_hbm, xbuf, ybuf, gbuf, gsem,
                ssem):
    i = pl.program_id(0)
    n_used = nt_ref[0]
    lane = lax.broadcasted_iota(I32, (1, LANES), 1)
    group_rows = ROW_GROUP * SUBLANES

    def n_groups(tile):
        return lax.shift_right_logical(nv_ref[tile] + (ROW_GROUP - 1), GROUP_SHIFT)

    def token_tile(hbm, tok):
        return hbm.at[pl.ds(pl.multiple_of(tok * SUBLANES, SUBLANES), SUBLANES)]

    def gather_row(tile, slot, r, priority):
        tok = src_ref[tile * TM + r]
        pltpu.make_async_copy(token_tile(h2_hbm, tok), _tile_rows(xbuf, slot, r),
                              gsem.at[slot]).start(priority=priority)
        gbuf[pl.ds(slot * TM + r, 1), :] = jnp.where(lane == 0, ga_ref[tok], gb_ref[tok])

    def scatter_row(tile, slot, r, priority):
        pltpu.make_async_copy(_tile_rows(ybuf, slot, r), token_tile(y_hbm, dst_ref[tile * TM + r]),
                              ssem.at[slot]).start(priority=priority)

    def row_loop(start_row, groups):
        def body(g, c):
            for u in range(ROW_GROUP):
                start_row(g * ROW_GROUP + u, u % 2)
            return c
        lax.fori_loop(0, groups, body, 0)

    def wait_groups(hbm, buf, slot, sem, groups, to_hbm):
        def body(g, c):
            rows = buf.at[pl.ds(pl.multiple_of((slot * TM + g * ROW_GROUP) * SUBLANES, group_rows),
                                group_rows)]
            block = hbm.at[pl.ds(0, group_rows)]
            copy = (pltpu.make_async_copy(rows, block, sem.at[slot]) if to_hbm
                    else pltpu.make_async_copy(block, rows, sem.at[slot]))
            copy.wait()
            return c
        lax.fori_loop(0, groups, body, 0)

    @pl.when(i == 0)
    def _():
        xbuf[...] = jnp.zeros_like(xbuf)
        gbuf[...] = jnp.zeros_like(gbuf)
        ybuf[...] = jnp.zeros_like(ybuf)
        spare = pltpu.make_async_copy(
            ybuf, y_hbm.at[pl.ds(n_tok * SUBLANES, 2 * TM * SUBLANES)], ssem.at[0])
        spare.start()
        spare.wait()
        for t in range(GATHER_AHEAD):
            @pl.when(t < n_used)
            def _():
                row_loop(lambda r, pr: gather_row(t, t, r, pr), n_groups(t))

    @pl.when(i + GATHER_AHEAD < n_used)
    def _():
        nxt = i + GATHER_AHEAD
        row_loop(lambda r, pr: gather_row(nxt, nxt % GATHER_SLOTS, r, pr), n_groups(nxt))

    @pl.when(jnp.logical_and(i >= 2, i - 2 < n_used))
    def _():
        wait_groups(y_hbm, ybuf, i % 2, ssem, n_groups(i - 2), True)

    @pl.when(i < n_used)
    def _():
        slot = i % 2
        xslot = i % GATHER_SLOTS
        groups = n_groups(i)
        wait_groups(h2_hbm, xbuf, xslot, gsem, groups, False)
        base = xslot * TM * SUBLANES
        xb = jnp.concatenate([_token_rows(xbuf, s, TM, base) for s in range(SUBLANES)],
                             axis=1).astype(BF16)
        gates = gbuf[pl.ds(xslot * TM, TM), :]
        y = jnp.zeros((TM, D_MODEL), F32)
        for e, (wg, wu, wd) in enumerate(((wga_ref, wua_ref, wda_ref), (wgb_ref, wub_ref, wdb_ref))):
            gt = _dot(xb, wg[...].astype(BF16))
            up = _dot(xb, wu[...].astype(BF16))
            act = gt * _sigmoid(gt) * up * gates[:, e:e + 1]
            y = y + _dot(act.astype(BF16), wd[...].astype(BF16))
        ybase = slot * TM * SUBLANES
        for s in range(SUBLANES):
            ybuf[pl.ds(ybase + s, TM, stride=SUBLANES), :] = y[:, s * LANES:(s + 1) * LANES]
        row_loop(lambda r, pr: scatter_row(i, slot, r, pr), groups)


def _moe(layer, h2, gate_a, gate_b, plan, w_gate, w_up, w_down):
    src, dst, ea, eb, nv, nt = plan
    n_t = ea.shape[0]
    n_tok = h2.shape[0] // SUBLANES

    def w_spec(shape, which):
        def index(i, src_ref, dst_ref, ea_ref, eb_ref, nv_ref, nt_ref, ga_ref, gb_ref):
            t = jnp.minimum(i, jnp.maximum(nt_ref[0] - 1, 0))
            e = (ea_ref, eb_ref)[which][t]
            return (layer, e, 0, 0)
        return pl.BlockSpec((None, None) + shape, index)

    gu = (D_MODEL, D_EXPERT)
    dn = (D_EXPERT, D_MODEL)
    grid_spec = pltpu.PrefetchScalarGridSpec(
        num_scalar_prefetch=8,
        grid=(n_t,),
        in_specs=[pl.BlockSpec(memory_space=pl.ANY),
                  w_spec(gu, 0), w_spec(gu, 0), w_spec(dn, 0),
                  w_spec(gu, 1), w_spec(gu, 1), w_spec(dn, 1)],
        out_specs=pl.BlockSpec(memory_space=pl.ANY),
        scratch_shapes=[pltpu.VMEM((GATHER_SLOTS * TM * SUBLANES, LANES), F32),
                        pltpu.VMEM((2 * TM * SUBLANES, LANES), F32),
                        pltpu.VMEM((GATHER_SLOTS * TM, LANES), F32),
                        pltpu.SemaphoreType.DMA((GATHER_SLOTS,)), pltpu.SemaphoreType.DMA((2,))],
    )
    return pl.pallas_call(
        functools.partial(_moe_kernel, n_tok),
        grid_spec=grid_spec,
        out_shape=jax.ShapeDtypeStruct(((n_tok + 2 * TM) * SUBLANES, LANES), F32),
        compiler_params=_cparams(("arbitrary",)),
        name=f"moe{layer}",
    )(src, dst, ea, eb, nv, nt, gate_a.reshape(n_tok), gate_b.reshape(n_tok), h2,
      w_gate, w_up, w_down, w_gate, w_up, w_down)


def kernel(x_prompt, x_sample, cache_k, cache_v, state_hgrn, c_prompt, c_sample, w_ada, b_ada,
           ln1, ln2, w_in, q_norm, k_norm, sinks, lb_param, o_norm, w_out, w_router, router_bias,
           w_e_gate, w_e_up, w_e_down):
    batch, seq, _ = x_prompt.shape
    dec_batch, dec_seq, _ = x_sample.shape
    depth = w_ada.shape[0]
    assert dec_batch == GRP and dec_seq == SUB and seq % ROWS == 0 and seq % HGRN_ROWS == 0
    tiles_per_batch = seq // ROWS
    n_prompt = batch * seq
    n_tok = n_prompt + dec_batch * dec_seq
    n_tiles = n_tok // ROWS
    prompt_tiles = n_prompt // ROWS
    n_cache = cache_k.shape[2]

    seg = np.arange(Q_A) // HEAD_DIM
    bd = jnp.asarray((seg[:, None] == seg[None, :]).astype(np.float32) / HEAD_DIM, BF16)

    def hgrn_consts(length):
        selb, sell, masks = _hgrn_consts(length)
        return jnp.asarray(selb, BF16), jnp.asarray(sell, BF16), jnp.asarray(masks)

    consts_p = hgrn_consts(CHUNK)
    consts_s = hgrn_consts(dec_seq)
    upper = jnp.asarray(np.triu(np.ones((ROWS, ROWS), np.float32), 1), BF16)
    lower = jnp.asarray(np.tril(np.ones((GRP, GRP), np.float32), -1), BF16)

    w_in_b = w_in.astype(BF16)
    w_out_b = w_out.astype(BF16)
    q_norm_t = jnp.tile(q_norm, (1, N_HEADS_A)).reshape(depth, 1, Q_A)
    k_norm_t = jnp.tile(k_norm, (1, N_KV_A)).reshape(depth, 1, KV_A)
    o_norm_t = jnp.tile(o_norm, (1, N_HEADS_B)).reshape(depth, 1, V_B)
    ln1_r = ln1.reshape(depth, 1, D_MODEL)
    ln2_r = ln2.reshape(depth, 1, D_MODEL)
    w_router_t = w_router.T
    router_bias_c = router_bias.reshape(N_EXPERTS, 1)

    mod = _adaln(c_prompt, c_sample, w_ada, b_ada).reshape(depth, batch + 1, GRP, N_MOD * D_MODEL)

    cache_k2 = cache_k.reshape(depth, dec_batch, n_cache, KV_A)
    cache_v2 = cache_v.reshape(depth, dec_batch, n_cache, KV_A)

    src = (x_prompt, x_sample)
    kp, vp, sp, kss, vss, sss = [], [], [], [], [], []
    for l in range(depth):
        x, q4, k, v, qb, logf, kb, ib, gb, kwin, vwin = _inproj(
            l, src, n_tiles, prompt_tiles, mod, ln1_r, w_in_b, q_norm_t, k_norm_t, lb_param, bd,
            tiles_per_batch)

        oap4 = _attn_prompt(q4, k, v, sinks[l], batch, seq)
        oas4 = _attn_sample(q4, k, v, cache_k2[l], cache_v2[l], sinks[l], prompt_tiles,
                            n_prompt, dec_batch, dec_seq)
        obp, s_p = _hgrn_prompt(qb, kb, ib, gb, logf, consts_p, bd, o_norm_t[l], batch, seq)
        obs, s_s = _hgrn_sample(qb, kb, ib, gb, logf, state_hgrn[l], consts_s,
                                bd, o_norm_t[l], n_prompt, dec_batch, dec_seq)

        x1, h2, cls3, gate_a, gate_b = _outproj(l, x, oap4, obp, oas4, obs, mod, ln2_r, w_out_b,
                                                w_router_t, router_bias_c, tiles_per_batch)
        y = _moe(l, h2, gate_a, gate_b, _route_plan(cls3, upper, lower), w_e_gate, w_e_up,
                 w_e_down)
        src = (x1, y)

        kp.append(kwin.reshape(batch, WINDOW, N_KV_A, HEAD_DIM))
        vp.append(vwin.reshape(batch, WINDOW, N_KV_A, HEAD_DIM))
        sp.append(s_p)
        kss.append(k[n_prompt:].reshape(dec_batch, dec_seq, N_KV_A, HEAD_DIM))
        vss.append(v[n_prompt:].reshape(dec_batch, dec_seq, N_KV_A, HEAD_DIM))
        sss.append(s_s)

    yp, ys = _final(depth - 1, src[0], src[1], mod, batch, seq, tiles_per_batch)
    return (yp, ys, jnp.stack(kp), jnp.stack(vp), jnp.stack(sp),
            jnp.stack(kss), jnp.stack(vss), jnp.stack(sss))
```

```python
import functools

import numpy as np
import jax
import jax.numpy as jnp
from jax import lax
from jax.experimental import pallas as pl
from jax.experimental.pallas import tpu as pltpu

F32 = jnp.float32
BF16 = jnp.bfloat16
I32 = jnp.int32

D_MODEL = 1024
HEAD_DIM = 64
N_HEADS_A = 8
N_KV_A = 2
GQA_GROUP = N_HEADS_A // N_KV_A
N_HEADS_B = 8
DK_B = 64
DV_B = 64
Q_A = N_HEADS_A * HEAD_DIM
KV_A = N_KV_A * HEAD_DIM
W_B = N_HEADS_B * DK_B
V_B = N_HEADS_B * DV_B
CHUNK = 64
WINDOW = 128
N_EXPERTS = 16
N_GROUPS = 4
EXPERTS_PER_GROUP = 4
D_EXPERT = D_MODEL // 2
RMS_EPS = 1e-6
NEG_INF = -1e30
N_MOD = 6

LANES = 128
SUBLANES = 8
ROWS = 512
GRP = 32
SUB = ROWS // GRP
ATT_ROWS = 512
HGRN_ROWS = 2048
N_PAIR = N_HEADS_B // 2
SAMPLE_SEQS = 8
HGRN_SPAN_LIMIT = 80.0
TM = 256
PAIRS_PER_GROUP = 6
PAIR_TABLE = ((0, 1), (0, 2), (0, 3), (1, 3), (1, 2), (3, 2))
N_CLASS = N_GROUPS * PAIRS_PER_GROUP
ROW_GROUP = 8
GROUP_SHIFT = ROW_GROUP.bit_length() - 1
GATHER_AHEAD = 3
GATHER_SLOTS = GATHER_AHEAD + 1
VMEM_LIMIT = 56 * 1024 * 1024


def _cparams(sem):
    return pltpu.CompilerParams(dimension_semantics=sem, vmem_limit_bytes=VMEM_LIMIT)


def _dot(a, b):
    return jnp.dot(a, b, preferred_element_type=F32)


def _dot_nt(a, b):
    return lax.dot_general(a, b, (((1,), (1,)), ((), ())), preferred_element_type=F32)


def _dot_tn(a, b):
    return lax.dot_general(a, b, (((0,), (0,)), ((), ())), preferred_element_type=F32)


def _sigmoid(x):
    return 1.0 / (1.0 + jnp.exp(-x))


def _split2(x):
    hi = x.astype(BF16)
    lo = (x - hi.astype(F32)).astype(BF16)
    return hi, lo


def _split3(x):
    hi = x.astype(BF16)
    r = x - hi.astype(F32)
    mid = r.astype(BF16)
    lo = (r - mid.astype(F32)).astype(BF16)
    return hi, mid, lo


def _seg_mean(sq, bd):
    return _dot(sq.astype(BF16), bd)


def _head(x, j):
    return x[:, j * HEAD_DIM:(j + 1) * HEAD_DIM]


def _token_rows(ref, s, n, base=0):
    return ref[pl.ds(base + s, n, stride=SUBLANES), :]


def _adaln_kernel(n_prompt, c_ref, w_ref, b_ref, o_ref):
    c = c_ref[...]
    a = c * _sigmoid(c)
    a_hi, a_lo = _split2(a)
    w_hi, w_lo = _split2(w_ref[...])
    acc = _dot(a_hi, w_hi) + _dot(a_lo, w_hi) + _dot(a_hi, w_lo) + b_ref[...]
    for b in range(n_prompt):
        o_ref[b * GRP:(b + 1) * GRP, :] = jnp.broadcast_to(acc[b:b + 1], (GRP, acc.shape[1]))
    o_ref[n_prompt * GRP:, :] = acc[n_prompt:]


def _adaln(c_prompt, c_sample, w_ada, b_ada):
    depth = w_ada.shape[0]
    n_prompt = c_prompt.shape[0]
    c_all = jnp.concatenate([c_prompt, c_sample], axis=0)
    rows = n_prompt * GRP + c_sample.shape[0]
    tn = D_MODEL
    return pl.pallas_call(
        functools.partial(_adaln_kernel, n_prompt),
        grid=(depth, N_MOD * D_MODEL // tn),
        in_specs=[
            pl.BlockSpec(c_all.shape, lambda l, n: (0, 0)),
            pl.BlockSpec((None, D_MODEL, tn), lambda l, n: (l, 0, n)),
            pl.BlockSpec((None, 1, tn), lambda l, n: (l, 0, n)),
        ],
        out_specs=pl.BlockSpec((None, rows, tn), lambda l, n: (l, 0, n)),
        out_shape=jax.ShapeDtypeStruct((depth, rows, N_MOD * D_MODEL), F32),
        compiler_params=_cparams(("arbitrary", "arbitrary")),
        name="adaln",
    )(c_all, w_ada, b_ada.reshape(depth, 1, N_MOD * D_MODEL))


def _mod_spec(layer, j, tiles_per_batch):
    return pl.BlockSpec((None, None, GRP, D_MODEL),
                        lambda i, *_: (layer, i // tiles_per_batch, 0, j))


def _by_group(fn, xs, mod_refs):
    outs = []
    for g in range(GRP):
        rows = slice(g * SUB, (g + 1) * SUB)
        outs.append(fn(*[x[rows] for x in xs], *[m[g:g + 1, :] for m in mod_refs]))
    return jnp.concatenate(outs, axis=0)


def _gated_add(x, gate_ref, y):
    return _by_group(lambda xg, yg, gg: xg + gg * yg, (x, y), (gate_ref,))


def _modulated_norm(x, ln_ref, sc_ref, sh_ref):
    ms = jnp.mean(x * x, axis=-1, keepdims=True)
    xn = x * lax.rsqrt(ms + RMS_EPS)
    scale = ln_ref[...] * (1.0 + sc_ref[...])
    return _by_group(lambda xg, sc, sh: xg * sc + sh, (xn,), (scale, sh_ref))


def _prompt_tile_spec(prompt_tiles, tiles_per_batch):
    def index(i):
        t = jnp.minimum(i, prompt_tiles - 1)
        return (t // tiles_per_batch, t % tiles_per_batch, 0)
    return pl.BlockSpec((None, ROWS, D_MODEL), index)


def _moe_output_tile(y_ref):
    return jnp.concatenate([_token_rows(y_ref, s, ROWS) for s in range(SUBLANES)], axis=1)


def _inproj_kernel(layer, prompt_tiles, n_src, *refs):
    src = refs[:n_src]
    sh_ref, sc_ref, ln_ref, w_ref, qn_ref, kn_ref, lbp_ref, bd_ref = refs[n_src:n_src + 8]
    (x_out, q_out, k_out, v_out, qb_out, logf_out, kb_out, ib_out, gb_out,
     kwin_out, vwin_out) = refs[n_src + 8:]
    if layer == 0:
        xp_ref, xs_ref = src
        x = jnp.where(pl.program_id(0) < prompt_tiles, xp_ref[...],
                      xs_ref[...].reshape(ROWS, D_MODEL))
    else:
        x1_ref, y_ref, g2_ref = src
        x = _gated_add(x1_ref[...], g2_ref, _moe_output_tile(y_ref))
    x_out[...] = x
    hb = _modulated_norm(x, ln_ref, sc_ref, sh_ref).astype(BF16)
    bd = bd_ref[...]

    c0 = Q_A + 2 * KV_A
    c1 = c0 + 2 * W_B
    qa = _dot(hb, w_ref[:, 0:Q_A])
    kv = _dot(hb, w_ref[:, Q_A:c0])
    qb = _dot(hb, w_ref[:, c0:c0 + W_B])
    z = _dot(hb, w_ref[:, c0 + W_B:c1])
    ib = _dot(hb, w_ref[:, c1:c1 + V_B])
    gb = _dot(hb, w_ref[:, c1 + V_B:c1 + 2 * V_B])

    qa = qa * lax.rsqrt(_seg_mean(qa * qa, bd) + RMS_EPS) * qn_ref[...]
    qa = (qa * (HEAD_DIM ** -0.5)).astype(BF16)
    for hh in range(GQA_GROUP):
        q_out[hh] = jnp.concatenate([_head(qa, g * GQA_GROUP + hh) for g in range(N_KV_A)], axis=1)

    ka = kv[:, :KV_A]
    ka = ka * lax.rsqrt(_seg_mean(ka * ka, bd[:KV_A, :KV_A]) + RMS_EPS) * kn_ref[...]
    k_out[...] = ka
    v_out[...] = kv[:, KV_A:]

    @pl.when(pl.program_id(0) < prompt_tiles)
    def _():
        kwin_out[...] = ka[ROWS - WINDOW:]
        vwin_out[...] = kv[ROWS - WINDOW:, KV_A:]

    qb_out[...] = (qb * (DK_B ** -0.5)).astype(BF16)
    ib_out[...] = ib.astype(BF16)
    gb_out[...] = gb.astype(BF16)

    p = lbp_ref[...]
    e = jnp.exp(p - jnp.max(p, axis=0, keepdims=True))
    sm = e / jnp.sum(e, axis=0, keepdims=True)
    cum0 = sm[0:1]
    cum = cum0
    for r in range(1, layer + 1):
        cum = cum + sm[r:r + 1]
    lb = cum - cum0

    pos = z >= 0.0
    t = jnp.exp(-jnp.abs(z))
    num = jnp.where(pos, jnp.log(1.0 + lb * t), jnp.maximum(jnp.log(lb + t), z))
    logf_out[...] = num - jnp.log(1.0 + t)
    kb_out[...] = ((1.0 - lb) * jnp.where(pos, t, 1.0) / (1.0 + t)).astype(BF16)


def _inproj(layer, src, n_tiles, prompt_tiles, mod, ln1, w_in_b, q_norm_t, k_norm_t, lb_param, bd,
            tiles_per_batch):
    depth = lb_param.shape[0]
    n = n_tiles * ROWS
    in_width = w_in_b.shape[-1]
    x_spec = pl.BlockSpec((ROWS, D_MODEL), lambda i: (i, 0))
    if layer == 0:
        src_specs = [_prompt_tile_spec(prompt_tiles, tiles_per_batch),
                     pl.BlockSpec((GRP, SUB, D_MODEL), lambda i: (0, 0, 0))]
        src_args = list(src)
    else:
        src_specs = [x_spec, pl.BlockSpec((ROWS * SUBLANES, LANES), lambda i: (i, 0)),
                     _mod_spec(layer - 1, 5, tiles_per_batch)]
        src_args = list(src) + [mod]
    row_spec = lambda w: pl.BlockSpec((ROWS, w), lambda i: (i, 0))
    win_spec = pl.BlockSpec(
        (None, WINDOW, KV_A), lambda i: (jnp.minimum(i, prompt_tiles - 1) // tiles_per_batch, 0, 0))
    return pl.pallas_call(
        functools.partial(_inproj_kernel, layer, prompt_tiles, len(src_specs)),
        grid=(n_tiles,),
        in_specs=src_specs + [
            _mod_spec(layer, 0, tiles_per_batch), _mod_spec(layer, 1, tiles_per_batch),
            pl.BlockSpec((None, 1, D_MODEL), lambda i: (layer, 0, 0)),
            pl.BlockSpec((None, D_MODEL, in_width), lambda i: (layer, 0, 0)),
            pl.BlockSpec((None, 1, Q_A), lambda i: (layer, 0, 0)),
            pl.BlockSpec((None, 1, KV_A), lambda i: (layer, 0, 0)),
            pl.BlockSpec((depth, W_B), lambda i: (0, 0)),
            pl.BlockSpec((Q_A, Q_A), lambda i: (0, 0)),
        ],
        out_specs=[
            x_spec,
            pl.BlockSpec((None, GQA_GROUP, ROWS, LANES), lambda i: (i, 0, 0, 0)),
            row_spec(KV_A), row_spec(KV_A),
            row_spec(W_B), row_spec(W_B), row_spec(W_B), row_spec(V_B), row_spec(V_B),
            win_spec, win_spec,
        ],
        out_shape=[
            jax.ShapeDtypeStruct((n, D_MODEL), F32),
            jax.ShapeDtypeStruct((n_tiles, GQA_GROUP, ROWS, LANES), BF16),
            jax.ShapeDtypeStruct((n, KV_A), F32),
            jax.ShapeDtypeStruct((n, KV_A), F32),
            jax.ShapeDtypeStruct((n, W_B), BF16),
            jax.ShapeDtypeStruct((n, W_B), F32),
            jax.ShapeDtypeStruct((n, W_B), BF16),
            jax.ShapeDtypeStruct((n, V_B), BF16),
            jax.ShapeDtypeStruct((n, V_B), BF16),
            jax.ShapeDtypeStruct((prompt_tiles // tiles_per_batch, WINDOW, KV_A), F32),
            jax.ShapeDtypeStruct((prompt_tiles // tiles_per_batch, WINDOW, KV_A), F32),
        ],
        compiler_params=_cparams(("arbitrary",)),
        name=f"inproj{layer}",
    )(*src_args, mod, mod, ln1, w_in_b, q_norm_t, k_norm_t, lb_param, bd)


def _final_kernel(prompt_tiles, x1_ref, y_ref, g2_ref, yp_ref, ys_ref):
    i = pl.program_id(0)
    x = _gated_add(x1_ref[...], g2_ref, _moe_output_tile(y_ref))

    @pl.when(i < prompt_tiles)
    def _():
        yp_ref[...] = x

    @pl.when(i >= prompt_tiles)
    def _():
        ys_ref[...] = x.reshape(GRP, SUB, D_MODEL)


def _final(layer, x1, y, mod, batch, seq, tiles_per_batch):
    n_tiles = x1.shape[0] // ROWS
    prompt_tiles = batch * tiles_per_batch
    return pl.pallas_call(
        functools.partial(_final_kernel, prompt_tiles),
        grid=(n_tiles,),
        in_specs=[pl.BlockSpec((ROWS, D_MODEL), lambda i: (i, 0)),
                  pl.BlockSpec((ROWS * SUBLANES, LANES), lambda i: (i, 0)),
                  _mod_spec(layer, 5, tiles_per_batch)],
        out_specs=[_prompt_tile_spec(prompt_tiles, tiles_per_batch),
                   pl.BlockSpec((GRP, SUB, D_MODEL), lambda i: (0, 0, 0))],
        out_shape=[jax.ShapeDtypeStruct((batch, seq, D_MODEL), F32),
                   jax.ShapeDtypeStruct((GRP, SUB, D_MODEL), F32)],
        compiler_params=_cparams(("arbitrary",)),
        name="final",
    )(x1, y, mod)


def _attn_tables(sinks_ref, q_len, n_k, k_off):
    n_q = GQA_GROUP * q_len
    row = lax.broadcasted_iota(I32, (n_q, n_k), 0)
    col = lax.broadcasted_iota(I32, (n_q, n_k), 1)
    dist = jnp.abs((row % q_len) + k_off - col).astype(F32)
    hh_col = lax.broadcasted_iota(I32, (n_q, 1), 0) // q_len
    bias, sinks = [], []
    for g in range(N_KV_A):
        head = (g * GQA_GROUP + 1 + hh_col).astype(F32)
        bias.append(jnp.exp2(-8.0 * head / N_HEADS_A) * dist)
        sink = jnp.zeros((n_q, 1), F32)
        for j in range(GQA_GROUP):
            sink = jnp.where(hh_col == j, sinks_ref[g * GQA_GROUP + j], sink)
        sinks.append(sink)
    return bias, sinks


def _attend(q, k, v, bias, sinks, first_valid=None):
    n_q = q.shape[0]
    n_k = k.shape[0]
    lane = lax.broadcasted_iota(I32, (n_k, LANES), 1)
    out = jnp.zeros((n_q, LANES), F32)
    for g in range(N_KV_A):
        in_g = (lane // HEAD_DIM) == g
        kg = jnp.where(in_g, k, 0.0).astype(BF16)
        vg = jnp.where(in_g, v, 0.0).astype(BF16)
        s = _dot_nt(q, kg) - bias[g]
        if first_valid is not None:
            col = lax.broadcasted_iota(I32, (n_q, n_k), 1)
            s = jnp.where(col >= first_valid, s, NEG_INF)
        m = jnp.maximum(jnp.max(s, axis=-1, keepdims=True), sinks[g])
        p = jnp.exp(s - m)
        denom = jnp.sum(p, axis=-1, keepdims=True) + jnp.exp(sinks[g] - m)
        out = out + _dot(p.astype(BF16), vg) / denom
    return out


def _attn_prompt_kernel(sinks_ref, q_ref, kp_ref, kc_ref, vp_ref, vc_ref, o_ref):
    j = pl.program_id(1)
    kcat = jnp.concatenate([kp_ref[...], kc_ref[...]], axis=0)
    vcat = jnp.concatenate([vp_ref[...], vc_ref[...]], axis=0)
    bias, sinks = _attn_tables(sinks_ref, CHUNK, WINDOW + CHUNK, WINDOW)

    for c in range(ATT_ROWS // CHUNK):
        q = q_ref[:, c * CHUNK:(c + 1) * CHUNK, :].reshape(GQA_GROUP * CHUNK, LANES)
        k = kcat[c * CHUNK:c * CHUNK + WINDOW + CHUNK]
        v = vcat[c * CHUNK:c * CHUNK + WINDOW + CHUNK]
        first_valid = WINDOW - c * CHUNK - j * ATT_ROWS
        out = _attend(q, k, v, bias, sinks, first_valid)
        o_ref[:, c * CHUNK:(c + 1) * CHUNK, :] = out.reshape(GQA_GROUP, CHUNK, LANES).astype(BF16)


def _attn_prompt(q4, k, v, sinks, batch, seq):
    steps = seq // ATT_ROWS
    per_tile = ROWS // ATT_ROWS
    win_blocks = ATT_ROWS // WINDOW
    n_tiles = batch * seq // ROWS
    cur = pl.BlockSpec((ATT_ROWS, KV_A), lambda b, j: (b * steps + j, 0))
    prev = pl.BlockSpec(
        (WINDOW, KV_A), lambda b, j: (b * steps * win_blocks + jnp.maximum(j * win_blocks - 1, 0), 0))
    qo = pl.BlockSpec((None, GQA_GROUP, ATT_ROWS, LANES),
                      lambda b, j: ((b * steps + j) // per_tile, 0, j % per_tile, 0))
    return pl.pallas_call(
        _attn_prompt_kernel,
        grid=(batch, steps),
        in_specs=[pl.BlockSpec(memory_space=pltpu.SMEM), qo, prev, cur, prev, cur],
        out_specs=qo,
        out_shape=jax.ShapeDtypeStruct((n_tiles, GQA_GROUP, ROWS, LANES), BF16),
        compiler_params=_cparams(("arbitrary", "arbitrary")),
        name="attn_prompt",
    )(sinks, q4, k, k, v, v)


def _attn_sample_kernel(n_new, sinks_ref, q_ref, kc_ref, kn_ref, vc_ref, vn_ref, o_ref):
    n_cache = kc_ref.shape[1]
    bias, sinks = _attn_tables(sinks_ref, n_new, n_cache + n_new, n_cache)
    for r in range(SAMPLE_SEQS):
        rows = slice(r * n_new, (r + 1) * n_new)
        q = q_ref[:, rows, :].reshape(GQA_GROUP * n_new, LANES)
        k = jnp.concatenate([kc_ref[r], kn_ref[rows, :]], axis=0)
        v = jnp.concatenate([vc_ref[r], vn_ref[rows, :]], axis=0)
        out = _attend(q, k, v, bias, sinks)
        o_ref[:, rows, :] = out.reshape(GQA_GROUP, n_new, LANES).astype(BF16)


def _attn_sample(q4, k, v, cache_k, cache_v, sinks, tile, row0, dec_batch, dec_seq):
    n_cache = cache_k.shape[1]
    rows = SAMPLE_SEQS * dec_seq
    q_spec = pl.BlockSpec((None, GQA_GROUP, rows, LANES), lambda b: (tile, 0, b, 0))
    new = pl.BlockSpec((rows, KV_A), lambda b: (row0 // rows + b, 0))
    cache = pl.BlockSpec((SAMPLE_SEQS, n_cache, KV_A), lambda b: (b, 0, 0))
    return pl.pallas_call(
        functools.partial(_attn_sample_kernel, dec_seq),
        grid=(dec_batch // SAMPLE_SEQS,),
        in_specs=[pl.BlockSpec(memory_space=pltpu.SMEM), q_spec, cache, new, cache, new],
        out_specs=pl.BlockSpec((None, GQA_GROUP, rows, LANES), lambda b: (0, 0, b, 0)),
        out_shape=jax.ShapeDtypeStruct((1, GQA_GROUP, dec_batch * dec_seq, LANES), BF16),
        compiler_params=_cparams(("arbitrary",)),
        name="attn_sample",
    )(sinks, q4, cache_k, k, cache_v, v)


def _level_sizes(length):
    sizes = []
    m = length // 2
    while m >= 1:
        sizes.append(m)
        m //= 2
    return sizes


def _hgrn_consts(length):
    t = np.arange(length)
    tri = (t[None, :] <= t[:, None]).astype(np.float32)
    after = (t[None, :] > t[:, None]).astype(np.float32)
    levels = []
    masks = []
    s = np.tile(t, 2)[None, :]
    tt = t[:, None]
    for m in _level_sizes(length):
        blk = t // m
        same = blk[None, :] == blk[:, None]
        q_rows = (blk % 2 == 1)[:, None]
        sel = np.where(q_rows, same & (t[None, :] <= t[:, None]), same & (t[None, :] > t[:, None]))
        levels.append(sel.astype(np.float32))
        masks.append(((tt // m) % 2 == 1) & ((s // m) == (tt // m) - 1))
    masks.append(s == tt)
    masks.append(s <= tt)
    base = np.concatenate([tri, after], axis=0)
    lev = np.concatenate(levels, axis=0)
    return (np.concatenate([base] * 3, axis=1), np.concatenate([lev] * 3, axis=1),
            np.stack(masks).astype(np.int32))


def _stack2(x, m0, m1):
    xb = x.astype(BF16)
    return jnp.concatenate([xb * m0, xb * m1], axis=0)


def _hgrn_intra(qp, kp, b, masks, m0, m1, arg_ref, sl):
    length = qp.shape[0]
    n_lev = len(_level_sizes(length))
    if arg_ref is None:
        mid = length // 2 - 1
        rel = b - b[mid:mid + 1]
        a = _dot_nt((qp * jnp.exp(rel)).astype(BF16), _stack2(kp * jnp.exp(-rel), m0, m1))
        a = jnp.where(masks[n_lev + 1], a, 0.0)
    else:
        a = _dot_nt(qp.astype(BF16), _stack2(kp, m0, m1))
        a = jnp.where(masks[n_lev], a, 0.0)
        for lev in range(n_lev):
            e = jnp.exp(arg_ref[lev * length:(lev + 1) * length, sl])
            pr = _dot_nt((qp * e).astype(BF16), _stack2(kp * e, m0, m1))
            a = jnp.where(masks[lev], pr, a)
    return a.astype(BF16)


def _hgrn_finish(a, qp, kp, ip, base, sl, st_ref, seq, p, m0, m1, same_head):
    length = qp.shape[0]
    e_b = jnp.exp(base[0:length, sl])
    e_k = jnp.exp(base[length:2 * length, sl])
    st = st_ref[seq, p]
    o = _dot(jnp.concatenate([a, (qp * e_b).astype(BF16)], axis=1),
             jnp.concatenate([_stack2(ip, m0, m1), st.T.astype(BF16)], axis=0))
    upd = _dot_tn(ip.astype(BF16), (kp * e_k).astype(BF16))
    st_ref[seq, p] = st * e_b[length - 1:length, :] + jnp.where(same_head, upd, 0.0)
    return o


def _state_to_pairs(s0_ref, st_ref, n_seq):
    zero = jnp.zeros((DK_B, DV_B), F32)
    for r in range(n_seq):
        for p in range(N_PAIR):
            top = jnp.concatenate([s0_ref[r, 2 * p], zero], axis=1)
            bot = jnp.concatenate([zero, s0_ref[r, 2 * p + 1]], axis=1)
            st_ref[r, p] = jnp.concatenate([top, bot], axis=0).T


def _pairs_to_state(st_ref, s_ref, n_seq):
    for r in range(n_seq):
        for p in range(N_PAIR):
            d = st_ref[r, p].T
            s_ref[r, 2 * p] = d[:DK_B, :DV_B]
            s_ref[r, 2 * p + 1] = d[DK_B:, DV_B:]


def _hgrn_kernel(n_seq, n_chunk, length, carry_state, *refs):
    if carry_state:
        (q_ref, k_ref, i_ref, g_ref, f_ref, selb_ref, sell_ref, mask_ref, bd_ref, on_ref,
         o_ref, s_ref, st_ref, base_ref, arg_ref) = refs

        @pl.when(pl.program_id(1) == 0)
        def _():
            st_ref[...] = jnp.zeros_like(st_ref)
    else:
        (q_ref, k_ref, i_ref, g_ref, f_ref, s0_ref, selb_ref, sell_ref, mask_ref, bd_ref, on_ref,
         o_ref, s_ref, st_ref, base_ref, arg_ref) = refs
        _state_to_pairs(s0_ref, st_ref, n_seq)

    bd = bd_ref[...]
    onorm = on_ref[...]
    n_items = n_seq * n_chunk
    mid = length // 2 - 1
    lane = lax.broadcasted_iota(I32, (length, LANES), 1)
    m0 = (lane < DK_B).astype(BF16)
    m1 = (lane >= DK_B).astype(BF16)
    rowp = lax.broadcasted_iota(I32, (LANES, LANES), 0) // DV_B
    colp = lax.broadcasted_iota(I32, (LANES, LANES), 1) // DK_B
    same_head = rowp == colp
    masks = [mask_ref[m] != 0 for m in range(mask_ref.shape[0])]

    def pieces(rows):
        return jnp.concatenate(_split3(f_ref[rows, :]), axis=0)

    def pair(ref, rows, p):
        return ref[rows, p * LANES:(p + 1) * LANES].astype(F32)

    span = jnp.float32(0.0)
    for it in range(n_items):
        base = _dot(selb_ref[...], pieces(slice(it * length, (it + 1) * length)))
        base_ref[it * 2 * length:(it + 1) * 2 * length, :] = base
        b_mid = base[mid:mid + 1]
        span = jnp.maximum(span, jnp.max(base[0:1] - b_mid))
        span = jnp.maximum(span, jnp.max(b_mid - base[length - 1:length]))
    safe = span <= HGRN_SPAN_LIMIT

    def finish(rows, base, seq, a_of):
        outs = []
        for p in range(N_PAIR):
            sl = slice(p * LANES, (p + 1) * LANES)
            outs.append(_hgrn_finish(a_of(p), pair(q_ref, rows, p), pair(k_ref, rows, p),
                                     pair(i_ref, rows, p), base, sl, st_ref, seq, p, m0, m1,
                                     same_head))
        o = jnp.concatenate(outs, axis=1)
        g = g_ref[rows, :].astype(F32)
        o = o * lax.rsqrt(_seg_mean(o * o, bd) + RMS_EPS) * onorm * (g * _sigmoid(g))
        o_ref[rows, :] = o.astype(BF16)

    @pl.when(safe)
    def _():
        for it in range(n_items):
            rows = slice(it * length, (it + 1) * length)
            base = base_ref[it * 2 * length:(it + 1) * 2 * length, :]

            def a_of(p, rows=rows, base=base):
                sl = slice(p * LANES, (p + 1) * LANES)
                return _hgrn_intra(pair(q_ref, rows, p), pair(k_ref, rows, p), base[0:length, sl],
                                   masks, m0, m1, None, sl)

            finish(rows, base, it // n_chunk, a_of)

    @pl.when(jnp.logical_not(safe))
    def _():
        def body(it, carry):
            rows = pl.ds(pl.multiple_of(it * length, length), length)
            arg_ref[...] = _dot(sell_ref[...], pieces(rows))
            base = base_ref[pl.ds(pl.multiple_of(it * 2 * length, 2 * length), 2 * length), :]
            seq = it // n_chunk if n_chunk > 1 and n_seq > 1 else (it if n_chunk == 1 else 0)

            def a_of(p):
                sl = slice(p * LANES, (p + 1) * LANES)
                return _hgrn_intra(pair(q_ref, rows, p), pair(k_ref, rows, p), base[0:length, sl],
                                   masks, m0, m1, arg_ref, sl)

            finish(rows, base, seq, a_of)
            return carry

        lax.fori_loop(0, n_items, body, 0)

    if carry_state:
        @pl.when(pl.program_id(1) == pl.num_programs(1) - 1)
        def _():
            _pairs_to_state(st_ref, s_ref, n_seq)
    else:
        _pairs_to_state(st_ref, s_ref, n_seq)


def _hgrn_scratch(n_seq, n_items, length, sell):
    return [pltpu.VMEM((n_seq, N_PAIR, LANES, LANES), F32),
            pltpu.VMEM((n_items * 2 * length, W_B), F32),
            pltpu.VMEM((sell.shape[0], W_B), F32)]


def _hgrn_prompt(qb, kb, ib, gb, logf, consts, bd, onorm_t, batch, seq):
    selb, sell, masks = consts
    steps = seq // HGRN_ROWS
    n_chunk = HGRN_ROWS // CHUNK
    row = pl.BlockSpec((HGRN_ROWS, W_B), lambda b, j: (b * steps + j, 0))
    whole = lambda a: pl.BlockSpec(a.shape, lambda b, j: (0,) * a.ndim)
    return pl.pallas_call(
        functools.partial(_hgrn_kernel, 1, n_chunk, CHUNK, True),
        grid=(batch, steps),
        in_specs=[row, row, row, row, row, whole(selb), whole(sell), whole(masks), whole(bd),
                  whole(onorm_t)],
        out_specs=[row, pl.BlockSpec((1, N_HEADS_B, DK_B, DV_B), lambda b, j: (b, 0, 0, 0))],
        out_shape=[jax.ShapeDtypeStruct((batch * seq, V_B), BF16),
                   jax.ShapeDtypeStruct((batch, N_HEADS_B, DK_B, DV_B), F32)],
        scratch_shapes=_hgrn_scratch(1, n_chunk, CHUNK, sell),
        compiler_params=_cparams(("arbitrary", "arbitrary")),
        name="hgrn_prompt",
    )(qb, kb, ib, gb, logf, selb, sell, masks, bd, onorm_t)


def _hgrn_sample(qb, kb, ib, gb, logf, s0, consts, bd, onorm_t, row0, dec_batch, dec_seq):
    selb, sell, masks = consts
    rows = SAMPLE_SEQS * dec_seq
    row = pl.BlockSpec((rows, W_B), lambda b: (row0 // rows + b, 0))
    whole = lambda a: pl.BlockSpec(a.shape, lambda b: (0,) * a.ndim)
    state = pl.BlockSpec((SAMPLE_SEQS, N_HEADS_B, DK_B, DV_B), lambda b: (b, 0, 0, 0))
    return pl.pallas_call(
        functools.partial(_hgrn_kernel, SAMPLE_SEQS, 1, dec_seq, False),
        grid=(dec_batch // SAMPLE_SEQS,),
        in_specs=[row, row, row, row, row, state, whole(selb), whole(sell), whole(masks),
                  whole(bd), whole(onorm_t)],
        out_specs=[pl.BlockSpec((rows, V_B), lambda b: (b, 0)), state],
        out_shape=[jax.ShapeDtypeStruct((dec_batch * dec_seq, V_B), BF16),
                   jax.ShapeDtypeStruct((dec_batch, N_HEADS_B, DK_B, DV_B), F32)],
        scratch_shapes=_hgrn_scratch(SAMPLE_SEQS, SAMPLE_SEQS, dec_seq, sell),
        compiler_params=_cparams(("arbitrary",)),
        name="hgrn_sample",
    )(qb, kb, ib, gb, logf, s0, selb, sell, masks, bd, onorm_t)


def _outproj_kernel(prompt_tiles, x_ref, oap_ref, obp_ref, oas_ref, obs_ref, g1_ref, sh_ref, sc_ref,
                    ln_ref, w_ref, wr_ref, rb_ref, x1_out, h2_out, cls_out, ga_out, gb_out):
    def mix_of(oa_ref, ob_ref):
        heads = [_head(oa_ref[h % GQA_GROUP], h // GQA_GROUP) for h in range(N_HEADS_A)]
        return jnp.concatenate(heads + [ob_ref[...]], axis=1)

    is_prompt = pl.program_id(0) < prompt_tiles
    mix = jnp.where(is_prompt, mix_of(oap_ref, obp_ref), mix_of(oas_ref, obs_ref))
    att = _dot(mix, w_ref[...])
    x1 = _gated_add(x_ref[...], g1_ref, att)
    x1_out[...] = x1
    h = _modulated_norm(x1, ln_ref, sc_ref, sh_ref)

    for s in range(SUBLANES):
        h2_out[pl.ds(s, ROWS, stride=SUBLANES), :] = h[:, s * LANES:(s + 1) * LANES]

    h_hi, h_lo = _split2(h)
    w_hi, w_lo = _split2(wr_ref[...])
    hi_terms = _dot_nt(jnp.concatenate([w_hi, w_lo], axis=0), h_hi)
    logits = hi_terms[0:N_EXPERTS] + hi_terms[N_EXPERTS:] + _dot_nt(w_hi, h_lo)
    aff = _sigmoid(logits)
    sel = aff + rb_ref[...]
    e_idx = lax.broadcasted_iota(I32, (N_EXPERTS, ROWS), 0)
    l_idx = lax.broadcasted_iota(I32, (EXPERTS_PER_GROUP, ROWS), 0)

    def first_argmax(vals, idx, big):
        top = jnp.max(vals, axis=0, keepdims=True)
        return top, jnp.min(jnp.where(vals == top, idx, big), axis=0, keepdims=True)

    g_scores = []
    for g in range(N_GROUPS):
        sg = sel[g * EXPERTS_PER_GROUP:(g + 1) * EXPERTS_PER_GROUP]
        m1, i1 = first_argmax(sg, l_idx, EXPERTS_PER_GROUP)
        m2 = jnp.max(jnp.where(l_idx == i1, -jnp.inf, sg), axis=0, keepdims=True)
        g_scores.append(m1 + m2)
    g_score = jnp.concatenate(g_scores, axis=0)
    g_idx = lax.broadcasted_iota(I32, (N_GROUPS, ROWS), 0)
    _, g_sel = first_argmax(g_score, g_idx, N_GROUPS)
    masked = jnp.where((e_idx // EXPERTS_PER_GROUP) == g_sel, sel, -jnp.inf)
    _, e1 = first_argmax(masked, e_idx, N_EXPERTS)
    _, e2 = first_argmax(jnp.where(e_idx == e1, -jnp.inf, masked), e_idx, N_EXPERTS)
    w1 = jnp.sum(jnp.where(e_idx == e1, aff, 0.0), axis=0, keepdims=True)
    w2 = jnp.sum(jnp.where(e_idx == e2, aff, 0.0), axis=0, keepdims=True)
    den = w1 + w2
    w1 = w1 / den
    w2 = w2 / den
    lo_first = e1 < e2
    gate_lo = jnp.where(lo_first, w1, w2)
    gate_hi = jnp.where(lo_first, w2, w1)
    a = jnp.minimum(e1, e2) - g_sel * EXPERTS_PER_GROUP
    b = jnp.maximum(e1, e2) - g_sel * EXPERTS_PER_GROUP
    pair = jnp.where(a == 0, b - 1, jnp.where(a == 1, 6 - b, 5))
    cls_out[...] = g_sel * PAIRS_PER_GROUP + pair
    ga_out[...] = jnp.where(a == 2, gate_hi, gate_lo)
    gb_out[...] = jnp.where(a == 2, gate_lo, gate_hi)


def _outproj(layer, x4, oap4, obp, oas4, obs, mod, ln2, w_out_b, w_router_t, router_bias_c,
             tiles_per_batch):
    n_tiles = x4.shape[0] // ROWS
    x_spec = pl.BlockSpec((ROWS, D_MODEL), lambda i: (i, 0))
    prompt_tiles = oap4.shape[0]
    last = prompt_tiles - 1
    lane_row = pl.BlockSpec((None, 1, ROWS), lambda i: (i, 0, 0))
    return pl.pallas_call(
        functools.partial(_outproj_kernel, prompt_tiles),
        grid=(n_tiles,),
        in_specs=[
            x_spec,
            pl.BlockSpec((None, GQA_GROUP, ROWS, LANES), lambda i: (jnp.minimum(i, last), 0, 0, 0)),
            pl.BlockSpec((ROWS, V_B), lambda i: (jnp.minimum(i, last), 0)),
            pl.BlockSpec((None, GQA_GROUP, ROWS, LANES), lambda i: (0, 0, 0, 0)),
            pl.BlockSpec((ROWS, V_B), lambda i: (0, 0)),
            _mod_spec(layer, 2, tiles_per_batch), _mod_spec(layer, 3, tiles_per_batch),
            _mod_spec(layer, 4, tiles_per_batch),
            pl.BlockSpec((None, 1, D_MODEL), lambda i: (layer, 0, 0)),
            pl.BlockSpec((None, Q_A + V_B, D_MODEL), lambda i: (layer, 0, 0)),
            pl.BlockSpec((N_EXPERTS, D_MODEL), lambda i: (0, 0)),
            pl.BlockSpec((N_EXPERTS, 1), lambda i: (0, 0)),
        ],
        out_specs=[
            x_spec,
            pl.BlockSpec((ROWS * SUBLANES, LANES), lambda i: (i, 0)),
            lane_row, lane_row, lane_row,
        ],
        out_shape=[
            jax.ShapeDtypeStruct(x4.shape, F32),
            jax.ShapeDtypeStruct((n_tiles * ROWS * SUBLANES, LANES), F32),
            jax.ShapeDtypeStruct((n_tiles, 1, ROWS), I32),
            jax.ShapeDtypeStruct((n_tiles, 1, ROWS), F32),
            jax.ShapeDtypeStruct((n_tiles, 1, ROWS), F32),
        ],
        compiler_params=_cparams(("arbitrary",)),
        name=f"outproj{layer}",
    )(x4, oap4, obp, oas4, obs, mod, mod, mod, ln2, w_out_b, w_router_t, router_bias_c)


def _rank_kernel(cls_ref, upper_ref, lower_ref, dest_ref, count_ref, rank_ref):
    n_tiles = cls_ref.shape[0]
    c_idx = lax.broadcasted_iota(I32, (GRP, ROWS), 0)
    upper = upper_ref[...]

    def rank_body(i, carry):
        onehot = c_idx == cls_ref[i]
        within = _dot(onehot.astype(BF16), upper)
        rank_ref[i] = jnp.sum(jnp.where(onehot, within + carry, 0.0), axis=0, keepdims=True)
        return carry + jnp.sum(onehot.astype(F32), axis=1, keepdims=True)

    count = lax.fori_loop(0, n_tiles, rank_body, jnp.zeros((GRP, 1), F32))
    count_ref[...] = jnp.broadcast_to(count, (GRP, LANES)).astype(I32)
    tiles = jnp.floor((count + (TM - 1)) * (1.0 / TM))
    first_tile = _dot(lower_ref[...], jnp.broadcast_to(tiles, (GRP, LANES)).astype(BF16))
    start = first_tile[:, 0:1] * TM

    def dest_body(i, carry):
        onehot = c_idx == cls_ref[i]
        dest = rank_ref[i] + jnp.sum(jnp.where(onehot, start, 0.0), axis=0, keepdims=True)
        dest_ref[i] = dest.astype(I32)
        return carry

    lax.fori_loop(0, n_tiles, dest_body, 0)


def _plan_kernel(n_tok, n_t, dest_ref, count_ref, src_ref, dst_ref, ea_ref, eb_ref, nv_ref, nt_ref):
    groups = TM // ROW_GROUP

    def fill(first_group, end_group):
        def body(p8, carry):
            tile_parity = lax.shift_right_logical(p8, groups.bit_length() - 1) & 1
            spare = n_tok + tile_parity * TM + (p8 & (groups - 1)) * ROW_GROUP
            for u in range(ROW_GROUP):
                src_ref[p8 * ROW_GROUP + u] = 0
                dst_ref[p8 * ROW_GROUP + u] = spare + u
            return carry
        lax.fori_loop(first_group, end_group, body, 0)

    tile = 0
    for c in range(N_CLASS):
        g, pi = divmod(c, PAIRS_PER_GROUP)
        a, b = PAIR_TABLE[pi]
        cnt = count_ref[c, 0]
        n_c = lax.shift_right_logical(cnt + (TM - 1), TM.bit_length() - 1)
        first = tile

        def mark(t, carry, g=g, a=a, b=b, cnt=cnt, first=first):
            ea_ref[t] = g * EXPERTS_PER_GROUP + a
            eb_ref[t] = g * EXPERTS_PER_GROUP + b
            nv_ref[t] = jnp.minimum(cnt - (t - first) * TM, TM)
            return carry

        lax.fori_loop(tile, tile + n_c, mark, 0)

        fill(lax.shift_right_logical(first * TM + cnt, GROUP_SHIFT), (first + n_c) * groups)
        tile = tile + n_c
    nt_ref[0] = tile

    def unused(t, carry):
        ea_ref[t] = 0
        eb_ref[t] = 0
        nv_ref[t] = 0
        return carry

    lax.fori_loop(tile, n_t, unused, 0)

    fill(tile * groups, n_t * groups)

    def place(t, carry):
        d = dest_ref[t]
        src_ref[d] = t
        dst_ref[d] = t
        return carry

    lax.fori_loop(0, n_tok, place, 0, unroll=8)


def _route_plan(cls3, upper, lower):
    n_tiles = cls3.shape[0]
    n_tok = n_tiles * ROWS
    n_t = (n_tok + N_CLASS * (TM - 1)) // TM + 2
    dest, count = pl.pallas_call(
        _rank_kernel,
        out_shape=[jax.ShapeDtypeStruct((n_tiles, 1, ROWS), I32),
                   jax.ShapeDtypeStruct((GRP, LANES), I32)],
        scratch_shapes=[pltpu.VMEM((n_tiles, 1, ROWS), F32)],
        name="moe_rank",
    )(cls3, upper, lower)
    smem = pl.BlockSpec(memory_space=pltpu.SMEM)
    return pl.pallas_call(
        functools.partial(_plan_kernel, n_tok, n_t),
        in_specs=[smem, smem],
        out_specs=[smem, smem, smem, smem, smem, smem],
        out_shape=[jax.ShapeDtypeStruct((n_t * TM,), I32),
                   jax.ShapeDtypeStruct((n_t * TM,), I32),
                   jax.ShapeDtypeStruct((n_t,), I32),
                   jax.ShapeDtypeStruct((n_t,), I32),
                   jax.ShapeDtypeStruct((n_t,), I32),
                   jax.ShapeDtypeStruct((1,), I32)],
        name="moe_plan",
    )(dest.reshape(n_tok), count)


def _tile_rows(ref, slot, r):
    return ref.at[pl.ds(pl.multiple_of((slot * TM + r) * SUBLANES, SUBLANES), SUBLANES)]


def _moe_kernel(n_tok, src_ref, dst_ref, ea_ref, eb_ref, nv_ref, nt_ref, ga_ref, gb_ref, h2_hbm,
                wga_ref, wua_ref, wda_ref, wgb_ref, wub_ref, wdb_ref, y_hbm, xbuf, ybuf, gbuf, gsem,
                ssem):
    i = pl.program_id(0)
    n_used = nt_ref[0]
    lane = lax.broadcasted_iota(I32, (1, LANES), 1)
    group_rows = ROW_GROUP * SUBLANES

    def n_groups(tile):
        return lax.shift_right_logical(nv_ref[tile] + (ROW_GROUP - 1), GROUP_SHIFT)

    def token_tile(hbm, tok):
        return hbm.at[pl.ds(pl.multiple_of(tok * SUBLANES, SUBLANES), SUBLANES)]

    def gather_row(tile, slot, r, priority):
        tok = src_ref[tile * TM + r]
        pltpu.make_async_copy(token_tile(h2_hbm, tok), _tile_rows(xbuf, slot, r),
                              gsem.at[slot]).start(priority=priority)
        gbuf[pl.ds(slot * TM + r, 1), :] = jnp.where(lane == 0, ga_ref[tok], gb_ref[tok])

    def scatter_row(tile, slot, r, priority):
        pltpu.make_async_copy(_tile_rows(ybuf, slot, r), token_tile(y_hbm, dst_ref[tile * TM + r]),
                              ssem.at[slot]).start(priority=priority)

    def row_loop(start_row, groups):
        def body(g, c):
            for u in range(ROW_GROUP):
                start_row(g * ROW_GROUP + u, u % 2)
            return c
        lax.fori_loop(0, groups, body, 0)

    def wait_groups(hbm, buf, slot, sem, groups, to_hbm):
        def body(g, c):
            rows = buf.at[pl.ds(pl.multiple_of((slot * TM + g * ROW_GROUP) * SUBLANES, group_rows),
                                group_rows)]
            block = hbm.at[pl.ds(0, group_rows)]
            copy = (pltpu.make_async_copy(rows, block, sem.at[slot]) if to_hbm
                    else pltpu.make_async_copy(block, rows, sem.at[slot]))
            copy.wait()
            return c
        lax.fori_loop(0, groups, body, 0)

    @pl.when(i == 0)
    def _():
        xbuf[...] = jnp.zeros_like(xbuf)
        gbuf[...] = jnp.zeros_like(gbuf)
        ybuf[...] = jnp.zeros_like(ybuf)
        spare = pltpu.make_async_copy(
            ybuf, y_hbm.at[pl.ds(n_tok * SUBLANES, 2 * TM * SUBLANES)], ssem.at[0])
        spare.start()
        spare.wait()
        for t in range(GATHER_AHEAD):
            @pl.when(t < n_used)
            def _():
                row_loop(lambda r, pr: gather_row(t, t, r, pr), n_groups(t))

    @pl.when(i + GATHER_AHEAD < n_used)
    def _():
        nxt = i + GATHER_AHEAD
        row_loop(lambda r, pr: gather_row(nxt, nxt % GATHER_SLOTS, r, pr), n_groups(nxt))

    @pl.when(jnp.logical_and(i >= 2, i - 2 < n_used))
    def _():
        wait_groups(y_hbm, ybuf, i % 2, ssem, n_groups(i - 2), True)

    @pl.when(i < n_used)
    def _():
        slot = i % 2
        xslot = i % GATHER_SLOTS
        groups = n_groups(i)
        wait_groups(h2_hbm, xbuf, xslot, gsem, groups, False)
        base = xslot * TM * SUBLANES
        xb = jnp.concatenate([_token_rows(xbuf, s, TM, base) for s in range(SUBLANES)],
                             axis=1).astype(BF16)
        gates = gbuf[pl.ds(xslot * TM, TM), :]
        y = jnp.zeros((TM, D_MODEL), F32)
        for e, (wg, wu, wd) in enumerate(((wga_ref, wua_ref, wda_ref), (wgb_ref, wub_ref, wdb_ref))):
            gt = _dot(xb, wg[...].astype(BF16))
            up = _dot(xb, wu[...].astype(BF16))
            act = gt * _sigmoid(gt) * up * gates[:, e:e + 1]
            y = y + _dot(act.astype(BF16), wd[...].astype(BF16))
        ybase = slot * TM * SUBLANES
        for s in range(SUBLANES):
            ybuf[pl.ds(ybase + s, TM, stride=SUBLANES), :] = y[:, s * LANES:(s + 1) * LANES]
        row_loop(lambda r, pr: scatter_row(i, slot, r, pr), groups)


def _moe(layer, h2, gate_a, gate_b, plan, w_gate, w_up, w_down):
    src, dst, ea, eb, nv, nt = plan
    n_t = ea.shape[0]
    n_tok = h2.shape[0] // SUBLANES

    def w_spec(shape, which):
        def index(i, src_ref, dst_ref, ea_ref, eb_ref, nv_ref, nt_ref, ga_ref, gb_ref):
            t = jnp.minimum(i, jnp.maximum(nt_ref[0] - 1, 0))
            e = (ea_ref, eb_ref)[which][t]
            return (layer, e, 0, 0)
        return pl.BlockSpec((None, None) + shape, index)

    gu = (D_MODEL, D_EXPERT)
    dn = (D_EXPERT, D_MODEL)
    grid_spec = pltpu.PrefetchScalarGridSpec(
        num_scalar_prefetch=8,
        grid=(n_t,),
        in_specs=[pl.BlockSpec(memory_space=pl.ANY),
                  w_spec(gu, 0), w_spec(gu, 0), w_spec(dn, 0),
                  w_spec(gu, 1), w_spec(gu, 1), w_spec(dn, 1)],
        out_specs=pl.BlockSpec(memory_space=pl.ANY),
        scratch_shapes=[pltpu.VMEM((GATHER_SLOTS * TM * SUBLANES, LANES), F32),
                        pltpu.VMEM((2 * TM * SUBLANES, LANES), F32),
                        pltpu.VMEM((GATHER_SLOTS * TM, LANES), F32),
                        pltpu.SemaphoreType.DMA((GATHER_SLOTS,)), pltpu.SemaphoreType.DMA((2,))],
    )
    return pl.pallas_call(
        functools.partial(_moe_kernel, n_tok),
        grid_spec=grid_spec,
        out_shape=jax.ShapeDtypeStruct(((n_tok + 2 * TM) * SUBLANES, LANES), F32),
        compiler_params=_cparams(("arbitrary",)),
        name=f"moe{layer}",
    )(src, dst, ea, eb, nv, nt, gate_a.reshape(n_tok), gate_b.reshape(n_tok), h2,
      w_gate, w_up, w_down, w_gate, w_up, w_down)


def kernel(x_prompt, x_sample, cache_k, cache_v, state_hgrn, c_prompt, c_sample, w_ada, b_ada,
           ln1, ln2, w_in, q_norm, k_norm, sinks, lb_param, o_norm, w_out, w_router, router_bias,
           w_e_gate, w_e_up, w_e_down):
    batch, seq, _ = x_prompt.shape
    dec_batch, dec_seq, _ = x_sample.shape
    depth = w_ada.shape[0]
    assert dec_batch == GRP and dec_seq == SUB and seq % ROWS == 0 and seq % HGRN_ROWS == 0
    tiles_per_batch = seq // ROWS
    n_prompt = batch * seq
    n_tok = n_prompt + dec_batch * dec_seq
    n_tiles = n_tok // ROWS
    prompt_tiles = n_prompt // ROWS
    n_cache = cache_k.shape[2]

    seg = np.arange(Q_A) // HEAD_DIM
    bd = jnp.asarray((seg[:, None] == seg[None, :]).astype(np.float32) / HEAD_DIM, BF16)

    def hgrn_consts(length):
        selb, sell, masks = _hgrn_consts(length)
        return jnp.asarray(selb, BF16), jnp.asarray(sell, BF16), jnp.asarray(masks)

    consts_p = hgrn_consts(CHUNK)
    consts_s = hgrn_consts(dec_seq)
    upper = jnp.asarray(np.triu(np.ones((ROWS, ROWS), np.float32), 1), BF16)
    lower = jnp.asarray(np.tril(np.ones((GRP, GRP), np.float32), -1), BF16)

    w_in_b = w_in.astype(BF16)
    w_out_b = w_out.astype(BF16)
    q_norm_t = jnp.tile(q_norm, (1, N_HEADS_A)).reshape(depth, 1, Q_A)
    k_norm_t = jnp.tile(k_norm, (1, N_KV_A)).reshape(depth, 1, KV_A)
    o_norm_t = jnp.tile(o_norm, (1, N_HEADS_B)).reshape(depth, 1, V_B)
    ln1_r = ln1.reshape(depth, 1, D_MODEL)
    ln2_r = ln2.reshape(depth, 1, D_MODEL)
    w_router_t = w_router.T
    router_bias_c = router_bias.reshape(N_EXPERTS, 1)

    mod = _adaln(c_prompt, c_sample, w_ada, b_ada).reshape(depth, batch + 1, GRP, N_MOD * D_MODEL)

    cache_k2 = cache_k.reshape(depth, dec_batch, n_cache, KV_A)
    cache_v2 = cache_v.reshape(depth, dec_batch, n_cache, KV_A)

    src = (x_prompt, x_sample)
    kp, vp, sp, kss, vss, sss = [], [], [], [], [], []
    for l in range(depth):
        x, q4, k, v, qb, logf, kb, ib, gb, kwin, vwin = _inproj(
            l, src, n_tiles, prompt_tiles, mod, ln1_r, w_in_b, q_norm_t, k_norm_t, lb_param, bd,
            tiles_per_batch)

        oap4 = _attn_prompt(q4, k, v, sinks[l], batch, seq)
        oas4 = _attn_sample(q4, k, v, cache_k2[l], cache_v2[l], sinks[l], prompt_tiles,
                            n_prompt, dec_batch, dec_seq)
        obp, s_p = _hgrn_prompt(qb, kb, ib, gb, logf, consts_p, bd, o_norm_t[l], batch, seq)
        obs, s_s = _hgrn_sample(qb, kb, ib, gb, logf, state_hgrn[l], consts_s,
                                bd, o_norm_t[l], n_prompt, dec_batch, dec_seq)

        x1, h2, cls3, gate_a, gate_b = _outproj(l, x, oap4, obp, oas4, obs, mod, ln2_r, w_out_b,
                                                w_router_t, router_bias_c, tiles_per_batch)
        y = _moe(l, h2, gate_a, gate_b, _route_plan(cls3, upper, lower), w_e_gate, w_e_up,
                 w_e_down)
        src = (x1, y)

        kp.append(kwin.reshape(batch, WINDOW, N_KV_A, HEAD_DIM))
        vp.append(vwin.reshape(batch, WINDOW, N_KV_A, HEAD_DIM))
        sp.append(s_p)
        kss.append(k[n_prompt:].reshape(dec_batch, dec_seq, N_KV_A, HEAD_DIM))
        vss.append(v[n_prompt:].reshape(dec_batch, dec_seq, N_KV_A, HEAD_DIM))
        sss.append(s_s)

    yp, ys = _final(depth - 1, src[0], src[1], mod, batch, seq, tiles_per_batch)
    return (yp, ys, jnp.stack(kp), jnp.stack(vp), jnp.stack(sp),
            jnp.stack(kss), jnp.stack(vss), jnp.stack(sss))
```

```python
import functools

import numpy as np
import jax
import jax.numpy as jnp
from jax import lax
from jax.experimental import pallas as pl
from jax.experimental.pallas import tpu as pltpu

F32 = jnp.float32
BF16 = jnp.bfloat16
I32 = jnp.int32

D_MODEL = 1024
HEAD_DIM = 64
N_HEADS_A = 8
N_KV_A = 2
GQA_GROUP = N_HEADS_A // N_KV_A
N_HEADS_B = 8
DK_B = 64
DV_B = 64
Q_A = N_HEADS_A * HEAD_DIM
KV_A = N_KV_A * HEAD_DIM
W_B = N_HEADS_B * DK_B
V_B = N_HEADS_B * DV_B
CHUNK = 64
WINDOW = 128
N_EXPERTS = 16
N_GROUPS = 4
EXPERTS_PER_GROUP = 4
D_EXPERT = D_MODEL // 2
RMS_EPS = 1e-6
NEG_INF = -1e30
N_MOD = 6

LANES = 128
SUBLANES = 8
ROWS = 512
GRP = 32
SUB = ROWS // GRP
HALF_ROWS = ROWS // 2
ATT_ROWS = 512
HGRN_ROWS = 2048
N_PAIR = N_HEADS_B // 2
SAMPLE_SEQS = 8
HGRN_SPAN_LIMIT = 80.0
TM = 256
PAIRS_PER_GROUP = 6
PAIR_TABLE = ((0, 1), (0, 2), (0, 3), (1, 3), (1, 2), (3, 2))
N_CLASS = N_GROUPS * PAIRS_PER_GROUP
ROW_GROUP = 8
GROUP_SHIFT = ROW_GROUP.bit_length() - 1
GATHER_AHEAD = 3
GATHER_SLOTS = GATHER_AHEAD + 1
VMEM_LIMIT = 56 * 1024 * 1024


def _cparams(sem):
    return pltpu.CompilerParams(dimension_semantics=sem, vmem_limit_bytes=VMEM_LIMIT)


def _dot(a, b):
    return jnp.dot(a, b, preferred_element_type=F32)


def _dot_nt(a, b):
    return lax.dot_general(a, b, (((1,), (1,)), ((), ())), preferred_element_type=F32)


def _dot_tn(a, b):
    return lax.dot_general(a, b, (((0,), (0,)), ((), ())), preferred_element_type=F32)


def _sigmoid(x):
    return 1.0 / (1.0 + jnp.exp(-x))


def _split2(x):
    hi = x.astype(BF16)
    lo = (x - hi.astype(F32)).astype(BF16)
    return hi, lo


def _split3(x):
    hi = x.astype(BF16)
    r = x - hi.astype(F32)
    mid = r.astype(BF16)
    lo = (r - mid.astype(F32)).astype(BF16)
    return hi, mid, lo


def _seg_mean(sq, bd):
    return _dot(sq.astype(BF16), bd)


def _head(x, j):
    return x[:, j * HEAD_DIM:(j + 1) * HEAD_DIM]


def _token_rows(ref, s, n, base=0):
    return ref[pl.ds(base + s, n, stride=SUBLANES), :]


def _adaln_kernel(n_prompt, c_ref, w_ref, b_ref, o_ref):
    c = c_ref[...]
    a = c * _sigmoid(c)
    a_hi, a_lo = _split2(a)
    w_hi, w_lo = _split2(w_ref[...])
    acc = _dot(a_hi, w_hi) + _dot(a_lo, w_hi) + _dot(a_hi, w_lo) + b_ref[...]
    for b in range(n_prompt):
        o_ref[b * GRP:(b + 1) * GRP, :] = jnp.broadcast_to(acc[b:b + 1], (GRP, acc.shape[1]))
    o_ref[n_prompt * GRP:, :] = acc[n_prompt:]


def _adaln(c_prompt, c_sample, w_ada, b_ada):
    depth = w_ada.shape[0]
    n_prompt = c_prompt.shape[0]
    c_all = jnp.concatenate([c_prompt, c_sample], axis=0)
    rows = n_prompt * GRP + c_sample.shape[0]
    tn = D_MODEL
    return pl.pallas_call(
        functools.partial(_adaln_kernel, n_prompt),
        grid=(depth, N_MOD * D_MODEL // tn),
        in_specs=[
            pl.BlockSpec(c_all.shape, lambda l, n: (0, 0)),
            pl.BlockSpec((None, D_MODEL, tn), lambda l, n: (l, 0, n)),
            pl.BlockSpec((None, 1, tn), lambda l, n: (l, 0, n)),
        ],
        out_specs=pl.BlockSpec((None, rows, tn), lambda l, n: (l, 0, n)),
        out_shape=jax.ShapeDtypeStruct((depth, rows, N_MOD * D_MODEL), F32),
        compiler_params=_cparams(("arbitrary", "arbitrary")),
        name="adaln",
    )(c_all, w_ada, b_ada.reshape(depth, 1, N_MOD * D_MODEL))


def _mod_spec(layer, j, tiles_per_batch):
    return pl.BlockSpec((None, None, GRP, D_MODEL),
                        lambda i, *_: (layer, i // tiles_per_batch, 0, j))


def _by_group(fn, xs, mod_refs, first=0, count=GRP):
    outs = []
    for g in range(count):
        rows = slice(g * SUB, (g + 1) * SUB)
        outs.append(fn(*[x[rows] for x in xs], *[m[first + g:first + g + 1, :] for m in mod_refs]))
    return jnp.concatenate(outs, axis=0)


def _gated_add(x, gate_ref, y):
    return _by_group(lambda xg, yg, gg: xg + gg * yg, (x, y), (gate_ref,))


def _modulated_norm(x, ln_ref, sc_ref, sh_ref, first=0, count=GRP):
    ms = jnp.mean(x * x, axis=-1, keepdims=True)
    xn = x * lax.rsqrt(ms + RMS_EPS)
    scale = ln_ref[...] * (1.0 + sc_ref[...])
    return _by_group(lambda xg, sc, sh: xg * sc + sh, (xn,), (scale, sh_ref), first, count)


def _prompt_tile_spec(prompt_tiles, tiles_per_batch):
    def index(i):
        t = jnp.minimum(i, prompt_tiles - 1)
        return (t // tiles_per_batch, t % tiles_per_batch, 0)
    return pl.BlockSpec((None, ROWS, D_MODEL), index)


def _moe_output_tile(y_ref):
    return jnp.concatenate([_token_rows(y_ref, s, ROWS) for s in range(SUBLANES)], axis=1)


def _inproj_kernel(layer, prompt_tiles, n_src, *refs):
    src = refs[:n_src]
    sh_ref, sc_ref, ln_ref, w_ref, qn_ref, kn_ref, lbp_ref, bd_ref = refs[n_src:n_src + 8]
    (x_out, q_out, k_out, v_out, qb_out, logf_out, kb_out, ib_out, gb_out,
     kwin_out, vwin_out) = refs[n_src + 8:]
    if layer == 0:
        xp_ref, xs_ref = src
        x = jnp.where(pl.program_id(0) < prompt_tiles, xp_ref[...],
                      xs_ref[...].reshape(ROWS, D_MODEL))
    else:
        x1_ref, y_ref, g2_ref = src
        x = _gated_add(x1_ref[...], g2_ref, _moe_output_tile(y_ref))
    x_out[...] = x
    bd = bd_ref[...]

    p = lbp_ref[...]
    e = jnp.exp(p - jnp.max(p, axis=0, keepdims=True))
    sm = e / jnp.sum(e, axis=0, keepdims=True)
    cum0 = sm[0:1]
    cum = cum0
    for r in range(1, layer + 1):
        cum = cum + sm[r:r + 1]
    lb = cum - cum0

    c0 = Q_A + 2 * KV_A
    c1 = c0 + 2 * W_B
    for half in range(ROWS // HALF_ROWS):
        rows = slice(half * HALF_ROWS, (half + 1) * HALF_ROWS)
        groups = HALF_ROWS // SUB
        hb = _modulated_norm(x[rows], ln_ref, sc_ref, sh_ref, half * groups, groups).astype(BF16)

        qa = _dot(hb, w_ref[:, 0:Q_A])
        kv = _dot(hb, w_ref[:, Q_A:c0])
        qb = _dot(hb, w_ref[:, c0:c0 + W_B])
        z = _dot(hb, w_ref[:, c0 + W_B:c1])
        ib = _dot(hb, w_ref[:, c1:c1 + V_B])
        gb = _dot(hb, w_ref[:, c1 + V_B:c1 + 2 * V_B])

        qa = qa * lax.rsqrt(_seg_mean(qa * qa, bd) + RMS_EPS) * qn_ref[...]
        qa = (qa * (HEAD_DIM ** -0.5)).astype(BF16)
        for hh in range(GQA_GROUP):
            q_out[hh, rows, :] = jnp.concatenate(
                [_head(qa, g * GQA_GROUP + hh) for g in range(N_KV_A)], axis=1)

        ka = kv[:, :KV_A]
        ka = ka * lax.rsqrt(_seg_mean(ka * ka, bd[:KV_A, :KV_A]) + RMS_EPS) * kn_ref[...]
        k_out[rows, :] = ka
        v_out[rows, :] = kv[:, KV_A:]

        if (half + 1) * HALF_ROWS == ROWS:
            @pl.when(pl.program_id(0) < prompt_tiles)
            def _():
                kwin_out[...] = ka[HALF_ROWS - WINDOW:]
                vwin_out[...] = kv[HALF_ROWS - WINDOW:, KV_A:]

        qb_out[rows, :] = (qb * (DK_B ** -0.5)).astype(BF16)
        ib_out[rows, :] = ib.astype(BF16)
        gb_out[rows, :] = gb.astype(BF16)

        pos = z >= 0.0
        t = jnp.exp(-jnp.abs(z))
        num = jnp.where(pos, jnp.log(1.0 + lb * t), jnp.maximum(jnp.log(lb + t), z))
        logf_out[rows, :] = num - jnp.log(1.0 + t)
        kb_out[rows, :] = ((1.0 - lb) * jnp.where(pos, t, 1.0) / (1.0 + t)).astype(BF16)


def _inproj(layer, src, n_tiles, prompt_tiles, mod, ln1, w_in_b, q_norm_t, k_norm_t, lb_param, bd,
            tiles_per_batch):
    depth = lb_param.shape[0]
    n = n_tiles * ROWS
    in_width = w_in_b.shape[-1]
    x_spec = pl.BlockSpec((ROWS, D_MODEL), lambda i: (i, 0))
    if layer == 0:
        src_specs = [_prompt_tile_spec(prompt_tiles, tiles_per_batch),
                     pl.BlockSpec((GRP, SUB, D_MODEL), lambda i: (0, 0, 0))]
        src_args = list(src)
    else:
        src_specs = [x_spec, pl.BlockSpec((ROWS * SUBLANES, LANES), lambda i: (i, 0)),
                     _mod_spec(layer - 1, 5, tiles_per_batch)]
        src_args = list(src) + [mod]
    row_spec = lambda w: pl.BlockSpec((ROWS, w), lambda i: (i, 0))
    win_spec = pl.BlockSpec(
        (None, WINDOW, KV_A), lambda i: (jnp.minimum(i, prompt_tiles - 1) // tiles_per_batch, 0, 0))
    return pl.pallas_call(
        functools.partial(_inproj_kernel, layer, prompt_tiles, len(src_specs)),
        grid=(n_tiles,),
        in_specs=src_specs + [
            _mod_spec(layer, 0, tiles_per_batch), _mod_spec(layer, 1, tiles_per_batch),
            pl.BlockSpec((None, 1, D_MODEL), lambda i: (layer, 0, 0)),
            pl.BlockSpec((None, D_MODEL, in_width), lambda i: (layer, 0, 0)),
            pl.BlockSpec((None, 1, Q_A), lambda i: (layer, 0, 0)),
            pl.BlockSpec((None, 1, KV_A), lambda i: (layer, 0, 0)),
            pl.BlockSpec((depth, W_B), lambda i: (0, 0)),
            pl.BlockSpec((Q_A, Q_A), lambda i: (0, 0)),
        ],
        out_specs=[
            x_spec,
            pl.BlockSpec((None, GQA_GROUP, ROWS, LANES), lambda i: (i, 0, 0, 0)),
            row_spec(KV_A), row_spec(KV_A),
            row_spec(W_B), row_spec(W_B), row_spec(W_B), row_spec(V_B), row_spec(V_B),
            win_spec, win_spec,
        ],
        out_shape=[
            jax.ShapeDtypeStruct((n, D_MODEL), F32),
            jax.ShapeDtypeStruct((n_tiles, GQA_GROUP, ROWS, LANES), BF16),
            jax.ShapeDtypeStruct((n, KV_A), F32),
            jax.ShapeDtypeStruct((n, KV_A), F32),
            jax.ShapeDtypeStruct((n, W_B), BF16),
            jax.ShapeDtypeStruct((n, W_B), F32),
            jax.ShapeDtypeStruct((n, W_B), BF16),
            jax.ShapeDtypeStruct((n, V_B), BF16),
            jax.ShapeDtypeStruct((n, V_B), BF16),
            jax.ShapeDtypeStruct((prompt_tiles // tiles_per_batch, WINDOW, KV_A), F32),
            jax.ShapeDtypeStruct((prompt_tiles // tiles_per_batch, WINDOW, KV_A), F32),
        ],
        compiler_params=_cparams(("arbitrary",)),
        name=f"inproj{layer}",
    )(*src_args, mod, mod, ln1, w_in_b, q_norm_t, k_norm_t, lb_param, bd)


def _final_kernel(prompt_tiles, x1_ref, y_ref, g2_ref, yp_ref, ys_ref):
    i = pl.program_id(0)
    x = _gated_add(x1_ref[...], g2_ref, _moe_output_tile(y_ref))

    @pl.when(i < prompt_tiles)
    def _():
        yp_ref[...] = x

    @pl.when(i >= prompt_tiles)
    def _():
        ys_ref[...] = x.reshape(GRP, SUB, D_MODEL)


def _final(layer, x1, y, mod, batch, seq, tiles_per_batch):
    n_tiles = x1.shape[0] // ROWS
    prompt_tiles = batch * tiles_per_batch
    return pl.pallas_call(
        functools.partial(_final_kernel, prompt_tiles),
        grid=(n_tiles,),
        in_specs=[pl.BlockSpec((ROWS, D_MODEL), lambda i: (i, 0)),
                  pl.BlockSpec((ROWS * SUBLANES, LANES), lambda i: (i, 0)),
                  _mod_spec(layer, 5, tiles_per_batch)],
        out_specs=[_prompt_tile_spec(prompt_tiles, tiles_per_batch),
                   pl.BlockSpec((GRP, SUB, D_MODEL), lambda i: (0, 0, 0))],
        out_shape=[jax.ShapeDtypeStruct((batch, seq, D_MODEL), F32),
                   jax.ShapeDtypeStruct((GRP, SUB, D_MODEL), F32)],
        compiler_params=_cparams(("arbitrary",)),
        name="final",
    )(x1, y, mod)


def _attn_tables(sinks_ref, q_len, n_k, k_off):
    n_q = GQA_GROUP * q_len
    row = lax.broadcasted_iota(I32, (n_q, n_k), 0)
    col = lax.broadcasted_iota(I32, (n_q, n_k), 1)
    dist = jnp.abs((row % q_len) + k_off - col).astype(F32)
    hh_col = lax.broadcasted_iota(I32, (n_q, 1), 0) // q_len
    bias, sinks = [], []
    for g in range(N_KV_A):
        head = (g * GQA_GROUP + 1 + hh_col).astype(F32)
        bias.append(jnp.exp2(-8.0 * head / N_HEADS_A) * dist)
        sink = jnp.zeros((n_q, 1), F32)
        for j in range(GQA_GROUP):
            sink = jnp.where(hh_col == j, sinks_ref[g * GQA_GROUP + j], sink)
        sinks.append(sink)
    return bias, sinks


def _attend(q, k, v, bias, sinks, first_valid=None):
    n_q = q.shape[0]
    n_k = k.shape[0]
    lane = lax.broadcasted_iota(I32, (n_k, LANES), 1)
    out = jnp.zeros((n_q, LANES), F32)
    for g in range(N_KV_A):
        in_g = (lane // HEAD_DIM) == g
        kg = jnp.where(in_g, k, 0.0).astype(BF16)
        vg = jnp.where(in_g, v, 0.0).astype(BF16)
        s = _dot_nt(q, kg) - bias[g]
        if first_valid is not None:
            col = lax.broadcasted_iota(I32, (n_q, n_k), 1)
            s = jnp.where(col >= first_valid, s, NEG_INF)
        m = jnp.maximum(jnp.max(s, axis=-1, keepdims=True), sinks[g])
        p = jnp.exp(s - m)
        denom = jnp.sum(p, axis=-1, keepdims=True) + jnp.exp(sinks[g] - m)
        out = out + _dot(p.astype(BF16), vg) / denom
    return out


def _attn_prompt_kernel(sinks_ref, q_ref, kp_ref, kc_ref, vp_ref, vc_ref, o_ref):
    j = pl.program_id(1)
    kcat = jnp.concatenate([kp_ref[...], kc_ref[...]], axis=0)
    vcat = jnp.concatenate([vp_ref[...], vc_ref[...]], axis=0)
    bias, sinks = _attn_tables(sinks_ref, CHUNK, WINDOW + CHUNK, WINDOW)

    for c in range(ATT_ROWS // CHUNK):
        q = q_ref[:, c * CHUNK:(c + 1) * CHUNK, :].reshape(GQA_GROUP * CHUNK, LANES)
        k = kcat[c * CHUNK:c * CHUNK + WINDOW + CHUNK]
        v = vcat[c * CHUNK:c * CHUNK + WINDOW + CHUNK]
        first_valid = WINDOW - c * CHUNK - j * ATT_ROWS
        out = _attend(q, k, v, bias, sinks, first_valid)
        o_ref[:, c * CHUNK:(c + 1) * CHUNK, :] = out.reshape(GQA_GROUP, CHUNK, LANES).astype(BF16)


def _attn_prompt(q4, k, v, sinks, batch, seq):
    steps = seq // ATT_ROWS
    per_tile = ROWS // ATT_ROWS
    win_blocks = ATT_ROWS // WINDOW
    n_tiles = batch * seq // ROWS
    cur = pl.BlockSpec((ATT_ROWS, KV_A), lambda b, j: (b * steps + j, 0))
    prev = pl.BlockSpec(
        (WINDOW, KV_A), lambda b, j: (b * steps * win_blocks + jnp.maximum(j * win_blocks - 1, 0), 0))
    qo = pl.BlockSpec((None, GQA_GROUP, ATT_ROWS, LANES),
                      lambda b, j: ((b * steps + j) // per_tile, 0, j % per_tile, 0))
    return pl.pallas_call(
        _attn_prompt_kernel,
        grid=(batch, steps),
        in_specs=[pl.BlockSpec(memory_space=pltpu.SMEM), qo, prev, cur, prev, cur],
        out_specs=qo,
        out_shape=jax.ShapeDtypeStruct((n_tiles, GQA_GROUP, ROWS, LANES), BF16),
        compiler_params=_cparams(("arbitrary", "arbitrary")),
        name="attn_prompt",
    )(sinks, q4, k, k, v, v)


def _attn_sample_kernel(n_new, sinks_ref, q_ref, kc_ref, kn_ref, vc_ref, vn_ref, o_ref):
    n_cache = kc_ref.shape[1]
    bias, sinks = _attn_tables(sinks_ref, n_new, n_cache + n_new, n_cache)
    for r in range(SAMPLE_SEQS):
        rows = slice(r * n_new, (r + 1) * n_new)
        q = q_ref[:, rows, :].reshape(GQA_GROUP * n_new, LANES)
        k = jnp.concatenate([kc_ref[r], kn_ref[rows, :]], axis=0)
        v = jnp.concatenate([vc_ref[r], vn_ref[rows, :]], axis=0)
        out = _attend(q, k, v, bias, sinks)
        o_ref[:, rows, :] = out.reshape(GQA_GROUP, n_new, LANES).astype(BF16)


def _attn_sample(q4, k, v, cache_k, cache_v, sinks, tile, row0, dec_batch, dec_seq):
    n_cache = cache_k.shape[1]
    rows = SAMPLE_SEQS * dec_seq
    q_spec = pl.BlockSpec((None, GQA_GROUP, rows, LANES), lambda b: (tile, 0, b, 0))
    new = pl.BlockSpec((rows, KV_A), lambda b: (row0 // rows + b, 0))
    cache = pl.BlockSpec((SAMPLE_SEQS, n_cache, KV_A), lambda b: (b, 0, 0))
    return pl.pallas_call(
        functools.partial(_attn_sample_kernel, dec_seq),
        grid=(dec_batch // SAMPLE_SEQS,),
        in_specs=[pl.BlockSpec(memory_space=pltpu.SMEM), q_spec, cache, new, cache, new],
        out_specs=pl.BlockSpec((None, GQA_GROUP, rows, LANES), lambda b: (0, 0, b, 0)),
        out_shape=jax.ShapeDtypeStruct((1, GQA_GROUP, dec_batch * dec_seq, LANES), BF16),
        compiler_params=_cparams(("arbitrary",)),
        name="attn_sample",
    )(sinks, q4, cache_k, k, cache_v, v)


def _level_sizes(length):
    sizes = []
    m = length // 2
    while m >= 1:
        sizes.append(m)
        m //= 2
    return sizes


def _hgrn_consts(length):
    t = np.arange(length)
    tri = (t[None, :] <= t[:, None]).astype(np.float32)
    after = (t[None, :] > t[:, None]).astype(np.float32)
    levels = []
    masks = []
    s = np.tile(t, 2)[None, :]
    tt = t[:, None]
    for m in _level_sizes(length):
        blk = t // m
        same = blk[None, :] == blk[:, None]
        q_rows = (blk % 2 == 1)[:, None]
        sel = np.where(q_rows, same & (t[None, :] <= t[:, None]), same & (t[None, :] > t[:, None]))
        levels.append(sel.astype(np.float32))
        masks.append(((tt // m) % 2 == 1) & ((s // m) == (tt // m) - 1))
    masks.append(s == tt)
    masks.append(s <= tt)
    base = np.concatenate([tri, after], axis=0)
    lev = np.concatenate(levels, axis=0)
    return (np.concatenate([base] * 3, axis=1), np.concatenate([lev] * 3, axis=1),
            np.stack(masks).astype(np.int32))


def _stack2(x, m0, m1):
    xb = x.astype(BF16)
    return jnp.concatenate([xb * m0, xb * m1], axis=0)


def _hgrn_intra(qp, kp, b, masks, m0, m1, arg_ref, sl):
    length = qp.shape[0]
    n_lev = len(_level_sizes(length))
    if arg_ref is None:
        mid = length // 2 - 1
        rel = b - b[mid:mid + 1]
        a = _dot_nt((qp * jnp.exp(rel)).astype(BF16), _stack2(kp * jnp.exp(-rel), m0, m1))
        a = jnp.where(masks[n_lev + 1], a, 0.0)
    else:
        a = _dot_nt(qp.astype(BF16), _stack2(kp, m0, m1))
        a = jnp.where(masks[n_lev], a, 0.0)
        for lev in range(n_lev):
            e = jnp.exp(arg_ref[lev * length:(lev + 1) * length, sl])
            pr = _dot_nt((qp * e).astype(BF16), _stack2(kp * e, m0, m1))
            a = jnp.where(masks[lev], pr, a)
    return a.astype(BF16)


def _hgrn_finish(a, qp, kp, ip, base, sl, st_ref, seq, p, m0, m1, same_head):
    length = qp.shape[0]
    e_b = jnp.exp(base[0:length, sl])
    e_k = jnp.exp(base[length:2 * length, sl])
    st = st_ref[seq, p]
    o = _dot(jnp.concatenate([a, (qp * e_b).astype(BF16)], axis=1),
             jnp.concatenate([_stack2(ip, m0, m1), st.T.astype(BF16)], axis=0))
    upd = _dot_tn(ip.astype(BF16), (kp * e_k).astype(BF16))
    st_ref[seq, p] = st * e_b[length - 1:length, :] + jnp.where(same_head, upd, 0.0)
    return o


def _state_to_pairs(s0_ref, st_ref, n_seq):
    zero = jnp.zeros((DK_B, DV_B), F32)
    for r in range(n_seq):
        for p in range(N_PAIR):
            top = jnp.concatenate([s0_ref[r, 2 * p], zero], axis=1)
            bot = jnp.concatenate([zero, s0_ref[r, 2 * p + 1]], axis=1)
            st_ref[r, p] = jnp.concatenate([top, bot], axis=0).T


def _pairs_to_state(st_ref, s_ref, n_seq):
    for r in range(n_seq):
        for p in range(N_PAIR):
            d = st_ref[r, p].T
            s_ref[r, 2 * p] = d[:DK_B, :DV_B]
            s_ref[r, 2 * p + 1] = d[DK_B:, DV_B:]


def _hgrn_kernel(n_seq, n_chunk, length, carry_state, *refs):
    if carry_state:
        (q_ref, k_ref, i_ref, g_ref, f_ref, selb_ref, sell_ref, mask_ref, bd_ref, on_ref,
         o_ref, s_ref, st_ref, base_ref, arg_ref) = refs

        @pl.when(pl.program_id(1) == 0)
        def _():
            st_ref[...] = jnp.zeros_like(st_ref)
    else:
        (q_ref, k_ref, i_ref, g_ref, f_ref, s0_ref, selb_ref, sell_ref, mask_ref, bd_ref, on_ref,
         o_ref, s_ref, st_ref, base_ref, arg_ref) = refs
        _state_to_pairs(s0_ref, st_ref, n_seq)

    bd = bd_ref[...]
    onorm = on_ref[...]
    n_items = n_seq * n_chunk
    mid = length // 2 - 1
    lane = lax.broadcasted_iota(I32, (length, LANES), 1)
    m0 = (lane < DK_B).astype(BF16)
    m1 = (lane >= DK_B).astype(BF16)
    rowp = lax.broadcasted_iota(I32, (LANES, LANES), 0) // DV_B
    colp = lax.broadcasted_iota(I32, (LANES, LANES), 1) // DK_B
    same_head = rowp == colp
    masks = [mask_ref[m] != 0 for m in range(mask_ref.shape[0])]

    def pieces(rows):
        return jnp.concatenate(_split3(f_ref[rows, :]), axis=0)

    def pair(ref, rows, p):
        return ref[rows, p * LANES:(p + 1) * LANES].astype(F32)

    span = jnp.float32(0.0)
    for it in range(n_items):
        base = _dot(selb_ref[...], pieces(slice(it * length, (it + 1) * length)))
        base_ref[it * 2 * length:(it + 1) * 2 * length, :] = base
        b_mid = base[mid:mid + 1]
        span = jnp.maximum(span, jnp.max(base[0:1] - b_mid))
        span = jnp.maximum(span, jnp.max(b_mid - base[length - 1:length]))
    safe = span <= HGRN_SPAN_LIMIT

    def finish(rows, base, seq, a_of):
        outs = []
        for p in range(N_PAIR):
            sl = slice(p * LANES, (p + 1) * LANES)
            outs.append(_hgrn_finish(a_of(p), pair(q_ref, rows, p), pair(k_ref, rows, p),
                                     pair(i_ref, rows, p), base, sl, st_ref, seq, p, m0, m1,
                                     same_head))
        o = jnp.concatenate(outs, axis=1)
        g = g_ref[rows, :].astype(F32)
        o = o * lax.rsqrt(_seg_mean(o * o, bd) + RMS_EPS) * onorm * (g * _sigmoid(g))
        o_ref[rows, :] = o.astype(BF16)

    @pl.when(safe)
    def _():
        for it in range(n_items):
            rows = slice(it * length, (it + 1) * length)
            base = base_ref[it * 2 * length:(it + 1) * 2 * length, :]

            def a_of(p, rows=rows, base=base):
                sl = slice(p * LANES, (p + 1) * LANES)
                return _hgrn_intra(pair(q_ref, rows, p), pair(k_ref, rows, p), base[0:length, sl],
                                   masks, m0, m1, None, sl)

            finish(rows, base, it // n_chunk, a_of)

    @pl.when(jnp.logical_not(safe))
    def _():
        def body(it, carry):
            rows = pl.ds(pl.multiple_of(it * length, length), length)
            arg_ref[...] = _dot(sell_ref[...], pieces(rows))
            base = base_ref[pl.ds(pl.multiple_of(it * 2 * length, 2 * length), 2 * length), :]
            seq = it // n_chunk if n_chunk > 1 and n_seq > 1 else (it if n_chunk == 1 else 0)

            def a_of(p):
                sl = slice(p * LANES, (p + 1) * LANES)
                return _hgrn_intra(pair(q_ref, rows, p), pair(k_ref, rows, p), base[0:length, sl],
                                   masks, m0, m1, arg_ref, sl)

            finish(rows, base, seq, a_of)
            return carry

        lax.fori_loop(0, n_items, body, 0)

    if carry_state:
        @pl.when(pl.program_id(1) == pl.num_programs(1) - 1)
        def _():
            _pairs_to_state(st_ref, s_ref, n_seq)
    else:
        _pairs_to_state(st_ref, s_ref, n_seq)


def _hgrn_scratch(n_seq, n_items, length, sell):
    return [pltpu.VMEM((n_seq, N_PAIR, LANES, LANES), F32),
            pltpu.VMEM((n_items * 2 * length, W_B), F32),
            pltpu.VMEM((sell.shape[0], W_B), F32)]


def _hgrn_prompt(qb, kb, ib, gb, logf, consts, bd, onorm_t, batch, seq):
    selb, sell, masks = consts
    steps = seq // HGRN_ROWS
    n_chunk = HGRN_ROWS // CHUNK
    row = pl.BlockSpec((HGRN_ROWS, W_B), lambda b, j: (b * steps + j, 0))
    whole = lambda a: pl.BlockSpec(a.shape, lambda b, j: (0,) * a.ndim)
    return pl.pallas_call(
        functools.partial(_hgrn_kernel, 1, n_chunk, CHUNK, True),
        grid=(batch, steps),
        in_specs=[row, row, row, row, row, whole(selb), whole(sell), whole(masks), whole(bd),
                  whole(onorm_t)],
        out_specs=[row, pl.BlockSpec((1, N_HEADS_B, DK_B, DV_B), lambda b, j: (b, 0, 0, 0))],
        out_shape=[jax.ShapeDtypeStruct((batch * seq, V_B), BF16),
                   jax.ShapeDtypeStruct((batch, N_HEADS_B, DK_B, DV_B), F32)],
        scratch_shapes=_hgrn_scratch(1, n_chunk, CHUNK, sell),
        compiler_params=_cparams(("arbitrary", "arbitrary")),
        name="hgrn_prompt",
    )(qb, kb, ib, gb, logf, selb, sell, masks, bd, onorm_t)


def _hgrn_sample(qb, kb, ib, gb, logf, s0, consts, bd, onorm_t, row0, dec_batch, dec_seq):
    selb, sell, masks = consts
    rows = SAMPLE_SEQS * dec_seq
    row = pl.BlockSpec((rows, W_B), lambda b: (row0 // rows + b, 0))
    whole = lambda a: pl.BlockSpec(a.shape, lambda b: (0,) * a.ndim)
    state = pl.BlockSpec((SAMPLE_SEQS, N_HEADS_B, DK_B, DV_B), lambda b: (b, 0, 0, 0))
    return pl.pallas_call(
        functools.partial(_hgrn_kernel, SAMPLE_SEQS, 1, dec_seq, False),
        grid=(dec_batch // SAMPLE_SEQS,),
        in_specs=[row, row, row, row, row, state, whole(selb), whole(sell), whole(masks),
                  whole(bd), whole(onorm_t)],
        out_specs=[pl.BlockSpec((rows, V_B), lambda b: (b, 0)), state],
        out_shape=[jax.ShapeDtypeStruct((dec_batch * dec_seq, V_B), BF16),
                   jax.ShapeDtypeStruct((dec_batch, N_HEADS_B, DK_B, DV_B), F32)],
        scratch_shapes=_hgrn_scratch(SAMPLE_SEQS, SAMPLE_SEQS, dec_seq, sell),
        compiler_params=_cparams(("arbitrary",)),
        name="hgrn_sample",
    )(qb, kb, ib, gb, logf, s0, selb, sell, masks, bd, onorm_t)


def _outproj_kernel(prompt_tiles, x_ref, oap_ref, obp_ref, oas_ref, obs_ref, g1_ref, sh_ref, sc_ref,
                    ln_ref, w_ref, wr_ref, rb_ref, x1_out, h2_out, cls_out, ga_out, gb_out):
    def mix_of(oa_ref, ob_ref):
        heads = [_head(oa_ref[h % GQA_GROUP], h // GQA_GROUP) for h in range(N_HEADS_A)]
        return jnp.concatenate(heads + [ob_ref[...]], axis=1)

    is_prompt = pl.program_id(0) < prompt_tiles
    mix = jnp.where(is_prompt, mix_of(oap_ref, obp_ref), mix_of(oas_ref, obs_ref))
    att = _dot(mix, w_ref[...])
    x1 = _gated_add(x_ref[...], g1_ref, att)
    x1_out[...] = x1
    h = _modulated_norm(x1, ln_ref, sc_ref, sh_ref)

    for s in range(SUBLANES):
        h2_out[pl.ds(s, ROWS, stride=SUBLANES), :] = h[:, s * LANES:(s + 1) * LANES]

    h_hi, h_lo = _split2(h)
    w_hi, w_lo = _split2(wr_ref[...])
    hi_terms = _dot_nt(jnp.concatenate([w_hi, w_lo], axis=0), h_hi)
    logits = hi_terms[0:N_EXPERTS] + hi_terms[N_EXPERTS:] + _dot_nt(w_hi, h_lo)
    aff = _sigmoid(logits)
    sel = aff + rb_ref[...]
    e_idx = lax.broadcasted_iota(I32, (N_EXPERTS, ROWS), 0)
    l_idx = lax.broadcasted_iota(I32, (EXPERTS_PER_GROUP, ROWS), 0)

    def first_argmax(vals, idx, big):
        top = jnp.max(vals, axis=0, keepdims=True)
        return top, jnp.min(jnp.where(vals == top, idx, big), axis=0, keepdims=True)

    g_scores = []
    for g in range(N_GROUPS):
        sg = sel[g * EXPERTS_PER_GROUP:(g + 1) * EXPERTS_PER_GROUP]
        m1, i1 = first_argmax(sg, l_idx, EXPERTS_PER_GROUP)
        m2 = jnp.max(jnp.where(l_idx == i1, -jnp.inf, sg), axis=0, keepdims=True)
        g_scores.append(m1 + m2)
    g_score = jnp.concatenate(g_scores, axis=0)
    g_idx = lax.broadcasted_iota(I32, (N_GROUPS, ROWS), 0)
    _, g_sel = first_argmax(g_score, g_idx, N_GROUPS)
    masked = jnp.where((e_idx // EXPERTS_PER_GROUP) == g_sel, sel, -jnp.inf)
    _, e1 = first_argmax(masked, e_idx, N_EXPERTS)
    _, e2 = first_argmax(jnp.where(e_idx == e1, -jnp.inf, masked), e_idx, N_EXPERTS)
    w1 = jnp.sum(jnp.where(e_idx == e1, aff, 0.0), axis=0, keepdims=True)
    w2 = jnp.sum(jnp.where(e_idx == e2, aff, 0.0), axis=0, keepdims=True)
    den = w1 + w2
    w1 = w1 / den
    w2 = w2 / den
    lo_first = e1 < e2
    gate_lo = jnp.where(lo_first, w1, w2)
    gate_hi = jnp.where(lo_first, w2, w1)
    a = jnp.minimum(e1, e2) - g_sel * EXPERTS_PER_GROUP
    b = jnp.maximum(e1, e2) - g_sel * EXPERTS_PER_GROUP
    pair = jnp.where(a == 0, b - 1, jnp.where(a == 1, 6 - b, 5))
    cls_out[...] = g_sel * PAIRS_PER_GROUP + pair
    ga_out[...] = jnp.where(a == 2, gate_hi, gate_lo)
    gb_out[...] = jnp.where(a == 2, gate_lo, gate_hi)


def _outproj(layer, x4, oap4, obp, oas4, obs, mod, ln2, w_out_b, w_router_t, router_bias_c,
             tiles_per_batch):
    n_tiles = x4.shape[0] // ROWS
    x_spec = pl.BlockSpec((ROWS, D_MODEL), lambda i: (i, 0))
    prompt_tiles = oap4.shape[0]
    last = prompt_tiles - 1
    lane_row = pl.BlockSpec((None, 1, ROWS), lambda i: (i, 0, 0))
    return pl.pallas_call(
        functools.partial(_outproj_kernel, prompt_tiles),
        grid=(n_tiles,),
        in_specs=[
            x_spec,
            pl.BlockSpec((None, GQA_GROUP, ROWS, LANES), lambda i: (jnp.minimum(i, last), 0, 0, 0)),
            pl.BlockSpec((ROWS, V_B), lambda i: (jnp.minimum(i, last), 0)),
            pl.BlockSpec((None, GQA_GROUP, ROWS, LANES), lambda i: (0, 0, 0, 0)),
            pl.BlockSpec((ROWS, V_B), lambda i: (0, 0)),
            _mod_spec(layer, 2, tiles_per_batch), _mod_spec(layer, 3, tiles_per_batch),
            _mod_spec(layer, 4, tiles_per_batch),
            pl.BlockSpec((None, 1, D_MODEL), lambda i: (layer, 0, 0)),
            pl.BlockSpec((None, Q_A + V_B, D_MODEL), lambda i: (layer, 0, 0)),
            pl.BlockSpec((N_EXPERTS, D_MODEL), lambda i: (0, 0)),
            pl.BlockSpec((N_EXPERTS, 1), lambda i: (0, 0)),
        ],
        out_specs=[
            x_spec,
            pl.BlockSpec((ROWS * SUBLANES, LANES), lambda i: (i, 0)),
            lane_row, lane_row, lane_row,
        ],
        out_shape=[
            jax.ShapeDtypeStruct(x4.shape, F32),
            jax.ShapeDtypeStruct((n_tiles * ROWS * SUBLANES, LANES), F32),
            jax.ShapeDtypeStruct((n_tiles, 1, ROWS), I32),
            jax.ShapeDtypeStruct((n_tiles, 1, ROWS), F32),
            jax.ShapeDtypeStruct((n_tiles, 1, ROWS), F32),
        ],
        compiler_params=_cparams(("arbitrary",)),
        name=f"outproj{layer}",
    )(x4, oap4, obp, oas4, obs, mod, mod, mod, ln2, w_out_b, w_router_t, router_bias_c)


def _rank_kernel(cls_ref, upper_ref, lower_ref, dest_ref, count_ref, rank_ref):
    n_tiles = cls_ref.shape[0]
    c_idx = lax.broadcasted_iota(I32, (GRP, ROWS), 0)
    upper = upper_ref[...]

    def rank_body(i, carry):
        onehot = c_idx == cls_ref[i]
        within = _dot(onehot.astype(BF16), upper)
        rank_ref[i] = jnp.sum(jnp.where(onehot, within + carry, 0.0), axis=0, keepdims=True)
        return carry + jnp.sum(onehot.astype(F32), axis=1, keepdims=True)

    count = lax.fori_loop(0, n_tiles, rank_body, jnp.zeros((GRP, 1), F32))
    count_ref[...] = jnp.broadcast_to(count, (GRP, LANES)).astype(I32)
    tiles = jnp.floor((count + (TM - 1)) * (1.0 / TM))
    first_tile = _dot(lower_ref[...], jnp.broadcast_to(tiles, (GRP, LANES)).astype(BF16))
    start = first_tile[:, 0:1] * TM

    def dest_body(i, carry):
        onehot = c_idx == cls_ref[i]
        dest = rank_ref[i] + jnp.sum(jnp.where(onehot, start, 0.0), axis=0, keepdims=True)
        dest_ref[i] = dest.astype(I32)
        return carry

    lax.fori_loop(0, n_tiles, dest_body, 0)


def _plan_kernel(n_tok, n_t, dest_ref, count_ref, src_ref, dst_ref, ea_ref, eb_ref, nv_ref, nt_ref):
    groups = TM // ROW_GROUP

    def fill(first_group, end_group):
        def body(p8, carry):
            tile_parity = lax.shift_right_logical(p8, groups.bit_length() - 1) & 1
            spare = n_tok + tile_parity * TM + (p8 & (groups - 1)) * ROW_GROUP
            for u in range(ROW_GROUP):
                src_ref[p8 * ROW_GROUP + u] = 0
                dst_ref[p8 * ROW_GROUP + u] = spare + u
            return carry
        lax.fori_loop(first_group, end_group, body, 0)

    tile = 0
    for c in range(N_CLASS):
        g, pi = divmod(c, PAIRS_PER_GROUP)
        a, b = PAIR_TABLE[pi]
        cnt = count_ref[c, 0]
        n_c = lax.shift_right_logical(cnt + (TM - 1), TM.bit_length() - 1)
        first = tile

        def mark(t, carry, g=g, a=a, b=b, cnt=cnt, first=first):
            ea_ref[t] = g * EXPERTS_PER_GROUP + a
            eb_ref[t] = g * EXPERTS_PER_GROUP + b
            nv_ref[t] = jnp.minimum(cnt - (t - first) * TM, TM)
            return carry

        lax.fori_loop(tile, tile + n_c, mark, 0)

        fill(lax.shift_right_logical(first * TM + cnt, GROUP_SHIFT), (first + n_c) * groups)
        tile = tile + n_c
    nt_ref[0] = tile

    def unused(t, carry):
        ea_ref[t] = 0
        eb_ref[t] = 0
        nv_ref[t] = 0
        return carry

    lax.fori_loop(tile, n_t, unused, 0)

    fill(tile * groups, n_t * groups)

    def place(t, carry):
        d = dest_ref[t]
        src_ref[d] = t
        dst_ref[d] = t
        return carry

    lax.fori_loop(0, n_tok, place, 0, unroll=8)


def _route_plan(cls3, upper, lower):
    n_tiles = cls3.shape[0]
    n_tok = n_tiles * ROWS
    n_t = (n_tok + N_CLASS * (TM - 1)) // TM + 2
    dest, count = pl.pallas_call(
        _rank_kernel,
        out_shape=[jax.ShapeDtypeStruct((n_tiles, 1, ROWS), I32),
                   jax.ShapeDtypeStruct((GRP, LANES), I32)],
        scratch_shapes=[pltpu.VMEM((n_tiles, 1, ROWS), F32)],
        name="moe_rank",
    )(cls3, upper, lower)
    smem = pl.BlockSpec(memory_space=pltpu.SMEM)
    return pl.pallas_call(
        functools.partial(_plan_kernel, n_tok, n_t),
        in_specs=[smem, smem],
        out_specs=[smem, smem, smem, smem, smem, smem],
        out_shape=[jax.ShapeDtypeStruct((n_t * TM,), I32),
                   jax.ShapeDtypeStruct((n_t * TM,), I32),
                   jax.ShapeDtypeStruct((n_t,), I32),
                   jax.ShapeDtypeStruct((n_t,), I32),
                   jax.ShapeDtypeStruct((n_t,), I32),
                   jax.ShapeDtypeStruct((1,), I32)],
        name="moe_plan",
    )(dest.reshape(n_tok), count)


def _tile_rows(ref, slot, r):
    return ref.at[pl.ds(pl.multiple_of((slot * TM + r) * SUBLANES, SUBLANES), SUBLANES)]


def _moe_kernel(n_tok, src_ref, dst_ref, ea_ref, eb_ref, nv_ref, nt_ref, ga_ref, gb_ref, h2_hbm,
                wga_ref, wua_ref, wda_ref, wgb_ref, wub_ref, wdb_ref, y_hbm, xbuf, ybuf, gbuf, gsem,
                ssem):
    i = pl.program_id(0)
    n_used = nt_ref[0]
    lane = lax.broadcasted_iota(I32, (1, LANES), 1)
    group_rows = ROW_GROUP * SUBLANES

    def n_groups(tile):
        return lax.shift_right_logical(nv_ref[tile] + (ROW_GROUP - 1), GROUP_SHIFT)

    def token_tile(hbm, tok):
        return hbm.at[pl.ds(pl.multiple_of(tok * SUBLANES, SUBLANES), SUBLANES)]

    def gather_row(tile, slot, r, priority):
        tok = src_ref[tile * TM + r]
        pltpu.make_async_copy(token_tile(h2_hbm, tok), _tile_rows(xbuf, slot, r),
                              gsem.at[slot]).start(priority=priority)
        gbuf[pl.ds(slot * TM + r, 1), :] = jnp.where(lane == 0, ga_ref[tok], gb_ref[tok])

    def scatter_row(tile, slot, r, priority):
        pltpu.make_async_copy(_tile_rows(ybuf, slot, r), token_tile(y_hbm, dst_ref[tile * TM + r]),
                              ssem.at[slot]).start(priority=priority)

    def row_loop(start_row, groups):
        def body(g, c):
            for u in range(ROW_GROUP):
                start_row(g * ROW_GROUP + u, u % 2)
            return c
        lax.fori_loop(0, groups, body, 0)

    def wait_groups(hbm, buf, slot, sem, groups, to_hbm):
        def body(g, c):
            rows = buf.at[pl.ds(pl.multiple_of((slot * TM + g * ROW_GROUP) * SUBLANES, group_rows),
                                group_rows)]
            block = hbm.at[pl.ds(0, group_rows)]
            copy = (pltpu.make_async_copy(rows, block, sem.at[slot]) if to_hbm
                    else pltpu.make_async_copy(block, rows, sem.at[slot]))
            copy.wait()
            return c
        lax.fori_loop(0, groups, body, 0)

    @pl.when(i == 0)
    def _():
        xbuf[...] = jnp.zeros_like(xbuf)
        gbuf[...] = jnp.zeros_like(gbuf)
        ybuf[...] = jnp.zeros_like(ybuf)
        spare = pltpu.make_async_copy(
            ybuf, y_hbm.at[pl.ds(n_tok * SUBLANES, 2 * TM * SUBLANES)], ssem.at[0])
        spare.start()
        spare.wait()
        for t in range(GATHER_AHEAD):
            @pl.when(t < n_used)
            def _():
                row_loop(lambda r, pr: gather_row(t, t, r, pr), n_groups(t))

    @pl.when(i + GATHER_AHEAD < n_used)
    def _():
        nxt = i + GATHER_AHEAD
        row_loop(lambda r, pr: gather_row(nxt, nxt % GATHER_SLOTS, r, pr), n_groups(nxt))

    @pl.when(jnp.logical_and(i >= 2, i - 2 < n_used))
    def _():
        wait_groups(y_hbm, ybuf, i % 2, ssem, n_groups(i - 2), True)

    @pl.when(i < n_used)
    def _():
        slot = i % 2
        xslot = i % GATHER_SLOTS
        groups = n_groups(i)
        wait_groups(h2_hbm, xbuf, xslot, gsem, groups, False)
        base = xslot * TM * SUBLANES
        xb = jnp.concatenate([_token_rows(xbuf, s, TM, base) for s in range(SUBLANES)],
                             axis=1).astype(BF16)
        gates = gbuf[pl.ds(xslot * TM, TM), :]
        y = jnp.zeros((TM, D_MODEL), F32)
        for e, (wg, wu, wd) in enumerate(((wga_ref, wua_ref, wda_ref), (wgb_ref, wub_ref, wdb_ref))):
            gt = _dot(xb, wg[...].astype(BF16))
            up = _dot(xb, wu[...].astype(BF16))
            act = gt * _sigmoid(gt) * up * gates[:, e:e + 1]
            y = y + _dot(act.astype(BF16), wd[...].astype(BF16))
        ybase = slot * TM * SUBLANES
        for s in range(SUBLANES):
            ybuf[pl.ds(ybase + s, TM, stride=SUBLANES), :] = y[:, s * LANES:(s + 1) * LANES]
        row_loop(lambda r, pr: scatter_row(i, slot, r, pr), groups)


def _moe(layer, h2, gate_a, gate_b, plan, w_gate, w_up, w_down):
    src, dst, ea, eb, nv, nt = plan
    n_t = ea.shape[0]
    n_tok = h2.shape[0] // SUBLANES

    def w_spec(shape, which):
        def index(i, src_ref, dst_ref, ea_ref, eb_ref, nv_ref, nt_ref, ga_ref, gb_ref):
            t = jnp.minimum(i, jnp.maximum(nt_ref[0] - 1, 0))
            e = (ea_ref, eb_ref)[which][t]
            return (layer, e, 0, 0)
        return pl.BlockSpec((None, None) + shape, index)

    gu = (D_MODEL, D_EXPERT)
    dn = (D_EXPERT, D_MODEL)
    grid_spec = pltpu.PrefetchScalarGridSpec(
        num_scalar_prefetch=8,
        grid=(n_t,),
        in_specs=[pl.BlockSpec(memory_space=pl.ANY),
                  w_spec(gu, 0), w_spec(gu, 0), w_spec(dn, 0),
                  w_spec(gu, 1), w_spec(gu, 1), w_spec(dn, 1)],
        out_specs=pl.BlockSpec(memory_space=pl.ANY),
        scratch_shapes=[pltpu.VMEM((GATHER_SLOTS * TM * SUBLANES, LANES), F32),
                        pltpu.VMEM((2 * TM * SUBLANES, LANES), F32),
                        pltpu.VMEM((GATHER_SLOTS * TM, LANES), F32),
                        pltpu.SemaphoreType.DMA((GATHER_SLOTS,)), pltpu.SemaphoreType.DMA((2,))],
    )
    return pl.pallas_call(
        functools.partial(_moe_kernel, n_tok),
        grid_spec=grid_spec,
        out_shape=jax.ShapeDtypeStruct(((n_tok + 2 * TM) * SUBLANES, LANES), F32),
        compiler_params=_cparams(("arbitrary",)),
        name=f"moe{layer}",
    )(src, dst, ea, eb, nv, nt, gate_a.reshape(n_tok), gate_b.reshape(n_tok), h2,
      w_gate, w_up, w_down, w_gate, w_up, w_down)


def kernel(x_prompt, x_sample, cache_k, cache_v, state_hgrn, c_prompt, c_sample, w_ada, b_ada,
           ln1, ln2, w_in, q_norm, k_norm, sinks, lb_param, o_norm, w_out, w_router, router_bias,
           w_e_gate, w_e_up, w_e_down):
    batch, seq, _ = x_prompt.shape
    dec_batch, dec_seq, _ = x_sample.shape
    depth = w_ada.shape[0]
    assert dec_batch == GRP and dec_seq == SUB and seq % ROWS == 0 and seq % HGRN_ROWS == 0
    tiles_per_batch = seq // ROWS
    n_prompt = batch * seq
    n_tok = n_prompt + dec_batch * dec_seq
    n_tiles = n_tok // ROWS
    prompt_tiles = n_prompt // ROWS
    n_cache = cache_k.shape[2]

    seg = np.arange(Q_A) // HEAD_DIM
    bd = jnp.asarray((seg[:, None] == seg[None, :]).astype(np.float32) / HEAD_DIM, BF16)

    def hgrn_consts(length):
        selb, sell, masks = _hgrn_consts(length)
        return jnp.asarray(selb, BF16), jnp.asarray(sell, BF16), jnp.asarray(masks)

    consts_p = hgrn_consts(CHUNK)
    consts_s = hgrn_consts(dec_seq)
    upper = jnp.asarray(np.triu(np.ones((ROWS, ROWS), np.float32), 1), BF16)
    lower = jnp.asarray(np.tril(np.ones((GRP, GRP), np.float32), -1), BF16)

    w_in_b = w_in.astype(BF16)
    w_out_b = w_out.astype(BF16)
    q_norm_t = jnp.tile(q_norm, (1, N_HEADS_A)).reshape(depth, 1, Q_A)
    k_norm_t = jnp.tile(k_norm, (1, N_KV_A)).reshape(depth, 1, KV_A)
    o_norm_t = jnp.tile(o_norm, (1, N_HEADS_B)).reshape(depth, 1, V_B)
    ln1_r = ln1.reshape(depth, 1, D_MODEL)
    ln2_r = ln2.reshape(depth, 1, D_MODEL)
    w_router_t = w_router.T
    router_bias_c = router_bias.reshape(N_EXPERTS, 1)

    mod = _adaln(c_prompt, c_sample, w_ada, b_ada).reshape(depth, batch + 1, GRP, N_MOD * D_MODEL)

    cache_k2 = cache_k.reshape(depth, dec_batch, n_cache, KV_A)
    cache_v2 = cache_v.reshape(depth, dec_batch, n_cache, KV_A)

    src = (x_prompt, x_sample)
    kp, vp, sp, kss, vss, sss = [], [], [], [], [], []
    for l in range(depth):
        x, q4, k, v, qb, logf, kb, ib, gb, kwin, vwin = _inproj(
            l, src, n_tiles, prompt_tiles, mod, ln1_r, w_in_b, q_norm_t, k_norm_t, lb_param, bd,
            tiles_per_batch)

        oap4 = _attn_prompt(q4, k, v, sinks[l], batch, seq)
        oas4 = _attn_sample(q4, k, v, cache_k2[l], cache_v2[l], sinks[l], prompt_tiles,
                            n_prompt, dec_batch, dec_seq)
        obp, s_p = _hgrn_prompt(qb, kb, ib, gb, logf, consts_p, bd, o_norm_t[l], batch, seq)
        obs, s_s = _hgrn_sample(qb, kb, ib, gb, logf, state_hgrn[l], consts_s,
                                bd, o_norm_t[l], n_prompt, dec_batch, dec_seq)

        x1, h2, cls3, gate_a, gate_b = _outproj(l, x, oap4, obp, oas4, obs, mod, ln2_r, w_out_b,
                                                w_router_t, router_bias_c, tiles_per_batch)
        y = _moe(l, h2, gate_a, gate_b, _route_plan(cls3, upper, lower), w_e_gate, w_e_up,
                 w_e_down)
        src = (x1, y)

        kp.append(kwin.reshape(batch, WINDOW, N_KV_A, HEAD_DIM))
        vp.append(vwin.reshape(batch, WINDOW, N_KV_A, HEAD_DIM))
        sp.append(s_p)
        kss.append(k[n_prompt:].reshape(dec_batch, dec_seq, N_KV_A, HEAD_DIM))
        vss.append(v[n_prompt:].reshape(dec_batch, dec_seq, N_KV_A, HEAD_DIM))
        sss.append(s_s)

    yp, ys = _final(depth - 1, src[0], src[1], mod, batch, seq, tiles_per_batch)
    return (yp, ys, jnp.stack(kp), jnp.stack(vp), jnp.stack(sp),
            jnp.stack(kss), jnp.stack(vss), jnp.stack(sss))
```
